```python
import jax, jax.numpy as jnp
from jax import lax
import numpy as np

D_MODEL = 2048
BATCH = 8
SEQ = 4096
DEPTH = 4

BRANCH_WIDTH = D_MODEL // 2
N_BRANCHES = 3
RET_HEADS = 4
RET_WIDTH = BRANCH_WIDTH
RET_HEAD_DIM = RET_WIDTH // RET_HEADS
RET_CHUNK = 128
RET_ROPE_BASE = 10000.0
POOL_WINDOWS = (2, 4, 8, 16)
POOL_GROUPS = len(POOL_WINDOWS)
POOL_WIDTH = BRANCH_WIDTH
POOL_GROUP_DIM = POOL_WIDTH // POOL_GROUPS
ATT_HEAD_DIM = 128
ATT_WIDTH = BRANCH_WIDTH
ATT_Q_HEADS = ATT_WIDTH // ATT_HEAD_DIM
ATT_KV_HEADS = ATT_Q_HEADS // 4
ATT_KV_WIDTH = ATT_KV_HEADS * ATT_HEAD_DIM
ATT_WINDOW = 128
ATT_BLOCK = 128
ROPE_THETA = 500000.0
ROPE_DIMS = ATT_HEAD_DIM // 4
RMS_EPS = 1e-6
NEG_BIG = -1e30

IN_SIZES = (RET_WIDTH, RET_WIDTH, RET_WIDTH, RET_WIDTH,
            POOL_WIDTH, POOL_WIDTH,
            ATT_WIDTH, ATT_KV_WIDTH, ATT_KV_WIDTH, ATT_WIDTH,
            N_BRANCHES * D_MODEL)
IN_WIDTH = sum(IN_SIZES)

kernel_name = "hybrid_retention_pool_swa_encoder"


def rms_norm(x, gain=None):
    x32 = x.astype(jnp.float32)
    y = x32 * lax.rsqrt(jnp.mean(x32 * x32, axis=-1, keepdims=True) + RMS_EPS)
    if gain is not None:
        y = y * gain.astype(jnp.float32)
    return y.astype(x.dtype)


def rotary(x, inv_freq):
    S = x.shape[1]
    half = inv_freq.shape[0]
    ang = jnp.arange(S, dtype=jnp.float32)[:, None] * inv_freq[None, :]
    cos = jnp.cos(ang)[None, :, None, :]
    sin = jnp.sin(ang)[None, :, None, :]
    xr = x[..., :2 * half].astype(jnp.float32)
    x1, x2 = xr[..., :half], xr[..., half:]
    rot = jnp.concatenate([x1 * cos - x2 * sin, x2 * cos + x1 * sin], axis=-1).astype(x.dtype)
    return jnp.concatenate([rot, x[..., 2 * half:]], axis=-1)


def retention(q, k, v, a_fwd, a_bwd):
    B, S, H, Dh = q.shape
    C = RET_CHUNK
    nC = S // C
    dt = q.dtype
    inv = 1.0 / (RET_ROPE_BASE ** jnp.linspace(0.0, 1.0, Dh // 2, dtype=jnp.float32))
    q = rotary(q, inv)
    k = rotary(k, inv) * jnp.asarray(Dh ** -0.5, dt)
    lg_f = -jnp.exp(a_fwd.astype(jnp.float32))
    lg_b = -jnp.exp(a_bwd.astype(jnp.float32))
    j = jnp.arange(C, dtype=jnp.float32)
    lag = j[:, None] - j[None, :]
    alag = jnp.abs(lag)[None]
    dmask = jnp.where(lag[None] >= 0,
                      jnp.exp(lg_f[:, None, None] * alag),
                      jnp.exp(lg_b[:, None, None] * alag)).astype(dt)
    qc = q.reshape(B, nC, C, H, Dh)
    kc = k.reshape(B, nC, C, H, Dh)
    vc = v.reshape(B, nC, C, H, Dh)
    scores = jnp.einsum('bnjhd,bnlhd->bnhjl', qc, kc) * dmask[None, None]
    out = jnp.einsum('bnhjl,bnlhe->bnjhe', scores, vc)
    w_f = jnp.exp(lg_f[None, :] * (C - 1 - j)[:, None]).astype(dt)
    w_b = jnp.exp(lg_b[None, :] * j[:, None]).astype(dt)
    kv_f = jnp.einsum('bnlhd,lh,bnlhe->nbhde', kc, w_f, vc)
    kv_b = jnp.einsum('bnlhd,lh,bnlhe->nbhde', kc, w_b, vc)
    dec_f = jnp.exp(lg_f * C).astype(dt)[None, :, None, None]
    dec_b = jnp.exp(lg_b * C).astype(dt)[None, :, None, None]

    def step_f(state, kv):
        return state * dec_f + kv, state

    def step_b(state, kv):
        return state * dec_b + kv, state

    init = jnp.zeros((B, H, Dh, Dh), dt)
    _, s_f = lax.scan(step_f, init, kv_f)
    _, s_b = lax.scan(step_b, init, kv_b, reverse=True)
    q_f = jnp.exp(lg_f[None, :] * (j + 1.0)[:, None]).astype(dt)
    q_b = jnp.exp(lg_b[None, :] * (C - j)[:, None]).astype(dt)
    out = (out
           + jnp.einsum('bnjhd,nbhde->bnjhe', qc * q_f[None, None, :, :, None], s_f)
           + jnp.einsum('bnjhd,nbhde->bnjhe', qc * q_b[None, None, :, :, None], s_b))
    out = rms_norm(out.reshape(B, S, H, Dh))
    return out.reshape(B, S, H * Dh)


def multiscale_pool(u, pool_w, pool_scale):
    B, S, _ = u.shape
    ug = u.reshape(B, S, POOL_GROUPS, POOL_GROUP_DIM).astype(jnp.float32)
    cs = jnp.pad(jnp.cumsum(ug, axis=1), ((0, 0), (1, 0), (0, 0), (0, 0)))
    pos = jnp.arange(S)
    groups = []
    for g, w in enumerate(POOL_WINDOWS):
        lo = jnp.clip(pos - w // 2, 0, S)
        hi = jnp.clip(pos + w // 2, 0, S)
        cnt = (hi - lo).astype(jnp.float32)[None, :, None]
        csg = cs[:, :, g]
        mean = (csg[:, hi] - csg[:, lo]) / cnt
        groups.append(mean - ug[:, :, g])
    p = jnp.stack(groups, axis=2).astype(u.dtype)
    y = jnp.einsum('bsgd,gde->bsge', p, pool_w).reshape(B, S, POOL_WIDTH)
    return y * pool_scale


def windowed_gqa(q, k, v, q_gain, k_gain, sink):
    B, S, Hq, Dh = q.shape
    Hkv = k.shape[2]
    G = Hq // Hkv
    nB = S // ATT_BLOCK
    L = ATT_BLOCK
    inv = ROPE_THETA ** (-jnp.arange(ROPE_DIMS // 2, dtype=jnp.float32) / (ROPE_DIMS // 2))
    q = rotary(rms_norm(q, q_gain), inv)
    k = rotary(rms_norm(k, k_gain), inv)
    qb = q.reshape(B, nB, L, Hkv, G, Dh)
    pad = ((0, 0), (1, 1), (0, 0), (0, 0), (0, 0))
    kp = jnp.pad(k.reshape(B, nB, L, Hkv, Dh), pad)
    vp = jnp.pad(v.reshape(B, nB, L, Hkv, Dh), pad)
    kw = jnp.concatenate([kp[:, :-2], kp[:, 1:-1], kp[:, 2:]], axis=2)
    vw = jnp.concatenate([vp[:, :-2], vp[:, 1:-1], vp[:, 2:]], axis=2)
    s = jnp.einsum('bnqkgd,bnskd->bnkgqs', qb, kw).astype(jnp.float32) * (Dh ** -0.5)
    blk = jnp.arange(nB)[:, None]
    qpos = blk * L + jnp.arange(L)[None, :]
    kpos = (blk - 1) * L + jnp.arange(3 * L)[None, :]
    diff = kpos[:, None, :] - qpos[:, :, None]
    valid = ((jnp.abs(diff) <= ATT_WINDOW)
             & (kpos >= 0)[:, None, :] & (kpos < S)[:, None, :])
    s = jnp.where(valid[None, :, None, None], s, NEG_BIG)
    sk = sink.astype(jnp.float32).reshape(Hkv, G)[None, None, :, :, None, None]
    m = jnp.maximum(jnp.max(s, axis=-1, keepdims=True), sk)
    p = jnp.exp(s - m)
    p = p / (jnp.sum(p, axis=-1, keepdims=True) + jnp.exp(sk - m))
    o = jnp.einsum('bnkgqs,bnskd->bnqkgd', p.astype(v.dtype), vw)
    return o.reshape(B, S, Hq * Dh)


def hybrid_layer(x, norm_g, w_in, a_fwd, a_bwd, pool_w, pool_scale,
                 q_gain, k_gain, sink, w_ret, w_pool, w_att, w_out):
    B, S, D = x.shape
    h = rms_norm(x, norm_g)
    z = jnp.einsum('bsd,de->bse', h, w_in)
    (rq, rk, rv, rg, pv, pg, aq, ak, av, ag, mg) = jnp.split(
        z, list(np.cumsum(IN_SIZES)[:-1]), axis=-1)
    ya = retention(rq.reshape(B, S, RET_HEADS, RET_HEAD_DIM),
                   rk.reshape(B, S, RET_HEADS, RET_HEAD_DIM),
                   rv.reshape(B, S, RET_HEADS, RET_HEAD_DIM), a_fwd, a_bwd)
    ya = jnp.einsum('bse,ed->bsd', ya * jax.nn.silu(rg), w_ret)
    yb = multiscale_pool(pv, pool_w, pool_scale)
    yb = jnp.einsum('bse,ed->bsd', yb * jax.nn.silu(pg), w_pool)
    yc = windowed_gqa(aq.reshape(B, S, ATT_Q_HEADS, ATT_HEAD_DIM),
                      ak.reshape(B, S, ATT_KV_HEADS, ATT_HEAD_DIM),
                      av.reshape(B, S, ATT_KV_HEADS, ATT_HEAD_DIM), q_gain, k_gain, sink)
    yc = jnp.einsum('bse,ed->bsd', yc * jax.nn.silu(ag), w_att)
    gates = jax.nn.sigmoid(mg.astype(jnp.float32)).astype(x.dtype).reshape(B, S, N_BRANCHES, D)
    merged = gates[:, :, 0] * ya + gates[:, :, 1] * yb + gates[:, :, 2] * yc
    return x + jnp.einsum('bsd,de->bse', merged, w_out)


def _fwd_setup_inputs(seed: int = 0) -> dict:
    key = jax.random.key(seed)
    ks = jax.random.split(key, 14)
    f32 = jnp.float32
    D = D_MODEL
    nrm = jax.random.normal
    base = np.log(-np.log1p(-(2.0 ** (-5.0 - np.arange(RET_HEADS))))).astype(np.float32)
    base = jnp.asarray(base)[None, :]
    return {
        "x": nrm(ks[0], (BATCH, SEQ, D), f32),
        "norm_g": 1.0 + 0.02 * nrm(ks[1], (DEPTH, D), f32),
        "w_in": nrm(ks[2], (DEPTH, D, IN_WIDTH), f32) * (D ** -0.5),
        "ret_decay_fwd": base + 0.1 * nrm(ks[3], (DEPTH, RET_HEADS), f32),
        "ret_decay_bwd": base + 0.1 * nrm(ks[4], (DEPTH, RET_HEADS), f32),
        "pool_w": nrm(ks[5], (DEPTH, POOL_GROUPS, POOL_GROUP_DIM, POOL_GROUP_DIM), f32) * (POOL_GROUP_DIM ** -0.5),
        "pool_scale": 1.0 + 0.02 * nrm(ks[6], (DEPTH, POOL_WIDTH), f32),
        "attn_q_gain": 1.0 + 0.02 * nrm(ks[7], (DEPTH, ATT_HEAD_DIM), f32),
        "attn_k_gain": 1.0 + 0.02 * nrm(ks[8], (DEPTH, ATT_HEAD_DIM), f32),
        "attn_sink": 0.5 * nrm(ks[9], (DEPTH, ATT_Q_HEADS), f32),
        "w_ret": nrm(ks[10], (DEPTH, RET_WIDTH, D), f32) * (RET_WIDTH ** -0.5),
        "w_pool": nrm(ks[11], (DEPTH, POOL_WIDTH, D), f32) * (POOL_WIDTH ** -0.5),
        "w_att": nrm(ks[12], (DEPTH, ATT_WIDTH, D), f32) * (ATT_WIDTH ** -0.5),
        "w_out": nrm(ks[13], (DEPTH, D, D), f32) * (D ** -0.5),
    }


def _fwd_reference(x, norm_g, w_in, ret_decay_fwd, ret_decay_bwd, pool_w, pool_scale,
              attn_q_gain, attn_k_gain, attn_sink, w_ret, w_pool, w_att, w_out):
    for l in range(DEPTH):
        x = hybrid_layer(x, norm_g[l], w_in[l], ret_decay_fwd[l], ret_decay_bwd[l],
                         pool_w[l], pool_scale[l], attn_q_gain[l], attn_k_gain[l],
                         attn_sink[l], w_ret[l], w_pool[l], w_att[l], w_out[l])
    return x


import jax as _jax
import jax.numpy as _jnp

TWIN_FORMAT = 'train_step'
FWD_PARAMS = ['x', 'norm_g', 'w_in', 'ret_decay_fwd', 'ret_decay_bwd', 'pool_w', 'pool_scale', 'attn_q_gain', 'attn_k_gain', 'attn_sink', 'w_ret', 'w_pool', 'w_att', 'w_out']
TWIN_WEIGHTS = ['norm_g', 'w_in', 'ret_decay_fwd', 'ret_decay_bwd', 'pool_w', 'pool_scale', 'attn_q_gain', 'attn_k_gain', 'attn_sink', 'w_ret', 'w_pool', 'w_att', 'w_out']
TWIN_DIFF_INPUT = 'x'
TWIN_INPUTS = ['x', 'norm_g', 'w_in', 'ret_decay_fwd', 'ret_decay_bwd', 'pool_w', 'pool_scale', 'attn_q_gain', 'attn_k_gain', 'attn_sink', 'w_ret', 'w_pool', 'w_att', 'w_out', 'loss_target', 'm_norm_g', 'm_w_in', 'm_ret_decay_fwd', 'm_ret_decay_bwd', 'm_pool_w', 'm_pool_scale', 'm_attn_q_gain', 'm_attn_k_gain', 'm_attn_sink', 'm_w_ret', 'm_w_pool', 'm_w_att', 'm_w_out', 'v_norm_g', 'v_w_in', 'v_ret_decay_fwd', 'v_ret_decay_bwd', 'v_pool_w', 'v_pool_scale', 'v_attn_q_gain', 'v_attn_k_gain', 'v_attn_sink', 'v_w_ret', 'v_w_pool', 'v_w_att', 'v_w_out']
TWIN_OUTPUTS = ['loss', 'grad_x', 'grad_norm_g', 'grad_w_in', 'grad_ret_decay_fwd', 'grad_ret_decay_bwd', 'grad_pool_w', 'grad_pool_scale', 'grad_attn_q_gain', 'grad_attn_k_gain', 'grad_attn_sink', 'grad_w_ret', 'grad_w_pool', 'grad_w_att', 'grad_w_out', 'delta_norm_g', 'delta_w_in', 'delta_ret_decay_fwd', 'delta_ret_decay_bwd', 'delta_pool_w', 'delta_pool_scale', 'delta_attn_q_gain', 'delta_attn_k_gain', 'delta_attn_sink', 'delta_w_ret', 'delta_w_pool', 'delta_w_att', 'delta_w_out', 'new_m_norm_g', 'new_m_w_in', 'new_m_ret_decay_fwd', 'new_m_ret_decay_bwd', 'new_m_pool_w', 'new_m_pool_scale', 'new_m_attn_q_gain', 'new_m_attn_k_gain', 'new_m_attn_sink', 'new_m_w_ret', 'new_m_w_pool', 'new_m_w_att', 'new_m_w_out', 'new_v_norm_g', 'new_v_w_in', 'new_v_ret_decay_fwd', 'new_v_ret_decay_bwd', 'new_v_pool_w', 'new_v_pool_scale', 'new_v_attn_q_gain', 'new_v_attn_k_gain', 'new_v_attn_sink', 'new_v_w_ret', 'new_v_w_pool', 'new_v_w_att', 'new_v_w_out']
TWIN_LEAF_KINDS = {'loss': 'loss', 'grad_x': 'grad_x', 'grad_norm_g': 'grad_w', 'grad_w_in': 'grad_w', 'grad_ret_decay_fwd': 'grad_w', 'grad_ret_decay_bwd': 'grad_w', 'grad_pool_w': 'grad_w', 'grad_pool_scale': 'grad_w', 'grad_attn_q_gain': 'grad_w', 'grad_attn_k_gain': 'grad_w', 'grad_attn_sink': 'grad_w', 'grad_w_ret': 'grad_w', 'grad_w_pool': 'grad_w', 'grad_w_att': 'grad_w', 'grad_w_out': 'grad_w', 'delta_norm_g': 'delta_w', 'delta_w_in': 'delta_w', 'delta_ret_decay_fwd': 'delta_w', 'delta_ret_decay_bwd': 'delta_w', 'delta_pool_w': 'delta_w', 'delta_pool_scale': 'delta_w', 'delta_attn_q_gain': 'delta_w', 'delta_attn_k_gain': 'delta_w', 'delta_attn_sink': 'delta_w', 'delta_w_ret': 'delta_w', 'delta_w_pool': 'delta_w', 'delta_w_att': 'delta_w', 'delta_w_out': 'delta_w', 'new_m_norm_g': 'new_m', 'new_m_w_in': 'new_m', 'new_m_ret_decay_fwd': 'new_m', 'new_m_ret_decay_bwd': 'new_m', 'new_m_pool_w': 'new_m', 'new_m_pool_scale': 'new_m', 'new_m_attn_q_gain': 'new_m', 'new_m_attn_k_gain': 'new_m', 'new_m_attn_sink': 'new_m', 'new_m_w_ret': 'new_m', 'new_m_w_pool': 'new_m', 'new_m_w_att': 'new_m', 'new_m_w_out': 'new_m', 'new_v_norm_g': 'new_v', 'new_v_w_in': 'new_v', 'new_v_ret_decay_fwd': 'new_v', 'new_v_ret_decay_bwd': 'new_v', 'new_v_pool_w': 'new_v', 'new_v_pool_scale': 'new_v', 'new_v_attn_q_gain': 'new_v', 'new_v_attn_k_gain': 'new_v', 'new_v_attn_sink': 'new_v', 'new_v_w_ret': 'new_v', 'new_v_w_pool': 'new_v', 'new_v_w_att': 'new_v', 'new_v_w_out': 'new_v'}


def _forward(args):
    return _fwd_reference(*[args[k] for k in FWD_PARAMS])


def _output_shape():
    def fwd():
        inp = _fwd_setup_inputs(0)
        return _fwd_reference(*[inp[k] for k in FWD_PARAMS])
    out = _jax.eval_shape(fwd)
    return out.shape, out.dtype

N_MICROBATCH = 1
ADAM_LR = 0.001
ADAM_B1 = 0.9
ADAM_B2 = 0.999
ADAM_EPS = 1e-08
ADAM_WD = 0.01
ADAM_STEP = 10
PER_EXAMPLE_BATCH_AXIS = {'x': 0, 'loss_target': 0}
SHARED_INPUTS = []
_WEIGHT_DTYPES = {'norm_g': _jnp.float32, 'w_in': _jnp.float32, 'ret_decay_fwd': _jnp.float32, 'ret_decay_bwd': _jnp.float32, 'pool_w': _jnp.float32, 'pool_scale': _jnp.float32, 'attn_q_gain': _jnp.float32, 'attn_k_gain': _jnp.float32, 'attn_sink': _jnp.float32, 'w_ret': _jnp.float32, 'w_pool': _jnp.float32, 'w_att': _jnp.float32, 'w_out': _jnp.float32}
MOMENT_SCALE = {'norm_g': 5.048142e+00, 'w_in': 7.836352e-02, 'ret_decay_fwd': 7.780706e-01, 'ret_decay_bwd': 8.463106e-01, 'pool_w': 1.661069e-01, 'pool_scale': 2.601706e+00, 'attn_q_gain': 2.149240e-01, 'attn_k_gain': 2.147187e-01, 'attn_sink': 1.124545e-02, 'w_ret': 8.835269e-02, 'w_pool': 7.906437e-02, 'w_att': 8.915359e-03, 'w_out': 1.177283e-01}


def _to_microbatches(a, axis):
    t = _jnp.moveaxis(a, axis, 0)
    t = t.reshape((N_MICROBATCH, t.shape[0] // N_MICROBATCH) + t.shape[1:])
    return _jnp.moveaxis(t, 1, axis + 1)


def setup_inputs(seed: int = 0) -> dict:
    inp = _fwd_setup_inputs(seed)
    key = _jax.random.fold_in(_jax.random.key(seed), 7919)
    shape, _ = _output_shape()
    out = dict(inp)
    out["loss_target"] = _jax.random.normal(_jax.random.fold_in(key, 0), shape, _jnp.float32)
    for i, name in enumerate(TWIN_WEIGHTS):
        w = inp[name].astype(_jnp.float32)
        if MOMENT_SCALE is None:
            s = _jnp.sqrt(_jnp.mean(_jnp.square(w)) + 1e-30)
        else:
            s = MOMENT_SCALE[name]
        km, kv = _jax.random.split(_jax.random.fold_in(key, i + 1))
        out[name] = w
        out["m_" + name] = s * _jax.random.normal(km, w.shape, _jnp.float32)
        out["v_" + name] = (s * s) * _jax.random.uniform(kv, w.shape, _jnp.float32, 0.5, 1.5)
    if N_MICROBATCH > 1:
        for name, axis in PER_EXAMPLE_BATCH_AXIS.items():
            out[name] = _to_microbatches(out[name], axis)
    return {'x': out['x'], 'norm_g': out['norm_g'], 'w_in': out['w_in'], 'ret_decay_fwd': out['ret_decay_fwd'], 'ret_decay_bwd': out['ret_decay_bwd'], 'pool_w': out['pool_w'], 'pool_scale': out['pool_scale'], 'attn_q_gain': out['attn_q_gain'], 'attn_k_gain': out['attn_k_gain'], 'attn_sink': out['attn_sink'], 'w_ret': out['w_ret'], 'w_pool': out['w_pool'], 'w_att': out['w_att'], 'w_out': out['w_out'], 'loss_target': out['loss_target'], 'm_norm_g': out['m_norm_g'], 'm_w_in': out['m_w_in'], 'm_ret_decay_fwd': out['m_ret_decay_fwd'], 'm_ret_decay_bwd': out['m_ret_decay_bwd'], 'm_pool_w': out['m_pool_w'], 'm_pool_scale': out['m_pool_scale'], 'm_attn_q_gain': out['m_attn_q_gain'], 'm_attn_k_gain': out['m_attn_k_gain'], 'm_attn_sink': out['m_attn_sink'], 'm_w_ret': out['m_w_ret'], 'm_w_pool': out['m_w_pool'], 'm_w_att': out['m_w_att'], 'm_w_out': out['m_w_out'], 'v_norm_g': out['v_norm_g'], 'v_w_in': out['v_w_in'], 'v_ret_decay_fwd': out['v_ret_decay_fwd'], 'v_ret_decay_bwd': out['v_ret_decay_bwd'], 'v_pool_w': out['v_pool_w'], 'v_pool_scale': out['v_pool_scale'], 'v_attn_q_gain': out['v_attn_q_gain'], 'v_attn_k_gain': out['v_attn_k_gain'], 'v_attn_sink': out['v_attn_sink'], 'v_w_ret': out['v_w_ret'], 'v_w_pool': out['v_w_pool'], 'v_w_att': out['v_w_att'], 'v_w_out': out['v_w_out']}


def _loss(weights, diff, rest, loss_target):
    with _jax.named_scope("forward"):
        args = {**rest, TWIN_DIFF_INPUT: diff, **{k: w.astype(_WEIGHT_DTYPES[k]) for k, w in weights.items()}}
        y = _forward(args)
    with _jax.named_scope("loss_head"):
        err = _jnp.square(y.astype(_jnp.float32) - loss_target)
        return 0.5 * _jnp.sum(_jnp.mean(err, axis=-1)) if err.ndim else 0.5 * err


def _adamw(w, g, m, v):
    m = ADAM_B1 * m + (1.0 - ADAM_B1) * g
    v = ADAM_B2 * v + (1.0 - ADAM_B2) * _jnp.square(g)
    m_hat = m / (1.0 - ADAM_B1 ** ADAM_STEP)
    v_hat = v / (1.0 - ADAM_B2 ** ADAM_STEP)
    delta = -ADAM_LR * (m_hat / (_jnp.sqrt(v_hat) + ADAM_EPS) + ADAM_WD * w)
    return delta, m, v


def reference(x, norm_g, w_in, ret_decay_fwd, ret_decay_bwd, pool_w, pool_scale, attn_q_gain, attn_k_gain, attn_sink, w_ret, w_pool, w_att, w_out, loss_target, m_norm_g, m_w_in, m_ret_decay_fwd, m_ret_decay_bwd, m_pool_w, m_pool_scale, m_attn_q_gain, m_attn_k_gain, m_attn_sink, m_w_ret, m_w_pool, m_w_att, m_w_out, v_norm_g, v_w_in, v_ret_decay_fwd, v_ret_decay_bwd, v_pool_w, v_pool_scale, v_attn_q_gain, v_attn_k_gain, v_attn_sink, v_w_ret, v_w_pool, v_w_att, v_w_out):
    given = dict(x=x, norm_g=norm_g, w_in=w_in, ret_decay_fwd=ret_decay_fwd, ret_decay_bwd=ret_decay_bwd, pool_w=pool_w, pool_scale=pool_scale, attn_q_gain=attn_q_gain, attn_k_gain=attn_k_gain, attn_sink=attn_sink, w_ret=w_ret, w_pool=w_pool, w_att=w_att, w_out=w_out, loss_target=loss_target, m_norm_g=m_norm_g, m_w_in=m_w_in, m_ret_decay_fwd=m_ret_decay_fwd, m_ret_decay_bwd=m_ret_decay_bwd, m_pool_w=m_pool_w, m_pool_scale=m_pool_scale, m_attn_q_gain=m_attn_q_gain, m_attn_k_gain=m_attn_k_gain, m_attn_sink=m_attn_sink, m_w_ret=m_w_ret, m_w_pool=m_w_pool, m_w_att=m_w_att, m_w_out=m_w_out, v_norm_g=v_norm_g, v_w_in=v_w_in, v_ret_decay_fwd=v_ret_decay_fwd, v_ret_decay_bwd=v_ret_decay_bwd, v_pool_w=v_pool_w, v_pool_scale=v_pool_scale, v_attn_q_gain=v_attn_q_gain, v_attn_k_gain=v_attn_k_gain, v_attn_sink=v_attn_sink, v_w_ret=v_w_ret, v_w_pool=v_w_pool, v_w_att=v_w_att, v_w_out=v_w_out)
    weights = {n: given[n] for n in TWIN_WEIGHTS}
    shared = {n: given[n] for n in SHARED_INPUTS}
    per_example = {n: given[n] for n in ['x']}
    grad_fn = _jax.value_and_grad(_loss, argnums=(0, 1))

    def one_microbatch(ex, loss_target):
        ex = dict(ex)
        diff = ex.pop(TWIN_DIFF_INPUT)
        return grad_fn(weights, diff, {**shared, **ex}, loss_target)

    if N_MICROBATCH == 1:
        loss, (grad_w, grad_x) = one_microbatch(per_example, given["loss_target"])
    else:
        def body(carry, xs):
            loss_sum, grad_sum = carry
            l_k, (gw_k, gx_k) = one_microbatch(xs[0], xs[1])
            with _jax.named_scope("update"):
                return (loss_sum + l_k, _jax.tree.map(_jnp.add, grad_sum, gw_k)), gx_k

        init = (_jnp.zeros((), _jnp.float32), _jax.tree.map(_jnp.zeros_like, weights))
        (loss, grad_w), grad_x = _jax.lax.scan(body, init, (per_example, given["loss_target"]))
    with _jax.named_scope("update"):
        delta_w, new_m, new_v = {}, {}, {}
        for n in TWIN_WEIGHTS:
            delta_w[n], new_m[n], new_v[n] = _adamw(weights[n], grad_w[n], given["m_" + n], given["v_" + n])
    return (loss, grad_x, *[grad_w[n] for n in TWIN_WEIGHTS], *[delta_w[n] for n in TWIN_WEIGHTS],
            *[new_m[n] for n in TWIN_WEIGHTS], *[new_v[n] for n in TWIN_WEIGHTS])
```

```python
import functools

import jax
import jax.numpy as jnp
from jax import lax
from jax.experimental import pallas as pl
from jax.experimental.pallas import tpu as pltpu

F32 = jnp.float32
BF16 = jnp.bfloat16
MESH = pl.DeviceIdType.MESH

RMS_EPS = 1e-6
NEG_BIG = -1e30
CHUNK = 128
RET_HEADS = 4
POOL_GROUPS = 4
ATT_HEAD_DIM = 128
ATT_GROUP = 4
ROPE_DIMS = 32
RET_ROPE_BASE = 10000.0
ROPE_THETA = 500000.0
N_BRANCHES = 3
N_CHIPS = 4

ADAM_LR = 0.001
ADAM_B1 = 0.9
ADAM_B2 = 0.999
ADAM_EPS = 1e-08
ADAM_WD = 0.01
ADAM_STEP = 10

VMEM_LIMIT_V7X = 56 * 1024 * 1024

TN = (((0,), (0,)), ((), ()))
NT = (((1,), (1,)), ((), ()))


def _params(*sem):
    return pltpu.CompilerParams(dimension_semantics=tuple(sem), vmem_limit_bytes=VMEM_LIMIT_V7X)


def _sigmoid(x):
    return 1.0 / (1.0 + jnp.exp(-x))


def _silu(x):
    return x * _sigmoid(x)


def _dsilu(x):
    s = _sigmoid(x)
    return s * (1.0 + x * (1.0 - s))


def _dot(a, b):
    return jnp.dot(a, b, preferred_element_type=F32)


def _dot_tn(a, b):
    return lax.dot_general(a, b, TN, preferred_element_type=F32)


def _dot_nt(a, b):
    return lax.dot_general(a, b, NT, preferred_element_type=F32)


def _bf(x):
    return x.astype(BF16)


def _f32(x):
    return x.astype(F32)


def _sum11(x):
    return jnp.sum(jnp.sum(x, axis=1, keepdims=True), axis=0, keepdims=True)


class _Dims:
    def __init__(self, seq, d_model, depth):
        self.S, self.D, self.L = seq, d_model, depth
        bw = d_model // 2
        self.BW = bw
        self.RD = bw // RET_HEADS
        self.PD = bw // POOL_GROUPS
        self.AH = bw // ATT_HEAD_DIM
        self.AKV = self.AH // ATT_GROUP
        self.AKW = self.AKV * ATT_HEAD_DIM
        self.o_rq, self.o_rk, self.o_rv, self.o_rg = 0, bw, 2 * bw, 3 * bw
        self.o_pv, self.o_pg = 4 * bw, 5 * bw
        self.o_aq = 6 * bw
        self.o_ak = 7 * bw
        self.o_av = 7 * bw + self.AKW
        self.o_ag = 7 * bw + 2 * self.AKW
        self.o_mg = 8 * bw + 2 * self.AKW
        self.INW = self.o_mg + N_BRANCHES * d_model
        self.nC = seq // CHUNK
        self.tm = min(512, seq)


def _inproj(x, norm_g, w_in, l, dm):
    S, D, N = dm.S, dm.D, dm.INW
    tm, tn = min(1024, S), 512

    def body(x_ref, g_ref, w_ref, z_ref, h_ref):
        @pl.when(pl.program_id(1) == 0)
        def _():
            xv = x_ref[...]
            r = lax.rsqrt(jnp.mean(xv * xv, axis=-1, keepdims=True) + RMS_EPS)
            h_ref[...] = _bf(xv * r * g_ref[...])

        z_ref[...] = _bf(_dot(h_ref[...], w_ref[...]))

    return pl.pallas_call(
        body, name="inproj", grid=(S // tm, N // tn),
        in_specs=[pl.BlockSpec((tm, D), lambda i, j: (i, 0)),
                  pl.BlockSpec((None, 1, D), lambda i, j: (l, 0, 0)),
                  pl.BlockSpec((None, D, tn), lambda i, j: (l, 0, j))],
        out_specs=[pl.BlockSpec((tm, tn), lambda i, j: (i, j)),
                   pl.BlockSpec((tm, D), lambda i, j: (i, 0))],
        out_shape=[jax.ShapeDtypeStruct((S, N), BF16), jax.ShapeDtypeStruct((S, D), BF16)],
        compiler_params=_params("parallel", "arbitrary"),
    )(x, norm_g.reshape(dm.L, 1, D), w_in)


def _rot_half(x, cs, sn):
    h = cs.shape[-1]
    x1, x2 = x[:, :h], x[:, h:]
    return jnp.concatenate([x1 * cs - x2 * sn, x2 * cs + x1 * sn], axis=-1)


def _ret_tables(af_ref, ab_ref):
    C = CHUNK
    lgf = -jnp.exp(af_ref[...])[:, :1]
    lgb = -jnp.exp(ab_ref[...])[:, :1]
    ri = lax.broadcasted_iota(jnp.int32, (C, C), 0)
    ci = lax.broadcasted_iota(jnp.int32, (C, C), 1)
    lag = _f32(ri - ci)
    alag = jnp.abs(lag)
    low = lag >= 0
    dmask = jnp.where(low, jnp.exp(lgf * alag), jnp.exp(lgb * alag))
    j = _f32(lax.broadcasted_iota(jnp.int32, (C, 1), 0))
    t = dict(lgf=lgf, lgb=lgb, alag=alag, low=low, dmask=dmask, j=j,
             w_f=jnp.exp(lgf * (C - 1.0 - j)), w_b=jnp.exp(lgb * j),
             q_f=jnp.exp(lgf * (j + 1.0)), q_b=jnp.exp(lgb * (C - j)),
             dec_f=jnp.exp(lgf * C), dec_b=jnp.exp(lgb * C))
    return t


def _decay_rows(a, dm):
    return jnp.broadcast_to(a.reshape(dm.L * RET_HEADS, 1, 1), (dm.L * RET_HEADS, 1, 128))


def _ret_fwd(z, a_f, a_b, cosr, sinr, l, dm):
    S, RD, C, nC = dm.S, dm.RD, CHUNK, dm.nC
    scale = RD ** -0.5
    cb = lambda off: (lambda h: (0, off // RD + h))

    def body(q_ref, k_ref, v_ref, g_ref, af_ref, ab_ref, cos_ref, sin_ref, o_ref, qh_s, kh_s, sf_s, st_s):
        t = _ret_tables(af_ref, ab_ref)

        st_s[...] = jnp.zeros_like(st_s)

        def fwd_pass(n, carry):
            r = pl.ds(pl.multiple_of(n * C, C), C)
            cs, sn = cos_ref[r, :], sin_ref[r, :]
            qh = _rot_half(_f32(q_ref[r, :]), cs, sn)
            kh = _rot_half(_f32(k_ref[r, :]), cs, sn) * scale
            qh_s[r, :] = _bf(qh)
            kh_s[r, :] = _bf(kh)
            st = st_s[...]
            sf_s[n] = _bf(st)
            st_s[...] = st * t["dec_f"] + _dot_tn(_bf(kh * t["w_f"]), v_ref[r, :])
            return carry

        lax.fori_loop(0, nC, fwd_pass, 0)
        st_s[...] = jnp.zeros_like(st_s)

        def bwd_pass(i, carry):
            n = nC - 1 - i
            r = pl.ds(pl.multiple_of(n * C, C), C)
            qhb, khb, vb = qh_s[r, :], kh_s[r, :], v_ref[r, :]
            p = _bf(_dot_nt(qhb, khb) * t["dmask"])
            qhf = _f32(qhb)
            sb = st_s[...]
            out = (_dot(p, vb) + _dot(_bf(qhf * t["q_f"]), sf_s[n]) + _dot(_bf(qhf * t["q_b"]), _bf(sb)))
            y = out * lax.rsqrt(jnp.mean(out * out, axis=-1, keepdims=True) + RMS_EPS)
            o_ref[r, :] = _bf(y * _silu(_f32(g_ref[r, :])))
            st_s[...] = sb * t["dec_b"] + _dot_tn(_bf(_f32(khb) * t["w_b"]), vb)
            return carry

        lax.fori_loop(0, nC, bwd_pass, 0)

    zs = lambda off: pl.BlockSpec((S, RD), cb(off))
    dec = pl.BlockSpec((None, 1, 128), lambda h: (l * RET_HEADS + h, 0, 0))
    tab = pl.BlockSpec((S, RD // 2), lambda h: (0, 0))
    return pl.pallas_call(
        body, name="ret_fwd", grid=(RET_HEADS,),
        in_specs=[zs(dm.o_rq), zs(dm.o_rk), zs(dm.o_rv), zs(dm.o_rg), dec, dec, tab, tab],
        out_specs=pl.BlockSpec((S, RD), lambda h: (0, h)),
        out_shape=jax.ShapeDtypeStruct((S, dm.BW), BF16),
        scratch_shapes=[pltpu.VMEM((S, RD), BF16), pltpu.VMEM((S, RD), BF16),
                        pltpu.VMEM((nC, RD, RD), BF16), pltpu.VMEM((RD, RD), F32)],
        compiler_params=_params("arbitrary"),
    )(z, z, z, z, _decay_rows(a_f, dm), _decay_rows(a_b, dm), cosr, sinr)


def _band(first_row, first_col, lo, hi, shape):
    r = lax.broadcasted_iota(jnp.int32, shape, 0) + first_row
    c = lax.broadcasted_iota(jnp.int32, shape, 1) + first_col
    d = c - r
    return jnp.where((d >= lo) & (d <= hi), 1.0, 0.0).astype(BF16)


def _pool_counts(first_row, half, S):
    pos = lax.broadcasted_iota(jnp.int32, (CHUNK, 1), 0) + first_row
    lo = jnp.clip(pos - half, 0, S)
    hi = jnp.clip(pos + half, 0, S)
    return 1.0 / _f32(hi - lo)


def _win_start(n, S):
    return pl.multiple_of(jnp.clip((n - 1) * CHUNK, 0, S - 3 * CHUNK), CHUNK)


def _pool_fwd(z, pool_w, pool_scale, l, dm):
    S, PD, C, nC = dm.S, dm.PD, CHUNK, dm.nC

    def body(u_ref, g_ref, w_ref, sc_ref, o_ref):
        half = jnp.left_shift(1, pl.program_id(0))
        w = w_ref[...]
        sc = sc_ref[...]

        def blk(n, carry):
            row0 = pl.multiple_of(n * C, C)
            r = pl.ds(row0, C)
            st = _win_start(n, S)
            band = _band(row0, st, -half, half - 1, (C, 3 * C))
            mean = _dot(band, u_ref[pl.ds(st, 3 * C), :]) * _pool_counts(row0, half, S)
            p = mean - _f32(u_ref[r, :])
            y = _dot(_bf(p), w) * sc
            o_ref[r, :] = _bf(y * _silu(_f32(g_ref[r, :])))
            return carry

        lax.fori_loop(0, nC, blk, 0)

    return pl.pallas_call(
        body, name="pool_fwd", grid=(POOL_GROUPS,),
        in_specs=[pl.BlockSpec((S, PD), lambda g: (0, dm.o_pv // PD + g)),
                  pl.BlockSpec((S, PD), lambda g: (0, dm.o_pg // PD + g)),
                  pl.BlockSpec((None, PD, PD), lambda g: (l * POOL_GROUPS + g, 0, 0)),
                  pl.BlockSpec((None, 1, PD), lambda g: (l * POOL_GROUPS + g, 0, 0))],
        out_specs=pl.BlockSpec((S, PD), lambda g: (0, g)),
        out_shape=jax.ShapeDtypeStruct((S, dm.BW), BF16),
        compiler_params=_params("arbitrary"),
    )(z, z, pool_w, pool_scale.reshape(dm.L * POOL_GROUPS, 1, PD))


def _rot_part(x, cs, s_up, s_dn):
    h = ROPE_DIMS // 2
    return x * cs + pltpu.roll(x, ATT_HEAD_DIM - h, 1) * s_up + pltpu.roll(x, h, 1) * s_dn


def _att_tables(S):
    h = ROPE_DIMS // 2
    inv = ROPE_THETA ** (-jnp.arange(h, dtype=F32) / h)
    ang = jnp.arange(S, dtype=F32)[:, None] * inv[None, :]
    cos, sin = jnp.cos(ang), jnp.sin(ang)
    pad = jnp.zeros((S, ATT_HEAD_DIM - 2 * h), F32)
    zero = jnp.zeros((S, h), F32)
    cs = jnp.concatenate([cos, cos, pad + 1.0], axis=1)
    s_up = jnp.concatenate([-sin, zero, pad], axis=1)
    s_dn = jnp.concatenate([zero, sin, pad], axis=1)
    return cs, s_up, s_dn


def _ret_rope_tables(S, RD):
    h = RD // 2
    inv = 1.0 / (RET_ROPE_BASE ** jnp.linspace(0.0, 1.0, h, dtype=F32))
    ang = jnp.arange(S, dtype=F32)[:, None] * inv[None, :]
    return jnp.cos(ang), jnp.sin(ang)


def _att_norm_rot(x, gain, cs, s_up, s_dn):
    u = x * lax.rsqrt(jnp.mean(x * x, axis=-1, keepdims=True) + RMS_EPS)
    return _rot_part(u * gain, cs, s_up, s_dn), u


def _att_probs(q4, kw, sink_col, row0, st):
    C = CHUNK
    s = _dot_nt(q4, kw) * (ATT_HEAD_DIM ** -0.5)
    qpos = lax.broadcasted_iota(jnp.int32, (ATT_GROUP * C, 3 * C), 0) % C + row0
    kpos = lax.broadcasted_iota(jnp.int32, (ATT_GROUP * C, 3 * C), 1) + st
    s = jnp.where(jnp.abs(kpos - qpos) <= CHUNK, s, NEG_BIG)
    m = jnp.maximum(jnp.max(s, axis=-1, keepdims=True), sink_col)
    e = jnp.exp(s - m)
    es = jnp.exp(sink_col - m)
    inv = 1.0 / (jnp.sum(e, axis=-1, keepdims=True) + es)
    return e * inv, es * inv


def _sink_col(sink_ref):
    head = lax.broadcasted_iota(jnp.int32, (ATT_GROUP * CHUNK, 1), 0) // CHUNK
    col = jnp.zeros((ATT_GROUP * CHUNK, 1), F32)
    for g in range(ATT_GROUP):
        col = jnp.where(head == g, sink_ref[g:g + 1, :1], col)
    return col


def _sink_rows(sink, dm):
    return jnp.broadcast_to(sink.reshape(dm.L * dm.AKV, ATT_GROUP, 1), (dm.L * dm.AKV, ATT_GROUP, 128))


def _att_fwd(z, q_gain, k_gain, sink, tabs, l, dm):
    S, C, nC, HD, G = dm.S, CHUNK, dm.nC, ATT_HEAD_DIM, ATT_GROUP
    GW = G * HD

    def body(q_ref, k_ref, v_ref, g_ref, qg_ref, kg_ref, sk_ref, cs_ref, su_ref, sd_ref, o_ref, qn_s, kn_s):
        def prep(n, carry):
            r = pl.ds(pl.multiple_of(n * C, C), C)
            cs, su, sd = cs_ref[r, :], su_ref[r, :], sd_ref[r, :]
            kn_s[r, :] = _bf(_att_norm_rot(_f32(k_ref[r, :]), kg_ref[...], cs, su, sd)[0])
            for g in range(G):
                qn_s[g, r, :] = _bf(_att_norm_rot(_f32(q_ref[r, g * HD:(g + 1) * HD]), qg_ref[...], cs, su, sd)[0])
            return carry

        lax.fori_loop(0, nC, prep, 0)
        sink_col = _sink_col(sk_ref)

        def blk(n, carry):
            row0 = pl.multiple_of(n * C, C)
            r = pl.ds(row0, C)
            st = _win_start(n, S)
            w = pl.ds(st, 3 * C)
            q4 = qn_s[:, r, :].reshape(G * C, HD)
            p, _ = _att_probs(q4, kn_s[w, :], sink_col, row0, st)
            o = _dot(_bf(p), v_ref[w, :])
            for g in range(G):
                gate = _silu(_f32(g_ref[r, g * HD:(g + 1) * HD]))
                o_ref[r, g * HD:(g + 1) * HD] = _bf(o[g * C:(g + 1) * C, :] * gate)
            return carry

        lax.fori_loop(0, nC, blk, 0)

    tab = pl.BlockSpec((S, HD), lambda h: (0, 0), pipeline_mode=pl.Buffered(1))
    gain = pl.BlockSpec((None, 1, HD), lambda h: (l, 0, 0))
    return pl.pallas_call(
        body, name="att_fwd", grid=(dm.AKV,),
        in_specs=[pl.BlockSpec((S, GW), lambda h: (0, dm.o_aq // GW + h)),
                  pl.BlockSpec((S, HD), lambda h: (0, dm.o_ak // HD + h)),
                  pl.BlockSpec((S, HD), lambda h: (0, dm.o_av // HD + h)),
                  pl.BlockSpec((S, GW), lambda h: (0, dm.o_ag // GW + h)),
                  gain, gain,
                  pl.BlockSpec((None, G, 128), lambda h: (l * dm.AKV + h, 0, 0)),
                  tab, tab, tab],
        out_specs=pl.BlockSpec((S, GW), lambda h: (0, h)),
        out_shape=jax.ShapeDtypeStruct((S, dm.BW), BF16),
        scratch_shapes=[pltpu.VMEM((G, S, HD), BF16), pltpu.VMEM((S, HD), BF16)],
        compiler_params=_params("arbitrary"),
    )(z, z, z, z, q_gain.reshape(dm.L, 1, HD), k_gain.reshape(dm.L, 1, HD), _sink_rows(sink, dm), *tabs)


def _merge_out(x, z, ya_pre, yb_pre, yc_pre, w_ret, w_pool, w_att, w_out, l, dm):
    S, D, BW = dm.S, dm.D, dm.BW
    tm = min(256, S)

    def body(x_ref, mg_ref, a_ref, b_ref, c_ref, wr_ref, wp_ref, wa_ref, wo_ref, xo_ref, ya_ref, yb_ref, yc_ref, m_ref):
        ya = _dot(a_ref[...], wr_ref[...])
        yb = _dot(b_ref[...], wp_ref[...])
        yc = _dot(c_ref[...], wa_ref[...])
        ya_ref[...], yb_ref[...], yc_ref[...] = _bf(ya), _bf(yb), _bf(yc)
        g0 = _sigmoid(_f32(mg_ref[:, 0:D]))
        g1 = _sigmoid(_f32(mg_ref[:, D:2 * D]))
        g2 = _sigmoid(_f32(mg_ref[:, 2 * D:3 * D]))
        merged = _bf(g0 * ya + g1 * yb + g2 * yc)
        m_ref[...] = merged
        xo_ref[...] = x_ref[...] + _dot(merged, wo_ref[...])

    act = lambda w: pl.BlockSpec((tm, w), lambda i: (i, 0))
    wsp = lambda r: pl.BlockSpec((None, r, D), lambda i: (l, 0, 0), pipeline_mode=pl.Buffered(1))
    mg = z[:, dm.o_mg:]
    return pl.pallas_call(
        body, name="merge_out", grid=(S // tm,),
        in_specs=[act(D), act(N_BRANCHES * D), act(BW), act(BW), act(BW), wsp(BW), wsp(BW), wsp(BW), wsp(D)],
        out_specs=[act(D)] * 5,
        out_shape=[jax.ShapeDtypeStruct((S, D), F32)] + [jax.ShapeDtypeStruct((S, D), BF16)] * 4,
        compiler_params=_params("parallel"),
    )(x, mg, ya_pre, yb_pre, yc_pre, w_ret, w_pool, w_att, w_out)


def _loss_grad(y, target, dm):
    S, D = dm.S, dm.D
    tm = dm.tm

    def body(y_ref, t_ref, s_ref, d_ref):
        @pl.when(pl.program_id(0) == 0)
        def _():
            s_ref[...] = jnp.zeros_like(s_ref)

        e = y_ref[...] - t_ref[...]
        d_ref[...] = e * (1.0 / D)
        s_ref[...] += jnp.sum(jnp.sum(e * e, axis=-1, keepdims=True), axis=0, keepdims=True)

    row = pl.BlockSpec((tm, D), lambda i: (i, 0))
    return pl.pallas_call(
        body, name="loss_grad", grid=(S // tm,), in_specs=[row, row],
        out_specs=[pl.BlockSpec((1, 128), lambda i: (0, 0)), row],
        out_shape=[jax.ShapeDtypeStruct((1, 128), F32), jax.ShapeDtypeStruct((S, D), F32)],
        compiler_params=_params("arbitrary"),
    )(y, target)


def _merge_bwd(dx, z, ya, yb, yc, w_ret, w_pool, w_att, w_out, l, dm):
    S, D, BW = dm.S, dm.D, dm.BW
    tm = min(128, S)

    def body(dx_ref, mg_ref, ya_ref, yb_ref, yc_ref, wr_ref, wp_ref, wa_ref, wo_ref,
             dmg_ref, dya_ref, dyb_ref, dyc_ref, da_ref, db_ref, dc_ref, dxb_ref):
        dxb = _bf(dx_ref[...])
        dxb_ref[...] = dxb
        dm_ = _dot_nt(dxb, wo_ref[...])
        for k, (y_ref, dy_ref, w_ref, dp_ref) in enumerate(((ya_ref, dya_ref, wr_ref, da_ref),
                                                            (yb_ref, dyb_ref, wp_ref, db_ref),
                                                            (yc_ref, dyc_ref, wa_ref, dc_ref))):
            g = _sigmoid(_f32(mg_ref[:, k * D:(k + 1) * D]))
            dmg_ref[:, k * D:(k + 1) * D] = _bf(dm_ * _f32(y_ref[...]) * g * (1.0 - g))
            dy = _bf(dm_ * g)
            dy_ref[...] = dy
            dp_ref[...] = _bf(_dot_nt(dy, w_ref[...]))

    act = lambda w: pl.BlockSpec((tm, w), lambda i: (i, 0))
    wsp = lambda r: pl.BlockSpec((None, r, D), lambda i: (l, 0, 0), pipeline_mode=pl.Buffered(1))
    mg = z[:, dm.o_mg:]
    return pl.pallas_call(
        body, name="merge_bwd", grid=(S // tm,),
        in_specs=[act(D), act(N_BRANCHES * D), act(D), act(D), act(D), wsp(BW), wsp(BW), wsp(BW), wsp(D)],
        out_specs=[act(N_BRANCHES * D), act(D), act(D), act(D), act(BW), act(BW), act(BW), act(D)],
        out_shape=[jax.ShapeDtypeStruct((S, N_BRANCHES * D), BF16)] + [jax.ShapeDtypeStruct((S, D), BF16)] * 3
        + [jax.ShapeDtypeStruct((S, BW), BF16)] * 3 + [jax.ShapeDtypeStruct((S, D), BF16)],
        compiler_params=_params("parallel"),
    )(dx, mg, ya, yb, yc, w_ret, w_pool, w_att, w_out)


def _ret_bwd(z, dya_pre, a_f, a_b, cosr, sinr, l, dm):
    S, RD, C, nC = dm.S, dm.RD, CHUNK, dm.nC
    scale = RD ** -0.5
    H = RET_HEADS

    def body(q_ref, k_ref, v_ref, g_ref, dy_ref, af_ref, ab_ref, cos_ref, sin_ref,
             dq_ref, dk_ref, dv_ref, dg_ref, dd_ref,
             qh_s, kh_s, sf_s, sb_s, do_s, dqh_s, dkh_s, dv_s, st_s, lam_s):
        t = _ret_tables(af_ref, ab_ref)
        zero_rr = jnp.zeros((RD, RD), F32)

        st_s[...] = zero_rr

        def pass0(n, carry):
            r = pl.ds(pl.multiple_of(n * C, C), C)
            cs, sn = cos_ref[r, :], sin_ref[r, :]
            qh = _rot_half(_f32(q_ref[r, :]), cs, sn)
            kh = _rot_half(_f32(k_ref[r, :]), cs, sn) * scale
            qh_s[r, :] = _bf(qh)
            kh_s[r, :] = _bf(kh)
            st = st_s[...]
            sf_s[n] = _bf(st)
            st_s[...] = st * t["dec_f"] + _dot_tn(_bf(kh * t["w_f"]), v_ref[r, :])
            return carry

        lax.fori_loop(0, nC, pass0, 0)

        st_s[...] = zero_rr
        lam_s[...] = zero_rr

        def pass1(i, acc):
            glf, glb = acc
            n = nC - 1 - i
            r = pl.ds(pl.multiple_of(n * C, C), C)
            qhb, khb, vb = qh_s[r, :], kh_s[r, :], v_ref[r, :]
            qhf, khf = _f32(qhb), _f32(khb)
            sc_ = _dot_nt(qhb, khb)
            p = _bf(sc_ * t["dmask"])
            sb = st_s[...]
            sbb = _bf(sb)
            sb_s[n] = sbb
            sfb = sf_s[n]
            qf_b, qb_b = _bf(qhf * t["q_f"]), _bf(qhf * t["q_b"])
            out = _dot(p, vb) + _dot(qf_b, sfb) + _dot(qb_b, sbb)
            rr = lax.rsqrt(jnp.mean(out * out, axis=-1, keepdims=True) + RMS_EPS)
            y = out * rr
            g = _f32(g_ref[r, :])
            dya = _f32(dy_ref[r, :])
            dg_ref[r, :] = _bf(dya * y * _dsilu(g))
            dyn = dya * _silu(g)
            dout = rr * (dyn - y * jnp.mean(dyn * y, axis=-1, keepdims=True))
            dob = _bf(dout)
            do_s[r, :] = dob
            dp = _dot_nt(dob, vb)
            dv = _dot_tn(p, dob)
            ds = _bf(dp * t["dmask"])
            dqh = _dot(ds, khb)
            dkh = _dot_tn(ds, qhb)
            dd = dp * sc_ * t["dmask"] * t["alag"]
            glf = glf + _sum11(jnp.where(t["low"], dd, 0.0))
            glb = glb + _sum11(jnp.where(t["low"], 0.0, dd))
            tf = _dot_nt(dob, sfb)
            tb = _dot_nt(dob, sbb)
            dqh = dqh + tf * t["q_f"] + tb * t["q_b"]
            glf = glf + _sum11(jnp.sum(tf * qhf, axis=-1, keepdims=True) * t["q_f"] * (t["j"] + 1.0))
            glb = glb + _sum11(jnp.sum(tb * qhf, axis=-1, keepdims=True) * t["q_b"] * (C - t["j"]))
            lam = lam_s[...]
            lamb = _bf(lam)
            glf = glf + _sum11(lam * _f32(sfb)) * t["dec_f"] * C
            u = _dot_nt(vb, lamb)
            dkh = dkh + u * t["w_f"]
            glf = glf + _sum11(jnp.sum(u * khf, axis=-1, keepdims=True) * t["w_f"] * (C - 1.0 - t["j"]))
            dv = dv + _dot(_bf(khf * t["w_f"]), lamb)
            lam_s[...] = _dot_tn(qf_b, dob) + lam * t["dec_f"]
            dqh_s[r, :] = dqh
            dkh_s[r, :] = dkh
            dv_s[r, :] = dv
            st_s[...] = sb * t["dec_b"] + _dot_tn(_bf(khf * t["w_b"]), vb)
            return glf, glb

        glf, glb = lax.fori_loop(0, nC, pass1, (jnp.zeros((1, 1), F32), jnp.zeros((1, 1), F32)))

        lam_s[...] = zero_rr

        def pass2(n, glb):
            r = pl.ds(pl.multiple_of(n * C, C), C)
            qhb, khb, vb, dob = qh_s[r, :], kh_s[r, :], v_ref[r, :], do_s[r, :]
            qhf, khf = _f32(qhb), _f32(khb)
            lam = lam_s[...]
            lamb = _bf(lam)
            glb = glb + _sum11(lam * _f32(sb_s[n])) * t["dec_b"] * C
            u = _dot_nt(vb, lamb)
            dkh = dkh_s[r, :] + u * t["w_b"]
            glb = glb + _sum11(jnp.sum(u * khf, axis=-1, keepdims=True) * t["w_b"] * t["j"])
            dv = dv_s[r, :] + _dot(_bf(khf * t["w_b"]), lamb)
            lam_s[...] = _dot_tn(_bf(qhf * t["q_b"]), dob) + lam * t["dec_b"]
            cs, sn = cos_ref[r, :], sin_ref[r, :]
            dq_ref[r, :] = _bf(_rot_half(dqh_s[r, :], cs, -sn))
            dk_ref[r, :] = _bf(_rot_half(dkh * scale, cs, -sn))
            dv_ref[r, :] = _bf(dv)
            return glb

        glb = lax.fori_loop(0, nC, pass2, glb)
        row = lax.broadcasted_iota(jnp.int32, (8, 128), 0)
        da_f = glf * t["lgf"]
        da_b = glb * t["lgb"]
        dd_ref[...] = jnp.where(row == 0, da_f, jnp.where(row == 1, da_b, 0.0))

    one = pl.Buffered(1)
    zs = lambda off: pl.BlockSpec((S, RD), lambda h: (0, off // RD + h), pipeline_mode=one)
    dec = pl.BlockSpec((None, 1, 128), lambda h: (l * H + h, 0, 0))
    tab = pl.BlockSpec((S, RD // 2), lambda h: (0, 0), pipeline_mode=one)
    osp = lambda k: pl.BlockSpec((S, RD), lambda h: (0, k * H + h), pipeline_mode=one)
    return pl.pallas_call(
        body, name="ret_bwd", grid=(H,),
        in_specs=[zs(dm.o_rq), zs(dm.o_rk), zs(dm.o_rv), zs(dm.o_rg),
                  pl.BlockSpec((S, RD), lambda h: (0, h), pipeline_mode=one), dec, dec, tab, tab],
        out_specs=[pl.BlockSpec((S, RD), lambda h: (0, h), pipeline_mode=one)] * 4
        + [pl.BlockSpec((None, 8, 128), lambda h: (h, 0, 0))],
        out_shape=[jax.ShapeDtypeStruct((S, dm.BW), BF16)] * 4 + [jax.ShapeDtypeStruct((H, 8, 128), F32)],
        scratch_shapes=[pltpu.VMEM((S, RD), BF16), pltpu.VMEM((S, RD), BF16),
                        pltpu.VMEM((nC, RD, RD), BF16), pltpu.VMEM((nC, RD, RD), BF16),
                        pltpu.VMEM((S, RD), BF16),
                        pltpu.VMEM((S, RD), F32), pltpu.VMEM((S, RD), F32), pltpu.VMEM((S, RD), F32),
                        pltpu.VMEM((RD, RD), F32), pltpu.VMEM((RD, RD), F32)],
        compiler_params=_params("arbitrary"),
    )(z, z, z, z, dya_pre, _decay_rows(a_f, dm), _decay_rows(a_b, dm), cosr, sinr)


def _pool_bwd(z, dyb_pre, pool_w, pool_scale, l, dm):
    S, PD, C, nC = dm.S, dm.PD, CHUNK, dm.nC

    def body(u_ref, g_ref, dy_ref, w_ref, sc_ref, du_ref, dg_ref, dw_ref, dsc_ref, dp_s, dpc_s, dw_s):
        half = jnp.left_shift(1, pl.program_id(0))
        w = w_ref[...]
        sc = sc_ref[...]
        dw_s[...] = jnp.zeros_like(dw_s)
        dsc_ref[...] = jnp.zeros_like(dsc_ref)

        def blk1(n, carry):
            row0 = pl.multiple_of(n * C, C)
            r = pl.ds(row0, C)
            st = _win_start(n, S)
            band = _band(row0, st, -half, half - 1, (C, 3 * C))
            inv = _pool_counts(row0, half, S)
            pb = _bf(_dot(band, u_ref[pl.ds(st, 3 * C), :]) * inv - _f32(u_ref[r, :]))
            ylin = _dot(pb, w)
            y = ylin * sc
            g = _f32(g_ref[r, :])
            dyb = _f32(dy_ref[r, :])
            dg_ref[r, :] = _bf(dyb * y * _dsilu(g))
            dy = dyb * _silu(g)
            dsc_ref[...] += jnp.sum(dy * ylin, axis=0, keepdims=True)
            dyl = _bf(dy * sc)
            dw_s[...] += _dot_tn(pb, dyl)
            dp = _dot_nt(dyl, w)
            dp_s[r, :] = dp
            dpc_s[r, :] = _bf(dp * inv)
            return carry

        lax.fori_loop(0, nC, blk1, 0)
        dw_ref[...] = _bf(dw_s[...])

        def blk2(n, carry):
            row0 = pl.multiple_of(n * C, C)
            r = pl.ds(row0, C)
            st = _win_start(n, S)
            band_t = _band(row0, st, -half + 1, half, (C, 3 * C))
            du_ref[r, :] = _bf(_dot(band_t, dpc_s[pl.ds(st, 3 * C), :]) - dp_s[r, :])
            return carry

        lax.fori_loop(0, nC, blk2, 0)

    G = POOL_GROUPS
    return pl.pallas_call(
        body, name="pool_bwd", grid=(G,),
        in_specs=[pl.BlockSpec((S, PD), lambda g: (0, dm.o_pv // PD + g)),
                  pl.BlockSpec((S, PD), lambda g: (0, dm.o_pg // PD + g)),
                  pl.BlockSpec((S, PD), lambda g: (0, g)),
                  pl.BlockSpec((None, PD, PD), lambda g: (l * G + g, 0, 0)),
                  pl.BlockSpec((None, 1, PD), lambda g: (l * G + g, 0, 0))],
        out_specs=[pl.BlockSpec((S, PD), lambda g: (0, g)), pl.BlockSpec((S, PD), lambda g: (0, g)),
                   pl.BlockSpec((None, PD, PD), lambda g: (g, 0, 0)), pl.BlockSpec((None, 1, PD), lambda g: (g, 0, 0))],
        out_shape=[jax.ShapeDtypeStruct((S, dm.BW), BF16), jax.ShapeDtypeStruct((S, dm.BW), BF16),
                   jax.ShapeDtypeStruct((G, PD, PD), BF16), jax.ShapeDtypeStruct((G, 1, PD), F32)],
        scratch_shapes=[pltpu.VMEM((S, PD), F32), pltpu.VMEM((S, PD), BF16), pltpu.VMEM((PD, PD), F32)],
        compiler_params=_params("arbitrary"),
    )(z, z, dyb_pre, pool_w, pool_scale.reshape(dm.L * G, 1, PD))


def _att_bwd(z, dyc_pre, q_gain, k_gain, sink, tabs, l, dm):
    S, C, nC, HD, G = dm.S, CHUNK, dm.nC, ATT_HEAD_DIM, ATT_GROUP
    GW = G * HD
    scale = HD ** -0.5

    def body(q_ref, k_ref, v_ref, g_ref, dy_ref, qg_ref, kg_ref, sk_ref, cs_ref, su_ref, sd_ref,
             dq_ref, dk_ref, dv_ref, dg_ref, sm_ref, qn_s, kn_s, dqn_s, dkn_s, dv_s):
        def prep(n, carry):
            r = pl.ds(pl.multiple_of(n * C, C), C)
            cs, su, sd = cs_ref[r, :], su_ref[r, :], sd_ref[r, :]
            kn_s[r, :] = _bf(_att_norm_rot(_f32(k_ref[r, :]), kg_ref[...], cs, su, sd)[0])
            for g in range(G):
                qn_s[g, r, :] = _bf(_att_norm_rot(_f32(q_ref[r, g * HD:(g + 1) * HD]), qg_ref[...], cs, su, sd)[0])
            return carry

        lax.fori_loop(0, nC, prep, 0)
        dkn_s[...] = jnp.zeros_like(dkn_s)
        dv_s[...] = jnp.zeros_like(dv_s)
        sink_col = _sink_col(sk_ref)

        def blk(n, dsink):
            row0 = pl.multiple_of(n * C, C)
            r = pl.ds(row0, C)
            st = _win_start(n, S)
            w = pl.ds(st, 3 * C)
            q4 = qn_s[:, r, :].reshape(G * C, HD)
            kw, vw = kn_s[w, :], v_ref[w, :]
            p, ps = _att_probs(q4, kw, sink_col, row0, st)
            pb = _bf(p)
            o = _dot(pb, vw)
            do_parts = []
            for g in range(G):
                cols = slice(g * HD, (g + 1) * HD)
                gate = _f32(g_ref[r, cols])
                dy = _f32(dy_ref[r, cols])
                dg_ref[r, cols] = _bf(dy * o[g * C:(g + 1) * C, :] * _dsilu(gate))
                do_parts.append(dy * _silu(gate))
            dob = _bf(jnp.concatenate(do_parts, axis=0))
            dp = _dot_nt(dob, vw)
            drow = jnp.sum(p * dp, axis=-1, keepdims=True)
            ds = _bf(p * (dp - drow))
            dsink = dsink - ps * drow
            dqn_s[:, r, :] = (_dot(ds, kw) * scale).reshape(G, C, HD)
            dkn_s[w, :] += _dot_tn(ds, q4) * scale
            dv_s[w, :] += _dot_tn(pb, dob)
            return dsink

        dsink = lax.fori_loop(0, nC, blk, jnp.zeros((G * C, 1), F32))

        def fin(n, acc):
            dqg, dkg = acc
            r = pl.ds(pl.multiple_of(n * C, C), C)
            cs, su, sd = cs_ref[r, :], su_ref[r, :], sd_ref[r, :]

            def norm_bwd(x, gain, dqn):
                rr = lax.rsqrt(jnp.mean(x * x, axis=-1, keepdims=True) + RMS_EPS)
                u = x * rr
                dw = _rot_part(dqn, cs, -su, -sd)
                du = dw * gain
                return rr * (du - u * jnp.mean(du * u, axis=-1, keepdims=True)), jnp.sum(dw * u, axis=0, keepdims=True)

            dk, gk = norm_bwd(_f32(k_ref[r, :]), kg_ref[...], dkn_s[r, :])
            dk_ref[r, :] = _bf(dk)
            dv_ref[r, :] = _bf(dv_s[r, :])
            dkg = dkg + gk
            for g in range(G):
                cols = slice(g * HD, (g + 1) * HD)
                dq, gq = norm_bwd(_f32(q_ref[r, cols]), qg_ref[...], dqn_s[g, r, :])
                dq_ref[r, cols] = _bf(dq)
                dqg = dqg + gq
            return dqg, dkg

        dqg, dkg = lax.fori_loop(0, nC, fin, (jnp.zeros((1, HD), F32), jnp.zeros((1, HD), F32)))
        sm_ref[...] = jnp.zeros_like(sm_ref)
        sm_ref[0:1, :] = dqg
        sm_ref[1:2, :] = dkg
        for g in range(G):
            sm_ref[2 + g:3 + g, :] = jnp.broadcast_to(_sum11(dsink[g * C:(g + 1) * C, :]), (1, HD))

    one = pl.Buffered(1)
    tab = pl.BlockSpec((S, HD), lambda h: (0, 0), pipeline_mode=one)
    gain = pl.BlockSpec((None, 1, HD), lambda h: (l, 0, 0))
    wide = lambda off: pl.BlockSpec((S, GW), lambda h: (0, off // GW + h), pipeline_mode=one)
    thin = lambda off: pl.BlockSpec((S, HD), lambda h: (0, off // HD + h), pipeline_mode=one)
    return pl.pallas_call(
        body, name="att_bwd", grid=(dm.AKV,),
        in_specs=[wide(dm.o_aq), thin(dm.o_ak), thin(dm.o_av), wide(dm.o_ag), wide(0), gain, gain,
                  pl.BlockSpec((None, G, 128), lambda h: (l * dm.AKV + h, 0, 0)), tab, tab, tab],
        out_specs=[wide(0), thin(0), thin(0), wide(0), pl.BlockSpec((None, 8, 128), lambda h: (h, 0, 0))],
        out_shape=[jax.ShapeDtypeStruct((S, dm.BW), BF16), jax.ShapeDtypeStruct((S, dm.AKW), BF16),
                   jax.ShapeDtypeStruct((S, dm.AKW), BF16), jax.ShapeDtypeStruct((S, dm.BW), BF16),
                   jax.ShapeDtypeStruct((dm.AKV, 8, 128), F32)],
        scratch_shapes=[pltpu.VMEM((G, S, HD), BF16), pltpu.VMEM((S, HD), BF16),
                        pltpu.VMEM((G, S, HD), F32), pltpu.VMEM((S, HD), F32), pltpu.VMEM((S, HD), F32)],
        compiler_params=_params("arbitrary"),
    )(z, z, z, z, dyc_pre, q_gain.reshape(dm.L, 1, HD), k_gain.reshape(dm.L, 1, HD), _sink_rows(sink, dm), *tabs)


def _grad_matmul(a, b, name):
    S, M = a.shape
    N = b.shape[1]
    tm, tn = min(1024, M), min(512, N)

    def body(a_ref, b_ref, o_ref):
        o_ref[...] = _bf(_dot_tn(a_ref[...], b_ref[...]))

    return pl.pallas_call(
        body, name=name, grid=(M // tm, N // tn),
        in_specs=[pl.BlockSpec((S, tm), lambda i, j: (0, i)), pl.BlockSpec((S, tn), lambda i, j: (0, j))],
        out_specs=pl.BlockSpec((tm, tn), lambda i, j: (i, j)),
        out_shape=jax.ShapeDtypeStruct((M, N), BF16),
        compiler_params=_params("parallel", "parallel"),
    )(a, b)


def _inproj_bwd(dz_parts, w_in, x, norm_g, dx_out, l, dm):
    S, D = dm.S, dm.D
    tm, tk = dm.tm, 512
    steps = []
    for pi, (arr, off) in enumerate(dz_parts):
        for kb in range(arr.shape[1] // tk):
            steps.append((pi, kb, off // tk + kb))
    nk = len(steps)
    n_parts = len(dz_parts)
    part_of = jnp.asarray([s[0] for s in steps], jnp.int32)
    blk_in = jnp.asarray([s[1] for s in steps], jnp.int32)
    blk_w = jnp.asarray([s[2] for s in steps], jnp.int32)

    def body(part_ref, blk_ref, wblk_ref, *refs):
        dz_refs = refs[:n_parts]
        w_ref, x_ref, g_ref, dxo_ref, dx_ref, dg_ref, acc = refs[n_parts:]
        k = pl.program_id(1)

        @pl.when((pl.program_id(0) == 0) & (k == 0))
        def _():
            dg_ref[...] = jnp.zeros_like(dg_ref)

        @pl.when(k == 0)
        def _():
            acc[...] = jnp.zeros_like(acc)

        for pi in range(n_parts):
            @pl.when(part_ref[k] == pi)
            def _(pi=pi):
                acc[...] += _dot_nt(dz_refs[pi][...], w_ref[...])

        @pl.when(k == nk - 1)
        def _():
            xv = x_ref[...]
            rr = lax.rsqrt(jnp.mean(xv * xv, axis=-1, keepdims=True) + RMS_EPS)
            u = xv * rr
            dh = acc[...]
            dg_ref[...] += jnp.sum(dh * u, axis=0, keepdims=True)
            du = dh * g_ref[...]
            dx_ref[...] = dxo_ref[...] + rr * (du - u * jnp.mean(du * u, axis=-1, keepdims=True))

    def dz_spec(pi):
        def imap(i, k, part_ref, blk_ref, wblk_ref):
            return (i, jnp.where(part_ref[k] == pi, blk_ref[k], 0))
        return pl.BlockSpec((tm, tk), imap)

    row = pl.BlockSpec((tm, D), lambda i, k, *_: (i, 0))
    grid_spec = pltpu.PrefetchScalarGridSpec(
        num_scalar_prefetch=3, grid=(S // tm, nk),
        in_specs=[dz_spec(pi) for pi in range(n_parts)]
        + [pl.BlockSpec((None, D, tk), lambda i, k, p, b, wb: (l, 0, wb[k])), row,
           pl.BlockSpec((None, 1, D), lambda i, k, *_: (l, 0, 0)), row],
        out_specs=[row, pl.BlockSpec((1, D), lambda i, k, *_: (0, 0))],
        scratch_shapes=[pltpu.VMEM((tm, D), F32)])
    return pl.pallas_call(
        body, name="inproj_bwd", grid_spec=grid_spec,
        out_shape=[jax.ShapeDtypeStruct((S, D), F32), jax.ShapeDtypeStruct((1, D), F32)],
        compiler_params=_params("arbitrary", "arbitrary"),
    )(part_of, blk_in, blk_w, *[a for a, _ in dz_parts], w_in, x, norm_g.reshape(dm.L, 1, D), dx_out)


def _place():
    x, y, c = lax.axis_index("x"), lax.axis_index("y"), lax.axis_index("c")
    return x, y, c


def _chip_peers(x, y):
    return [(1 - x, y), (x, 1 - y), (1 - x, 1 - y)]


class _Sharded:
    def __init__(self, kind, a, r, cc):
        self.kind, self.A, self.R, self.Cc = kind, a, r, cc
        if kind == "col":
            self.shard = (a, r, cc // N_CHIPS)
            self.half_rows = r // 2
        else:
            self.shard = (a, r // N_CHIPS, cc)
            self.half_rows = r // N_CHIPS // 2
        self.half = (a, self.half_rows, self.shard[2])

    def in_full(self, ref, chip, core=None):
        hr = self.half_rows
        if self.kind == "col":
            rows = pl.ds(0, self.R) if core is None else pl.ds(core * hr, hr)
            return ref.at[:, rows, pl.ds(chip * self.shard[2], self.shard[2])]
        rows = pl.ds(chip * self.shard[1], self.shard[1]) if core is None else pl.ds(chip * self.shard[1] + core * hr, hr)
        return ref.at[:, rows, :]

    def in_shard(self, ref, core):
        return ref.at[:, pl.ds(core * self.half_rows, self.half_rows), :]


HBM_ANY = pl.BlockSpec(memory_space=pl.ANY)


def _all_gather_weights(shards, specs):
    n = len(shards)

    def body(*refs):
        src, dst = refs[:n], refs[n:2 * n]
        ici_send, ici_recv, d2d_send, d2d_recv, loc_sem = refs[2 * n:]
        x, y, c = _place()
        me = 2 * x + y
        sibling = (x, y, 1 - c)
        peers = _chip_peers(x, y)
        local = [pltpu.make_async_copy(src[t], specs[t].in_full(dst[t], me), loc_sem.at[t]) for t in range(n)]
        for cp in local:
            cp.start()
        sends = []
        for t in range(n):
            for p, (px, py) in enumerate(peers):
                cp = pltpu.make_async_remote_copy(
                    src_ref=specs[t].in_shard(src[t], c), dst_ref=specs[t].in_full(dst[t], me, c),
                    send_sem=ici_send.at[t * 3 + p], recv_sem=ici_recv.at[t * 3 + p],
                    device_id=(px, py, c), device_id_type=MESH)
                cp.start()
                sends.append(cp)
        fwds = []
        for t in range(n):
            for p, (px, py) in enumerate(peers):
                part = specs[t].in_full(dst[t], 2 * px + py, c)
                pltpu.make_async_remote_copy(
                    src_ref=part, dst_ref=part, send_sem=ici_send.at[t * 3 + p], recv_sem=ici_recv.at[t * 3 + p],
                    device_id=(px, py, c), device_id_type=MESH).wait_recv()
                cp = pltpu.make_async_remote_copy(
                    src_ref=part, dst_ref=part, send_sem=d2d_send.at[t * 3 + p], recv_sem=d2d_recv.at[t * 3 + p],
                    device_id=sibling, device_id_type=MESH)
                cp.start()
                fwds.append(cp)
        for t in range(n):
            for p, (px, py) in enumerate(peers):
                part = specs[t].in_full(dst[t], 2 * px + py, 1 - c)
                pltpu.make_async_remote_copy(
                    src_ref=part, dst_ref=part, send_sem=d2d_send.at[t * 3 + p], recv_sem=d2d_recv.at[t * 3 + p],
                    device_id=sibling, device_id_type=MESH).wait_recv()
        for cp in sends + fwds:
            cp.wait_send()
        for cp in local:
            cp.wait()

    return pl.pallas_call(
        body, name="all_gather_weights",
        in_specs=[HBM_ANY] * n, out_specs=[HBM_ANY] * n,
        out_shape=[jax.ShapeDtypeStruct((s.A, s.R, s.Cc), BF16) for s in specs],
        scratch_shapes=[pltpu.SemaphoreType.DMA((3 * n,)), pltpu.SemaphoreType.DMA((3 * n,)),
                        pltpu.SemaphoreType.DMA((3 * n,)), pltpu.SemaphoreType.DMA((3 * n,)),
                        pltpu.SemaphoreType.DMA((n,))],
    )(*shards)


def _pair_exchange(grads, specs):
    n = len(grads)

    def body(*refs):
        src, land = refs[:n], refs[n:2 * n]
        send_sem, recv_sem = refs[2 * n:]
        x, y, c = _place()
        sibling = (x, y, 1 - c)
        cps = []
        for t in range(n):
            for j in range(N_CHIPS):
                cp = pltpu.make_async_remote_copy(
                    src_ref=specs[t].in_full(src[t], j, 1 - c), dst_ref=land[t].at[j],
                    send_sem=send_sem.at[t * N_CHIPS + j], recv_sem=recv_sem.at[t * N_CHIPS + j],
                    device_id=sibling, device_id_type=MESH)
                cp.start()
                cps.append(cp)
        for cp in cps:
            cp.wait()

    return pl.pallas_call(
        body, name="grad_pair_exchange",
        in_specs=[HBM_ANY] * n, out_specs=[HBM_ANY] * n,
        out_shape=[jax.ShapeDtypeStruct((N_CHIPS,) + s.half, BF16) for s in specs],
        scratch_shapes=[pltpu.SemaphoreType.DMA((N_CHIPS * n,)), pltpu.SemaphoreType.DMA((N_CHIPS * n,))],
    )(*grads)


def _row_block(rows, width, itemsize):
    target = max(16, (2 * 1024 * 1024) // (width * itemsize))
    best = None
    for rb in range(min(rows, target), 0, -1):
        if rows % rb == 0 and (rb % 16 == 0 or rb == rows):
            best = rb
            break
    return best if best is not None else rows


def _pair_sum(grad, land, spec, core, name):
    A, hr, w = spec.half
    rb = _row_block(hr, w, 2)
    nrb = hr // rb
    if spec.kind == "col":
        g_spec = pl.BlockSpec((None, rb, w), lambda j, a, r, core: (a, core[0] * nrb + r, j))
    else:
        g_spec = pl.BlockSpec((None, rb, w), lambda j, a, r, core: (a, (j * 2 + core[0]) * nrb + r, 0))

    def body(core_ref, g_ref, l_ref, o_ref):
        o_ref[...] = _bf(_f32(g_ref[...]) + _f32(l_ref[...]))

    blk = pl.BlockSpec((None, None, rb, w), lambda j, a, r, core: (j, a, r, 0))
    grid_spec = pltpu.PrefetchScalarGridSpec(num_scalar_prefetch=1, grid=(N_CHIPS, A, nrb),
                                             in_specs=[g_spec, blk], out_specs=blk)
    return pl.pallas_call(
        body, name=name, grid_spec=grid_spec,
        out_shape=jax.ShapeDtypeStruct((N_CHIPS,) + spec.half, BF16),
        compiler_params=_params("parallel", "parallel", "parallel"),
    )(core, grad, land)


def _chip_exchange(pair_sums, specs):
    n = len(pair_sums)

    def body(*refs):
        src, land = refs[:n], refs[n:2 * n]
        send_sem, recv_sem = refs[2 * n:]
        x, y, c = _place()
        cps = []
        for t in range(n):
            for p, (px, py) in enumerate(_chip_peers(x, y)):
                cp = pltpu.make_async_remote_copy(
                    src_ref=src[t].at[2 * px + py], dst_ref=land[t].at[p],
                    send_sem=send_sem.at[t * 3 + p], recv_sem=recv_sem.at[t * 3 + p],
                    device_id=(px, py, c), device_id_type=MESH)
                cp.start()
                cps.append(cp)
        for cp in cps:
            cp.wait()

    return pl.pallas_call(
        body, name="grad_chip_exchange",
        in_specs=[HBM_ANY] * n, out_specs=[HBM_ANY] * n,
        out_shape=[jax.ShapeDtypeStruct((3,) + s.half, BF16) for s in specs],
        scratch_shapes=[pltpu.SemaphoreType.DMA((3 * n,)), pltpu.SemaphoreType.DMA((3 * n,))],
    )(*pair_sums)


def _chip_sum(pair_sum, land, spec, chip, name):
    A, hr, w = spec.half
    rb = _row_block(hr, w, 4)

    def body(chip_ref, p_ref, l0_ref, l1_ref, l2_ref, o_ref):
        o_ref[...] = ((_f32(p_ref[...]) + _f32(l0_ref[...])) + _f32(l1_ref[...])) + _f32(l2_ref[...])

    own = pl.BlockSpec((None, None, rb, w), lambda a, r, chip: (chip[0], a, r, 0))
    slot = lambda p: pl.BlockSpec((None, None, rb, w), lambda a, r, chip: (p, a, r, 0))
    grid_spec = pltpu.PrefetchScalarGridSpec(
        num_scalar_prefetch=1, grid=(A, hr // rb), in_specs=[own, slot(0), slot(1), slot(2)],
        out_specs=pl.BlockSpec((None, rb, w), lambda a, r, chip: (a, r, 0)))
    return pl.pallas_call(
        body, name=name, grid_spec=grid_spec, out_shape=jax.ShapeDtypeStruct(spec.half, F32),
        compiler_params=_params("parallel", "parallel"),
    )(chip, pair_sum, land, land, land)


def _half_exchange(halves, specs):
    n = len(halves)

    def body(*refs):
        src, dst = refs[:n], refs[n:2 * n]
        send_sem, recv_sem, loc_sem = refs[2 * n:]
        x, y, c = _place()
        sibling = (x, y, 1 - c)
        cps = []
        for t in range(n):
            mine = specs[t].in_shard(dst[t], c)
            loc = pltpu.make_async_copy(src[t], mine, loc_sem.at[t])
            loc.start()
            rem = pltpu.make_async_remote_copy(src_ref=src[t], dst_ref=mine, send_sem=send_sem.at[t],
                                               recv_sem=recv_sem.at[t], device_id=sibling, device_id_type=MESH)
            rem.start()
            cps += [loc, rem]
        for cp in cps:
            cp.wait()

    return pl.pallas_call(
        body, name="grad_half_exchange",
        in_specs=[HBM_ANY] * n, out_specs=[HBM_ANY] * n,
        out_shape=[jax.ShapeDtypeStruct(s.shard, F32) for s in specs],
        scratch_shapes=[pltpu.SemaphoreType.DMA((n,)), pltpu.SemaphoreType.DMA((n,)), pltpu.SemaphoreType.DMA((n,))],
    )(*halves)


def _adamw_math(w, g, m, v):
    m = ADAM_B1 * m + (1.0 - ADAM_B1) * g
    v = ADAM_B2 * v + (1.0 - ADAM_B2) * (g * g)
    m_hat = m / (1.0 - ADAM_B1 ** ADAM_STEP)
    v_hat = v / (1.0 - ADAM_B2 ** ADAM_STEP)
    delta = -ADAM_LR * (m_hat / (jnp.sqrt(v_hat) + ADAM_EPS) + ADAM_WD * w)
    return delta, m, v


def _adamw(w, g, m, v, name):
    A, R, C = w.shape
    rb = _row_block(R, C, 4 * 4)

    def body(w_ref, g_ref, m_ref, v_ref, d_ref, mo_ref, vo_ref):
        d_ref[...], mo_ref[...], vo_ref[...] = _adamw_math(w_ref[...], g_ref[...], m_ref[...], v_ref[...])

    blk = pl.BlockSpec((None, rb, C), lambda a, r: (a, r, 0))
    return pl.pallas_call(
        body, name=name, grid=(A, R // rb), in_specs=[blk] * 4, out_specs=[blk] * 3,
        out_shape=[jax.ShapeDtypeStruct((A, R, C), F32)] * 3,
        compiler_params=_params("parallel", "parallel"),
    )(w, g, m, v)


def _small_update(g_part, w, m, v):
    R = g_part.shape[0]
    n_dev = 8

    def body(g_ref, w_ref, m_ref, v_ref, go_ref, d_ref, mo_ref, vo_ref, all_s, send_sem, recv_sem):
        x, y, c = _place()
        me = 4 * x + 2 * y + c
        all_s[me] = g_ref[...]
        cps = []
        for k in range(1, n_dev):
            peer = (x ^ ((k >> 2) & 1), y ^ ((k >> 1) & 1), c ^ (k & 1))
            cp = pltpu.make_async_remote_copy(src_ref=g_ref, dst_ref=all_s.at[me], send_sem=send_sem.at[k],
                                              recv_sem=recv_sem.at[k], device_id=peer, device_id_type=MESH)
            cp.start()
            cps.append(cp)
        for cp in cps:
            cp.wait()
        g = all_s[0]
        for d in range(1, n_dev):
            g = g + all_s[d]
        go_ref[...] = g
        d_ref[...], mo_ref[...], vo_ref[...] = _adamw_math(w_ref[...], g, m_ref[...], v_ref[...])

    vm = pl.BlockSpec(memory_space=pltpu.VMEM)
    return pl.pallas_call(
        body, name="small_update", in_specs=[vm] * 4, out_specs=[vm] * 4,
        out_shape=[jax.ShapeDtypeStruct((R, 128), F32)] * 4,
        scratch_shapes=[pltpu.VMEM((n_dev, R, 128), F32), pltpu.SemaphoreType.DMA((n_dev,)),
                        pltpu.SemaphoreType.DMA((n_dev,))],
        compiler_params=pltpu.CompilerParams(vmem_limit_bytes=VMEM_LIMIT_V7X),
    )(g_part, w, m, v)


def _pack_small(parts):
    flat = jnp.concatenate([p.reshape(-1) for p in parts])
    pad = (-flat.shape[0]) % 1024
    return jnp.pad(flat, (0, pad)).reshape(-1, 128)


def _unpack_small(packed, like):
    flat = packed.reshape(-1)
    out, at = [], 0
    for p in like:
        out.append(flat[at:at + p.size].reshape(p.shape))
        at += p.size
    return out


def _local_step(x, target, norm_g, a_f, a_b, pool_scale, q_gain, k_gain, sink, w_in, w_ret, w_pool, w_att, w_out,
                pool_w, dm):
    L, S, D, BW = dm.L, dm.S, dm.D, dm.BW
    cosr, sinr = _ret_rope_tables(S, dm.RD)
    tabs = _att_tables(S)
    saved = []
    for l in range(L):
        z, h = _inproj(x, norm_g, w_in, l, dm)
        ya_pre = _ret_fwd(z, a_f, a_b, cosr, sinr, l, dm)
        yb_pre = _pool_fwd(z, pool_w, pool_scale, l, dm)
        yc_pre = _att_fwd(z, q_gain, k_gain, sink, tabs, l, dm)
        x_next, ya, yb, yc, merged = _merge_out(x, z, ya_pre, yb_pre, yc_pre, w_ret, w_pool, w_att, w_out, l, dm)
        saved.append((x, z, h, ya_pre, yb_pre, yc_pre, ya, yb, yc, merged))
        x = x_next
    sq, dx = _loss_grad(x, target, dm)

    g_in, g_ret, g_pool, g_att, g_out, g_pw = [], [], [], [], [], []
    g_norm, g_af, g_ab, g_ps, g_qg, g_kg, g_sink = [], [], [], [], [], [], []
    for l in reversed(range(L)):
        x_l, z, h, ya_pre, yb_pre, yc_pre, ya, yb, yc, merged = saved[l]
        dmg, dya, dyb, dyc, dya_pre, dyb_pre, dyc_pre, dxb = _merge_bwd(dx, z, ya, yb, yc, w_ret, w_pool, w_att,
                                                                        w_out, l, dm)
        g_out.append(_grad_matmul(merged, dxb, "grad_w_out"))
        g_ret.append(_grad_matmul(ya_pre, dya, "grad_w_ret"))
        g_pool.append(_grad_matmul(yb_pre, dyb, "grad_w_pool"))
        g_att.append(_grad_matmul(yc_pre, dyc, "grad_w_att"))
        drq, drk, drv, drg, ddec = _ret_bwd(z, dya_pre, a_f, a_b, cosr, sinr, l, dm)
        dpv, dpg, dpw, dps = _pool_bwd(z, dyb_pre, pool_w, pool_scale, l, dm)
        daq, dak, dav, dag, dsm = _att_bwd(z, dyc_pre, q_gain, k_gain, sink, tabs, l, dm)
        parts = [(drq, dm.o_rq), (drk, dm.o_rk), (drv, dm.o_rv), (drg, dm.o_rg), (dpv, dm.o_pv), (dpg, dm.o_pg),
                 (daq, dm.o_aq), (dak, dm.o_ak), (dav, dm.o_av), (dag, dm.o_ag), (dmg, dm.o_mg)]
        dz = jnp.concatenate([p for p, _ in parts], axis=1)
        g_in.append(_grad_matmul(h, dz, "grad_w_in"))
        dx, dng = _inproj_bwd([(dz, 0)], w_in, x_l, norm_g, dx, l, dm)
        g_pw.append(dpw)
        g_norm.append(dng.reshape(D))
        g_af.append(ddec[:, 0, 0])
        g_ab.append(ddec[:, 1, 0])
        g_ps.append(dps.reshape(BW))
        g_qg.append(jnp.sum(dsm[:, 0, :], axis=0))
        g_kg.append(jnp.sum(dsm[:, 1, :], axis=0))
        g_sink.append(dsm[:, 2:2 + ATT_GROUP, 0].reshape(dm.AH))
    rev = lambda xs: jnp.stack(xs[::-1])
    big = [rev(g_in), rev(g_ret), rev(g_pool), rev(g_att), rev(g_out), rev(g_pw).reshape(L * POOL_GROUPS, dm.PD, dm.PD)]
    small = [rev(g_norm), rev(g_af), rev(g_ab), rev(g_ps), rev(g_qg), rev(g_kg), rev(g_sink)]
    return sq, dx, big, small


def kernel(x, norm_g, w_in, ret_decay_fwd, ret_decay_bwd, pool_w, pool_scale, attn_q_gain, attn_k_gain, attn_sink, w_ret, w_pool, w_att, w_out, loss_target, m_norm_g, m_w_in, m_ret_decay_fwd, m_ret_decay_bwd, m_pool_w, m_pool_scale, m_attn_q_gain, m_attn_k_gain, m_attn_sink, m_w_ret, m_w_pool, m_w_att, m_w_out, v_norm_g, v_w_in, v_ret_decay_fwd, v_ret_decay_bwd, v_pool_w, v_pool_scale, v_attn_q_gain, v_attn_k_gain, v_attn_sink, v_w_ret, v_w_pool, v_w_att, v_w_out):
    S, D = x.shape[1], x.shape[2]
    L = norm_g.shape[0]
    dm = _Dims(S, D, L)
    PD = dm.PD
    xi, yi, ci = _place()
    chip = (2 * xi + yi).astype(jnp.int32).reshape(1)
    core = ci.astype(jnp.int32).reshape(1)

    specs = [_Sharded("col", L, D, dm.INW), _Sharded("col", L, dm.BW, D), _Sharded("col", L, dm.BW, D),
             _Sharded("col", L, dm.BW, D), _Sharded("row", L, D, D), _Sharded("row", L * POOL_GROUPS, PD, PD)]
    as3 = lambda a, s: a.reshape(s.shard)
    big_w = [w_in, w_ret, w_pool, w_att, w_out, pool_w]
    big_m = [m_w_in, m_w_ret, m_w_pool, m_w_att, m_w_out, m_pool_w]
    big_v = [v_w_in, v_w_ret, v_w_pool, v_w_att, v_w_out, v_pool_w]
    big_w3 = [as3(a, s) for a, s in zip(big_w, specs)]
    full = _all_gather_weights([_bf(a) for a in big_w3], specs)
    f_in, f_ret, f_pool, f_att, f_out, f_pw = full

    sq, grad_x, big_g, small_g = _local_step(
        x[0], loss_target[0], norm_g, ret_decay_fwd, ret_decay_bwd, pool_scale, attn_q_gain, attn_k_gain, attn_sink,
        f_in, f_ret, f_pool, f_att, f_out, f_pw, dm)
    loss = lax.psum(sq[0, 0] * (0.5 / D), ("x", "y", "c"))

    lands = _pair_exchange(big_g, specs)
    names = ["w_in", "w_ret", "w_pool", "w_att", "w_out", "pool_w"]
    pair = [_pair_sum(g, ld, s, core, "pair_sum_" + nm) for g, ld, s, nm in zip(big_g, lands, specs, names)]
    lands2 = _chip_exchange(pair, specs)
    halves = [_chip_sum(p, ld, s, chip, "chip_sum_" + nm) for p, ld, s, nm in zip(pair, lands2, specs, names)]
    grads3 = _half_exchange(halves, specs)
    upd = [_adamw(w, g, as3(m, s), as3(v, s), "adamw_" + nm)
           for w, g, m, v, s, nm in zip(big_w3, grads3, big_m, big_v, specs, names)]
    back = lambda a, like: a.reshape(like.shape)
    g_big = [back(g, w) for g, w in zip(grads3, big_w)]
    d_big = [back(u[0], w) for u, w in zip(upd, big_w)]
    m_big = [back(u[1], w) for u, w in zip(upd, big_w)]
    v_big = [back(u[2], w) for u, w in zip(upd, big_w)]

    small_w = [norm_g, ret_decay_fwd, ret_decay_bwd, pool_scale, attn_q_gain, attn_k_gain, attn_sink]
    small_m = [m_norm_g, m_ret_decay_fwd, m_ret_decay_bwd, m_pool_scale, m_attn_q_gain, m_attn_k_gain, m_attn_sink]
    small_v = [v_norm_g, v_ret_decay_fwd, v_ret_decay_bwd, v_pool_scale, v_attn_q_gain, v_attn_k_gain, v_attn_sink]
    sg, sd, sm, sv = _small_update(_pack_small(small_g), _pack_small(small_w), _pack_small(small_m), _pack_small(small_v))
    g_sm, d_sm, m_sm, v_sm = (_unpack_small(a, small_w) for a in (sg, sd, sm, sv))

    def ordered(big, small):
        return [small[0], big[0], small[1], small[2], big[5], small[3], small[4], small[5], small[6],
                big[1], big[2], big[3], big[4]]

    return (loss, grad_x[None], *ordered(g_big, g_sm), *ordered(d_big, d_sm), *ordered(m_big, m_sm),
            *ordered(v_big, v_sm))
```

```python
import jax
import jax.numpy as jnp
from jax import lax
from jax.experimental import pallas as pl
from jax.experimental.pallas import tpu as pltpu

F32 = jnp.float32
BF16 = jnp.bfloat16
MESH = pl.DeviceIdType.MESH

RMS_EPS = 1e-6
NEG_BIG = -1e30
CHUNK = 128
RET_HEADS = 4
POOL_GROUPS = 4
ATT_HEAD_DIM = 128
ATT_GROUP = 4
ROPE_DIMS = 32
RET_ROPE_BASE = 10000.0
ROPE_THETA = 500000.0
N_BRANCHES = 3
N_CHIPS = 4

ADAM_LR = 0.001
ADAM_B1 = 0.9
ADAM_B2 = 0.999
ADAM_EPS = 1e-08
ADAM_WD = 0.01
ADAM_STEP = 10

VMEM_LIMIT_V7X = 56 * 1024 * 1024

TN = (((0,), (0,)), ((), ()))
NT = (((1,), (1,)), ((), ()))

HBM_ANY = pl.BlockSpec(memory_space=pl.ANY)
ONE_BUFFER = pl.Buffered(1)


def _sigmoid(x):
    return 1.0 / (1.0 + jnp.exp(-x))


def _silu(x):
    return x * _sigmoid(x)


def _dsilu(x):
    s = _sigmoid(x)
    return s * (1.0 + x * (1.0 - s))


def _dot(a, b):
    return jnp.dot(a, b, preferred_element_type=F32)


def _dot_tn(a, b):
    return lax.dot_general(a, b, TN, preferred_element_type=F32)


def _dot_nt(a, b):
    return lax.dot_general(a, b, NT, preferred_element_type=F32)


def _bf(x):
    return x.astype(BF16)


def _f32(x):
    return x.astype(F32)


def _sum11(x):
    return jnp.sum(jnp.sum(x, axis=1, keepdims=True), axis=0, keepdims=True)


class _Dims:
    def __init__(self, seq, d_model, depth):
        self.S, self.D, self.L = seq, d_model, depth
        bw = d_model // 2
        self.BW = bw
        self.RD = bw // RET_HEADS
        self.PD = bw // POOL_GROUPS
        self.AH = bw // ATT_HEAD_DIM
        self.AKV = self.AH // ATT_GROUP
        self.AKW = self.AKV * ATT_HEAD_DIM
        self.o_rq, self.o_rk, self.o_rv, self.o_rg = 0, bw, 2 * bw, 3 * bw
        self.o_pv, self.o_pg = 4 * bw, 5 * bw
        self.o_aq = 6 * bw
        self.o_ak = 7 * bw
        self.o_av = 7 * bw + self.AKW
        self.o_ag = 7 * bw + 2 * self.AKW
        self.o_mg = 8 * bw + 2 * self.AKW
        self.INW = self.o_mg + N_BRANCHES * d_model
        self.nC = seq // CHUNK
        self.tm = min(512, seq)


def _place():
    return lax.axis_index("x"), lax.axis_index("y"), lax.axis_index("c")


def _chip_peers(x, y):
    return [(1 - x, y), (x, 1 - y), (1 - x, 1 - y)]


class _Sharded:
    def __init__(self, name, kind, a, r, cc):
        self.name, self.kind, self.A, self.R, self.Cc = name, kind, a, r, cc
        self.full = (a, r, cc)
        if kind == "col":
            self.shard = (a, r, cc // N_CHIPS)
            self.half_rows = r // 2
        else:
            self.shard = (a, r // N_CHIPS, cc)
            self.half_rows = r // N_CHIPS // 2
        self.half = (a, self.half_rows, self.shard[2])

    def in_full(self, ref, chip, core=None):
        hr = self.half_rows
        if self.kind == "col":
            rows = pl.ds(0, self.R) if core is None else pl.ds(core * hr, hr)
            return ref.at[:, rows, pl.ds(chip * self.shard[2], self.shard[2])]
        rows = pl.ds(chip * self.shard[1], self.shard[1]) if core is None else pl.ds(chip * self.shard[1] + core * hr, hr)
        return ref.at[:, rows, :]

    def in_shard(self, ref, core):
        return ref.at[:, pl.ds(core * self.half_rows, self.half_rows), :]


class _Task:
    def __init__(self, ro, rw, new, n_sem, copies):
        self.ro, self.rw, self.new, self.n_sem, self.copies = list(ro), list(rw), list(new), n_sem, copies


def _remote(src, dst, send_sem, recv_sem, k, device):
    def make():
        return pltpu.make_async_remote_copy(src_ref=src, dst_ref=dst, send_sem=send_sem.at[k], recv_sem=recv_sem.at[k],
                                            device_id=device, device_id_type=MESH)
    return make


def _gather_ici_task(bufs, specs):
    def copies(ro, rw, new, ss, rs):
        x, y, c = _place()
        me = 2 * x + y
        out = []
        for t, spec in enumerate(specs):
            for p, (px, py) in enumerate(_chip_peers(x, y)):
                mine = spec.in_full(rw[t], me, c)
                theirs = spec.in_full(rw[t], 2 * px + py, c)
                k = t * 3 + p
                out.append((_remote(mine, mine, ss, rs, k, (px, py, c)), _remote(mine, mine, ss, rs, k, (px, py, c)),
                            _remote(theirs, theirs, ss, rs, k, (px, py, c))))
        return out
    return _Task([], bufs, [], 3 * len(specs), copies)


def _gather_d2d_task(bufs, specs):
    def copies(ro, rw, new, ss, rs):
        x, y, c = _place()
        sib = (x, y, 1 - c)
        out = []
        for t, spec in enumerate(specs):
            for p, (px, py) in enumerate(_chip_peers(x, y)):
                got = spec.in_full(rw[t], 2 * px + py, c)
                gets = spec.in_full(rw[t], 2 * px + py, 1 - c)
                k = t * 3 + p
                out.append((_remote(got, got, ss, rs, k, sib), _remote(got, got, ss, rs, k, sib),
                            _remote(gets, gets, ss, rs, k, sib)))
        return out
    return _Task([], bufs, [], 3 * len(specs), copies)


def _pair_task(grads, specs):
    def copies(ro, rw, new, ss, rs):
        x, y, c = _place()
        sib = (x, y, 1 - c)
        out = []
        for t, spec in enumerate(specs):
            for j in range(N_CHIPS):
                k = t * N_CHIPS + j
                cp = _remote(spec.in_full(ro[t], j, 1 - c), new[t].at[j], ss, rs, k, sib)
                out.append((cp, cp, cp))
        return out
    return _Task(grads, [], [jax.ShapeDtypeStruct((N_CHIPS,) + s.half, BF16) for s in specs], N_CHIPS * len(specs), copies)


def _chip_task(pair_sums, specs):
    def copies(ro, rw, new, ss, rs):
        x, y, c = _place()
        out = []
        for t in range(len(specs)):
            for p, (px, py) in enumerate(_chip_peers(x, y)):
                cp = _remote(ro[t].at[2 * px + py], new[t].at[p], ss, rs, t * 3 + p, (px, py, c))
                out.append((cp, cp, cp))
        return out
    return _Task(pair_sums, [], [jax.ShapeDtypeStruct((3,) + s.half, BF16) for s in specs], 3 * len(specs), copies)


def _half_task(shards, specs):
    def copies(ro, rw, new, ss, rs):
        x, y, c = _place()
        sib = (x, y, 1 - c)
        out = []
        for t, spec in enumerate(specs):
            mine, theirs = spec.in_shard(rw[t], c), spec.in_shard(rw[t], 1 - c)
            out.append((_remote(mine, mine, ss, rs, t, sib), _remote(mine, mine, ss, rs, t, sib),
                        _remote(theirs, theirs, ss, rs, t, sib)))
        return out
    return _Task([], shards, [], len(specs), copies)


def _call(body, *, name, grid, in_specs, out_specs, out_shape, args, scratch_shapes=(), tasks=()):
    tasks = [t for t in tasks if t is not None]
    n_in, n_out, n_scr = len(in_specs), len(out_specs), len(scratch_shapes)
    ro = [a for t in tasks for a in t.ro]
    rw = [a for t in tasks for a in t.rw]
    new = [s for t in tasks for s in t.new]
    n_ro, n_rw, n_new = len(ro), len(rw), len(new)

    def wrapped(*refs):
        ins = refs[:n_in]
        ro_refs = refs[n_in:n_in + n_ro]
        at = n_in + n_ro + n_rw
        outs = refs[at:at + n_out]
        rw_refs = refs[at + n_out:at + n_out + n_rw]
        new_refs = refs[at + n_out + n_rw:at + n_out + n_rw + n_new]
        at = at + n_out + n_rw + n_new
        scr = refs[at:at + n_scr]
        sems = refs[at + n_scr:]

        def task_copies():
            found, a, b, d = [], 0, 0, 0
            for i, t in enumerate(tasks):
                found += t.copies(ro_refs[a:a + len(t.ro)], rw_refs[b:b + len(t.rw)], new_refs[d:d + len(t.new)],
                                  sems[2 * i], sems[2 * i + 1])
                a, b, d = a + len(t.ro), b + len(t.rw), d + len(t.new)
            return found

        if tasks:
            first = pl.program_id(0) == 0
            last = pl.program_id(0) == grid[0] - 1
            for ax in range(1, len(grid)):
                first = first & (pl.program_id(ax) == 0)
                last = last & (pl.program_id(ax) == grid[ax] - 1)

            @pl.when(first)
            def _():
                for cp, _, _ in task_copies():
                    cp().start()

        body(*ins, *outs, *scr)

        if tasks:
            @pl.when(last)
            def _():
                found = task_copies()
                for _, _, recv in found:
                    recv().wait_recv()
                for _, send, _ in found:
                    send().wait_send()

    sem_shapes = []
    for t in tasks:
        sem_shapes += [pltpu.SemaphoreType.DMA((t.n_sem,)), pltpu.SemaphoreType.DMA((t.n_sem,))]
    res = pl.pallas_call(
        wrapped, name=name, grid=grid,
        in_specs=list(in_specs) + [HBM_ANY] * (n_ro + n_rw),
        out_specs=list(out_specs) + [HBM_ANY] * (n_rw + n_new),
        out_shape=list(out_shape) + [jax.ShapeDtypeStruct(a.shape, a.dtype) for a in rw] + new,
        scratch_shapes=list(scratch_shapes) + sem_shapes,
        input_output_aliases={n_in + n_ro + i: n_out + i for i in range(n_rw)},
        compiler_params=pltpu.CompilerParams(dimension_semantics=("arbitrary",) * len(grid),
                                             vmem_limit_bytes=VMEM_LIMIT_V7X),
    )(*args, *ro, *rw)
    own, rest = list(res[:n_out]), list(res[n_out:])
    per_task, b, d = [], 0, n_rw
    for t in tasks:
        per_task.append((rest[b:b + len(t.rw)], rest[d:d + len(t.new)]))
        b, d = b + len(t.rw), d + len(t.new)
    return own, per_task


def _comm_only(name, tasks):
    ro = [a for t in tasks for a in t.ro]
    rw = [a for t in tasks for a in t.rw]
    new = [s for t in tasks for s in t.new]
    n_ro, n_rw, n_new = len(ro), len(rw), len(new)

    def body(*refs):
        ro_refs = refs[:n_ro]
        rw_refs = refs[n_ro + n_rw:n_ro + 2 * n_rw]
        new_refs = refs[n_ro + 2 * n_rw:n_ro + 2 * n_rw + n_new]
        sems = refs[n_ro + 2 * n_rw + n_new:]
        found, a, b, d = [], 0, 0, 0
        for i, t in enumerate(tasks):
            found += t.copies(ro_refs[a:a + len(t.ro)], rw_refs[b:b + len(t.rw)], new_refs[d:d + len(t.new)],
                              sems[2 * i], sems[2 * i + 1])
            a, b, d = a + len(t.ro), b + len(t.rw), d + len(t.new)
        for cp, _, _ in found:
            cp().start()
        for _, _, recv in found:
            recv().wait_recv()
        for _, send, _ in found:
            send().wait_send()

    sem_shapes = []
    for t in tasks:
        sem_shapes += [pltpu.SemaphoreType.DMA((t.n_sem,)), pltpu.SemaphoreType.DMA((t.n_sem,))]
    res = pl.pallas_call(
        body, name=name,
        in_specs=[HBM_ANY] * (n_ro + n_rw), out_specs=[HBM_ANY] * (n_rw + n_new),
        out_shape=[jax.ShapeDtypeStruct(a.shape, a.dtype) for a in rw] + new,
        scratch_shapes=sem_shapes,
        input_output_aliases={n_ro + i: i for i in range(n_rw)},
    )(*ro, *rw)
    res = list(res)
    per_task, b, d = [], 0, n_rw
    for t in tasks:
        per_task.append((res[b:b + len(t.rw)], res[d:d + len(t.new)]))
        b, d = b + len(t.rw), d + len(t.new)
    return per_task


def _inproj(x, norm_g, w_in, dm, tasks=()):
    S, D, N = dm.S, dm.D, dm.INW
    tm, tn = min(1024, S), 512

    def body(x_ref, g_ref, w_ref, z_ref, h_ref):
        @pl.when(pl.program_id(1) == 0)
        def _():
            xv = x_ref[...]
            r = lax.rsqrt(jnp.mean(xv * xv, axis=-1, keepdims=True) + RMS_EPS)
            h_ref[...] = _bf(xv * r * g_ref[...])

        z_ref[...] = _bf(_dot(h_ref[...], w_ref[...]))

    return _call(
        body, name="inproj", grid=(S // tm, N // tn),
        in_specs=[pl.BlockSpec((tm, D), lambda i, j: (i, 0)),
                  pl.BlockSpec((1, D), lambda i, j: (0, 0)),
                  pl.BlockSpec((None, D, tn), lambda i, j: (0, 0, j))],
        out_specs=[pl.BlockSpec((tm, tn), lambda i, j: (i, j)),
                   pl.BlockSpec((tm, D), lambda i, j: (i, 0))],
        out_shape=[jax.ShapeDtypeStruct((S, N), BF16), jax.ShapeDtypeStruct((S, D), BF16)],
        args=(x, norm_g, w_in), tasks=tasks)


def _rot_half(x, cs, sn):
    h = cs.shape[-1]
    x1, x2 = x[:, :h], x[:, h:]
    return jnp.concatenate([x1 * cs - x2 * sn, x2 * cs + x1 * sn], axis=-1)


def _ret_tables(af_ref, ab_ref):
    C = CHUNK
    lgf = -jnp.exp(af_ref[...])[:, :1]
    lgb = -jnp.exp(ab_ref[...])[:, :1]
    ri = lax.broadcasted_iota(jnp.int32, (C, C), 0)
    ci = lax.broadcasted_iota(jnp.int32, (C, C), 1)
    lag = _f32(ri - ci)
    alag = jnp.abs(lag)
    low = lag >= 0
    dmask = jnp.where(low, jnp.exp(lgf * alag), jnp.exp(lgb * alag))
    j = _f32(lax.broadcasted_iota(jnp.int32, (C, 1), 0))
    return dict(lgf=lgf, lgb=lgb, alag=alag, low=low, dmask=dmask, j=j,
                w_f=jnp.exp(lgf * (C - 1.0 - j)), w_b=jnp.exp(lgb * j),
                q_f=jnp.exp(lgf * (j + 1.0)), q_b=jnp.exp(lgb * (C - j)),
                dec_f=jnp.exp(lgf * C), dec_b=jnp.exp(lgb * C))


def _decay_rows(a):
    return jnp.broadcast_to(a.reshape(RET_HEADS, 1, 1), (RET_HEADS, 1, 128))


def _ret_fwd(z, a_f, a_b, cosr, sinr, dm):
    S, RD, C, nC = dm.S, dm.RD, CHUNK, dm.nC
    scale = RD ** -0.5

    def body(q_ref, k_ref, v_ref, g_ref, af_ref, ab_ref, cos_ref, sin_ref, o_ref, qh_s, kh_s, sf_s, st_s):
        t = _ret_tables(af_ref, ab_ref)
        st_s[...] = jnp.zeros_like(st_s)

        def fwd_pass(n, carry):
            r = pl.ds(pl.multiple_of(n * C, C), C)
            cs, sn = cos_ref[r, :], sin_ref[r, :]
            qh = _rot_half(_f32(q_ref[r, :]), cs, sn)
            kh = _rot_half(_f32(k_ref[r, :]), cs, sn) * scale
            qh_s[r, :] = _bf(qh)
            kh_s[r, :] = _bf(kh)
            st = st_s[...]
            sf_s[n] = _bf(st)
            st_s[...] = st * t["dec_f"] + _dot_tn(_bf(kh * t["w_f"]), v_ref[r, :])
            return carry

        lax.fori_loop(0, nC, fwd_pass, 0)
        st_s[...] = jnp.zeros_like(st_s)

        def bwd_pass(i, carry):
            n = nC - 1 - i
            r = pl.ds(pl.multiple_of(n * C, C), C)
            qhb, khb, vb = qh_s[r, :], kh_s[r, :], v_ref[r, :]
            p = _bf(_dot_nt(qhb, khb) * t["dmask"])
            qhf = _f32(qhb)
            sb = st_s[...]
            out = (_dot(p, vb) + _dot(_bf(qhf * t["q_f"]), sf_s[n]) + _dot(_bf(qhf * t["q_b"]), _bf(sb)))
            y = out * lax.rsqrt(jnp.mean(out * out, axis=-1, keepdims=True) + RMS_EPS)
            o_ref[r, :] = _bf(y * _silu(_f32(g_ref[r, :])))
            st_s[...] = sb * t["dec_b"] + _dot_tn(_bf(_f32(khb) * t["w_b"]), vb)
            return carry

        lax.fori_loop(0, nC, bwd_pass, 0)

    zs = lambda off: pl.BlockSpec((S, RD), lambda h: (0, off // RD + h))
    dec = pl.BlockSpec((None, 1, 128), lambda h: (h, 0, 0))
    tab = pl.BlockSpec((S, RD // 2), lambda h: (0, 0))
    own, _ = _call(
        body, name="ret_fwd", grid=(RET_HEADS,),
        in_specs=[zs(dm.o_rq), zs(dm.o_rk), zs(dm.o_rv), zs(dm.o_rg), dec, dec, tab, tab],
        out_specs=[pl.BlockSpec((S, RD), lambda h: (0, h))],
        out_shape=[jax.ShapeDtypeStruct((S, dm.BW), BF16)],
        scratch_shapes=[pltpu.VMEM((S, RD), BF16), pltpu.VMEM((S, RD), BF16),
                        pltpu.VMEM((nC, RD, RD), BF16), pltpu.VMEM((RD, RD), F32)],
        args=(z, z, z, z, _decay_rows(a_f), _decay_rows(a_b), cosr, sinr))
    return own[0]


def _band(first_row, first_col, lo, hi, shape):
    r = lax.broadcasted_iota(jnp.int32, shape, 0) + first_row
    c = lax.broadcasted_iota(jnp.int32, shape, 1) + first_col
    d = c - r
    return jnp.where((d >= lo) & (d <= hi), 1.0, 0.0).astype(BF16)


def _pool_counts(first_row, half, S):
    pos = lax.broadcasted_iota(jnp.int32, (CHUNK, 1), 0) + first_row
    lo = jnp.clip(pos - half, 0, S)
    hi = jnp.clip(pos + half, 0, S)
    return 1.0 / _f32(hi - lo)


def _win_start(n, S):
    return pl.multiple_of(jnp.clip((n - 1) * CHUNK, 0, S - 3 * CHUNK), CHUNK)


def _pool_fwd(z, pool_w, pool_scale, dm):
    S, PD, C, nC = dm.S, dm.PD, CHUNK, dm.nC

    def body(u_ref, g_ref, w_ref, sc_ref, o_ref):
        half = jnp.left_shift(1, pl.program_id(0))
        w = w_ref[...]
        sc = sc_ref[...]

        def blk(n, carry):
            row0 = pl.multiple_of(n * C, C)
            r = pl.ds(row0, C)
            st = _win_start(n, S)
            band = _band(row0, st, -half, half - 1, (C, 3 * C))
            mean = _dot(band, u_ref[pl.ds(st, 3 * C), :]) * _pool_counts(row0, half, S)
            p = mean - _f32(u_ref[r, :])
            y = _dot(_bf(p), w) * sc
            o_ref[r, :] = _bf(y * _silu(_f32(g_ref[r, :])))
            return carry

        lax.fori_loop(0, nC, blk, 0)

    own, _ = _call(
        body, name="pool_fwd", grid=(POOL_GROUPS,),
        in_specs=[pl.BlockSpec((S, PD), lambda g: (0, dm.o_pv // PD + g)),
                  pl.BlockSpec((S, PD), lambda g: (0, dm.o_pg // PD + g)),
                  pl.BlockSpec((None, PD, PD), lambda g: (g, 0, 0)),
                  pl.BlockSpec((None, 1, PD), lambda g: (g, 0, 0))],
        out_specs=[pl.BlockSpec((S, PD), lambda g: (0, g))],
        out_shape=[jax.ShapeDtypeStruct((S, dm.BW), BF16)],
        args=(z, z, pool_w, pool_scale))
    return own[0]


def _rot_part(x, cs, s_up, s_dn):
    h = ROPE_DIMS // 2
    return x * cs + pltpu.roll(x, ATT_HEAD_DIM - h, 1) * s_up + pltpu.roll(x, h, 1) * s_dn


def _att_tables(S):
    h = ROPE_DIMS // 2
    inv = ROPE_THETA ** (-jnp.arange(h, dtype=F32) / h)
    ang = jnp.arange(S, dtype=F32)[:, None] * inv[None, :]
    cos, sin = jnp.cos(ang), jnp.sin(ang)
    pad = jnp.zeros((S, ATT_HEAD_DIM - 2 * h), F32)
    zero = jnp.zeros((S, h), F32)
    cs = jnp.concatenate([cos, cos, pad + 1.0], axis=1)
    s_up = jnp.concatenate([-sin, zero, pad], axis=1)
    s_dn = jnp.concatenate([zero, sin, pad], axis=1)
    return cs, s_up, s_dn


def _ret_rope_tables(S, RD):
    h = RD // 2
    inv = 1.0 / (RET_ROPE_BASE ** jnp.linspace(0.0, 1.0, h, dtype=F32))
    ang = jnp.arange(S, dtype=F32)[:, None] * inv[None, :]
    return jnp.cos(ang), jnp.sin(ang)


def _att_norm_rot(x, gain, cs, s_up, s_dn):
    u = x * lax.rsqrt(jnp.mean(x * x, axis=-1, keepdims=True) + RMS_EPS)
    return _rot_part(u * gain, cs, s_up, s_dn)


def _att_probs(q4, kw, sink_col, row0, st):
    C = CHUNK
    s = _dot_nt(q4, kw) * (ATT_HEAD_DIM ** -0.5)
    qpos = lax.broadcasted_iota(jnp.int32, (ATT_GROUP * C, 3 * C), 0) % C + row0
    kpos = lax.broadcasted_iota(jnp.int32, (ATT_GROUP * C, 3 * C), 1) + st
    s = jnp.where(jnp.abs(kpos - qpos) <= CHUNK, s, NEG_BIG)
    m = jnp.maximum(jnp.max(s, axis=-1, keepdims=True), sink_col)
    e = jnp.exp(s - m)
    es = jnp.exp(sink_col - m)
    inv = 1.0 / (jnp.sum(e, axis=-1, keepdims=True) + es)
    return e * inv, es * inv


def _sink_col(sink_ref):
    head = lax.broadcasted_iota(jnp.int32, (ATT_GROUP * CHUNK, 1), 0) // CHUNK
    col = jnp.zeros((ATT_GROUP * CHUNK, 1), F32)
    for g in range(ATT_GROUP):
        col = jnp.where(head == g, sink_ref[g:g + 1, :1], col)
    return col


def _sink_rows(sink, dm):
    return jnp.broadcast_to(sink.reshape(dm.AKV, ATT_GROUP, 1), (dm.AKV, ATT_GROUP, 128))


def _att_prep(q_ref, k_ref, qg_ref, kg_ref, cs_ref, su_ref, sd_ref, qn_s, kn_s, nC):
    C, HD = CHUNK, ATT_HEAD_DIM

    def prep(n, carry):
        r = pl.ds(pl.multiple_of(n * C, C), C)
        cs, su, sd = cs_ref[r, :], su_ref[r, :], sd_ref[r, :]
        kn_s[r, :] = _bf(_att_norm_rot(_f32(k_ref[r, :]), kg_ref[...], cs, su, sd))
        for g in range(ATT_GROUP):
            qn_s[g, r, :] = _bf(_att_norm_rot(_f32(q_ref[r, g * HD:(g + 1) * HD]), qg_ref[...], cs, su, sd))
        return carry

    lax.fori_loop(0, nC, prep, 0)


def _att_fwd(z, q_gain, k_gain, sink, tabs, dm, tasks=()):
    S, C, nC, HD, G = dm.S, CHUNK, dm.nC, ATT_HEAD_DIM, ATT_GROUP
    GW = G * HD

    def body(q_ref, k_ref, v_ref, g_ref, qg_ref, kg_ref, sk_ref, cs_ref, su_ref, sd_ref, o_ref, qn_s, kn_s):
        _att_prep(q_ref, k_ref, qg_ref, kg_ref, cs_ref, su_ref, sd_ref, qn_s, kn_s, nC)
        sink_col = _sink_col(sk_ref)

        def blk(n, carry):
            row0 = pl.multiple_of(n * C, C)
            r = pl.ds(row0, C)
            st = _win_start(n, S)
            w = pl.ds(st, 3 * C)
            q4 = qn_s[:, r, :].reshape(G * C, HD)
            p, _ = _att_probs(q4, kn_s[w, :], sink_col, row0, st)
            o = _dot(_bf(p), v_ref[w, :])
            for g in range(G):
                gate = _silu(_f32(g_ref[r, g * HD:(g + 1) * HD]))
                o_ref[r, g * HD:(g + 1) * HD] = _bf(o[g * C:(g + 1) * C, :] * gate)
            return carry

        lax.fori_loop(0, nC, blk, 0)

    tab = pl.BlockSpec((S, HD), lambda h: (0, 0), pipeline_mode=ONE_BUFFER)
    gain = pl.BlockSpec((1, HD), lambda h: (0, 0))
    return _call(
        body, name="att_fwd", grid=(dm.AKV,),
        in_specs=[pl.BlockSpec((S, GW), lambda h: (0, dm.o_aq // GW + h)),
                  pl.BlockSpec((S, HD), lambda h: (0, dm.o_ak // HD + h)),
                  pl.BlockSpec((S, HD), lambda h: (0, dm.o_av // HD + h)),
                  pl.BlockSpec((S, GW), lambda h: (0, dm.o_ag // GW + h)),
                  gain, gain, pl.BlockSpec((None, G, 128), lambda h: (h, 0, 0)), tab, tab, tab],
        out_specs=[pl.BlockSpec((S, GW), lambda h: (0, h))],
        out_shape=[jax.ShapeDtypeStruct((S, dm.BW), BF16)],
        scratch_shapes=[pltpu.VMEM((G, S, HD), BF16), pltpu.VMEM((S, HD), BF16)],
        args=(z, z, z, z, q_gain, k_gain, _sink_rows(sink, dm), *tabs), tasks=tasks)


def _merge_out(x, mg, ya_pre, yb_pre, yc_pre, w_ret, w_pool, w_att, w_out, dm, tasks=()):
    S, D, BW = dm.S, dm.D, dm.BW
    tm = min(256, S)

    def body(x_ref, mg_ref, a_ref, b_ref, c_ref, wr_ref, wp_ref, wa_ref, wo_ref, xo_ref, ya_ref, yb_ref, yc_ref, m_ref):
        ya = _dot(a_ref[...], wr_ref[...])
        yb = _dot(b_ref[...], wp_ref[...])
        yc = _dot(c_ref[...], wa_ref[...])
        ya_ref[...], yb_ref[...], yc_ref[...] = _bf(ya), _bf(yb), _bf(yc)
        g0 = _sigmoid(_f32(mg_ref[:, 0:D]))
        g1 = _sigmoid(_f32(mg_ref[:, D:2 * D]))
        g2 = _sigmoid(_f32(mg_ref[:, 2 * D:3 * D]))
        merged = _bf(g0 * ya + g1 * yb + g2 * yc)
        m_ref[...] = merged
        xo_ref[...] = x_ref[...] + _dot(merged, wo_ref[...])

    act = lambda w: pl.BlockSpec((tm, w), lambda i: (i, 0))
    wsp = lambda r: pl.BlockSpec((None, r, D), lambda i: (0, 0, 0), pipeline_mode=ONE_BUFFER)
    return _call(
        body, name="merge_out", grid=(S // tm,),
        in_specs=[act(D), act(N_BRANCHES * D), act(BW), act(BW), act(BW), wsp(BW), wsp(BW), wsp(BW), wsp(D)],
        out_specs=[act(D)] * 5,
        out_shape=[jax.ShapeDtypeStruct((S, D), F32)] + [jax.ShapeDtypeStruct((S, D), BF16)] * 4,
        args=(x, mg, ya_pre, yb_pre, yc_pre, w_ret, w_pool, w_att, w_out), tasks=tasks)


def _loss_grad(y, target, dm):
    S, D = dm.S, dm.D
    tm = dm.tm

    def body(y_ref, t_ref, s_ref, d_ref):
        @pl.when(pl.program_id(0) == 0)
        def _():
            s_ref[...] = jnp.zeros_like(s_ref)

        e = y_ref[...] - t_ref[...]
        d_ref[...] = e * (1.0 / D)
        s_ref[...] += jnp.sum(jnp.sum(e * e, axis=-1, keepdims=True), axis=0, keepdims=True)

    row = pl.BlockSpec((tm, D), lambda i: (i, 0))
    own, _ = _call(
        body, name="loss_grad", grid=(S // tm,), in_specs=[row, row],
        out_specs=[pl.BlockSpec((1, 128), lambda i: (0, 0)), row],
        out_shape=[jax.ShapeDtypeStruct((1, 128), F32), jax.ShapeDtypeStruct((S, D), F32)],
        args=(y, target))
    return own


def _merge_bwd(dx, mg, ya, yb, yc, w_ret, w_pool, w_att, w_out, dm, tasks=()):
    S, D, BW = dm.S, dm.D, dm.BW
    tm = min(128, S)

    def body(dx_ref, mg_ref, ya_ref, yb_ref, yc_ref, wr_ref, wp_ref, wa_ref, wo_ref,
             dmg_ref, dya_ref, dyb_ref, dyc_ref, da_ref, db_ref, dc_ref, dxb_ref):
        dxb = _bf(dx_ref[...])
        dxb_ref[...] = dxb
        dm_ = _dot_nt(dxb, wo_ref[...])
        for k, (y_ref, dy_ref, w_ref, dp_ref) in enumerate(((ya_ref, dya_ref, wr_ref, da_ref),
                                                            (yb_ref, dyb_ref, wp_ref, db_ref),
                                                            (yc_ref, dyc_ref, wa_ref, dc_ref))):
            g = _sigmoid(_f32(mg_ref[:, k * D:(k + 1) * D]))
            dmg_ref[:, k * D:(k + 1) * D] = _bf(dm_ * _f32(y_ref[...]) * g * (1.0 - g))
            dy = _bf(dm_ * g)
            dy_ref[...] = dy
            dp_ref[...] = _bf(_dot_nt(dy, w_ref[...]))

    act = lambda w: pl.BlockSpec((tm, w), lambda i: (i, 0))
    wsp = lambda r: pl.BlockSpec((None, r, D), lambda i: (0, 0, 0), pipeline_mode=ONE_BUFFER)
    return _call(
        body, name="merge_bwd", grid=(S // tm,),
        in_specs=[act(D), act(N_BRANCHES * D), act(D), act(D), act(D), wsp(BW), wsp(BW), wsp(BW), wsp(D)],
        out_specs=[act(N_BRANCHES * D), act(D), act(D), act(D), act(BW), act(BW), act(BW), act(D)],
        out_shape=[jax.ShapeDtypeStruct((S, N_BRANCHES * D), BF16)] + [jax.ShapeDtypeStruct((S, D), BF16)] * 3
        + [jax.ShapeDtypeStruct((S, BW), BF16)] * 3 + [jax.ShapeDtypeStruct((S, D), BF16)],
        args=(dx, mg, ya, yb, yc, w_ret, w_pool, w_att, w_out), tasks=tasks)


def _ret_bwd(z, dya_pre, a_f, a_b, cosr, sinr, dm, tasks=()):
    S, RD, C, nC = dm.S, dm.RD, CHUNK, dm.nC
    scale = RD ** -0.5
    H = RET_HEADS

    def body(q_ref, k_ref, v_ref, g_ref, dy_ref, af_ref, ab_ref, cos_ref, sin_ref,
             dq_ref, dk_ref, dv_ref, dg_ref, dd_ref,
             qh_s, kh_s, sf_s, sb_s, do_s, dqh_s, dkh_s, dv_s, st_s, lam_s):
        t = _ret_tables(af_ref, ab_ref)
        zero_rr = jnp.zeros((RD, RD), F32)

        st_s[...] = zero_rr

        def pass0(n, carry):
            r = pl.ds(pl.multiple_of(n * C, C), C)
            cs, sn = cos_ref[r, :], sin_ref[r, :]
            qh = _rot_half(_f32(q_ref[r, :]), cs, sn)
            kh = _rot_half(_f32(k_ref[r, :]), cs, sn) * scale
            qh_s[r, :] = _bf(qh)
            kh_s[r, :] = _bf(kh)
            st = st_s[...]
            sf_s[n] = _bf(st)
            st_s[...] = st * t["dec_f"] + _dot_tn(_bf(kh * t["w_f"]), v_ref[r, :])
            return carry

        lax.fori_loop(0, nC, pass0, 0)

        st_s[...] = zero_rr
        lam_s[...] = zero_rr

        def pass1(i, acc):
            glf, glb = acc
            n = nC - 1 - i
            r = pl.ds(pl.multiple_of(n * C, C), C)
            qhb, khb, vb = qh_s[r, :], kh_s[r, :], v_ref[r, :]
            qhf, khf = _f32(qhb), _f32(khb)
            sc_ = _dot_nt(qhb, khb)
            p = _bf(sc_ * t["dmask"])
            sb = st_s[...]
            sbb = _bf(sb)
            sb_s[n] = sbb
            sfb = sf_s[n]
            qf_b, qb_b = _bf(qhf * t["q_f"]), _bf(qhf * t["q_b"])
            out = _dot(p, vb) + _dot(qf_b, sfb) + _dot(qb_b, sbb)
            rr = lax.rsqrt(jnp.mean(out * out, axis=-1, keepdims=True) + RMS_EPS)
            y = out * rr
            g = _f32(g_ref[r, :])
            dya = _f32(dy_ref[r, :])
            dg_ref[r, :] = _bf(dya * y * _dsilu(g))
            dyn = dya * _silu(g)
            dout = rr * (dyn - y * jnp.mean(dyn * y, axis=-1, keepdims=True))
            dob = _bf(dout)
            do_s[r, :] = dob
            dp = _dot_nt(dob, vb)
            dv = _dot_tn(p, dob)
            ds = _bf(dp * t["dmask"])
            dqh = _dot(ds, khb)
            dkh = _dot_tn(ds, qhb)
            dd = dp * sc_ * t["dmask"] * t["alag"]
            glf = glf + _sum11(jnp.where(t["low"], dd, 0.0))
            glb = glb + _sum11(jnp.where(t["low"], 0.0, dd))
            tf = _dot_nt(dob, sfb)
            tb = _dot_nt(dob, sbb)
            dqh = dqh + tf * t["q_f"] + tb * t["q_b"]
            glf = glf + _sum11(jnp.sum(tf * qhf, axis=-1, keepdims=True) * t["q_f"] * (t["j"] + 1.0))
            glb = glb + _sum11(jnp.sum(tb * qhf, axis=-1, keepdims=True) * t["q_b"] * (C - t["j"]))
            lam = lam_s[...]
            lamb = _bf(lam)
            glf = glf + _sum11(lam * _f32(sfb)) * t["dec_f"] * C
            u = _dot_nt(vb, lamb)
            dkh = dkh + u * t["w_f"]
            glf = glf + _sum11(jnp.sum(u * khf, axis=-1, keepdims=True) * t["w_f"] * (C - 1.0 - t["j"]))
            dv = dv + _dot(_bf(khf * t["w_f"]), lamb)
            lam_s[...] = _dot_tn(qf_b, dob) + lam * t["dec_f"]
            dqh_s[r, :] = dqh
            dkh_s[r, :] = dkh
            dv_s[r, :] = dv
            st_s[...] = sb * t["dec_b"] + _dot_tn(_bf(khf * t["w_b"]), vb)
            return glf, glb

        glf, glb = lax.fori_loop(0, nC, pass1, (jnp.zeros((1, 1), F32), jnp.zeros((1, 1), F32)))

        lam_s[...] = zero_rr

        def pass2(n, glb):
            r = pl.ds(pl.multiple_of(n * C, C), C)
            qhb, khb, vb, dob = qh_s[r, :], kh_s[r, :], v_ref[r, :], do_s[r, :]
            qhf, khf = _f32(qhb), _f32(khb)
            lam = lam_s[...]
            lamb = _bf(lam)
            glb = glb + _sum11(lam * _f32(sb_s[n])) * t["dec_b"] * C
            u = _dot_nt(vb, lamb)
            dkh = dkh_s[r, :] + u * t["w_b"]
            glb = glb + _sum11(jnp.sum(u * khf, axis=-1, keepdims=True) * t["w_b"] * t["j"])
            dv = dv_s[r, :] + _dot(_bf(khf * t["w_b"]), lamb)
            lam_s[...] = _dot_tn(_bf(qhf * t["q_b"]), dob) + lam * t["dec_b"]
            cs, sn = cos_ref[r, :], sin_ref[r, :]
            dq_ref[r, :] = _bf(_rot_half(dqh_s[r, :], cs, -sn))
            dk_ref[r, :] = _bf(_rot_half(dkh * scale, cs, -sn))
            dv_ref[r, :] = _bf(dv)
            return glb

        glb = lax.fori_loop(0, nC, pass2, glb)
        row = lax.broadcasted_iota(jnp.int32, (8, 128), 0)
        da_f = glf * t["lgf"]
        da_b = glb * t["lgb"]
        dd_ref[...] = jnp.where(row == 0, da_f, jnp.where(row == 1, da_b, 0.0))

    zs = lambda off: pl.BlockSpec((S, RD), lambda h: (0, off // RD + h), pipeline_mode=ONE_BUFFER)
    col = pl.BlockSpec((S, RD), lambda h: (0, h), pipeline_mode=ONE_BUFFER)
    dec = pl.BlockSpec((None, 1, 128), lambda h: (h, 0, 0))
    tab = pl.BlockSpec((S, RD // 2), lambda h: (0, 0), pipeline_mode=ONE_BUFFER)
    return _call(
        body, name="ret_bwd", grid=(H,),
        in_specs=[zs(dm.o_rq), zs(dm.o_rk), zs(dm.o_rv), zs(dm.o_rg), col, dec, dec, tab, tab],
        out_specs=[col] * 4 + [pl.BlockSpec((None, 8, 128), lambda h: (h, 0, 0))],
        out_shape=[jax.ShapeDtypeStruct((S, dm.BW), BF16)] * 4 + [jax.ShapeDtypeStruct((H, 8, 128), F32)],
        scratch_shapes=[pltpu.VMEM((S, RD), BF16), pltpu.VMEM((S, RD), BF16),
                        pltpu.VMEM((nC, RD, RD), BF16), pltpu.VMEM((nC, RD, RD), BF16),
                        pltpu.VMEM((S, RD), BF16),
                        pltpu.VMEM((S, RD), F32), pltpu.VMEM((S, RD), F32), pltpu.VMEM((S, RD), F32),
                        pltpu.VMEM((RD, RD), F32), pltpu.VMEM((RD, RD), F32)],
        args=(z, z, z, z, dya_pre, _decay_rows(a_f), _decay_rows(a_b), cosr, sinr), tasks=tasks)


def _pool_bwd(z, dyb_pre, pool_w, pool_scale, dm):
    S, PD, C, nC = dm.S, dm.PD, CHUNK, dm.nC

    def body(u_ref, g_ref, dy_ref, w_ref, sc_ref, du_ref, dg_ref, dw_ref, dsc_ref, dp_s, dpc_s, dw_s):
        half = jnp.left_shift(1, pl.program_id(0))
        w = w_ref[...]
        sc = sc_ref[...]
        dw_s[...] = jnp.zeros_like(dw_s)
        dsc_ref[...] = jnp.zeros_like(dsc_ref)

        def blk1(n, carry):
            row0 = pl.multiple_of(n * C, C)
            r = pl.ds(row0, C)
            st = _win_start(n, S)
            band = _band(row0, st, -half, half - 1, (C, 3 * C))
            inv = _pool_counts(row0, half, S)
            pb = _bf(_dot(band, u_ref[pl.ds(st, 3 * C), :]) * inv - _f32(u_ref[r, :]))
            ylin = _dot(pb, w)
            y = ylin * sc
            g = _f32(g_ref[r, :])
            dyb = _f32(dy_ref[r, :])
            dg_ref[r, :] = _bf(dyb * y * _dsilu(g))
            dy = dyb * _silu(g)
            dsc_ref[...] += jnp.sum(dy * ylin, axis=0, keepdims=True)
            dyl = _bf(dy * sc)
            dw_s[...] += _dot_tn(pb, dyl)
            dp = _dot_nt(dyl, w)
            dp_s[r, :] = dp
            dpc_s[r, :] = _bf(dp * inv)
            return carry

        lax.fori_loop(0, nC, blk1, 0)
        dw_ref[...] = _bf(dw_s[...])

        def blk2(n, carry):
            row0 = pl.multiple_of(n * C, C)
            r = pl.ds(row0, C)
            st = _win_start(n, S)
            band_t = _band(row0, st, -half + 1, half, (C, 3 * C))
            du_ref[r, :] = _bf(_dot(band_t, dpc_s[pl.ds(st, 3 * C), :]) - dp_s[r, :])
            return carry

        lax.fori_loop(0, nC, blk2, 0)

    G = POOL_GROUPS
    own, _ = _call(
        body, name="pool_bwd", grid=(G,),
        in_specs=[pl.BlockSpec((S, PD), lambda g: (0, dm.o_pv // PD + g)),
                  pl.BlockSpec((S, PD), lambda g: (0, dm.o_pg // PD + g)),
                  pl.BlockSpec((S, PD), lambda g: (0, g)),
                  pl.BlockSpec((None, PD, PD), lambda g: (g, 0, 0)),
                  pl.BlockSpec((None, 1, PD), lambda g: (g, 0, 0))],
        out_specs=[pl.BlockSpec((S, PD), lambda g: (0, g)), pl.BlockSpec((S, PD), lambda g: (0, g)),
                   pl.BlockSpec((None, PD, PD), lambda g: (g, 0, 0)), pl.BlockSpec((None, 1, PD), lambda g: (g, 0, 0))],
        out_shape=[jax.ShapeDtypeStruct((S, dm.BW), BF16), jax.ShapeDtypeStruct((S, dm.BW), BF16),
                   jax.ShapeDtypeStruct((G, PD, PD), BF16), jax.ShapeDtypeStruct((G, 1, PD), F32)],
        scratch_shapes=[pltpu.VMEM((S, PD), F32), pltpu.VMEM((S, PD), BF16), pltpu.VMEM((PD, PD), F32)],
        args=(z, z, dyb_pre, pool_w, pool_scale))
    return own


def _att_bwd(z, dyc_pre, q_gain, k_gain, sink, tabs, dm, tasks=()):
    S, C, nC, HD, G = dm.S, CHUNK, dm.nC, ATT_HEAD_DIM, ATT_GROUP
    GW = G * HD
    scale = HD ** -0.5

    def body(q_ref, k_ref, v_ref, g_ref, dy_ref, qg_ref, kg_ref, sk_ref, cs_ref, su_ref, sd_ref,
             dq_ref, dk_ref, dv_ref, dg_ref, sm_ref, qn_s, kn_s, dqn_s, dkn_s, dv_s):
        _att_prep(q_ref, k_ref, qg_ref, kg_ref, cs_ref, su_ref, sd_ref, qn_s, kn_s, nC)
        dkn_s[...] = jnp.zeros_like(dkn_s)
        dv_s[...] = jnp.zeros_like(dv_s)
        sink_col = _sink_col(sk_ref)

        def blk(n, dsink):
            row0 = pl.multiple_of(n * C, C)
            r = pl.ds(row0, C)
            st = _win_start(n, S)
            w = pl.ds(st, 3 * C)
            q4 = qn_s[:, r, :].reshape(G * C, HD)
            kw, vw = kn_s[w, :], v_ref[w, :]
            p, ps = _att_probs(q4, kw, sink_col, row0, st)
            pb = _bf(p)
            o = _dot(pb, vw)
            do_parts = []
            for g in range(G):
                cols = slice(g * HD, (g + 1) * HD)
                gate = _f32(g_ref[r, cols])
                dy = _f32(dy_ref[r, cols])
                dg_ref[r, cols] = _bf(dy * o[g * C:(g + 1) * C, :] * _dsilu(gate))
                do_parts.append(dy * _silu(gate))
            dob = _bf(jnp.concatenate(do_parts, axis=0))
            dp = _dot_nt(dob, vw)
            drow = jnp.sum(p * dp, axis=-1, keepdims=True)
            ds = _bf(p * (dp - drow))
            dsink = dsink - ps * drow
            dqn_s[:, r, :] = (_dot(ds, kw) * scale).reshape(G, C, HD)
            dkn_s[w, :] += _dot_tn(ds, q4) * scale
            dv_s[w, :] += _dot_tn(pb, dob)
            return dsink

        dsink = lax.fori_loop(0, nC, blk, jnp.zeros((G * C, 1), F32))

        def fin(n, acc):
            dqg, dkg = acc
            r = pl.ds(pl.multiple_of(n * C, C), C)
            cs, su, sd = cs_ref[r, :], su_ref[r, :], sd_ref[r, :]

            def norm_bwd(x, gain, dqn):
                rr = lax.rsqrt(jnp.mean(x * x, axis=-1, keepdims=True) + RMS_EPS)
                u = x * rr
                dw = _rot_part(dqn, cs, -su, -sd)
                du = dw * gain
                return rr * (du - u * jnp.mean(du * u, axis=-1, keepdims=True)), jnp.sum(dw * u, axis=0, keepdims=True)

            dk, gk = norm_bwd(_f32(k_ref[r, :]), kg_ref[...], dkn_s[r, :])
            dk_ref[r, :] = _bf(dk)
            dv_ref[r, :] = _bf(dv_s[r, :])
            dkg = dkg + gk
            for g in range(G):
                cols = slice(g * HD, (g + 1) * HD)
                dq, gq = norm_bwd(_f32(q_ref[r, cols]), qg_ref[...], dqn_s[g, r, :])
                dq_ref[r, cols] = _bf(dq)
                dqg = dqg + gq
            return dqg, dkg

        dqg, dkg = lax.fori_loop(0, nC, fin, (jnp.zeros((1, HD), F32), jnp.zeros((1, HD), F32)))
        sm_ref[...] = jnp.zeros_like(sm_ref)
        sm_ref[0:1, :] = dqg
        sm_ref[1:2, :] = dkg
        for g in range(G):
            sm_ref[2 + g:3 + g, :] = jnp.broadcast_to(_sum11(dsink[g * C:(g + 1) * C, :]), (1, HD))

    tab = pl.BlockSpec((S, HD), lambda h: (0, 0), pipeline_mode=ONE_BUFFER)
    gain = pl.BlockSpec((1, HD), lambda h: (0, 0))
    wide = lambda off: pl.BlockSpec((S, GW), lambda h: (0, off // GW + h), pipeline_mode=ONE_BUFFER)
    thin = lambda off: pl.BlockSpec((S, HD), lambda h: (0, off // HD + h), pipeline_mode=ONE_BUFFER)
    return _call(
        body, name="att_bwd", grid=(dm.AKV,),
        in_specs=[wide(dm.o_aq), thin(dm.o_ak), thin(dm.o_av), wide(dm.o_ag), wide(0), gain, gain,
                  pl.BlockSpec((None, G, 128), lambda h: (h, 0, 0)), tab, tab, tab],
        out_specs=[wide(0), thin(0), thin(0), wide(0), pl.BlockSpec((None, 8, 128), lambda h: (h, 0, 0))],
        out_shape=[jax.ShapeDtypeStruct((S, dm.BW), BF16), jax.ShapeDtypeStruct((S, dm.AKW), BF16),
                   jax.ShapeDtypeStruct((S, dm.AKW), BF16), jax.ShapeDtypeStruct((S, dm.BW), BF16),
                   jax.ShapeDtypeStruct((dm.AKV, 8, 128), F32)],
        scratch_shapes=[pltpu.VMEM((G, S, HD), BF16), pltpu.VMEM((S, HD), BF16),
                        pltpu.VMEM((G, S, HD), F32), pltpu.VMEM((S, HD), F32), pltpu.VMEM((S, HD), F32)],
        args=(z, z, z, z, dyc_pre, q_gain, k_gain, _sink_rows(sink, dm), *tabs), tasks=tasks)


def _grad_matmul(a, b, name):
    S, M = a.shape
    N = b.shape[1]
    tm, tn = min(1024, M), min(512, N)

    def body(a_ref, b_ref, o_ref):
        o_ref[...] = _bf(_dot_tn(a_ref[...], b_ref[...]))

    own, _ = _call(
        body, name=name, grid=(M // tm, N // tn),
        in_specs=[pl.BlockSpec((S, tm), lambda i, j: (0, i)), pl.BlockSpec((S, tn), lambda i, j: (0, j))],
        out_specs=[pl.BlockSpec((None, tm, tn), lambda i, j: (0, i, j))],
        out_shape=[jax.ShapeDtypeStruct((1, M, N), BF16)],
        args=(a, b))
    return own[0]


def _inproj_bwd(dz, w_in, x, norm_g, dx_out, dm, tasks=()):
    S, D = dm.S, dm.D
    tm, tk = dm.tm, 512
    nk = dm.INW // tk

    def body(dz_ref, w_ref, x_ref, g_ref, dxo_ref, dx_ref, dg_ref, acc):
        k = pl.program_id(1)

        @pl.when((pl.program_id(0) == 0) & (k == 0))
        def _():
            dg_ref[...] = jnp.zeros_like(dg_ref)

        @pl.when(k == 0)
        def _():
            acc[...] = jnp.zeros_like(acc)

        acc[...] += _dot_nt(dz_ref[...], w_ref[...])

        @pl.when(k == nk - 1)
        def _():
            xv = x_ref[...]
            rr = lax.rsqrt(jnp.mean(xv * xv, axis=-1, keepdims=True) + RMS_EPS)
            u = xv * rr
            dh = acc[...]
            dg_ref[...] += jnp.sum(dh * u, axis=0, keepdims=True)
            du = dh * g_ref[...]
            dx_ref[...] = dxo_ref[...] + rr * (du - u * jnp.mean(du * u, axis=-1, keepdims=True))

    row = pl.BlockSpec((tm, D), lambda i, k: (i, 0))
    return _call(
        body, name="inproj_bwd", grid=(S // tm, nk),
        in_specs=[pl.BlockSpec((tm, tk), lambda i, k: (i, k)),
                  pl.BlockSpec((None, D, tk), lambda i, k: (0, 0, k)), row,
                  pl.BlockSpec((1, D), lambda i, k: (0, 0)), row],
        out_specs=[row, pl.BlockSpec((1, D), lambda i, k: (0, 0))],
        out_shape=[jax.ShapeDtypeStruct((S, D), F32), jax.ShapeDtypeStruct((1, D), F32)],
        scratch_shapes=[pltpu.VMEM((tm, D), F32)],
        args=(dz, w_in, x, norm_g, dx_out), tasks=tasks)


def _row_block(rows, width, itemsize):
    target = max(16, (2 * 1024 * 1024) // (width * itemsize))
    for rb in range(min(rows, target), 0, -1):
        if rows % rb == 0 and (rb % 16 == 0 or rb == rows):
            return rb
    return rows


def _prefetch_call(body, *, name, grid, in_specs, out_specs, out_shape, args, aliases=None):
    grid_spec = pltpu.PrefetchScalarGridSpec(num_scalar_prefetch=1, grid=grid, in_specs=in_specs, out_specs=out_specs)
    return pl.pallas_call(
        body, name=name, grid_spec=grid_spec, out_shape=out_shape, input_output_aliases=aliases or {},
        compiler_params=pltpu.CompilerParams(dimension_semantics=("arbitrary",) * len(grid),
                                             vmem_limit_bytes=VMEM_LIMIT_V7X),
    )(*args)


def _place_shard(w, l, spec, chip):
    A, rows, width = spec.shard
    rb = _row_block(rows, width, 4)
    nrb = rows // rb
    if spec.kind == "col":
        out_map = lambda a, r, chip: (a, r, chip[0])
    else:
        out_map = lambda a, r, chip: (a, chip[0] * nrb + r, 0)

    def body(chip_ref, w_ref, o_ref):
        o_ref[...] = _bf(w_ref[...])

    return _prefetch_call(
        body, name="place_" + spec.name, grid=(A, nrb),
        in_specs=[pl.BlockSpec((None, rb, width), lambda a, r, chip: (l * A + a, r, 0))],
        out_specs=pl.BlockSpec((None, rb, width), out_map),
        out_shape=jax.ShapeDtypeStruct(spec.full, BF16), args=(chip, w))


def _pair_sum(grad, land, spec, core):
    A, hr, w = spec.half
    rb = _row_block(hr, w, 2)
    nrb = hr // rb
    if spec.kind == "col":
        g_spec = pl.BlockSpec((None, rb, w), lambda j, a, r, core: (a, core[0] * nrb + r, j))
    else:
        g_spec = pl.BlockSpec((None, rb, w), lambda j, a, r, core: (a, (j * 2 + core[0]) * nrb + r, 0))

    def body(core_ref, g_ref, l_ref, o_ref):
        o_ref[...] = _bf(_f32(g_ref[...]) + _f32(l_ref[...]))

    blk = pl.BlockSpec((None, None, rb, w), lambda j, a, r, core: (j, a, r, 0))
    return _prefetch_call(
        body, name="pair_sum_" + spec.name, grid=(N_CHIPS, A, nrb), in_specs=[g_spec, blk], out_specs=blk,
        out_shape=jax.ShapeDtypeStruct((N_CHIPS,) + spec.half, BF16), args=(core, grad, land))


def _chip_sum(pair_sum, land, spec, chip_core):
    A, hr, w = spec.half
    rb = _row_block(hr, w, 4)
    nrb = hr // rb

    def body(cc_ref, p_ref, l0_ref, l1_ref, l2_ref, o_ref):
        o_ref[...] = ((_f32(p_ref[...]) + _f32(l0_ref[...])) + _f32(l1_ref[...])) + _f32(l2_ref[...])

    own = pl.BlockSpec((None, None, rb, w), lambda a, r, cc: (cc[0], a, r, 0))
    slot = lambda p: pl.BlockSpec((None, None, rb, w), lambda a, r, cc: (p, a, r, 0))
    return _prefetch_call(
        body, name="chip_sum_" + spec.name, grid=(A, nrb), in_specs=[own, slot(0), slot(1), slot(2)],
        out_specs=pl.BlockSpec((None, rb, w), lambda a, r, cc: (a, cc[1] * nrb + r, 0)),
        out_shape=jax.ShapeDtypeStruct(spec.shard, F32), args=(chip_core, pair_sum, land, land, land))


def _adamw_math(w, g, m, v):
    m = ADAM_B1 * m + (1.0 - ADAM_B1) * g
    v = ADAM_B2 * v + (1.0 - ADAM_B2) * (g * g)
    m_hat = m / (1.0 - ADAM_B1 ** ADAM_STEP)
    v_hat = v / (1.0 - ADAM_B2 ** ADAM_STEP)
    delta = -ADAM_LR * (m_hat / (jnp.sqrt(v_hat) + ADAM_EPS) + ADAM_WD * w)
    return delta, m, v


def _adamw(w, g, m, v, l, depth, spec, carried):
    A, R, C = spec.shard
    rb = _row_block(R, C, 4 * 4)
    stacked = pl.BlockSpec((None, rb, C), lambda a, r: (l * A + a, r, 0))
    n_carry = 0 if carried is None else 4

    def body(w_ref, g_ref, m_ref, v_ref, *rest):
        go_ref, d_ref, mo_ref, vo_ref = rest[n_carry:]
        g = g_ref[...]
        go_ref[...] = g
        d_ref[...], mo_ref[...], vo_ref[...] = _adamw_math(w_ref[...], g, m_ref[...], v_ref[...])

    return pl.pallas_call(
        body, name="adamw_" + spec.name, grid=(A, R // rb),
        in_specs=[stacked, pl.BlockSpec((None, rb, C), lambda a, r: (a, r, 0)), stacked, stacked] + [HBM_ANY] * n_carry,
        out_specs=[stacked] * 4,
        out_shape=[jax.ShapeDtypeStruct((depth * A, R, C), F32)] * 4,
        input_output_aliases={4 + i: i for i in range(n_carry)},
        compiler_params=pltpu.CompilerParams(dimension_semantics=("arbitrary", "arbitrary"),
                                             vmem_limit_bytes=VMEM_LIMIT_V7X),
    )(w, g, m, v, *(carried or ()))


def _small_update(g_part, w, m, v):
    R = g_part.shape[0]
    n_dev = 8

    def body(g_ref, w_ref, m_ref, v_ref, go_ref, d_ref, mo_ref, vo_ref, all_s, send_sem, recv_sem):
        x, y, c = _place()
        me = 4 * x + 2 * y + c
        all_s[me] = g_ref[...]
        cps = []
        for k in range(1, n_dev):
            peer = (x ^ ((k >> 2) & 1), y ^ ((k >> 1) & 1), c ^ (k & 1))
            cp = pltpu.make_async_remote_copy(src_ref=g_ref, dst_ref=all_s.at[me], send_sem=send_sem.at[k],
                                              recv_sem=recv_sem.at[k], device_id=peer, device_id_type=MESH)
            cp.start()
            cps.append(cp)
        for cp in cps:
            cp.wait()
        g = all_s[0]
        for d in range(1, n_dev):
            g = g + all_s[d]
        go_ref[...] = g
        d_ref[...], mo_ref[...], vo_ref[...] = _adamw_math(w_ref[...], g, m_ref[...], v_ref[...])

    vm = pl.BlockSpec(memory_space=pltpu.VMEM)
    return pl.pallas_call(
        body, name="small_update", in_specs=[vm] * 4, out_specs=[vm] * 4,
        out_shape=[jax.ShapeDtypeStruct((R, 128), F32)] * 4,
        scratch_shapes=[pltpu.VMEM((n_dev, R, 128), F32), pltpu.SemaphoreType.DMA((n_dev,)),
                        pltpu.SemaphoreType.DMA((n_dev,))],
        compiler_params=pltpu.CompilerParams(vmem_limit_bytes=VMEM_LIMIT_V7X),
    )(g_part, w, m, v)


def _pack_small(parts):
    flat = jnp.concatenate([p.reshape(-1) for p in parts])
    pad = (-flat.shape[0]) % 1024
    return jnp.pad(flat, (0, pad)).reshape(-1, 128)


def _unpack_small(packed, like):
    flat = packed.reshape(-1)
    out, at = [], 0
    for p in like:
        out.append(flat[at:at + p.size].reshape(p.shape))
        at += p.size
    return out


def kernel(x, norm_g, w_in, ret_decay_fwd, ret_decay_bwd, pool_w, pool_scale, attn_q_gain, attn_k_gain, attn_sink, w_ret, w_pool, w_att, w_out, loss_target, m_norm_g, m_w_in, m_ret_decay_fwd, m_ret_decay_bwd, m_pool_w, m_pool_scale, m_attn_q_gain, m_attn_k_gain, m_attn_sink, m_w_ret, m_w_pool, m_w_att, m_w_out, v_norm_g, v_w_in, v_ret_decay_fwd, v_ret_decay_bwd, v_pool_w, v_pool_scale, v_attn_q_gain, v_attn_k_gain, v_attn_sink, v_w_ret, v_w_pool, v_w_att, v_w_out):
    S, D = x.shape[1], x.shape[2]
    L = norm_g.shape[0]
    dm = _Dims(S, D, L)
    PD, BW, G = dm.PD, dm.BW, POOL_GROUPS
    xi, yi, ci = _place()
    chip = (2 * xi + yi).astype(jnp.int32).reshape(1)
    core = ci.astype(jnp.int32).reshape(1)
    chip_core = jnp.concatenate([chip, core])

    specs = [_Sharded("w_in", "col", 1, D, dm.INW), _Sharded("w_ret", "col", 1, BW, D), _Sharded("w_pool", "col", 1, BW, D),
             _Sharded("w_att", "col", 1, BW, D), _Sharded("w_out", "row", 1, D, D), _Sharded("pool_w", "row", G, PD, PD)]
    n_big = len(specs)
    big_w = [w_in, w_ret, w_pool, w_att, w_out, pool_w]
    big_m = [m_w_in, m_w_ret, m_w_pool, m_w_att, m_w_out, m_pool_w]
    big_v = [v_w_in, v_w_ret, v_w_pool, v_w_att, v_w_out, v_pool_w]
    stack3 = lambda a, s: a.reshape((L * s.shard[0],) + s.shard[1:])
    big_w3 = [stack3(a, s) for a, s in zip(big_w, specs)]
    big_m3 = [stack3(a, s) for a, s in zip(big_m, specs)]
    big_v3 = [stack3(a, s) for a, s in zip(big_v, specs)]

    W = [[_place_shard(big_w3[t], l, specs[t], chip) for t in range(n_big)] for l in range(L)]
    (ici0, _), = _comm_only("gather_first_ici", [_gather_ici_task(W[0], specs)])
    (W[0], _), = _comm_only("gather_first_d2d", [_gather_d2d_task(ici0, specs)])

    cosr, sinr = _ret_rope_tables(S, dm.RD)
    tabs = _att_tables(S)
    xl = x[0]
    saved = []
    for l in range(L):
        nxt = l + 1 < L
        ng = norm_g[l].reshape(1, D)
        qg, kg = attn_q_gain[l].reshape(1, ATT_HEAD_DIM), attn_k_gain[l].reshape(1, ATT_HEAD_DIM)
        psc = pool_scale[l].reshape(G, 1, PD)
        f_in, f_ret, f_pool, f_att, f_out, f_pw = W[l]
        (z, h), tk = _inproj(xl, ng, f_in, dm, tasks=[_gather_ici_task(W[l + 1][:1], specs[:1])] if nxt else ())
        if nxt:
            W[l + 1][:1] = tk[0][0]
        ya_pre = _ret_fwd(z, ret_decay_fwd[l], ret_decay_bwd[l], cosr, sinr, dm)
        yb_pre = _pool_fwd(z, f_pw, psc, dm)
        (yc_pre,), tk = _att_fwd(z, qg, kg, attn_sink[l], tabs, dm,
                                 tasks=[_gather_ici_task(W[l + 1][1:], specs[1:])] if nxt else ())
        if nxt:
            W[l + 1][1:] = tk[0][0]
        mg = z[:, dm.o_mg:]
        (x_next, ya, yb, yc, merged), tk = _merge_out(xl, mg, ya_pre, yb_pre, yc_pre, f_ret, f_pool, f_att, f_out, dm,
                                                      tasks=[_gather_d2d_task(W[l + 1], specs)] if nxt else ())
        if nxt:
            W[l + 1] = tk[0][0]
        saved.append((xl, z, h, mg, ya_pre, yb_pre, yc_pre, ya, yb, yc, merged, ng, qg, kg, psc))
        xl = x_next
    sq, dx = _loss_grad(xl, loss_target[0], dm)
    loss = lax.psum(sq[0, 0] * (0.5 / D), ("x", "y", "c"))

    grads = [None] * L
    lands = [None] * L
    pairs = [None] * L
    lands2 = [None] * L
    shards = [None] * L
    carried = [None] * n_big
    small = [None] * L

    def pair_sums(k):
        pairs[k] = [_pair_sum(g, ld, s, core) for g, ld, s in zip(grads[k], lands[k], specs)]

    def chip_sums(k):
        shards[k] = [_chip_sum(p, ld, s, chip_core) for p, ld, s in zip(pairs[k], lands2[k], specs)]

    def adamw(k):
        for t in range(n_big):
            carried[t] = _adamw(big_w3[t], shards[k][t], big_m3[t], big_v3[t], k, L, specs[t], carried[t])

    for l in reversed(range(L)):
        xl, z, h, mg, ya_pre, yb_pre, yc_pre, ya, yb, yc, merged, ng, qg, kg, psc = saved[l]
        f_in, f_ret, f_pool, f_att, f_out, f_pw = W[l]
        up1, up2 = l + 1 < L, l + 2 < L
        own, tk = _merge_bwd(dx, mg, ya, yb, yc, f_ret, f_pool, f_att, f_out, dm,
                             tasks=[_pair_task(grads[l + 1], specs)] if up1 else ())
        dmg, dya, dyb, dyc, dya_pre, dyb_pre, dyc_pre, dxb = own
        if up1:
            lands[l + 1] = tk[0][1]
            pair_sums(l + 1)
        g_out = _grad_matmul(merged, dxb, "grad_w_out")
        g_ret = _grad_matmul(ya_pre, dya, "grad_w_ret")
        g_pool = _grad_matmul(yb_pre, dyb, "grad_w_pool")
        g_att = _grad_matmul(yc_pre, dyc, "grad_w_att")
        own, tk = _ret_bwd(z, dya_pre, ret_decay_fwd[l], ret_decay_bwd[l], cosr, sinr, dm,
                           tasks=[_chip_task(pairs[l + 1][1:], specs[1:])] if up1 else ())
        drq, drk, drv, drg, ddec = own
        if up1:
            lands2[l + 1] = [None] + tk[0][1]
        dpv, dpg, dpw, dps = _pool_bwd(z, dyb_pre, f_pw, psc, dm)
        own, tk = _att_bwd(z, dyc_pre, qg, kg, attn_sink[l], tabs, dm,
                           tasks=[_half_task(shards[l + 2], specs)] if up2 else ())
        daq, dak, dav, dag, dsm = own
        if up2:
            shards[l + 2] = tk[0][0]
            adamw(l + 2)
        dz = jnp.concatenate([drq, drk, drv, drg, dpv, dpg, daq, dak, dav, dag, dmg], axis=1)
        g_in = _grad_matmul(h, dz, "grad_w_in")
        (dx, dng), tk = _inproj_bwd(dz, f_in, xl, ng, dx, dm, tasks=[_chip_task(pairs[l + 1][:1], specs[:1])] if up1 else ())
        if up1:
            lands2[l + 1][0] = tk[0][1][0]
            chip_sums(l + 1)
        grads[l] = [g_in, g_ret, g_pool, g_att, g_out, dpw]
        small[l] = [dng.reshape(D), ddec[:, 0, 0], ddec[:, 1, 0], dps.reshape(BW), jnp.sum(dsm[:, 0, :], axis=0),
                    jnp.sum(dsm[:, 1, :], axis=0), dsm[:, 2:2 + ATT_GROUP, 0].reshape(dm.AH)]

    if L > 1:
        (shards[1], _), = _comm_only("grad_half_exchange", [_half_task(shards[1], specs)])
        adamw(1)
    (_, lands[0]), = _comm_only("grad_pair_exchange", [_pair_task(grads[0], specs)])
    pair_sums(0)
    (_, lands2[0]), = _comm_only("grad_chip_exchange", [_chip_task(pairs[0], specs)])
    chip_sums(0)
    (shards[0], _), = _comm_only("grad_half_exchange_last", [_half_task(shards[0], specs)])
    adamw(0)

    back = lambda a, like: a.reshape(like.shape)
    g_big, d_big, m_big, v_big = ([back(carried[t][i], big_w[t]) for t in range(n_big)] for i in range(4))

    small_g = [jnp.stack([small[l][i] for l in range(L)]) for i in range(7)]
    small_w = [norm_g, ret_decay_fwd, ret_decay_bwd, pool_scale, attn_q_gain, attn_k_gain, attn_sink]
    small_m = [m_norm_g, m_ret_decay_fwd, m_ret_decay_bwd, m_pool_scale, m_attn_q_gain, m_attn_k_gain, m_attn_sink]
    small_v = [v_norm_g, v_ret_decay_fwd, v_ret_decay_bwd, v_pool_scale, v_attn_q_gain, v_attn_k_gain, v_attn_sink]
    sg, sd, sm, sv = _small_update(_pack_small(small_g), _pack_small(small_w), _pack_small(small_m), _pack_small(small_v))
    g_sm, d_sm, m_sm, v_sm = (_unpack_small(a, small_w) for a in (sg, sd, sm, sv))

    def ordered(big, small_):
        return [small_[0], big[0], small_[1], small_[2], big[5], small_[3], small_[4], small_[5], small_[6],
                big[1], big[2], big[3], big[4]]

    return (loss, dx[None], *ordered(g_big, g_sm), *ordered(d_big, d_sm), *ordered(m_big, m_sm),
            *ordered(v_big, v_sm))
```

```python
import jax
import jax.numpy as jnp
from jax import lax
from jax.experimental import pallas as pl
from jax.experimental.pallas import tpu as pltpu

F32 = jnp.float32
BF16 = jnp.bfloat16
MESH = pl.DeviceIdType.MESH

RMS_EPS = 1e-6
NEG_BIG = -1e30
CHUNK = 128
RET_HEADS = 4
POOL_GROUPS = 4
ATT_HEAD_DIM = 128
ATT_GROUP = 4
ROPE_DIMS = 32
RET_ROPE_BASE = 10000.0
ROPE_THETA = 500000.0
N_BRANCHES = 3
N_CHIPS = 4

ADAM_LR = 0.001
ADAM_B1 = 0.9
ADAM_B2 = 0.999
ADAM_EPS = 1e-08
ADAM_WD = 0.01
ADAM_STEP = 10

VMEM_LIMIT_V7X = 56 * 1024 * 1024

TN = (((0,), (0,)), ((), ()))
NT = (((1,), (1,)), ((), ()))

LOOP_UNROLL = 2
HBM_ANY = pl.BlockSpec(memory_space=pl.ANY)
ONE_BUFFER = pl.Buffered(1)


def _sigmoid(x):
    return 1.0 / (1.0 + jnp.exp(-x))


def _silu(x):
    return x * _sigmoid(x)


def _dsilu(x):
    s = _sigmoid(x)
    return s * (1.0 + x * (1.0 - s))


def _dot(a, b):
    return jnp.dot(a, b, preferred_element_type=F32)


def _dot_tn(a, b):
    return lax.dot_general(a, b, TN, preferred_element_type=F32)


def _dot_nt(a, b):
    return lax.dot_general(a, b, NT, preferred_element_type=F32)


def _bf(x):
    return x.astype(BF16)


def _f32(x):
    return x.astype(F32)


def _loop(n, body, init):
    def several(i, carry):
        for u in range(LOOP_UNROLL):
            carry = body(i * LOOP_UNROLL + u, carry)
        return carry
    return lax.fori_loop(0, n // LOOP_UNROLL, several, init)


def _sum11(x):
    return jnp.sum(jnp.sum(x, axis=1, keepdims=True), axis=0, keepdims=True)


class _Dims:
    def __init__(self, seq, d_model, depth):
        self.S, self.D, self.L = seq, d_model, depth
        bw = d_model // 2
        self.BW = bw
        self.RD = bw // RET_HEADS
        self.PD = bw // POOL_GROUPS
        self.AH = bw // ATT_HEAD_DIM
        self.AKV = self.AH // ATT_GROUP
        self.AKW = self.AKV * ATT_HEAD_DIM
        self.o_rq, self.o_rk, self.o_rv, self.o_rg = 0, bw, 2 * bw, 3 * bw
        self.o_pv, self.o_pg = 4 * bw, 5 * bw
        self.o_aq = 6 * bw
        self.o_ak = 7 * bw
        self.o_av = 7 * bw + self.AKW
        self.o_ag = 7 * bw + 2 * self.AKW
        self.o_mg = 8 * bw + 2 * self.AKW
        self.INW = self.o_mg + N_BRANCHES * d_model
        self.nC = seq // CHUNK
        self.tm = min(512, seq)


def _place():
    return lax.axis_index("x"), lax.axis_index("y"), lax.axis_index("c")


def _chip_peers(x, y):
    return [(1 - x, y), (x, 1 - y), (1 - x, 1 - y)]


class _Sharded:
    def __init__(self, name, kind, a, r, cc):
        self.name, self.kind, self.A, self.R, self.Cc = name, kind, a, r, cc
        self.full = (a, r, cc)
        if kind == "col":
            self.shard = (a, r, cc // N_CHIPS)
            self.half_rows = r // 2
        else:
            self.shard = (a, r // N_CHIPS, cc)
            self.half_rows = r // N_CHIPS // 2
        self.half = (a, self.half_rows, self.shard[2])

    def in_full(self, ref, chip, core=None):
        hr = self.half_rows
        if self.kind == "col":
            rows = pl.ds(0, self.R) if core is None else pl.ds(core * hr, hr)
            return ref.at[:, rows, pl.ds(chip * self.shard[2], self.shard[2])]
        rows = pl.ds(chip * self.shard[1], self.shard[1]) if core is None else pl.ds(chip * self.shard[1] + core * hr, hr)
        return ref.at[:, rows, :]

    def in_shard(self, ref, core):
        return ref.at[:, pl.ds(core * self.half_rows, self.half_rows), :]


class _Task:
    def __init__(self, ro, rw, new, n_sem, copies):
        self.ro, self.rw, self.new, self.n_sem, self.copies = list(ro), list(rw), list(new), n_sem, copies


def _remote(src, dst, send_sem, recv_sem, k, device):
    def make():
        return pltpu.make_async_remote_copy(src_ref=src, dst_ref=dst, send_sem=send_sem.at[k], recv_sem=recv_sem.at[k],
                                            device_id=device, device_id_type=MESH)
    return make


def _gather_ici_task(bufs, specs):
    def copies(ro, rw, new, ss, rs):
        x, y, c = _place()
        me = 2 * x + y
        out = []
        for t, spec in enumerate(specs):
            for p, (px, py) in enumerate(_chip_peers(x, y)):
                mine = spec.in_full(rw[t], me, c)
                theirs = spec.in_full(rw[t], 2 * px + py, c)
                k = t * 3 + p
                out.append((_remote(mine, mine, ss, rs, k, (px, py, c)), _remote(mine, mine, ss, rs, k, (px, py, c)),
                            _remote(theirs, theirs, ss, rs, k, (px, py, c))))
        return out
    return _Task([], bufs, [], 3 * len(specs), copies)


def _gather_d2d_task(bufs, specs):
    def copies(ro, rw, new, ss, rs):
        x, y, c = _place()
        sib = (x, y, 1 - c)
        out = []
        for t, spec in enumerate(specs):
            for p, (px, py) in enumerate(_chip_peers(x, y)):
                got = spec.in_full(rw[t], 2 * px + py, c)
                gets = spec.in_full(rw[t], 2 * px + py, 1 - c)
                k = t * 3 + p
                out.append((_remote(got, got, ss, rs, k, sib), _remote(got, got, ss, rs, k, sib),
                            _remote(gets, gets, ss, rs, k, sib)))
        return out
    return _Task([], bufs, [], 3 * len(specs), copies)


def _pair_task(grads, specs):
    def copies(ro, rw, new, ss, rs):
        x, y, c = _place()
        sib = (x, y, 1 - c)
        out = []
        for t, spec in enumerate(specs):
            for j in range(N_CHIPS):
                k = t * N_CHIPS + j
                cp = _remote(spec.in_full(ro[t], j, 1 - c), new[t].at[j], ss, rs, k, sib)
                out.append((cp, cp, cp))
        return out
    return _Task(grads, [], [jax.ShapeDtypeStruct((N_CHIPS,) + s.half, BF16) for s in specs], N_CHIPS * len(specs), copies)


def _chip_task(pair_sums, specs):
    def copies(ro, rw, new, ss, rs):
        x, y, c = _place()
        out = []
        for t in range(len(specs)):
            for p, (px, py) in enumerate(_chip_peers(x, y)):
                cp = _remote(ro[t].at[2 * px + py], new[t].at[p], ss, rs, t * 3 + p, (px, py, c))
                out.append((cp, cp, cp))
        return out
    return _Task(pair_sums, [], [jax.ShapeDtypeStruct((3,) + s.half, BF16) for s in specs], 3 * len(specs), copies)


def _half_task(shards, specs):
    def copies(ro, rw, new, ss, rs):
        x, y, c = _place()
        sib = (x, y, 1 - c)
        out = []
        for t, spec in enumerate(specs):
            mine, theirs = spec.in_shard(rw[t], c), spec.in_shard(rw[t], 1 - c)
            out.append((_remote(mine, mine, ss, rs, t, sib), _remote(mine, mine, ss, rs, t, sib),
                        _remote(theirs, theirs, ss, rs, t, sib)))
        return out
    return _Task([], shards, [], len(specs), copies)


def _call(body, *, name, grid, in_specs, out_specs, out_shape, args, scratch_shapes=(), tasks=()):
    tasks = [t for t in tasks if t is not None]
    n_in, n_out, n_scr = len(in_specs), len(out_specs), len(scratch_shapes)
    ro = [a for t in tasks for a in t.ro]
    rw = [a for t in tasks for a in t.rw]
    new = [s for t in tasks for s in t.new]
    n_ro, n_rw, n_new = len(ro), len(rw), len(new)

    def wrapped(*refs):
        ins = refs[:n_in]
        ro_refs = refs[n_in:n_in + n_ro]
        at = n_in + n_ro + n_rw
        outs = refs[at:at + n_out]
        rw_refs = refs[at + n_out:at + n_out + n_rw]
        new_refs = refs[at + n_out + n_rw:at + n_out + n_rw + n_new]
        at = at + n_out + n_rw + n_new
        scr = refs[at:at + n_scr]
        sems = refs[at + n_scr:]

        def task_copies():
            found, a, b, d = [], 0, 0, 0
            for i, t in enumerate(tasks):
                found += t.copies(ro_refs[a:a + len(t.ro)], rw_refs[b:b + len(t.rw)], new_refs[d:d + len(t.new)],
                                  sems[2 * i], sems[2 * i + 1])
                a, b, d = a + len(t.ro), b + len(t.rw), d + len(t.new)
            return found

        if tasks:
            first = pl.program_id(0) == 0
            last = pl.program_id(0) == grid[0] - 1
            for ax in range(1, len(grid)):
                first = first & (pl.program_id(ax) == 0)
                last = last & (pl.program_id(ax) == grid[ax] - 1)

            @pl.when(first)
            def _():
                for cp, _, _ in task_copies():
                    cp().start()

        body(*ins, *outs, *scr)

        if tasks:
            @pl.when(last)
            def _():
                found = task_copies()
                for _, _, recv in found:
                    recv().wait_recv()
                for _, send, _ in found:
                    send().wait_send()

    sem_shapes = []
    for t in tasks:
        sem_shapes += [pltpu.SemaphoreType.DMA((t.n_sem,)), pltpu.SemaphoreType.DMA((t.n_sem,))]
    res = pl.pallas_call(
        wrapped, name=name, grid=grid,
        in_specs=list(in_specs) + [HBM_ANY] * (n_ro + n_rw),
        out_specs=list(out_specs) + [HBM_ANY] * (n_rw + n_new),
        out_shape=list(out_shape) + [jax.ShapeDtypeStruct(a.shape, a.dtype) for a in rw] + new,
        scratch_shapes=list(scratch_shapes) + sem_shapes,
        input_output_aliases={n_in + n_ro + i: n_out + i for i in range(n_rw)},
        compiler_params=pltpu.CompilerParams(dimension_semantics=("arbitrary",) * len(grid),
                                             vmem_limit_bytes=VMEM_LIMIT_V7X),
    )(*args, *ro, *rw)
    own, rest = list(res[:n_out]), list(res[n_out:])
    per_task, b, d = [], 0, n_rw
    for t in tasks:
        per_task.append((rest[b:b + len(t.rw)], rest[d:d + len(t.new)]))
        b, d = b + len(t.rw), d + len(t.new)
    return own, per_task


def _comm_only(name, tasks):
    ro = [a for t in tasks for a in t.ro]
    rw = [a for t in tasks for a in t.rw]
    new = [s for t in tasks for s in t.new]
    n_ro, n_rw, n_new = len(ro), len(rw), len(new)

    def body(*refs):
        ro_refs = refs[:n_ro]
        rw_refs = refs[n_ro + n_rw:n_ro + 2 * n_rw]
        new_refs = refs[n_ro + 2 * n_rw:n_ro + 2 * n_rw + n_new]
        sems = refs[n_ro + 2 * n_rw + n_new:]
        found, a, b, d = [], 0, 0, 0
        for i, t in enumerate(tasks):
            found += t.copies(ro_refs[a:a + len(t.ro)], rw_refs[b:b + len(t.rw)], new_refs[d:d + len(t.new)],
                              sems[2 * i], sems[2 * i + 1])
            a, b, d = a + len(t.ro), b + len(t.rw), d + len(t.new)
        for cp, _, _ in found:
            cp().start()
        for _, _, recv in found:
            recv().wait_recv()
        for _, send, _ in found:
            send().wait_send()

    sem_shapes = []
    for t in tasks:
        sem_shapes += [pltpu.SemaphoreType.DMA((t.n_sem,)), pltpu.SemaphoreType.DMA((t.n_sem,))]
    res = pl.pallas_call(
        body, name=name,
        in_specs=[HBM_ANY] * (n_ro + n_rw), out_specs=[HBM_ANY] * (n_rw + n_new),
        out_shape=[jax.ShapeDtypeStruct(a.shape, a.dtype) for a in rw] + new,
        scratch_shapes=sem_shapes,
        input_output_aliases={n_ro + i: i for i in range(n_rw)},
    )(*ro, *rw)
    res = list(res)
    per_task, b, d = [], 0, n_rw
    for t in tasks:
        per_task.append((res[b:b + len(t.rw)], res[d:d + len(t.new)]))
        b, d = b + len(t.rw), d + len(t.new)
    return per_task


def _inproj(x, norm_g, w_in, dm, tasks=()):
    S, D, N = dm.S, dm.D, dm.INW
    tm, tn = min(1024, S), 512

    def body(x_ref, g_ref, w_ref, z_ref, h_ref):
        @pl.when(pl.program_id(1) == 0)
        def _():
            xv = x_ref[...]
            r = lax.rsqrt(jnp.mean(xv * xv, axis=-1, keepdims=True) + RMS_EPS)
            h_ref[...] = _bf(xv * r * g_ref[...])

        z_ref[...] = _bf(_dot(h_ref[...], w_ref[...]))

    return _call(
        body, name="inproj", grid=(S // tm, N // tn),
        in_specs=[pl.BlockSpec((tm, D), lambda i, j: (i, 0)),
                  pl.BlockSpec((1, D), lambda i, j: (0, 0)),
                  pl.BlockSpec((None, D, tn), lambda i, j: (0, 0, j))],
        out_specs=[pl.BlockSpec((tm, tn), lambda i, j: (i, j)),
                   pl.BlockSpec((tm, D), lambda i, j: (i, 0))],
        out_shape=[jax.ShapeDtypeStruct((S, N), BF16), jax.ShapeDtypeStruct((S, D), BF16)],
        args=(x, norm_g, w_in), tasks=tasks)


def _rot_half(x, cs, sn):
    h = cs.shape[-1]
    x1, x2 = x[:, :h], x[:, h:]
    return jnp.concatenate([x1 * cs - x2 * sn, x2 * cs + x1 * sn], axis=-1)


def _ret_tables(af_ref, ab_ref):
    C = CHUNK
    lgf = -jnp.exp(af_ref[...])[:, :1]
    lgb = -jnp.exp(ab_ref[...])[:, :1]
    ri = lax.broadcasted_iota(jnp.int32, (C, C), 0)
    ci = lax.broadcasted_iota(jnp.int32, (C, C), 1)
    lag = _f32(ri - ci)
    alag = jnp.abs(lag)
    low = lag >= 0
    dmask = jnp.where(low, jnp.exp(lgf * alag), jnp.exp(lgb * alag))
    j = _f32(lax.broadcasted_iota(jnp.int32, (C, 1), 0))
    return dict(lgf=lgf, lgb=lgb, alag=alag, low=low, dmask=dmask, j=j,
                w_f=jnp.exp(lgf * (C - 1.0 - j)), w_b=jnp.exp(lgb * j),
                q_f=jnp.exp(lgf * (j + 1.0)), q_b=jnp.exp(lgb * (C - j)),
                dec_f=jnp.exp(lgf * C), dec_b=jnp.exp(lgb * C))


def _decay_rows(a):
    return jnp.broadcast_to(a.reshape(RET_HEADS, 1, 1), (RET_HEADS, 1, 128))


def _ret_fwd(z, a_f, a_b, cosr, sinr, dm):
    S, RD, C, nC = dm.S, dm.RD, CHUNK, dm.nC
    scale = RD ** -0.5

    def body(q_ref, k_ref, v_ref, g_ref, af_ref, ab_ref, cos_ref, sin_ref, o_ref, qh_s, kh_s, sf_s, st_s):
        t = _ret_tables(af_ref, ab_ref)
        st_s[...] = jnp.zeros_like(st_s)

        def fwd_pass(n, carry):
            r = pl.ds(pl.multiple_of(n * C, C), C)
            cs, sn = cos_ref[r, :], sin_ref[r, :]
            qh = _rot_half(_f32(q_ref[r, :]), cs, sn)
            kh = _rot_half(_f32(k_ref[r, :]), cs, sn) * scale
            qh_s[r, :] = _bf(qh)
            kh_s[r, :] = _bf(kh)
            st = st_s[...]
            sf_s[n] = _bf(st)
            st_s[...] = st * t["dec_f"] + _dot_tn(_bf(kh * t["w_f"]), v_ref[r, :])
            return carry

        _loop(nC, fwd_pass, 0)
        st_s[...] = jnp.zeros_like(st_s)

        def bwd_pass(i, carry):
            n = nC - 1 - i
            r = pl.ds(pl.multiple_of(n * C, C), C)
            qhb, khb, vb = qh_s[r, :], kh_s[r, :], v_ref[r, :]
            p = _bf(_dot_nt(qhb, khb) * t["dmask"])
            qhf = _f32(qhb)
            sb = st_s[...]
            out = (_dot(p, vb) + _dot(_bf(qhf * t["q_f"]), sf_s[n]) + _dot(_bf(qhf * t["q_b"]), _bf(sb)))
            y = out * lax.rsqrt(jnp.mean(out * out, axis=-1, keepdims=True) + RMS_EPS)
            o_ref[r, :] = _bf(y * _silu(_f32(g_ref[r, :])))
            st_s[...] = sb * t["dec_b"] + _dot_tn(_bf(_f32(khb) * t["w_b"]), vb)
            return carry

        _loop(nC, bwd_pass, 0)

    zs = lambda off: pl.BlockSpec((S, RD), lambda h: (0, off // RD + h))
    dec = pl.BlockSpec((None, 1, 128), lambda h: (h, 0, 0))
    tab = pl.BlockSpec((S, RD // 2), lambda h: (0, 0))
    own, _ = _call(
        body, name="ret_fwd", grid=(RET_HEADS,),
        in_specs=[zs(dm.o_rq), zs(dm.o_rk), zs(dm.o_rv), zs(dm.o_rg), dec, dec, tab, tab],
        out_specs=[pl.BlockSpec((S, RD), lambda h: (0, h))],
        out_shape=[jax.ShapeDtypeStruct((S, dm.BW), BF16)],
        scratch_shapes=[pltpu.VMEM((S, RD), BF16), pltpu.VMEM((S, RD), BF16),
                        pltpu.VMEM((nC, RD, RD), BF16), pltpu.VMEM((RD, RD), F32)],
        args=(z, z, z, z, _decay_rows(a_f), _decay_rows(a_b), cosr, sinr))
    return own[0]


def _band(first_row, first_col, lo, hi, shape):
    r = lax.broadcasted_iota(jnp.int32, shape, 0) + first_row
    c = lax.broadcasted_iota(jnp.int32, shape, 1) + first_col
    d = c - r
    return jnp.where((d >= lo) & (d <= hi), 1.0, 0.0).astype(BF16)


def _pool_counts(first_row, half, S):
    pos = lax.broadcasted_iota(jnp.int32, (CHUNK, 1), 0) + first_row
    lo = jnp.clip(pos - half, 0, S)
    hi = jnp.clip(pos + half, 0, S)
    return 1.0 / _f32(hi - lo)


def _win_start(n, S):
    return pl.multiple_of(jnp.clip((n - 1) * CHUNK, 0, S - 3 * CHUNK), CHUNK)


def _pool_fwd(z, pool_w, pool_scale, dm):
    S, PD, C, nC = dm.S, dm.PD, CHUNK, dm.nC

    def body(u_ref, g_ref, w_ref, sc_ref, o_ref):
        half = jnp.left_shift(1, pl.program_id(0))
        w = w_ref[...]
        sc = sc_ref[...]

        def blk(n, carry):
            row0 = pl.multiple_of(n * C, C)
            r = pl.ds(row0, C)
            st = _win_start(n, S)
            band = _band(row0, st, -half, half - 1, (C, 3 * C))
            mean = _dot(band, u_ref[pl.ds(st, 3 * C), :]) * _pool_counts(row0, half, S)
            p = mean - _f32(u_ref[r, :])
            y = _dot(_bf(p), w) * sc
            o_ref[r, :] = _bf(y * _silu(_f32(g_ref[r, :])))
            return carry

        _loop(nC, blk, 0)

    own, _ = _call(
        body, name="pool_fwd", grid=(POOL_GROUPS,),
        in_specs=[pl.BlockSpec((S, PD), lambda g: (0, dm.o_pv // PD + g)),
                  pl.BlockSpec((S, PD), lambda g: (0, dm.o_pg // PD + g)),
                  pl.BlockSpec((None, PD, PD), lambda g: (g, 0, 0)),
                  pl.BlockSpec((None, 1, PD), lambda g: (g, 0, 0))],
        out_specs=[pl.BlockSpec((S, PD), lambda g: (0, g))],
        out_shape=[jax.ShapeDtypeStruct((S, dm.BW), BF16)],
        args=(z, z, pool_w, pool_scale))
    return own[0]


def _rot_part(x, cs, s_up, s_dn):
    h = ROPE_DIMS // 2
    return x * cs + pltpu.roll(x, ATT_HEAD_DIM - h, 1) * s_up + pltpu.roll(x, h, 1) * s_dn


def _att_tables(S):
    h = ROPE_DIMS // 2
    inv = ROPE_THETA ** (-jnp.arange(h, dtype=F32) / h)
    ang = jnp.arange(S, dtype=F32)[:, None] * inv[None, :]
    cos, sin = jnp.cos(ang), jnp.sin(ang)
    pad = jnp.zeros((S, ATT_HEAD_DIM - 2 * h), F32)
    zero = jnp.zeros((S, h), F32)
    cs = jnp.concatenate([cos, cos, pad + 1.0], axis=1)
    s_up = jnp.concatenate([-sin, zero, pad], axis=1)
    s_dn = jnp.concatenate([zero, sin, pad], axis=1)
    return cs, s_up, s_dn


def _ret_rope_tables(S, RD):
    h = RD // 2
    inv = 1.0 / (RET_ROPE_BASE ** jnp.linspace(0.0, 1.0, h, dtype=F32))
    ang = jnp.arange(S, dtype=F32)[:, None] * inv[None, :]
    return jnp.cos(ang), jnp.sin(ang)


def _att_norm_rot(x, gain, cs, s_up, s_dn):
    u = x * lax.rsqrt(jnp.mean(x * x, axis=-1, keepdims=True) + RMS_EPS)
    return _rot_part(u * gain, cs, s_up, s_dn)


def _att_bias_tables(bias_s):
    C = CHUNK
    shape = (ATT_GROUP * C, 3 * C)
    rel = lax.broadcasted_iota(jnp.int32, shape, 1) - lax.broadcasted_iota(jnp.int32, shape, 0) % C
    for which, shift in enumerate((-C, 0, -2 * C)):
        bias_s[which] = jnp.where(jnp.abs(rel + shift) <= CHUNK, 0.0, NEG_BIG)


def _att_bias(bias_s, n, nC):
    return bias_s[jnp.where(n == 0, 1, jnp.where(n == nC - 1, 2, 0))]


def _att_probs(q4, kw, sink_col, bias):
    s = _dot_nt(q4, kw) * (ATT_HEAD_DIM ** -0.5) + bias
    m = jnp.maximum(jnp.max(s, axis=-1, keepdims=True), sink_col)
    e = jnp.exp(s - m)
    es = jnp.exp(sink_col - m)
    inv = 1.0 / (jnp.sum(e, axis=-1, keepdims=True) + es)
    return e * inv, es * inv


def _sink_col(sink_ref):
    head = lax.broadcasted_iota(jnp.int32, (ATT_GROUP * CHUNK, 1), 0) // CHUNK
    col = jnp.zeros((ATT_GROUP * CHUNK, 1), F32)
    for g in range(ATT_GROUP):
        col = jnp.where(head == g, sink_ref[g:g + 1, :1], col)
    return col


def _sink_rows(sink, dm):
    return jnp.broadcast_to(sink.reshape(dm.AKV, ATT_GROUP, 1), (dm.AKV, ATT_GROUP, 128))


def _att_prep(q_ref, k_ref, qg_ref, kg_ref, cs_ref, su_ref, sd_ref, qn_s, kn_s, nC):
    C, HD = CHUNK, ATT_HEAD_DIM

    def prep(n, carry):
        r = pl.ds(pl.multiple_of(n * C, C), C)
        cs, su, sd = cs_ref[r, :], su_ref[r, :], sd_ref[r, :]
        kn_s[r, :] = _bf(_att_norm_rot(_f32(k_ref[r, :]), kg_ref[...], cs, su, sd))
        for g in range(ATT_GROUP):
            qn_s[g, r, :] = _bf(_att_norm_rot(_f32(q_ref[r, g * HD:(g + 1) * HD]), qg_ref[...], cs, su, sd))
        return carry

    lax.fori_loop(0, nC, prep, 0)


def _att_fwd(z, q_gain, k_gain, sink, tabs, dm, tasks=()):
    S, C, nC, HD, G = dm.S, CHUNK, dm.nC, ATT_HEAD_DIM, ATT_GROUP
    GW = G * HD

    def body(q_ref, k_ref, v_ref, g_ref, qg_ref, kg_ref, sk_ref, cs_ref, su_ref, sd_ref, o_ref, qn_s, kn_s, bias_s):
        _att_prep(q_ref, k_ref, qg_ref, kg_ref, cs_ref, su_ref, sd_ref, qn_s, kn_s, nC)
        _att_bias_tables(bias_s)
        sink_col = _sink_col(sk_ref)

        def blk(n, carry):
            row0 = pl.multiple_of(n * C, C)
            r = pl.ds(row0, C)
            w = pl.ds(_win_start(n, S), 3 * C)
            q4 = qn_s[:, r, :].reshape(G * C, HD)
            p, _ = _att_probs(q4, kn_s[w, :], sink_col, _att_bias(bias_s, n, nC))
            o = _dot(_bf(p), v_ref[w, :])
            for g in range(G):
                gate = _silu(_f32(g_ref[r, g * HD:(g + 1) * HD]))
                o_ref[r, g * HD:(g + 1) * HD] = _bf(o[g * C:(g + 1) * C, :] * gate)
            return carry

        _loop(nC, blk, 0)

    tab = pl.BlockSpec((S, HD), lambda h: (0, 0), pipeline_mode=ONE_BUFFER)
    gain = pl.BlockSpec((1, HD), lambda h: (0, 0))
    return _call(
        body, name="att_fwd", grid=(dm.AKV,),
        in_specs=[pl.BlockSpec((S, GW), lambda h: (0, dm.o_aq // GW + h)),
                  pl.BlockSpec((S, HD), lambda h: (0, dm.o_ak // HD + h)),
                  pl.BlockSpec((S, HD), lambda h: (0, dm.o_av // HD + h)),
                  pl.BlockSpec((S, GW), lambda h: (0, dm.o_ag // GW + h)),
                  gain, gain, pl.BlockSpec((None, G, 128), lambda h: (h, 0, 0)), tab, tab, tab],
        out_specs=[pl.BlockSpec((S, GW), lambda h: (0, h))],
        out_shape=[jax.ShapeDtypeStruct((S, dm.BW), BF16)],
        scratch_shapes=[pltpu.VMEM((G, S, HD), BF16), pltpu.VMEM((S, HD), BF16), pltpu.VMEM((3, G * C, 3 * C), F32)],
        args=(z, z, z, z, q_gain, k_gain, _sink_rows(sink, dm), *tabs), tasks=tasks)


def _merge_gate_spec(tm, dm):
    return pl.BlockSpec((pl.Element(tm), pl.Element(N_BRANCHES * dm.D)), lambda i: (i * tm, dm.o_mg))


def _merge_out(x, mg, ya_pre, yb_pre, yc_pre, w_ret, w_pool, w_att, w_out, dm, tasks=()):
    S, D, BW = dm.S, dm.D, dm.BW
    tm = min(256, S)

    def body(x_ref, mg_ref, a_ref, b_ref, c_ref, wr_ref, wp_ref, wa_ref, wo_ref, xo_ref, ya_ref, yb_ref, yc_ref, m_ref):
        ya = _dot(a_ref[...], wr_ref[...])
        yb = _dot(b_ref[...], wp_ref[...])
        yc = _dot(c_ref[...], wa_ref[...])
        ya_ref[...], yb_ref[...], yc_ref[...] = _bf(ya), _bf(yb), _bf(yc)
        g0 = _sigmoid(_f32(mg_ref[:, 0:D]))
        g1 = _sigmoid(_f32(mg_ref[:, D:2 * D]))
        g2 = _sigmoid(_f32(mg_ref[:, 2 * D:3 * D]))
        merged = _bf(g0 * ya + g1 * yb + g2 * yc)
        m_ref[...] = merged
        xo_ref[...] = x_ref[...] + _dot(merged, wo_ref[...])

    act = lambda w: pl.BlockSpec((tm, w), lambda i: (i, 0))
    wsp = lambda r: pl.BlockSpec((None, r, D), lambda i: (0, 0, 0), pipeline_mode=ONE_BUFFER)
    return _call(
        body, name="merge_out", grid=(S // tm,),
        in_specs=[act(D), _merge_gate_spec(tm, dm), act(BW), act(BW), act(BW), wsp(BW), wsp(BW), wsp(BW), wsp(D)],
        out_specs=[act(D)] * 5,
        out_shape=[jax.ShapeDtypeStruct((S, D), F32)] + [jax.ShapeDtypeStruct((S, D), BF16)] * 4,
        args=(x, mg, ya_pre, yb_pre, yc_pre, w_ret, w_pool, w_att, w_out), tasks=tasks)


def _loss_grad(y, target, dm):
    S, D = dm.S, dm.D
    tm = dm.tm

    def body(y_ref, t_ref, s_ref, d_ref):
        @pl.when(pl.program_id(0) == 0)
        def _():
            s_ref[...] = jnp.zeros_like(s_ref)

        e = y_ref[...] - t_ref[...]
        d_ref[...] = e * (1.0 / D)
        s_ref[...] += jnp.sum(jnp.sum(e * e, axis=-1, keepdims=True), axis=0, keepdims=True)

    row = pl.BlockSpec((tm, D), lambda i: (i, 0))
    own, _ = _call(
        body, name="loss_grad", grid=(S // tm,), in_specs=[row, row],
        out_specs=[pl.BlockSpec((1, 128), lambda i: (0, 0)), row],
        out_shape=[jax.ShapeDtypeStruct((1, 128), F32), jax.ShapeDtypeStruct((S, D), F32)],
        args=(y, target))
    return own


def _merge_bwd_gates(dx, mg, ya, yb, yc, w_out, dm, tasks=()):
    S, D = dm.S, dm.D
    tm = min(256, S)

    def body(dx_ref, mg_ref, ya_ref, yb_ref, yc_ref, wo_ref, dmg_ref, dya_ref, dyb_ref, dyc_ref, dxb_ref):
        dxb = _bf(dx_ref[...])
        dxb_ref[...] = dxb
        dm_ = _dot_nt(dxb, wo_ref[...])
        for k, (y_ref, dy_ref) in enumerate(((ya_ref, dya_ref), (yb_ref, dyb_ref), (yc_ref, dyc_ref))):
            g = _sigmoid(_f32(mg_ref[:, k * D:(k + 1) * D]))
            dmg_ref[:, k * D:(k + 1) * D] = _bf(dm_ * _f32(y_ref[...]) * g * (1.0 - g))
            dy_ref[...] = _bf(dm_ * g)

    act = lambda w: pl.BlockSpec((tm, w), lambda i: (i, 0))
    return _call(
        body, name="merge_bwd_gates", grid=(S // tm,),
        in_specs=[act(D), _merge_gate_spec(tm, dm), act(D), act(D), act(D),
                  pl.BlockSpec((None, D, D), lambda i: (0, 0, 0), pipeline_mode=ONE_BUFFER)],
        out_specs=[act(N_BRANCHES * D), act(D), act(D), act(D), act(D)],
        out_shape=[jax.ShapeDtypeStruct((S, N_BRANCHES * D), BF16)] + [jax.ShapeDtypeStruct((S, D), BF16)] * 4,
        args=(dx, mg, ya, yb, yc, w_out), tasks=tasks)


def _merge_bwd_proj(dya, dyb, dyc, w_ret, w_pool, w_att, dm):
    S, D, BW = dm.S, dm.D, dm.BW
    tm = dm.tm

    def body(da_ref, db_ref, dc_ref, wr_ref, wp_ref, wa_ref, oa_ref, ob_ref, oc_ref):
        oa_ref[...] = _bf(_dot_nt(da_ref[...], wr_ref[...]))
        ob_ref[...] = _bf(_dot_nt(db_ref[...], wp_ref[...]))
        oc_ref[...] = _bf(_dot_nt(dc_ref[...], wa_ref[...]))

    act = lambda w: pl.BlockSpec((tm, w), lambda i: (i, 0))
    wsp = pl.BlockSpec((None, BW, D), lambda i: (0, 0, 0), pipeline_mode=ONE_BUFFER)
    own, _ = _call(
        body, name="merge_bwd_proj", grid=(S // tm,),
        in_specs=[act(D)] * 3 + [wsp] * 3, out_specs=[act(BW)] * 3,
        out_shape=[jax.ShapeDtypeStruct((S, BW), BF16)] * 3,
        args=(dya, dyb, dyc, w_ret, w_pool, w_att))
    return own


def _ret_bwd(z, dya_pre, a_f, a_b, cosr, sinr, dm, tasks=()):
    S, RD, C, nC = dm.S, dm.RD, CHUNK, dm.nC
    scale = RD ** -0.5
    H = RET_HEADS

    def body(q_ref, k_ref, v_ref, g_ref, dy_ref, af_ref, ab_ref, cos_ref, sin_ref,
             dq_ref, dk_ref, dv_ref, dg_ref, dd_ref,
             qh_s, kh_s, sf_s, sb_s, do_s, dqh_s, dkh_s, dv_s, st_s, lam_s):
        t = _ret_tables(af_ref, ab_ref)
        zero_rr = jnp.zeros((RD, RD), F32)

        st_s[...] = zero_rr

        def pass0(n, carry):
            r = pl.ds(pl.multiple_of(n * C, C), C)
            cs, sn = cos_ref[r, :], sin_ref[r, :]
            qh = _rot_half(_f32(q_ref[r, :]), cs, sn)
            kh = _rot_half(_f32(k_ref[r, :]), cs, sn) * scale
            qh_s[r, :] = _bf(qh)
            kh_s[r, :] = _bf(kh)
            st = st_s[...]
            sf_s[n] = _bf(st)
            st_s[...] = st * t["dec_f"] + _dot_tn(_bf(kh * t["w_f"]), v_ref[r, :])
            return carry

        _loop(nC, pass0, 0)

        st_s[...] = zero_rr
        lam_s[...] = zero_rr

        def pass1(i, acc):
            glf, glb = acc
            n = nC - 1 - i
            r = pl.ds(pl.multiple_of(n * C, C), C)
            qhb, khb, vb = qh_s[r, :], kh_s[r, :], v_ref[r, :]
            qhf, khf = _f32(qhb), _f32(khb)
            sc_ = _dot_nt(qhb, khb)
            p = _bf(sc_ * t["dmask"])
            sb = st_s[...]
            sbb = _bf(sb)
            sb_s[n] = sbb
            sfb = sf_s[n]
            qf_b, qb_b = _bf(qhf * t["q_f"]), _bf(qhf * t["q_b"])
            out = _dot(p, vb) + _dot(qf_b, sfb) + _dot(qb_b, sbb)
            rr = lax.rsqrt(jnp.mean(out * out, axis=-1, keepdims=True) + RMS_EPS)
            y = out * rr
            g = _f32(g_ref[r, :])
            dya = _f32(dy_ref[r, :])
            dg_ref[r, :] = _bf(dya * y * _dsilu(g))
            dyn = dya * _silu(g)
            dout = rr * (dyn - y * jnp.mean(dyn * y, axis=-1, keepdims=True))
            dob = _bf(dout)
            do_s[r, :] = dob
            dp = _dot_nt(dob, vb)
            dv = _dot_tn(p, dob)
            ds = _bf(dp * t["dmask"])
            dqh = _dot(ds, khb)
            dkh = _dot_tn(ds, qhb)
            dd = dp * sc_ * t["dmask"] * t["alag"]
            glf = glf + _sum11(jnp.where(t["low"], dd, 0.0))
            glb = glb + _sum11(jnp.where(t["low"], 0.0, dd))
            tf = _dot_nt(dob, sfb)
            tb = _dot_nt(dob, sbb)
            dqh = dqh + tf * t["q_f"] + tb * t["q_b"]
            glf = glf + _sum11(jnp.sum(tf * qhf, axis=-1, keepdims=True) * t["q_f"] * (t["j"] + 1.0))
            glb = glb + _sum11(jnp.sum(tb * qhf, axis=-1, keepdims=True) * t["q_b"] * (C - t["j"]))
            lam = lam_s[...]
            lamb = _bf(lam)
            glf = glf + _sum11(lam * _f32(sfb)) * t["dec_f"] * C
            u = _dot_nt(vb, lamb)
            dkh = dkh + u * t["w_f"]
            glf = glf + _sum11(jnp.sum(u * khf, axis=-1, keepdims=True) * t["w_f"] * (C - 1.0 - t["j"]))
            dv = dv + _dot(_bf(khf * t["w_f"]), lamb)
            lam_s[...] = _dot_tn(qf_b, dob) + lam * t["dec_f"]
            dqh_s[r, :] = dqh
            dkh_s[r, :] = dkh
            dv_s[r, :] = dv
            st_s[...] = sb * t["dec_b"] + _dot_tn(_bf(khf * t["w_b"]), vb)
            return glf, glb

        glf, glb = _loop(nC, pass1, (jnp.zeros((1, 1), F32), jnp.zeros((1, 1), F32)))

        lam_s[...] = zero_rr

        def pass2(n, glb):
            r = pl.ds(pl.multiple_of(n * C, C), C)
            qhb, khb, vb, dob = qh_s[r, :], kh_s[r, :], v_ref[r, :], do_s[r, :]
            qhf, khf = _f32(qhb), _f32(khb)
            lam = lam_s[...]
            lamb = _bf(lam)
            glb = glb + _sum11(lam * _f32(sb_s[n])) * t["dec_b"] * C
            u = _dot_nt(vb, lamb)
            dkh = dkh_s[r, :] + u * t["w_b"]
            glb = glb + _sum11(jnp.sum(u * khf, axis=-1, keepdims=True) * t["w_b"] * t["j"])
            dv = dv_s[r, :] + _dot(_bf(khf * t["w_b"]), lamb)
            lam_s[...] = _dot_tn(_bf(qhf * t["q_b"]), dob) + lam * t["dec_b"]
            cs, sn = cos_ref[r, :], sin_ref[r, :]
            dq_ref[r, :] = _bf(_rot_half(dqh_s[r, :], cs, -sn))
            dk_ref[r, :] = _bf(_rot_half(dkh * scale, cs, -sn))
            dv_ref[r, :] = _bf(dv)
            return glb

        glb = _loop(nC, pass2, glb)
        row = lax.broadcasted_iota(jnp.int32, (8, 128), 0)
        da_f = glf * t["lgf"]
        da_b = glb * t["lgb"]
        dd_ref[...] = jnp.where(row == 0, da_f, jnp.where(row == 1, da_b, 0.0))

    zs = lambda off: pl.BlockSpec((S, RD), lambda h: (0, off // RD + h), pipeline_mode=ONE_BUFFER)
    col = pl.BlockSpec((S, RD), lambda h: (0, h), pipeline_mode=ONE_BUFFER)
    dec = pl.BlockSpec((None, 1, 128), lambda h: (h, 0, 0))
    tab = pl.BlockSpec((S, RD // 2), lambda h: (0, 0), pipeline_mode=ONE_BUFFER)
    return _call(
        body, name="ret_bwd", grid=(H,),
        in_specs=[zs(dm.o_rq), zs(dm.o_rk), zs(dm.o_rv), zs(dm.o_rg), col, dec, dec, tab, tab],
        out_specs=[col] * 4 + [pl.BlockSpec((None, 8, 128), lambda h: (h, 0, 0))],
        out_shape=[jax.ShapeDtypeStruct((S, dm.BW), BF16)] * 4 + [jax.ShapeDtypeStruct((H, 8, 128), F32)],
        scratch_shapes=[pltpu.VMEM((S, RD), BF16), pltpu.VMEM((S, RD), BF16),
                        pltpu.VMEM((nC, RD, RD), BF16), pltpu.VMEM((nC, RD, RD), BF16),
                        pltpu.VMEM((S, RD), BF16),
                        pltpu.VMEM((S, RD), F32), pltpu.VMEM((S, RD), F32), pltpu.VMEM((S, RD), F32),
                        pltpu.VMEM((RD, RD), F32), pltpu.VMEM((RD, RD), F32)],
        args=(z, z, z, z, dya_pre, _decay_rows(a_f), _decay_rows(a_b), cosr, sinr), tasks=tasks)


def _pool_bwd(z, dyb_pre, pool_w, pool_scale, dm):
    S, PD, C, nC = dm.S, dm.PD, CHUNK, dm.nC

    def body(u_ref, g_ref, dy_ref, w_ref, sc_ref, du_ref, dg_ref, dw_ref, dsc_ref, dp_s, dpc_s, dw_s):
        half = jnp.left_shift(1, pl.program_id(0))
        w = w_ref[...]
        sc = sc_ref[...]
        dw_s[...] = jnp.zeros_like(dw_s)
        dsc_ref[...] = jnp.zeros_like(dsc_ref)

        def blk1(n, carry):
            row0 = pl.multiple_of(n * C, C)
            r = pl.ds(row0, C)
            st = _win_start(n, S)
            band = _band(row0, st, -half, half - 1, (C, 3 * C))
            inv = _pool_counts(row0, half, S)
            pb = _bf(_dot(band, u_ref[pl.ds(st, 3 * C), :]) * inv - _f32(u_ref[r, :]))
            ylin = _dot(pb, w)
            y = ylin * sc
            g = _f32(g_ref[r, :])
            dyb = _f32(dy_ref[r, :])
            dg_ref[r, :] = _bf(dyb * y * _dsilu(g))
            dy = dyb * _silu(g)
            dsc_ref[...] += jnp.sum(dy * ylin, axis=0, keepdims=True)
            dyl = _bf(dy * sc)
            dw_s[...] += _dot_tn(pb, dyl)
            dp = _dot_nt(dyl, w)
            dp_s[r, :] = dp
            dpc_s[r, :] = _bf(dp * inv)
            return carry

        _loop(nC, blk1, 0)
        dw_ref[...] = _bf(dw_s[...])

        def blk2(n, carry):
            row0 = pl.multiple_of(n * C, C)
            r = pl.ds(row0, C)
            st = _win_start(n, S)
            band_t = _band(row0, st, -half + 1, half, (C, 3 * C))
            du_ref[r, :] = _bf(_dot(band_t, dpc_s[pl.ds(st, 3 * C), :]) - dp_s[r, :])
            return carry

        _loop(nC, blk2, 0)

    G = POOL_GROUPS
    own, _ = _call(
        body, name="pool_bwd", grid=(G,),
        in_specs=[pl.BlockSpec((S, PD), lambda g: (0, dm.o_pv // PD + g)),
                  pl.BlockSpec((S, PD), lambda g: (0, dm.o_pg // PD + g)),
                  pl.BlockSpec((S, PD), lambda g: (0, g)),
                  pl.BlockSpec((None, PD, PD), lambda g: (g, 0, 0)),
                  pl.BlockSpec((None, 1, PD), lambda g: (g, 0, 0))],
        out_specs=[pl.BlockSpec((S, PD), lambda g: (0, g)), pl.BlockSpec((S, PD), lambda g: (0, g)),
                   pl.BlockSpec((None, PD, PD), lambda g: (g, 0, 0)), pl.BlockSpec((None, 1, PD), lambda g: (g, 0, 0))],
        out_shape=[jax.ShapeDtypeStruct((S, dm.BW), BF16), jax.ShapeDtypeStruct((S, dm.BW), BF16),
                   jax.ShapeDtypeStruct((G, PD, PD), BF16), jax.ShapeDtypeStruct((G, 1, PD), F32)],
        scratch_shapes=[pltpu.VMEM((S, PD), F32), pltpu.VMEM((S, PD), BF16), pltpu.VMEM((PD, PD), F32)],
        args=(z, z, dyb_pre, pool_w, pool_scale))
    return own


def _att_bwd(z, dyc_pre, q_gain, k_gain, sink, tabs, dm, tasks=()):
    S, C, nC, HD, G = dm.S, CHUNK, dm.nC, ATT_HEAD_DIM, ATT_GROUP
    GW = G * HD
    scale = HD ** -0.5

    def body(q_ref, k_ref, v_ref, g_ref, dy_ref, qg_ref, kg_ref, sk_ref, cs_ref, su_ref, sd_ref,
             dq_ref, dk_ref, dv_ref, dg_ref, sm_ref, qn_s, kn_s, dqn_s, dkn_s, dv_s, bias_s):
        _att_prep(q_ref, k_ref, qg_ref, kg_ref, cs_ref, su_ref, sd_ref, qn_s, kn_s, nC)
        _att_bias_tables(bias_s)
        dkn_s[...] = jnp.zeros_like(dkn_s)
        dv_s[...] = jnp.zeros_like(dv_s)
        sink_col = _sink_col(sk_ref)

        def blk(n, dsink):
            row0 = pl.multiple_of(n * C, C)
            r = pl.ds(row0, C)
            w = pl.ds(_win_start(n, S), 3 * C)
            q4 = qn_s[:, r, :].reshape(G * C, HD)
            kw, vw = kn_s[w, :], v_ref[w, :]
            p, ps = _att_probs(q4, kw, sink_col, _att_bias(bias_s, n, nC))
            pb = _bf(p)
            o = _dot(pb, vw)
            do_parts = []
            for g in range(G):
                cols = slice(g * HD, (g + 1) * HD)
                gate = _f32(g_ref[r, cols])
                dy = _f32(dy_ref[r, cols])
                dg_ref[r, cols] = _bf(dy * o[g * C:(g + 1) * C, :] * _dsilu(gate))
                do_parts.append(dy * _silu(gate))
            dob = _bf(jnp.concatenate(do_parts, axis=0))
            dp = _dot_nt(dob, vw)
            drow = jnp.sum(p * dp, axis=-1, keepdims=True)
            ds = _bf(p * (dp - drow))
            dsink = dsink - ps * drow
            dqn_s[:, r, :] = (_dot(ds, kw) * scale).reshape(G, C, HD)
            dkn_s[w, :] += _dot_tn(ds, q4) * scale
            dv_s[w, :] += _dot_tn(pb, dob)
            return dsink

        dsink = _loop(nC, blk, jnp.zeros((G * C, 1), F32))

        def fin(n, acc):
            dqg, dkg = acc
            r = pl.ds(pl.multiple_of(n * C, C), C)
            cs, su, sd = cs_ref[r, :], su_ref[r, :], sd_ref[r, :]

            def norm_bwd(x, gain, dqn):
                rr = lax.rsqrt(jnp.mean(x * x, axis=-1, keepdims=True) + RMS_EPS)
                u = x * rr
                dw = _rot_part(dqn, cs, -su, -sd)
                du = dw * gain
                return rr * (du - u * jnp.mean(du * u, axis=-1, keepdims=True)), jnp.sum(dw * u, axis=0, keepdims=True)

            dk, gk = norm_bwd(_f32(k_ref[r, :]), kg_ref[...], dkn_s[r, :])
            dk_ref[r, :] = _bf(dk)
            dv_ref[r, :] = _bf(dv_s[r, :])
            dkg = dkg + gk
            for g in range(G):
                cols = slice(g * HD, (g + 1) * HD)
                dq, gq = norm_bwd(_f32(q_ref[r, cols]), qg_ref[...], dqn_s[g, r, :])
                dq_ref[r, cols] = _bf(dq)
                dqg = dqg + gq
            return dqg, dkg

        dqg, dkg = lax.fori_loop(0, nC, fin, (jnp.zeros((1, HD), F32), jnp.zeros((1, HD), F32)))
        sm_ref[...] = jnp.zeros_like(sm_ref)
        sm_ref[0:1, :] = dqg
        sm_ref[1:2, :] = dkg
        for g in range(G):
            sm_ref[2 + g:3 + g, :] = jnp.broadcast_to(_sum11(dsink[g * C:(g + 1) * C, :]), (1, HD))

    tab = pl.BlockSpec((S, HD), lambda h: (0, 0), pipeline_mode=ONE_BUFFER)
    gain = pl.BlockSpec((1, HD), lambda h: (0, 0))
    wide = lambda off: pl.BlockSpec((S, GW), lambda h: (0, off // GW + h), pipeline_mode=ONE_BUFFER)
    thin = lambda off: pl.BlockSpec((S, HD), lambda h: (0, off // HD + h), pipeline_mode=ONE_BUFFER)
    return _call(
        body, name="att_bwd", grid=(dm.AKV,),
        in_specs=[wide(dm.o_aq), thin(dm.o_ak), thin(dm.o_av), wide(dm.o_ag), wide(0), gain, gain,
                  pl.BlockSpec((None, G, 128), lambda h: (h, 0, 0)), tab, tab, tab],
        out_specs=[wide(0), thin(0), thin(0), wide(0), pl.BlockSpec((None, 8, 128), lambda h: (h, 0, 0))],
        out_shape=[jax.ShapeDtypeStruct((S, dm.BW), BF16), jax.ShapeDtypeStruct((S, dm.AKW), BF16),
                   jax.ShapeDtypeStruct((S, dm.AKW), BF16), jax.ShapeDtypeStruct((S, dm.BW), BF16),
                   jax.ShapeDtypeStruct((dm.AKV, 8, 128), F32)],
        scratch_shapes=[pltpu.VMEM((G, S, HD), BF16), pltpu.VMEM((S, HD), BF16),
                        pltpu.VMEM((G, S, HD), F32), pltpu.VMEM((S, HD), F32), pltpu.VMEM((S, HD), F32),
                        pltpu.VMEM((3, G * C, 3 * C), F32)],
        args=(z, z, z, z, dyc_pre, q_gain, k_gain, _sink_rows(sink, dm), *tabs), tasks=tasks)


def _grad_matmul(a, b, name, tasks=()):
    S, M = a.shape
    N = b.shape[1]
    tm, tn = min(1024, M), min(512, N)

    def body(a_ref, b_ref, o_ref):
        o_ref[...] = _bf(_dot_tn(a_ref[...], b_ref[...]))

    own, tk = _call(
        body, name=name, grid=(M // tm, N // tn),
        in_specs=[pl.BlockSpec((S, tm), lambda i, j: (0, i)), pl.BlockSpec((S, tn), lambda i, j: (0, j))],
        out_specs=[pl.BlockSpec((None, tm, tn), lambda i, j: (0, i, j))],
        out_shape=[jax.ShapeDtypeStruct((1, M, N), BF16)],
        args=(a, b), tasks=tasks)
    return own[0], tk


def _inproj_bwd(dz, w_in, dm, tasks=()):
    S, D = dm.S, dm.D
    tm = dm.tm
    tk = dm.INW // N_CHIPS if (dm.INW // N_CHIPS) % 128 == 0 else 512
    nk = dm.INW // tk

    def body(dz_ref, w_ref, dh_ref):
        part = _dot_nt(dz_ref[...], w_ref[...])

        @pl.when(pl.program_id(1) == 0)
        def _():
            dh_ref[...] = part

        @pl.when(pl.program_id(1) != 0)
        def _():
            dh_ref[...] += part

    return _call(
        body, name="inproj_bwd", grid=(S // tm, nk),
        in_specs=[pl.BlockSpec((tm, tk), lambda i, k: (i, k)), pl.BlockSpec((None, D, tk), lambda i, k: (0, 0, k))],
        out_specs=[pl.BlockSpec((tm, D), lambda i, k: (i, 0))],
        out_shape=[jax.ShapeDtypeStruct((S, D), F32)],
        args=(dz, w_in), tasks=tasks)


def _norm_bwd(dh, x, norm_g, dx_out, dm):
    S, D = dm.S, dm.D
    tm = dm.tm

    def body(dh_ref, x_ref, g_ref, dxo_ref, dx_ref, dg_ref):
        @pl.when(pl.program_id(0) == 0)
        def _():
            dg_ref[...] = jnp.zeros_like(dg_ref)

        xv = x_ref[...]
        rr = lax.rsqrt(jnp.mean(xv * xv, axis=-1, keepdims=True) + RMS_EPS)
        u = xv * rr
        dh = dh_ref[...]
        dg_ref[...] += jnp.sum(dh * u, axis=0, keepdims=True)
        du = dh * g_ref[...]
        dx_ref[...] = dxo_ref[...] + rr * (du - u * jnp.mean(du * u, axis=-1, keepdims=True))

    row = pl.BlockSpec((tm, D), lambda i: (i, 0))
    vec = pl.BlockSpec((1, D), lambda i: (0, 0))
    own, _ = _call(
        body, name="norm_bwd", grid=(S // tm,), in_specs=[row, row, vec, row], out_specs=[row, vec],
        out_shape=[jax.ShapeDtypeStruct((S, D), F32), jax.ShapeDtypeStruct((1, D), F32)],
        args=(dh, x, norm_g, dx_out))
    return own


def _row_block(rows, width, itemsize):
    target = max(16, (2 * 1024 * 1024) // (width * itemsize))
    for rb in range(min(rows, target), 0, -1):
        if rows % rb == 0 and (rb % 16 == 0 or rb == rows):
            return rb
    return rows


def _prefetch_call(body, *, name, grid, in_specs, out_specs, out_shape, args, aliases=None):
    grid_spec = pltpu.PrefetchScalarGridSpec(num_scalar_prefetch=1, grid=grid, in_specs=in_specs, out_specs=out_specs)
    return pl.pallas_call(
        body, name=name, grid_spec=grid_spec, out_shape=out_shape, input_output_aliases=aliases or {},
        compiler_params=pltpu.CompilerParams(dimension_semantics=("arbitrary",) * len(grid),
                                             vmem_limit_bytes=VMEM_LIMIT_V7X),
    )(*args)


def _place_shard(w, l, spec, chip):
    A, rows, width = spec.shard
    rb = _row_block(rows, width, 4)
    nrb = rows // rb
    if spec.kind == "col":
        out_map = lambda a, r, chip: (a, r, chip[0])
    else:
        out_map = lambda a, r, chip: (a, chip[0] * nrb + r, 0)

    def body(chip_ref, w_ref, o_ref):
        o_ref[...] = _bf(w_ref[...])

    return _prefetch_call(
        body, name="place_" + spec.name, grid=(A, nrb),
        in_specs=[pl.BlockSpec((None, rb, width), lambda a, r, chip: (l * A + a, r, 0))],
        out_specs=pl.BlockSpec((None, rb, width), out_map),
        out_shape=jax.ShapeDtypeStruct(spec.full, BF16), args=(chip, w))


def _pair_sum(grad, land, spec, core):
    A, hr, w = spec.half
    rb = _row_block(hr, w, 2)
    nrb = hr // rb
    if spec.kind == "col":
        g_spec = pl.BlockSpec((None, rb, w), lambda j, a, r, core: (a, core[0] * nrb + r, j))
    else:
        g_spec = pl.BlockSpec((None, rb, w), lambda j, a, r, core: (a, (j * 2 + core[0]) * nrb + r, 0))

    def body(core_ref, g_ref, l_ref, o_ref):
        o_ref[...] = _bf(_f32(g_ref[...]) + _f32(l_ref[...]))

    blk = pl.BlockSpec((None, None, rb, w), lambda j, a, r, core: (j, a, r, 0))
    return _prefetch_call(
        body, name="pair_sum_" + spec.name, grid=(N_CHIPS, A, nrb), in_specs=[g_spec, blk], out_specs=blk,
        out_shape=jax.ShapeDtypeStruct((N_CHIPS,) + spec.half, BF16), args=(core, grad, land))


def _chip_sum(pair_sum, land, spec, chip_core):
    A, hr, w = spec.half
    rb = _row_block(hr, w, 4)
    nrb = hr // rb

    def body(cc_ref, p_ref, l0_ref, l1_ref, l2_ref, o_ref):
        o_ref[...] = ((_f32(p_ref[...]) + _f32(l0_ref[...])) + _f32(l1_ref[...])) + _f32(l2_ref[...])

    own = pl.BlockSpec((None, None, rb, w), lambda a, r, cc: (cc[0], a, r, 0))
    slot = lambda p: pl.BlockSpec((None, None, rb, w), lambda a, r, cc: (p, a, r, 0))
    return _prefetch_call(
        body, name="chip_sum_" + spec.name, grid=(A, nrb), in_specs=[own, slot(0), slot(1), slot(2)],
        out_specs=pl.BlockSpec((None, rb, w), lambda a, r, cc: (a, cc[1] * nrb + r, 0)),
        out_shape=jax.ShapeDtypeStruct(spec.shard, F32), args=(chip_core, pair_sum, land, land, land))


def _adamw_math(w, g, m, v):
    m = ADAM_B1 * m + (1.0 - ADAM_B1) * g
    v = ADAM_B2 * v + (1.0 - ADAM_B2) * (g * g)
    m_hat = m / (1.0 - ADAM_B1 ** ADAM_STEP)
    v_hat = v / (1.0 - ADAM_B2 ** ADAM_STEP)
    delta = -ADAM_LR * (m_hat / (jnp.sqrt(v_hat) + ADAM_EPS) + ADAM_WD * w)
    return delta, m, v


def _adamw(w, g, m, v, l, depth, spec, carried):
    A, R, C = spec.shard
    rb = _row_block(R, C, 4 * 4)
    stacked = pl.BlockSpec((None, rb, C), lambda a, r: (l * A + a, r, 0))
    n_carry = 0 if carried is None else 4

    def body(w_ref, g_ref, m_ref, v_ref, *rest):
        go_ref, d_ref, mo_ref, vo_ref = rest[n_carry:]
        g = g_ref[...]
        go_ref[...] = g
        d_ref[...], mo_ref[...], vo_ref[...] = _adamw_math(w_ref[...], g, m_ref[...], v_ref[...])

    return pl.pallas_call(
        body, name="adamw_" + spec.name, grid=(A, R // rb),
        in_specs=[stacked, pl.BlockSpec((None, rb, C), lambda a, r: (a, r, 0)), stacked, stacked] + [HBM_ANY] * n_carry,
        out_specs=[stacked] * 4,
        out_shape=[jax.ShapeDtypeStruct((depth * A, R, C), F32)] * 4,
        input_output_aliases={4 + i: i for i in range(n_carry)},
        compiler_params=pltpu.CompilerParams(dimension_semantics=("arbitrary", "arbitrary"),
                                             vmem_limit_bytes=VMEM_LIMIT_V7X),
    )(w, g, m, v, *(carried or ()))


def _small_update(g_part, w, m, v):
    R = g_part.shape[0]
    n_dev = 8

    def body(g_ref, w_ref, m_ref, v_ref, go_ref, d_ref, mo_ref, vo_ref, all_s, send_sem, recv_sem):
        x, y, c = _place()
        me = 4 * x + 2 * y + c
        all_s[me] = g_ref[...]
        cps = []
        for k in range(1, n_dev):
            peer = (x ^ ((k >> 2) & 1), y ^ ((k >> 1) & 1), c ^ (k & 1))
            cp = pltpu.make_async_remote_copy(src_ref=g_ref, dst_ref=all_s.at[me], send_sem=send_sem.at[k],
                                              recv_sem=recv_sem.at[k], device_id=peer, device_id_type=MESH)
            cp.start()
            cps.append(cp)
        for cp in cps:
            cp.wait()
        g = all_s[0]
        for d in range(1, n_dev):
            g = g + all_s[d]
        go_ref[...] = g
        d_ref[...], mo_ref[...], vo_ref[...] = _adamw_math(w_ref[...], g, m_ref[...], v_ref[...])

    vm = pl.BlockSpec(memory_space=pltpu.VMEM)
    return pl.pallas_call(
        body, name="small_update", in_specs=[vm] * 4, out_specs=[vm] * 4,
        out_shape=[jax.ShapeDtypeStruct((R, 128), F32)] * 4,
        scratch_shapes=[pltpu.VMEM((n_dev, R, 128), F32), pltpu.SemaphoreType.DMA((n_dev,)),
                        pltpu.SemaphoreType.DMA((n_dev,))],
        compiler_params=pltpu.CompilerParams(vmem_limit_bytes=VMEM_LIMIT_V7X),
    )(g_part, w, m, v)


def _pack_small(parts):
    flat = jnp.concatenate([p.reshape(-1) for p in parts])
    pad = (-flat.shape[0]) % 1024
    return jnp.pad(flat, (0, pad)).reshape(-1, 128)


def _unpack_small(packed, like):
    flat = packed.reshape(-1)
    out, at = [], 0
    for p in like:
        out.append(flat[at:at + p.size].reshape(p.shape))
        at += p.size
    return out


def kernel(x, norm_g, w_in, ret_decay_fwd, ret_decay_bwd, pool_w, pool_scale, attn_q_gain, attn_k_gain, attn_sink, w_ret, w_pool, w_att, w_out, loss_target, m_norm_g, m_w_in, m_ret_decay_fwd, m_ret_decay_bwd, m_pool_w, m_pool_scale, m_attn_q_gain, m_attn_k_gain, m_attn_sink, m_w_ret, m_w_pool, m_w_att, m_w_out, v_norm_g, v_w_in, v_ret_decay_fwd, v_ret_decay_bwd, v_pool_w, v_pool_scale, v_attn_q_gain, v_attn_k_gain, v_attn_sink, v_w_ret, v_w_pool, v_w_att, v_w_out):
    S, D = x.shape[1], x.shape[2]
    L = norm_g.shape[0]
    dm = _Dims(S, D, L)
    PD, BW, G = dm.PD, dm.BW, POOL_GROUPS
    xi, yi, ci = _place()
    chip = (2 * xi + yi).astype(jnp.int32).reshape(1)
    core = ci.astype(jnp.int32).reshape(1)
    chip_core = jnp.concatenate([chip, core])

    specs = [_Sharded("w_in", "col", 1, D, dm.INW), _Sharded("w_ret", "col", 1, BW, D), _Sharded("w_pool", "col", 1, BW, D),
             _Sharded("w_att", "col", 1, BW, D), _Sharded("w_out", "row", 1, D, D), _Sharded("pool_w", "row", G, PD, PD)]
    n_big = len(specs)
    big_w = [w_in, w_ret, w_pool, w_att, w_out, pool_w]
    big_m = [m_w_in, m_w_ret, m_w_pool, m_w_att, m_w_out, m_pool_w]
    big_v = [v_w_in, v_w_ret, v_w_pool, v_w_att, v_w_out, v_pool_w]
    stack3 = lambda a, s: a.reshape((L * s.shard[0],) + s.shard[1:])
    big_w3 = [stack3(a, s) for a, s in zip(big_w, specs)]
    big_m3 = [stack3(a, s) for a, s in zip(big_m, specs)]
    big_v3 = [stack3(a, s) for a, s in zip(big_v, specs)]

    W = [[_place_shard(big_w3[t], l, specs[t], chip) for t in range(n_big)] for l in range(L)]
    (ici0, _), = _comm_only("gather_first_ici", [_gather_ici_task(W[0], specs)])
    (W[0], _), = _comm_only("gather_first_d2d", [_gather_d2d_task(ici0, specs)])

    cosr, sinr = _ret_rope_tables(S, dm.RD)
    tabs = _att_tables(S)
    xl = x[0]
    saved = []
    for l in range(L):
        nxt = l + 1 < L
        ng = norm_g[l].reshape(1, D)
        qg, kg = attn_q_gain[l].reshape(1, ATT_HEAD_DIM), attn_k_gain[l].reshape(1, ATT_HEAD_DIM)
        psc = pool_scale[l].reshape(G, 1, PD)
        f_in, f_ret, f_pool, f_att, f_out, f_pw = W[l]
        (z, h), tk = _inproj(xl, ng, f_in, dm, tasks=[_gather_ici_task(W[l + 1][:1], specs[:1])] if nxt else ())
        if nxt:
            W[l + 1][:1] = tk[0][0]
        ya_pre = _ret_fwd(z, ret_decay_fwd[l], ret_decay_bwd[l], cosr, sinr, dm)
        yb_pre = _pool_fwd(z, f_pw, psc, dm)
        (yc_pre,), tk = _att_fwd(z, qg, kg, attn_sink[l], tabs, dm,
                                 tasks=[_gather_ici_task(W[l + 1][1:], specs[1:])] if nxt else ())
        if nxt:
            W[l + 1][1:] = tk[0][0]
        mg = z
        (x_next, ya, yb, yc, merged), tk = _merge_out(xl, mg, ya_pre, yb_pre, yc_pre, f_ret, f_pool, f_att, f_out, dm,
                                                      tasks=[_gather_d2d_task(W[l + 1], specs)] if nxt else ())
        if nxt:
            W[l + 1] = tk[0][0]
        saved.append((xl, z, h, mg, ya_pre, yb_pre, yc_pre, ya, yb, yc, merged, ng, qg, kg, psc))
        xl = x_next
    sq, dx = _loss_grad(xl, loss_target[0], dm)
    loss = lax.psum(sq[0, 0] * (0.5 / D), ("x", "y", "c"))

    grads = [None] * L
    lands = [None] * L
    pairs = [None] * L
    lands2 = [None] * L
    shards = [None] * L
    carried = [None] * n_big
    small = [None] * L

    def pair_sums(k):
        pairs[k] = [_pair_sum(g, ld, s, core) for g, ld, s in zip(grads[k], lands[k], specs)]

    def chip_sums(k):
        shards[k] = [_chip_sum(p, ld, s, chip_core) for p, ld, s in zip(pairs[k], lands2[k], specs)]

    def adamw(k):
        for t in range(n_big):
            carried[t] = _adamw(big_w3[t], shards[k][t], big_m3[t], big_v3[t], k, L, specs[t], carried[t])

    for l in reversed(range(L)):
        xl, z, h, mg, ya_pre, yb_pre, yc_pre, ya, yb, yc, merged, ng, qg, kg, psc = saved[l]
        f_in, f_ret, f_pool, f_att, f_out, f_pw = W[l]
        up1, up2 = l + 1 < L, l + 2 < L
        own, tk = _merge_bwd_gates(dx, mg, ya, yb, yc, f_out, dm, tasks=[_pair_task(grads[l + 1], specs)] if up1 else ())
        dmg, dya, dyb, dyc, dxb = own
        if up1:
            lands[l + 1] = tk[0][1]
            pair_sums(l + 1)
        dya_pre, dyb_pre, dyc_pre = _merge_bwd_proj(dya, dyb, dyc, f_ret, f_pool, f_att, dm)
        g_out, _ = _grad_matmul(merged, dxb, "grad_w_out")
        g_ret, _ = _grad_matmul(ya_pre, dya, "grad_w_ret")
        g_pool, _ = _grad_matmul(yb_pre, dyb, "grad_w_pool")
        g_att, _ = _grad_matmul(yc_pre, dyc, "grad_w_att")
        own, tk = _ret_bwd(z, dya_pre, ret_decay_fwd[l], ret_decay_bwd[l], cosr, sinr, dm,
                           tasks=[_chip_task(pairs[l + 1][1:], specs[1:])] if up1 else ())
        drq, drk, drv, drg, ddec = own
        if up1:
            lands2[l + 1] = [None] + tk[0][1]
        dpv, dpg, dpw, dps = _pool_bwd(z, dyb_pre, f_pw, psc, dm)
        own, tk = _att_bwd(z, dyc_pre, qg, kg, attn_sink[l], tabs, dm,
                           tasks=[_half_task(shards[l + 2], specs)] if up2 else ())
        daq, dak, dav, dag, dsm = own
        if up2:
            shards[l + 2] = tk[0][0]
            adamw(l + 2)
        dz = jnp.concatenate([drq, drk, drv, drg, dpv, dpg, daq, dak, dav, dag, dmg], axis=1)
        w_in_task = [_chip_task(pairs[l + 1][:1], specs[:1])] if up1 else ()
        if l > 0:
            g_in, _ = _grad_matmul(h, dz, "grad_w_in")
            (dh,), tk = _inproj_bwd(dz, f_in, dm, tasks=w_in_task)
        else:
            g_in, tk = _grad_matmul(h, dz, "grad_w_in", tasks=w_in_task)
        if up1:
            lands2[l + 1][0] = tk[0][1][0]
            chip_sums(l + 1)
        grads[l] = [g_in, g_ret, g_pool, g_att, g_out, dpw]
        if l == 0:
            (_, lands[0]), = _comm_only("grad_pair_exchange", [_pair_task(grads[0], specs)])
            pair_sums(0)
            (dh,), tk = _inproj_bwd(dz, f_in, dm, tasks=[_chip_task(pairs[0], specs)])
            lands2[0] = tk[0][1]
            chip_sums(0)
        dx, dng = _norm_bwd(dh, xl, ng, dx, dm)
        small[l] = [dng.reshape(D), ddec[:, 0, 0], ddec[:, 1, 0], dps.reshape(BW), jnp.sum(dsm[:, 0, :], axis=0),
                    jnp.sum(dsm[:, 1, :], axis=0), dsm[:, 2:2 + ATT_GROUP, 0].reshape(dm.AH)]

    rest = [k for k in (1, 0) if k < L]
    done = _comm_only("grad_half_exchange", [_half_task(shards[k], specs) for k in rest])
    for k, (both, _) in zip(rest, done):
        shards[k] = both
        adamw(k)

    back = lambda a, like: a.reshape(like.shape)
    g_big, d_big, m_big, v_big = ([back(carried[t][i], big_w[t]) for t in range(n_big)] for i in range(4))

    small_g = [jnp.stack([small[l][i] for l in range(L)]) for i in range(7)]
    small_w = [norm_g, ret_decay_fwd, ret_decay_bwd, pool_scale, attn_q_gain, attn_k_gain, attn_sink]
    small_m = [m_norm_g, m_ret_decay_fwd, m_ret_decay_bwd, m_pool_scale, m_attn_q_gain, m_attn_k_gain, m_attn_sink]
    small_v = [v_norm_g, v_ret_decay_fwd, v_ret_decay_bwd, v_pool_scale, v_attn_q_gain, v_attn_k_gain, v_attn_sink]
    sg, sd, sm, sv = _small_update(_pack_small(small_g), _pack_small(small_w), _pack_small(small_m), _pack_small(small_v))
    g_sm, d_sm, m_sm, v_sm = (_unpack_small(a, small_w) for a in (sg, sd, sm, sv))

    def ordered(big, small_):
        return [small_[0], big[0], small_[1], small_[2], big[5], small_[3], small_[4], small_[5], small_[6],
                big[1], big[2], big[3], big[4]]

    return (loss, dx[None], *ordered(g_big, g_sm), *ordered(d_big, d_sm), *ordered(m_big, m_sm),
            *ordered(v_big, v_sm))
```

```python
import jax
import jax.numpy as jnp
from jax import lax
from jax.experimental import pallas as pl
from jax.experimental.pallas import tpu as pltpu

F32 = jnp.float32
BF16 = jnp.bfloat16
MESH = pl.DeviceIdType.MESH

RMS_EPS = 1e-6
NEG_BIG = -1e30
CHUNK = 128
RET_HEADS = 4
POOL_GROUPS = 4
ATT_HEAD_DIM = 128
ATT_GROUP = 4
ROPE_DIMS = 32
RET_ROPE_BASE = 10000.0
ROPE_THETA = 500000.0
N_BRANCHES = 3
N_CHIPS = 4

ADAM_LR = 0.001
ADAM_B1 = 0.9
ADAM_B2 = 0.999
ADAM_EPS = 1e-08
ADAM_WD = 0.01
ADAM_STEP = 10

VMEM_LIMIT_V7X = 56 * 1024 * 1024

TN = (((0,), (0,)), ((), ()))
NT = (((1,), (1,)), ((), ()))

ATT_BLOCKS_PER_TRIP = 8
LOOP_UNROLL = 2
HBM_ANY = pl.BlockSpec(memory_space=pl.ANY)
ONE_BUFFER = pl.Buffered(1)


def _sigmoid(x):
    return 1.0 / (1.0 + jnp.exp(-x))


def _silu(x):
    return x * _sigmoid(x)


def _dsilu(x):
    s = _sigmoid(x)
    return s * (1.0 + x * (1.0 - s))


def _dot(a, b):
    return jnp.dot(a, b, preferred_element_type=F32)


def _dot_tn(a, b):
    return lax.dot_general(a, b, TN, preferred_element_type=F32)


def _dot_nt(a, b):
    return lax.dot_general(a, b, NT, preferred_element_type=F32)


def _bf(x):
    return x.astype(BF16)


def _f32(x):
    return x.astype(F32)


def _loop(n, body, init):
    def several(i, carry):
        for u in range(LOOP_UNROLL):
            carry = body(i * LOOP_UNROLL + u, carry)
        return carry
    return lax.fori_loop(0, n // LOOP_UNROLL, several, init)


def _sum11(x):
    return jnp.sum(jnp.sum(x, axis=1, keepdims=True), axis=0, keepdims=True)


class _Dims:
    def __init__(self, seq, d_model, depth):
        self.S, self.D, self.L = seq, d_model, depth
        bw = d_model // 2
        self.BW = bw
        self.RD = bw // RET_HEADS
        self.PD = bw // POOL_GROUPS
        self.AH = bw // ATT_HEAD_DIM
        self.AKV = self.AH // ATT_GROUP
        self.AKW = self.AKV * ATT_HEAD_DIM
        self.o_rq, self.o_rk, self.o_rv, self.o_rg = 0, bw, 2 * bw, 3 * bw
        self.o_pv, self.o_pg = 4 * bw, 5 * bw
        self.o_aq = 6 * bw
        self.o_ak = 7 * bw
        self.o_av = 7 * bw + self.AKW
        self.o_ag = 7 * bw + 2 * self.AKW
        self.o_mg = 8 * bw + 2 * self.AKW
        self.INW = self.o_mg + N_BRANCHES * d_model
        self.nC = seq // CHUNK
        self.tm = min(512, seq)


def _place():
    return lax.axis_index("x"), lax.axis_index("y"), lax.axis_index("c")


def _chip_peers(x, y):
    return [(1 - x, y), (x, 1 - y), (1 - x, 1 - y)]


class _Sharded:
    def __init__(self, name, kind, a, r, cc):
        self.name, self.kind, self.A, self.R, self.Cc = name, kind, a, r, cc
        self.full = (a, r, cc)
        if kind == "col":
            self.shard = (a, r, cc // N_CHIPS)
            self.half_rows = r // 2
        else:
            self.shard = (a, r // N_CHIPS, cc)
            self.half_rows = r // N_CHIPS // 2
        self.half = (a, self.half_rows, self.shard[2])

    def in_full(self, ref, chip, core=None):
        hr = self.half_rows
        if self.kind == "col":
            rows = pl.ds(0, self.R) if core is None else pl.ds(core * hr, hr)
            return ref.at[:, rows, pl.ds(chip * self.shard[2], self.shard[2])]
        rows = pl.ds(chip * self.shard[1], self.shard[1]) if core is None else pl.ds(chip * self.shard[1] + core * hr, hr)
        return ref.at[:, rows, :]

    def in_shard(self, ref, core):
        return ref.at[:, pl.ds(core * self.half_rows, self.half_rows), :]


class _Task:
    def __init__(self, ro, rw, new, n_sem, copies):
        self.ro, self.rw, self.new, self.n_sem, self.copies = list(ro), list(rw), list(new), n_sem, copies


def _remote(src, dst, send_sem, recv_sem, k, device):
    def make():
        return pltpu.make_async_remote_copy(src_ref=src, dst_ref=dst, send_sem=send_sem.at[k], recv_sem=recv_sem.at[k],
                                            device_id=device, device_id_type=MESH)
    return make


def _gather_ici_task(bufs, specs):
    def copies(ro, rw, new, ss, rs):
        x, y, c = _place()
        me = 2 * x + y
        out = []
        for t, spec in enumerate(specs):
            for p, (px, py) in enumerate(_chip_peers(x, y)):
                mine = spec.in_full(rw[t], me, c)
                theirs = spec.in_full(rw[t], 2 * px + py, c)
                k = t * 3 + p
                out.append((_remote(mine, mine, ss, rs, k, (px, py, c)), _remote(mine, mine, ss, rs, k, (px, py, c)),
                            _remote(theirs, theirs, ss, rs, k, (px, py, c))))
        return out
    return _Task([], bufs, [], 3 * len(specs), copies)


def _gather_d2d_task(bufs, specs):
    def copies(ro, rw, new, ss, rs):
        x, y, c = _place()
        sib = (x, y, 1 - c)
        out = []
        for t, spec in enumerate(specs):
            for p, (px, py) in enumerate(_chip_peers(x, y)):
                got = spec.in_full(rw[t], 2 * px + py, c)
                gets = spec.in_full(rw[t], 2 * px + py, 1 - c)
                k = t * 3 + p
                out.append((_remote(got, got, ss, rs, k, sib), _remote(got, got, ss, rs, k, sib),
                            _remote(gets, gets, ss, rs, k, sib)))
        return out
    return _Task([], bufs, [], 3 * len(specs), copies)


def _pair_task(grads, specs):
    def copies(ro, rw, new, ss, rs):
        x, y, c = _place()
        sib = (x, y, 1 - c)
        out = []
        for t, spec in enumerate(specs):
            for j in range(N_CHIPS):
                k = t * N_CHIPS + j
                cp = _remote(spec.in_full(ro[t], j, 1 - c), new[t].at[j], ss, rs, k, sib)
                out.append((cp, cp, cp))
        return out
    return _Task(grads, [], [jax.ShapeDtypeStruct((N_CHIPS,) + s.half, BF16) for s in specs], N_CHIPS * len(specs), copies)


def _chip_task(pair_sums, specs):
    def copies(ro, rw, new, ss, rs):
        x, y, c = _place()
        out = []
        for t in range(len(specs)):
            for p, (px, py) in enumerate(_chip_peers(x, y)):
                cp = _remote(ro[t].at[2 * px + py], new[t].at[p], ss, rs, t * 3 + p, (px, py, c))
                out.append((cp, cp, cp))
        return out
    return _Task(pair_sums, [], [jax.ShapeDtypeStruct((3,) + s.half, BF16) for s in specs], 3 * len(specs), copies)


def _half_task(shards, specs):
    def copies(ro, rw, new, ss, rs):
        x, y, c = _place()
        sib = (x, y, 1 - c)
        out = []
        for t, spec in enumerate(specs):
            mine, theirs = spec.in_shard(rw[t], c), spec.in_shard(rw[t], 1 - c)
            out.append((_remote(mine, mine, ss, rs, t, sib), _remote(mine, mine, ss, rs, t, sib),
                        _remote(theirs, theirs, ss, rs, t, sib)))
        return out
    return _Task([], shards, [], len(specs), copies)


def _call(body, *, name, grid, in_specs, out_specs, out_shape, args, scratch_shapes=(), tasks=(), carry=()):
    tasks = [t for t in tasks if t is not None]
    carry = list(carry)
    n_in, n_out, n_scr, n_carry = len(in_specs), len(out_specs), len(scratch_shapes), len(carry)
    ro = [a for t in tasks for a in t.ro]
    rw = [a for t in tasks for a in t.rw]
    new = [s for t in tasks for s in t.new]
    n_ro, n_rw, n_new = len(ro), len(rw), len(new)

    def wrapped(*refs):
        ins = refs[:n_in]
        ro_refs = refs[n_in + n_carry:n_in + n_carry + n_ro]
        at = n_in + n_carry + n_ro + n_rw
        outs = refs[at:at + n_out + n_carry]
        at = at + n_out + n_carry
        rw_refs = refs[at:at + n_rw]
        new_refs = refs[at + n_rw:at + n_rw + n_new]
        at = at + n_rw + n_new
        scr = refs[at:at + n_scr]
        sems = refs[at + n_scr:]

        def task_copies():
            found, a, b, d = [], 0, 0, 0
            for i, t in enumerate(tasks):
                found += t.copies(ro_refs[a:a + len(t.ro)], rw_refs[b:b + len(t.rw)], new_refs[d:d + len(t.new)],
                                  sems[2 * i], sems[2 * i + 1])
                a, b, d = a + len(t.ro), b + len(t.rw), d + len(t.new)
            return found

        if tasks:
            first = pl.program_id(0) == 0
            last = pl.program_id(0) == grid[0] - 1
            for ax in range(1, len(grid)):
                first = first & (pl.program_id(ax) == 0)
                last = last & (pl.program_id(ax) == grid[ax] - 1)

            @pl.when(first)
            def _():
                for cp, _, _ in task_copies():
                    cp().start()

        body(*ins, *outs, *scr)

        if tasks:
            @pl.when(last)
            def _():
                found = task_copies()
                for _, _, recv in found:
                    recv().wait_recv()
                for _, send, _ in found:
                    send().wait_send()

    sem_shapes = []
    for t in tasks:
        sem_shapes += [pltpu.SemaphoreType.DMA((t.n_sem,)), pltpu.SemaphoreType.DMA((t.n_sem,))]
    aliases = {n_in + i: n_out + i for i in range(n_carry)}
    aliases.update({n_in + n_carry + n_ro + i: n_out + n_carry + i for i in range(n_rw)})
    res = pl.pallas_call(
        wrapped, name=name, grid=grid,
        in_specs=list(in_specs) + [HBM_ANY] * (n_carry + n_ro + n_rw),
        out_specs=list(out_specs) + [HBM_ANY] * (n_carry + n_rw + n_new),
        out_shape=list(out_shape) + [jax.ShapeDtypeStruct(a.shape, a.dtype) for a in carry + rw] + new,
        scratch_shapes=list(scratch_shapes) + sem_shapes,
        input_output_aliases=aliases,
        compiler_params=pltpu.CompilerParams(dimension_semantics=("arbitrary",) * len(grid),
                                             vmem_limit_bytes=VMEM_LIMIT_V7X),
    )(*args, *carry, *ro, *rw)
    own, rest = list(res[:n_out + n_carry]), list(res[n_out + n_carry:])
    per_task, b, d = [], 0, n_rw
    for t in tasks:
        per_task.append((rest[b:b + len(t.rw)], rest[d:d + len(t.new)]))
        b, d = b + len(t.rw), d + len(t.new)
    return own, per_task


def _comm_only(name, tasks):
    ro = [a for t in tasks for a in t.ro]
    rw = [a for t in tasks for a in t.rw]
    new = [s for t in tasks for s in t.new]
    n_ro, n_rw, n_new = len(ro), len(rw), len(new)

    def body(*refs):
        ro_refs = refs[:n_ro]
        rw_refs = refs[n_ro + n_rw:n_ro + 2 * n_rw]
        new_refs = refs[n_ro + 2 * n_rw:n_ro + 2 * n_rw + n_new]
        sems = refs[n_ro + 2 * n_rw + n_new:]
        found, a, b, d = [], 0, 0, 0
        for i, t in enumerate(tasks):
            found += t.copies(ro_refs[a:a + len(t.ro)], rw_refs[b:b + len(t.rw)], new_refs[d:d + len(t.new)],
                              sems[2 * i], sems[2 * i + 1])
            a, b, d = a + len(t.ro), b + len(t.rw), d + len(t.new)
        for cp, _, _ in found:
            cp().start()
        for _, _, recv in found:
            recv().wait_recv()
        for _, send, _ in found:
            send().wait_send()

    sem_shapes = []
    for t in tasks:
        sem_shapes += [pltpu.SemaphoreType.DMA((t.n_sem,)), pltpu.SemaphoreType.DMA((t.n_sem,))]
    res = pl.pallas_call(
        body, name=name,
        in_specs=[HBM_ANY] * (n_ro + n_rw), out_specs=[HBM_ANY] * (n_rw + n_new),
        out_shape=[jax.ShapeDtypeStruct(a.shape, a.dtype) for a in rw] + new,
        scratch_shapes=sem_shapes,
        input_output_aliases={n_ro + i: i for i in range(n_rw)},
    )(*ro, *rw)
    res = list(res)
    per_task, b, d = [], 0, n_rw
    for t in tasks:
        per_task.append((res[b:b + len(t.rw)], res[d:d + len(t.new)]))
        b, d = b + len(t.rw), d + len(t.new)
    return per_task


def _inproj(x, norm_g, w_in, dm, tasks=()):
    S, D, N = dm.S, dm.D, dm.INW
    tm, tn = min(1024, S), 512

    def body(x_ref, g_ref, w_ref, z_ref, h_ref):
        @pl.when(pl.program_id(1) == 0)
        def _():
            xv = x_ref[...]
            r = lax.rsqrt(jnp.mean(xv * xv, axis=-1, keepdims=True) + RMS_EPS)
            h_ref[...] = _bf(xv * r * g_ref[...])

        z_ref[...] = _bf(_dot(h_ref[...], w_ref[...]))

    return _call(
        body, name="inproj", grid=(S // tm, N // tn),
        in_specs=[pl.BlockSpec((tm, D), lambda i, j: (i, 0)),
                  pl.BlockSpec((1, D), lambda i, j: (0, 0)),
                  pl.BlockSpec((None, D, tn), lambda i, j: (0, 0, j))],
        out_specs=[pl.BlockSpec((tm, tn), lambda i, j: (i, j)),
                   pl.BlockSpec((tm, D), lambda i, j: (i, 0))],
        out_shape=[jax.ShapeDtypeStruct((S, N), BF16), jax.ShapeDtypeStruct((S, D), BF16)],
        args=(x, norm_g, w_in), tasks=tasks)


def _rot_half(x, cs, sn):
    h = cs.shape[-1]
    x1, x2 = x[:, :h], x[:, h:]
    return jnp.concatenate([x1 * cs - x2 * sn, x2 * cs + x1 * sn], axis=-1)


def _ret_tables(af_ref, ab_ref):
    C = CHUNK
    lgf = -jnp.exp(af_ref[...])[:, :1]
    lgb = -jnp.exp(ab_ref[...])[:, :1]
    ri = lax.broadcasted_iota(jnp.int32, (C, C), 0)
    ci = lax.broadcasted_iota(jnp.int32, (C, C), 1)
    lag = _f32(ri - ci)
    alag = jnp.abs(lag)
    low = lag >= 0
    dmask = jnp.where(low, jnp.exp(lgf * alag), jnp.exp(lgb * alag))
    j = _f32(lax.broadcasted_iota(jnp.int32, (C, 1), 0))
    return dict(lgf=lgf, lgb=lgb, alag=alag, low=low, dmask=dmask, j=j,
                w_f=jnp.exp(lgf * (C - 1.0 - j)), w_b=jnp.exp(lgb * j),
                q_f=jnp.exp(lgf * (j + 1.0)), q_b=jnp.exp(lgb * (C - j)),
                dec_f=jnp.exp(lgf * C), dec_b=jnp.exp(lgb * C))


def _decay_rows(a):
    return jnp.broadcast_to(a.reshape(RET_HEADS, 1, 1), (RET_HEADS, 1, 128))


def _ret_fwd(z, a_f, a_b, cosr, sinr, dm):
    S, RD, C, nC = dm.S, dm.RD, CHUNK, dm.nC
    scale = RD ** -0.5

    def body(q_ref, k_ref, v_ref, g_ref, af_ref, ab_ref, cos_ref, sin_ref, o_ref, qh_s, kh_s, sf_s, st_s):
        t = _ret_tables(af_ref, ab_ref)
        st_s[...] = jnp.zeros_like(st_s)

        def fwd_pass(n, carry):
            r = pl.ds(pl.multiple_of(n * C, C), C)
            cs, sn = cos_ref[r, :], sin_ref[r, :]
            qh = _rot_half(_f32(q_ref[r, :]), cs, sn)
            kh = _rot_half(_f32(k_ref[r, :]), cs, sn) * scale
            qh_s[r, :] = _bf(qh)
            kh_s[r, :] = _bf(kh)
            st = st_s[...]
            sf_s[n] = _bf(st)
            st_s[...] = st * t["dec_f"] + _dot_tn(_bf(kh * t["w_f"]), v_ref[r, :])
            return carry

        _loop(nC, fwd_pass, 0)
        st_s[...] = jnp.zeros_like(st_s)

        def bwd_pass(i, carry):
            n = nC - 1 - i
            r = pl.ds(pl.multiple_of(n * C, C), C)
            qhb, khb, vb = qh_s[r, :], kh_s[r, :], v_ref[r, :]
            p = _bf(_dot_nt(qhb, khb) * t["dmask"])
            qhf = _f32(qhb)
            sb = st_s[...]
            out = (_dot(p, vb) + _dot(_bf(qhf * t["q_f"]), sf_s[n]) + _dot(_bf(qhf * t["q_b"]), _bf(sb)))
            y = out * lax.rsqrt(jnp.mean(out * out, axis=-1, keepdims=True) + RMS_EPS)
            o_ref[r, :] = _bf(y * _silu(_f32(g_ref[r, :])))
            st_s[...] = sb * t["dec_b"] + _dot_tn(_bf(_f32(khb) * t["w_b"]), vb)
            return carry

        _loop(nC, bwd_pass, 0)

    zs = lambda off: pl.BlockSpec((S, RD), lambda h: (0, off // RD + h))
    dec = pl.BlockSpec((None, 1, 128), lambda h: (h, 0, 0))
    tab = pl.BlockSpec((S, RD // 2), lambda h: (0, 0))
    own, _ = _call(
        body, name="ret_fwd", grid=(RET_HEADS,),
        in_specs=[zs(dm.o_rq), zs(dm.o_rk), zs(dm.o_rv), zs(dm.o_rg), dec, dec, tab, tab],
        out_specs=[pl.BlockSpec((S, RD), lambda h: (0, h))],
        out_shape=[jax.ShapeDtypeStruct((S, dm.BW), BF16)],
        scratch_shapes=[pltpu.VMEM((S, RD), BF16), pltpu.VMEM((S, RD), BF16),
                        pltpu.VMEM((nC, RD, RD), BF16), pltpu.VMEM((RD, RD), F32)],
        args=(z, z, z, z, _decay_rows(a_f), _decay_rows(a_b), cosr, sinr))
    return own[0]


def _band(first_row, first_col, lo, hi, shape):
    r = lax.broadcasted_iota(jnp.int32, shape, 0) + first_row
    c = lax.broadcasted_iota(jnp.int32, shape, 1) + first_col
    d = c - r
    return jnp.where((d >= lo) & (d <= hi), 1.0, 0.0).astype(BF16)


def _pool_counts(first_row, half, S, rows):
    pos = lax.broadcasted_iota(jnp.int32, (rows, 1), 0) + first_row
    lo = jnp.clip(pos - half, 0, S)
    hi = jnp.clip(pos + half, 0, S)
    return 1.0 / _f32(hi - lo)


def _pool_rows(S):
    return min(4 * CHUNK, S // 2)


def _pool_win_start(n, rows, S):
    return pl.multiple_of(jnp.clip(n * rows - CHUNK, 0, S - (rows + 2 * CHUNK)), CHUNK)


def _win_start(n, S):
    return pl.multiple_of(jnp.clip((n - 1) * CHUNK, 0, S - 3 * CHUNK), CHUNK)


def _pool_fwd(z, pool_w, pool_scale, dm):
    S, PD = dm.S, dm.PD
    PB = _pool_rows(S)
    WIN = PB + 2 * CHUNK

    def body(u_ref, g_ref, w_ref, sc_ref, o_ref):
        half = jnp.left_shift(1, pl.program_id(0))
        w = w_ref[...]
        sc = sc_ref[...]

        def blk(n, carry):
            row0 = pl.multiple_of(n * PB, PB)
            r = pl.ds(row0, PB)
            st = _pool_win_start(n, PB, S)
            band = _band(row0, st, -half, half - 1, (PB, WIN))
            mean = _dot(band, u_ref[pl.ds(st, WIN), :]) * _pool_counts(row0, half, S, PB)
            p = mean - _f32(u_ref[r, :])
            y = _dot(_bf(p), w) * sc
            o_ref[r, :] = _bf(y * _silu(_f32(g_ref[r, :])))
            return carry

        _loop(S // PB, blk, 0)

    own, _ = _call(
        body, name="pool_fwd", grid=(POOL_GROUPS,),
        in_specs=[pl.BlockSpec((S, PD), lambda g: (0, dm.o_pv // PD + g)),
                  pl.BlockSpec((S, PD), lambda g: (0, dm.o_pg // PD + g)),
                  pl.BlockSpec((None, PD, PD), lambda g: (g, 0, 0)),
                  pl.BlockSpec((None, 1, PD), lambda g: (g, 0, 0))],
        out_specs=[pl.BlockSpec((S, PD), lambda g: (0, g))],
        out_shape=[jax.ShapeDtypeStruct((S, dm.BW), BF16)],
        args=(z, z, pool_w, pool_scale))
    return own[0]


def _rot_part(x, cs, s_up, s_dn):
    h = ROPE_DIMS // 2
    return x * cs + pltpu.roll(x, ATT_HEAD_DIM - h, 1) * s_up + pltpu.roll(x, h, 1) * s_dn


def _att_tables(S):
    h = ROPE_DIMS // 2
    inv = ROPE_THETA ** (-jnp.arange(h, dtype=F32) / h)
    ang = jnp.arange(S, dtype=F32)[:, None] * inv[None, :]
    cos, sin = jnp.cos(ang), jnp.sin(ang)
    pad = jnp.zeros((S, ATT_HEAD_DIM - 2 * h), F32)
    zero = jnp.zeros((S, h), F32)
    cs = jnp.concatenate([cos, cos, pad + 1.0], axis=1)
    s_up = jnp.concatenate([-sin, zero, pad], axis=1)
    s_dn = jnp.concatenate([zero, sin, pad], axis=1)
    return cs, s_up, s_dn


def _ret_rope_tables(S, RD):
    h = RD // 2
    inv = 1.0 / (RET_ROPE_BASE ** jnp.linspace(0.0, 1.0, h, dtype=F32))
    ang = jnp.arange(S, dtype=F32)[:, None] * inv[None, :]
    return jnp.cos(ang), jnp.sin(ang)


def _att_norm_rot(x, gain, cs, s_up, s_dn):
    u = x * lax.rsqrt(jnp.mean(x * x, axis=-1, keepdims=True) + RMS_EPS)
    return _rot_part(u * gain, cs, s_up, s_dn)


def _att_bias_tables(bias_s):
    C = CHUNK
    shape = (ATT_GROUP * C, 3 * C)
    rel = lax.broadcasted_iota(jnp.int32, shape, 1) - lax.broadcasted_iota(jnp.int32, shape, 0) % C
    for which, shift in enumerate((-C, 0, -2 * C)):
        bias_s[which] = jnp.where(jnp.abs(rel + shift) <= CHUNK, 0.0, NEG_BIG)


def _att_bias(bias_s, n, nC):
    return bias_s[jnp.where(n == 0, 1, jnp.where(n == nC - 1, 2, 0))]


def _att_probs(q4, kw, sink_col, bias):
    s = _dot_nt(q4, kw) * (ATT_HEAD_DIM ** -0.5) + bias
    m = jnp.maximum(jnp.max(s, axis=-1, keepdims=True), sink_col)
    e = jnp.exp(s - m)
    es = jnp.exp(sink_col - m)
    inv = 1.0 / (jnp.sum(e, axis=-1, keepdims=True) + es)
    return e * inv, es * inv


def _sink_col(sink_ref):
    head = lax.broadcasted_iota(jnp.int32, (ATT_GROUP * CHUNK, 1), 0) // CHUNK
    col = jnp.zeros((ATT_GROUP * CHUNK, 1), F32)
    for g in range(ATT_GROUP):
        col = jnp.where(head == g, sink_ref[g:g + 1, :1], col)
    return col


def _sink_rows(sink, dm):
    return jnp.broadcast_to(sink.reshape(dm.AKV, ATT_GROUP, 1), (dm.AKV, ATT_GROUP, 128))


def _att_prep(q_ref, k_ref, qg_ref, kg_ref, cs_ref, su_ref, sd_ref, qn_s, kn_s, nC):
    C, HD = CHUNK, ATT_HEAD_DIM

    def prep(n, carry):
        r = pl.ds(pl.multiple_of(n * C, C), C)
        cs, su, sd = cs_ref[r, :], su_ref[r, :], sd_ref[r, :]
        kn_s[r, :] = _bf(_att_norm_rot(_f32(k_ref[r, :]), kg_ref[...], cs, su, sd))
        for g in range(ATT_GROUP):
            qn_s[g, r, :] = _bf(_att_norm_rot(_f32(q_ref[r, g * HD:(g + 1) * HD]), qg_ref[...], cs, su, sd))
        return carry

    lax.fori_loop(0, nC, prep, 0)


def _att_fwd(z, q_gain, k_gain, sink, tabs, dm, tasks=()):
    S, C, nC, HD, G = dm.S, CHUNK, dm.nC, ATT_HEAD_DIM, ATT_GROUP
    GW = G * HD

    def body(q_ref, k_ref, v_ref, g_ref, qg_ref, kg_ref, sk_ref, cs_ref, su_ref, sd_ref, o_ref, qn_s, kn_s, bias_s):
        _att_prep(q_ref, k_ref, qg_ref, kg_ref, cs_ref, su_ref, sd_ref, qn_s, kn_s, nC)
        _att_bias_tables(bias_s)
        sink_col = _sink_col(sk_ref)

        def blk(n, carry):
            row0 = pl.multiple_of(n * C, C)
            r = pl.ds(row0, C)
            w = pl.ds(_win_start(n, S), 3 * C)
            q4 = qn_s[:, r, :].reshape(G * C, HD)
            p, _ = _att_probs(q4, kn_s[w, :], sink_col, _att_bias(bias_s, n, nC))
            o = _dot(_bf(p), v_ref[w, :])
            for g in range(G):
                gate = _silu(_f32(g_ref[r, g * HD:(g + 1) * HD]))
                o_ref[r, g * HD:(g + 1) * HD] = _bf(o[g * C:(g + 1) * C, :] * gate)
            return carry

        _loop(nC, blk, 0)

    tab = pl.BlockSpec((S, HD), lambda h: (0, 0), pipeline_mode=ONE_BUFFER)
    gain = pl.BlockSpec((1, HD), lambda h: (0, 0))
    return _call(
        body, name="att_fwd", grid=(dm.AKV,),
        in_specs=[pl.BlockSpec((S, GW), lambda h: (0, dm.o_aq // GW + h)),
                  pl.BlockSpec((S, HD), lambda h: (0, dm.o_ak // HD + h)),
                  pl.BlockSpec((S, HD), lambda h: (0, dm.o_av // HD + h)),
                  pl.BlockSpec((S, GW), lambda h: (0, dm.o_ag // GW + h)),
                  gain, gain, pl.BlockSpec((None, G, 128), lambda h: (h, 0, 0)), tab, tab, tab],
        out_specs=[pl.BlockSpec((S, GW), lambda h: (0, h))],
        out_shape=[jax.ShapeDtypeStruct((S, dm.BW), BF16)],
        scratch_shapes=[pltpu.VMEM((G, S, HD), BF16), pltpu.VMEM((S, HD), BF16), pltpu.VMEM((3, G * C, 3 * C), F32)],
        args=(z, z, z, z, q_gain, k_gain, _sink_rows(sink, dm), *tabs), tasks=tasks)


def _head_bias_tables(bias_s):
    C = CHUNK
    rel = lax.broadcasted_iota(jnp.int32, (C, 3 * C), 1) - lax.broadcasted_iota(jnp.int32, (C, 3 * C), 0)
    for which, shift in enumerate((-C, 0, -2 * C)):
        bias_s[which] = jnp.where(jnp.abs(rel + shift) <= CHUNK, 0.0, NEG_BIG)


def _head_probs(q, kw, sink, bias):
    C = CHUNK
    s = _dot_nt(q, kw) * (ATT_HEAD_DIM ** -0.5) + bias
    widest = jnp.maximum(jnp.maximum(s[:, :C], s[:, C:2 * C]), s[:, 2 * C:])
    m = jnp.maximum(jnp.max(widest, axis=-1, keepdims=True), sink)
    e = jnp.exp(s - m)
    es = jnp.exp(sink - m)
    total = jnp.sum(e[:, :C] + e[:, C:2 * C] + e[:, 2 * C:], axis=-1, keepdims=True)
    inv = 1.0 / (total + es)
    return e * inv, es * inv


def _norm_rot_rows(x_ref, gain_ref, cs_ref, su_ref, sd_ref, out_s, nC):
    C = CHUNK

    def prep(n, carry):
        r = pl.ds(pl.multiple_of(n * C, C), C)
        out_s[r, :] = _bf(_att_norm_rot(_f32(x_ref[r, :]), gain_ref[...], cs_ref[r, :], su_ref[r, :], sd_ref[r, :]))
        return carry

    _loop(nC, prep, 0)


def _head_sink_rows(sink, dm):
    return jnp.broadcast_to(sink.reshape(dm.AH, 1, 1), (dm.AH, 1, 128))


def _att_fwd_heads(z, q_gain, k_gain, sink, tabs, dm, tasks=()):
    S, C, nC, HD, G = dm.S, CHUNK, dm.nC, ATT_HEAD_DIM, ATT_GROUP

    def body(q_ref, k_ref, v_ref, g_ref, qg_ref, kg_ref, sk_ref, cs_ref, su_ref, sd_ref, o_ref, qn_s, kn_s, bias_s):
        @pl.when(pl.program_id(1) == 0)
        def _():
            _norm_rot_rows(k_ref, kg_ref, cs_ref, su_ref, sd_ref, kn_s, nC)
            _head_bias_tables(bias_s)

        _norm_rot_rows(q_ref, qg_ref, cs_ref, su_ref, sd_ref, qn_s, nC)
        sink_v = sk_ref[...][:, :1]

        def blocks(i, carry):
            ns = [i * ATT_BLOCKS_PER_TRIP + u for u in range(ATT_BLOCKS_PER_TRIP)]
            rs = [pl.ds(pl.multiple_of(n * C, C), C) for n in ns]
            ws = [pl.ds(_win_start(n, S), 3 * C) for n in ns]
            qs = [qn_s[r, :] for r in rs]
            kws = [kn_s[w, :] for w in ws]
            vws = [v_ref[w, :] for w in ws]
            gates = [g_ref[r, :] for r in rs]
            biases = [_att_bias(bias_s, n, nC) for n in ns]
            ps = [_head_probs(q, kw, sink_v, b)[0] for q, kw, b in zip(qs, kws, biases)]
            outs = [_dot(_bf(p), vw) for p, vw in zip(ps, vws)]
            outs = [_bf(o * _silu(_f32(gt))) for o, gt in zip(outs, gates)]
            for r, o in zip(rs, outs):
                o_ref[r, :] = o
            return carry

        lax.fori_loop(0, nC // ATT_BLOCKS_PER_TRIP, blocks, 0)

    tab = pl.BlockSpec((S, HD), lambda h, g: (0, 0))
    gain = pl.BlockSpec((1, HD), lambda h, g: (0, 0))
    qcol = lambda off: pl.BlockSpec((S, HD), lambda h, g: (0, off // HD + h * G + g))
    kcol = lambda off: pl.BlockSpec((S, HD), lambda h, g: (0, off // HD + h))
    return _call(
        body, name="att_fwd", grid=(dm.AKV, G),
        in_specs=[qcol(dm.o_aq), kcol(dm.o_ak), kcol(dm.o_av), qcol(dm.o_ag), gain, gain,
                  pl.BlockSpec((None, 1, 128), lambda h, g: (h * G + g, 0, 0)), tab, tab, tab],
        out_specs=[qcol(0)],
        out_shape=[jax.ShapeDtypeStruct((S, dm.BW), BF16)],
        scratch_shapes=[pltpu.VMEM((S, HD), BF16), pltpu.VMEM((S, HD), BF16), pltpu.VMEM((3, C, 3 * C), F32)],
        args=(z, z, z, z, q_gain, k_gain, _head_sink_rows(sink, dm), *tabs), tasks=tasks)


def _att_bwd_heads(z, dyc_pre, q_gain, k_gain, sink, tabs, dm, tasks=()):
    S, C, nC, HD, G = dm.S, CHUNK, dm.nC, ATT_HEAD_DIM, ATT_GROUP
    scale = HD ** -0.5

    def body(q_ref, k_ref, v_ref, g_ref, dy_ref, qg_ref, kg_ref, sk_ref, cs_ref, su_ref, sd_ref,
             dq_ref, dk_ref, dv_ref, dg_ref, sm_ref, qn_s, kn_s, dqn_s, dkn_s, dv_s, bias_s):
        g_id = pl.program_id(1)
        row = lax.broadcasted_iota(jnp.int32, (8, HD), 0)

        @pl.when(g_id == 0)
        def _():
            _norm_rot_rows(k_ref, kg_ref, cs_ref, su_ref, sd_ref, kn_s, nC)
            _head_bias_tables(bias_s)
            dkn_s[...] = jnp.zeros_like(dkn_s)
            dv_s[...] = jnp.zeros_like(dv_s)
            sm_ref[...] = jnp.zeros_like(sm_ref)

        _norm_rot_rows(q_ref, qg_ref, cs_ref, su_ref, sd_ref, qn_s, nC)
        sink_v = sk_ref[...][:, :1]

        def blk(n, dsink):
            r = pl.ds(pl.multiple_of(n * C, C), C)
            w = pl.ds(_win_start(n, S), 3 * C)
            q, kw, vw = qn_s[r, :], kn_s[w, :], v_ref[w, :]
            p, ps = _head_probs(q, kw, sink_v, _att_bias(bias_s, n, nC))
            pb = _bf(p)
            o = _dot(pb, vw)
            gate = _f32(g_ref[r, :])
            dy = _f32(dy_ref[r, :])
            dg_ref[r, :] = _bf(dy * o * _dsilu(gate))
            dob = _bf(dy * _silu(gate))
            dp = _dot_nt(dob, vw)
            pdp = p * dp
            drow = jnp.sum(pdp[:, :C] + pdp[:, C:2 * C] + pdp[:, 2 * C:], axis=-1, keepdims=True)
            ds = _bf(p * (dp - drow))
            dqn_s[r, :] = _dot(ds, kw) * scale
            dkn_s[w, :] += _dot_tn(ds, q) * scale
            dv_s[w, :] += _dot_tn(pb, dob)
            return dsink - ps * drow

        dsink = _loop(nC, blk, jnp.zeros((C, 1), F32))

        def norm_bwd(x, gain, dqn, cs, su, sd):
            rr = lax.rsqrt(jnp.mean(x * x, axis=-1, keepdims=True) + RMS_EPS)
            u = x * rr
            dw = _rot_part(dqn, cs, -su, -sd)
            du = dw * gain
            return rr * (du - u * jnp.mean(du * u, axis=-1, keepdims=True)), jnp.sum(dw * u, axis=0, keepdims=True)

        def fin_q(n, dqg):
            r = pl.ds(pl.multiple_of(n * C, C), C)
            dq, gq = norm_bwd(_f32(q_ref[r, :]), qg_ref[...], dqn_s[r, :], cs_ref[r, :], su_ref[r, :], sd_ref[r, :])
            dq_ref[r, :] = _bf(dq)
            return dqg + gq

        dqg = _loop(nC, fin_q, jnp.zeros((1, HD), F32))
        sm_ref[...] += jnp.where(row == 0, dqg, 0.0) + jnp.where(row == 2 + g_id, _sum11(dsink), 0.0)

        @pl.when(g_id == G - 1)
        def _():
            def fin_k(n, dkg):
                r = pl.ds(pl.multiple_of(n * C, C), C)
                dk, gk = norm_bwd(_f32(k_ref[r, :]), kg_ref[...], dkn_s[r, :], cs_ref[r, :], su_ref[r, :], sd_ref[r, :])
                dk_ref[r, :] = _bf(dk)
                dv_ref[r, :] = _bf(dv_s[r, :])
                return dkg + gk

            dkg = _loop(nC, fin_k, jnp.zeros((1, HD), F32))
            sm_ref[...] += jnp.where(row == 1, dkg, 0.0)

    tab = pl.BlockSpec((S, HD), lambda h, g: (0, 0))
    gain = pl.BlockSpec((1, HD), lambda h, g: (0, 0))
    qcol = lambda off: pl.BlockSpec((S, HD), lambda h, g: (0, off // HD + h * G + g))
    kcol = lambda off: pl.BlockSpec((S, HD), lambda h, g: (0, off // HD + h))
    return _call(
        body, name="att_bwd", grid=(dm.AKV, G),
        in_specs=[qcol(dm.o_aq), kcol(dm.o_ak), kcol(dm.o_av), qcol(dm.o_ag), qcol(0), gain, gain,
                  pl.BlockSpec((None, 1, 128), lambda h, g: (h * G + g, 0, 0)), tab, tab, tab],
        out_specs=[qcol(0), kcol(0), kcol(0), qcol(0), pl.BlockSpec((None, 8, 128), lambda h, g: (h, 0, 0))],
        out_shape=[jax.ShapeDtypeStruct((S, dm.BW), BF16), jax.ShapeDtypeStruct((S, dm.AKW), BF16),
                   jax.ShapeDtypeStruct((S, dm.AKW), BF16), jax.ShapeDtypeStruct((S, dm.BW), BF16),
                   jax.ShapeDtypeStruct((dm.AKV, 8, 128), F32)],
        scratch_shapes=[pltpu.VMEM((S, HD), BF16), pltpu.VMEM((S, HD), BF16),
                        pltpu.VMEM((S, HD), F32), pltpu.VMEM((S, HD), F32), pltpu.VMEM((S, HD), F32),
                        pltpu.VMEM((3, C, 3 * C), F32)],
        args=(z, z, z, z, dyc_pre, q_gain, k_gain, _head_sink_rows(sink, dm), *tabs), tasks=tasks)


def _merge_gate_spec(tm, dm):
    return pl.BlockSpec((pl.Element(tm), pl.Element(N_BRANCHES * dm.D)), lambda i: (i * tm, dm.o_mg))


def _merge_out(x, mg, ya_pre, yb_pre, yc_pre, w_ret, w_pool, w_att, w_out, dm, tasks=()):
    S, D, BW = dm.S, dm.D, dm.BW
    tm = min(256, S)

    def body(x_ref, mg_ref, a_ref, b_ref, c_ref, wr_ref, wp_ref, wa_ref, wo_ref, xo_ref, ya_ref, yb_ref, yc_ref, m_ref):
        ya = _dot(a_ref[...], wr_ref[...])
        yb = _dot(b_ref[...], wp_ref[...])
        yc = _dot(c_ref[...], wa_ref[...])
        ya_ref[...], yb_ref[...], yc_ref[...] = _bf(ya), _bf(yb), _bf(yc)
        g0 = _sigmoid(_f32(mg_ref[:, 0:D]))
        g1 = _sigmoid(_f32(mg_ref[:, D:2 * D]))
        g2 = _sigmoid(_f32(mg_ref[:, 2 * D:3 * D]))
        merged = _bf(g0 * ya + g1 * yb + g2 * yc)
        m_ref[...] = merged
        xo_ref[...] = x_ref[...] + _dot(merged, wo_ref[...])

    act = lambda w: pl.BlockSpec((tm, w), lambda i: (i, 0))
    wsp = lambda r: pl.BlockSpec((None, r, D), lambda i: (0, 0, 0), pipeline_mode=ONE_BUFFER)
    return _call(
        body, name="merge_out", grid=(S // tm,),
        in_specs=[act(D), _merge_gate_spec(tm, dm), act(BW), act(BW), act(BW), wsp(BW), wsp(BW), wsp(BW), wsp(D)],
        out_specs=[act(D)] * 5,
        out_shape=[jax.ShapeDtypeStruct((S, D), F32)] + [jax.ShapeDtypeStruct((S, D), BF16)] * 4,
        args=(x, mg, ya_pre, yb_pre, yc_pre, w_ret, w_pool, w_att, w_out), tasks=tasks)


def _loss_grad(y, target, dm):
    S, D = dm.S, dm.D
    tm = dm.tm

    def body(y_ref, t_ref, s_ref, d_ref):
        @pl.when(pl.program_id(0) == 0)
        def _():
            s_ref[...] = jnp.zeros_like(s_ref)

        e = y_ref[...] - t_ref[...]
        d_ref[...] = e * (1.0 / D)
        s_ref[...] += jnp.sum(jnp.sum(e * e, axis=-1, keepdims=True), axis=0, keepdims=True)

    row = pl.BlockSpec((tm, D), lambda i: (i, 0))
    own, _ = _call(
        body, name="loss_grad", grid=(S // tm,), in_specs=[row, row],
        out_specs=[pl.BlockSpec((1, 128), lambda i: (0, 0)), row],
        out_shape=[jax.ShapeDtypeStruct((1, 128), F32), jax.ShapeDtypeStruct((S, D), F32)],
        args=(y, target))
    return own


def _merge_bwd_gates(dx, mg, ya, yb, yc, w_out, dm, tasks=()):
    S, D = dm.S, dm.D
    tm = min(256, S)

    def body(dx_ref, mg_ref, ya_ref, yb_ref, yc_ref, wo_ref, dmg_ref, dya_ref, dyb_ref, dyc_ref, dxb_ref):
        dxb = _bf(dx_ref[...])
        dxb_ref[...] = dxb
        dm_ = _dot_nt(dxb, wo_ref[...])
        for k, (y_ref, dy_ref) in enumerate(((ya_ref, dya_ref), (yb_ref, dyb_ref), (yc_ref, dyc_ref))):
            g = _sigmoid(_f32(mg_ref[:, k * D:(k + 1) * D]))
            dmg_ref[:, k * D:(k + 1) * D] = _bf(dm_ * _f32(y_ref[...]) * g * (1.0 - g))
            dy_ref[...] = _bf(dm_ * g)

    act = lambda w: pl.BlockSpec((tm, w), lambda i: (i, 0))
    return _call(
        body, name="merge_bwd_gates", grid=(S // tm,),
        in_specs=[act(D), _merge_gate_spec(tm, dm), act(D), act(D), act(D),
                  pl.BlockSpec((None, D, D), lambda i: (0, 0, 0), pipeline_mode=ONE_BUFFER)],
        out_specs=[_merge_gate_spec(tm, dm), act(D), act(D), act(D), act(D)],
        out_shape=[jax.ShapeDtypeStruct((S, dm.INW), BF16)] + [jax.ShapeDtypeStruct((S, D), BF16)] * 4,
        args=(dx, mg, ya, yb, yc, w_out), tasks=tasks)


def _merge_bwd_proj(dya, dyb, dyc, w_ret, w_pool, w_att, dm):
    S, D, BW = dm.S, dm.D, dm.BW
    tm = dm.tm

    def body(da_ref, db_ref, dc_ref, wr_ref, wp_ref, wa_ref, oa_ref, ob_ref, oc_ref):
        oa_ref[...] = _bf(_dot_nt(da_ref[...], wr_ref[...]))
        ob_ref[...] = _bf(_dot_nt(db_ref[...], wp_ref[...]))
        oc_ref[...] = _bf(_dot_nt(dc_ref[...], wa_ref[...]))

    act = lambda w: pl.BlockSpec((tm, w), lambda i: (i, 0))
    wsp = pl.BlockSpec((None, BW, D), lambda i: (0, 0, 0), pipeline_mode=ONE_BUFFER)
    own, _ = _call(
        body, name="merge_bwd_proj", grid=(S // tm,),
        in_specs=[act(D)] * 3 + [wsp] * 3, out_specs=[act(BW)] * 3,
        out_shape=[jax.ShapeDtypeStruct((S, BW), BF16)] * 3,
        args=(dya, dyb, dyc, w_ret, w_pool, w_att))
    return own


def _col_writes(dz_ref, sem, step, parts):
    return [pltpu.make_async_copy(src, dz_ref.at[:, pl.ds(pl.multiple_of(off + step * w, 128), w)], sem.at[k])
            for k, (src, off, w) in enumerate(parts)]


def _ret_bwd(z, dz, dya_pre, a_f, a_b, cosr, sinr, dm, tasks=()):
    S, RD, C, nC = dm.S, dm.RD, CHUNK, dm.nC
    scale = RD ** -0.5
    H = RET_HEADS

    def body(q_ref, k_ref, v_ref, g_ref, dy_ref, af_ref, ab_ref, cos_ref, sin_ref, dd_ref, dz_ref,
             qh_s, kh_s, sf_s, sb_s, do_s, dqh_s, dkh_s, dv_s, st_s, lam_s, dq_ref, dk_ref, dv_ref, dg_ref, wsem):
        head = pl.program_id(0)
        writes = lambda step: _col_writes(dz_ref, wsem, step, [(dq_ref, dm.o_rq, RD), (dk_ref, dm.o_rk, RD),
                                                               (dv_ref, dm.o_rv, RD), (dg_ref, dm.o_rg, RD)])
        t = _ret_tables(af_ref, ab_ref)
        zero_rr = jnp.zeros((RD, RD), F32)

        st_s[...] = zero_rr

        def pass0(n, carry):
            r = pl.ds(pl.multiple_of(n * C, C), C)
            cs, sn = cos_ref[r, :], sin_ref[r, :]
            qh = _rot_half(_f32(q_ref[r, :]), cs, sn)
            kh = _rot_half(_f32(k_ref[r, :]), cs, sn) * scale
            qh_s[r, :] = _bf(qh)
            kh_s[r, :] = _bf(kh)
            st = st_s[...]
            sf_s[n] = _bf(st)
            st_s[...] = st * t["dec_f"] + _dot_tn(_bf(kh * t["w_f"]), v_ref[r, :])
            return carry

        _loop(nC, pass0, 0)

        @pl.when(head > 0)
        def _():
            for cp in writes(head - 1):
                cp.wait()

        st_s[...] = zero_rr
        lam_s[...] = zero_rr

        def pass1(i, acc):
            glf, glb = acc
            n = nC - 1 - i
            r = pl.ds(pl.multiple_of(n * C, C), C)
            qhb, khb, vb = qh_s[r, :], kh_s[r, :], v_ref[r, :]
            qhf, khf = _f32(qhb), _f32(khb)
            sc_ = _dot_nt(qhb, khb)
            p = _bf(sc_ * t["dmask"])
            sb = st_s[...]
            sbb = _bf(sb)
            sb_s[n] = sbb
            sfb = sf_s[n]
            qf_b, qb_b = _bf(qhf * t["q_f"]), _bf(qhf * t["q_b"])
            out = _dot(p, vb) + _dot(qf_b, sfb) + _dot(qb_b, sbb)
            rr = lax.rsqrt(jnp.mean(out * out, axis=-1, keepdims=True) + RMS_EPS)
            y = out * rr
            g = _f32(g_ref[r, :])
            dya = _f32(dy_ref[r, :])
            dg_ref[r, :] = _bf(dya * y * _dsilu(g))
            dyn = dya * _silu(g)
            dout = rr * (dyn - y * jnp.mean(dyn * y, axis=-1, keepdims=True))
            dob = _bf(dout)
            do_s[r, :] = dob
            dp = _dot_nt(dob, vb)
            dv = _dot_tn(p, dob)
            ds = _bf(dp * t["dmask"])
            dqh = _dot(ds, khb)
            dkh = _dot_tn(ds, qhb)
            dd = dp * sc_ * t["dmask"] * t["alag"]
            glf = glf + _sum11(jnp.where(t["low"], dd, 0.0))
            glb = glb + _sum11(jnp.where(t["low"], 0.0, dd))
            tf = _dot_nt(dob, sfb)
            tb = _dot_nt(dob, sbb)
            dqh = dqh + tf * t["q_f"] + tb * t["q_b"]
            glf = glf + _sum11(jnp.sum(tf * qhf, axis=-1, keepdims=True) * t["q_f"] * (t["j"] + 1.0))
            glb = glb + _sum11(jnp.sum(tb * qhf, axis=-1, keepdims=True) * t["q_b"] * (C - t["j"]))
            lam = lam_s[...]
            lamb = _bf(lam)
            glf = glf + _sum11(lam * _f32(sfb)) * t["dec_f"] * C
            u = _dot_nt(vb, lamb)
            dkh = dkh + u * t["w_f"]
            glf = glf + _sum11(jnp.sum(u * khf, axis=-1, keepdims=True) * t["w_f"] * (C - 1.0 - t["j"]))
            dv = dv + _dot(_bf(khf * t["w_f"]), lamb)
            lam_s[...] = _dot_tn(qf_b, dob) + lam * t["dec_f"]
            dqh_s[r, :] = dqh
            dkh_s[r, :] = dkh
            dv_s[r, :] = dv
            st_s[...] = sb * t["dec_b"] + _dot_tn(_bf(khf * t["w_b"]), vb)
            return glf, glb

        glf, glb = _loop(nC, pass1, (jnp.zeros((1, 1), F32), jnp.zeros((1, 1), F32)))

        lam_s[...] = zero_rr

        def pass2(n, glb):
            r = pl.ds(pl.multiple_of(n * C, C), C)
            qhb, khb, vb, dob = qh_s[r, :], kh_s[r, :], v_ref[r, :], do_s[r, :]
            qhf, khf = _f32(qhb), _f32(khb)
            lam = lam_s[...]
            lamb = _bf(lam)
            glb = glb + _sum11(lam * _f32(sb_s[n])) * t["dec_b"] * C
            u = _dot_nt(vb, lamb)
            dkh = dkh_s[r, :] + u * t["w_b"]
            glb = glb + _sum11(jnp.sum(u * khf, axis=-1, keepdims=True) * t["w_b"] * t["j"])
            dv = dv_s[r, :] + _dot(_bf(khf * t["w_b"]), lamb)
            lam_s[...] = _dot_tn(_bf(qhf * t["q_b"]), dob) + lam * t["dec_b"]
            cs, sn = cos_ref[r, :], sin_ref[r, :]
            dq_ref[r, :] = _bf(_rot_half(dqh_s[r, :], cs, -sn))
            dk_ref[r, :] = _bf(_rot_half(dkh * scale, cs, -sn))
            dv_ref[r, :] = _bf(dv)
            return glb

        glb = _loop(nC, pass2, glb)
        row = lax.broadcasted_iota(jnp.int32, (8, 128), 0)
        da_f = glf * t["lgf"]
        da_b = glb * t["lgb"]
        dd_ref[...] = jnp.where(row == 0, da_f, jnp.where(row == 1, da_b, 0.0))
        for cp in writes(head):
            cp.start()

        @pl.when(head == H - 1)
        def _():
            for cp in writes(head):
                cp.wait()

    zs = lambda off: pl.BlockSpec((S, RD), lambda h: (0, off // RD + h), pipeline_mode=ONE_BUFFER)
    col = pl.BlockSpec((S, RD), lambda h: (0, h), pipeline_mode=ONE_BUFFER)
    dec = pl.BlockSpec((None, 1, 128), lambda h: (h, 0, 0))
    tab = pl.BlockSpec((S, RD // 2), lambda h: (0, 0), pipeline_mode=ONE_BUFFER)
    return _call(
        body, name="ret_bwd", grid=(H,),
        in_specs=[zs(dm.o_rq), zs(dm.o_rk), zs(dm.o_rv), zs(dm.o_rg), col, dec, dec, tab, tab],
        out_specs=[pl.BlockSpec((None, 8, 128), lambda h: (h, 0, 0))],
        out_shape=[jax.ShapeDtypeStruct((H, 8, 128), F32)],
        scratch_shapes=[pltpu.VMEM((S, RD), BF16), pltpu.VMEM((S, RD), BF16),
                        pltpu.VMEM((nC, RD, RD), BF16), pltpu.VMEM((nC, RD, RD), BF16),
                        pltpu.VMEM((S, RD), BF16),
                        pltpu.VMEM((S, RD), F32), pltpu.VMEM((S, RD), F32), pltpu.VMEM((S, RD), F32),
                        pltpu.VMEM((RD, RD), F32), pltpu.VMEM((RD, RD), F32)]
        + [pltpu.VMEM((S, RD), BF16)] * 4 + [pltpu.SemaphoreType.DMA((4,))],
        args=(z, z, z, z, dya_pre, _decay_rows(a_f), _decay_rows(a_b), cosr, sinr), tasks=tasks, carry=[dz])


def _pool_bwd(z, dz, dyb_pre, pool_w, pool_scale, dm):
    S, PD = dm.S, dm.PD
    PB = _pool_rows(S)
    WIN = PB + 2 * CHUNK
    G = POOL_GROUPS

    def body(u_ref, g_ref, dy_ref, w_ref, sc_ref, dw_ref, dsc_ref, dz_ref, dp_s, dpc_s, dw_s, du_ref, dg_ref, wsem):
        group = pl.program_id(0)
        writes = lambda step: _col_writes(dz_ref, wsem, step, [(du_ref, dm.o_pv, PD), (dg_ref, dm.o_pg, PD)])
        half = jnp.left_shift(1, group)
        w = w_ref[...]
        sc = sc_ref[...]
        dw_s[...] = jnp.zeros_like(dw_s)
        dsc_ref[...] = jnp.zeros_like(dsc_ref)

        @pl.when(group > 0)
        def _():
            for cp in writes(group - 1):
                cp.wait()

        def blk1(n, carry):
            row0 = pl.multiple_of(n * PB, PB)
            r = pl.ds(row0, PB)
            st = _pool_win_start(n, PB, S)
            band = _band(row0, st, -half, half - 1, (PB, WIN))
            inv = _pool_counts(row0, half, S, PB)
            pb = _bf(_dot(band, u_ref[pl.ds(st, WIN), :]) * inv - _f32(u_ref[r, :]))
            ylin = _dot(pb, w)
            y = ylin * sc
            g = _f32(g_ref[r, :])
            dyb = _f32(dy_ref[r, :])
            dg_ref[r, :] = _bf(dyb * y * _dsilu(g))
            dy = dyb * _silu(g)
            dsc_ref[...] += jnp.sum(dy * ylin, axis=0, keepdims=True)
            dyl = _bf(dy * sc)
            dw_s[...] += _dot_tn(pb, dyl)
            dp = _dot_nt(dyl, w)
            dp_s[r, :] = dp
            dpc_s[r, :] = _bf(dp * inv)
            return carry

        _loop(S // PB, blk1, 0)
        dw_ref[...] = _bf(dw_s[...])

        def blk2(n, carry):
            row0 = pl.multiple_of(n * PB, PB)
            r = pl.ds(row0, PB)
            st = _pool_win_start(n, PB, S)
            band_t = _band(row0, st, -half + 1, half, (PB, WIN))
            du_ref[r, :] = _bf(_dot(band_t, dpc_s[pl.ds(st, WIN), :]) - dp_s[r, :])
            return carry

        _loop(S // PB, blk2, 0)
        for cp in writes(group):
            cp.start()

        @pl.when(group == G - 1)
        def _():
            for cp in writes(group):
                cp.wait()

    own, _ = _call(
        body, name="pool_bwd", grid=(G,),
        in_specs=[pl.BlockSpec((S, PD), lambda g: (0, dm.o_pv // PD + g)),
                  pl.BlockSpec((S, PD), lambda g: (0, dm.o_pg // PD + g)),
                  pl.BlockSpec((S, PD), lambda g: (0, g)),
                  pl.BlockSpec((None, PD, PD), lambda g: (g, 0, 0)),
                  pl.BlockSpec((None, 1, PD), lambda g: (g, 0, 0))],
        out_specs=[pl.BlockSpec((None, PD, PD), lambda g: (g, 0, 0)), pl.BlockSpec((None, 1, PD), lambda g: (g, 0, 0))],
        out_shape=[jax.ShapeDtypeStruct((G, PD, PD), BF16), jax.ShapeDtypeStruct((G, 1, PD), F32)],
        scratch_shapes=[pltpu.VMEM((S, PD), F32), pltpu.VMEM((S, PD), BF16), pltpu.VMEM((PD, PD), F32),
                        pltpu.VMEM((S, PD), BF16), pltpu.VMEM((S, PD), BF16), pltpu.SemaphoreType.DMA((2,))],
        args=(z, z, dyb_pre, pool_w, pool_scale), carry=[dz])
    return own


def _att_bwd(z, dz, dyc_pre, q_gain, k_gain, sink, tabs, dm, tasks=()):
    S, C, nC, HD, G = dm.S, CHUNK, dm.nC, ATT_HEAD_DIM, ATT_GROUP
    GW = G * HD
    scale = HD ** -0.5

    def body(q_ref, k_ref, v_ref, g_ref, dy_ref, qg_ref, kg_ref, sk_ref, cs_ref, su_ref, sd_ref,
             sm_ref, dz_ref, qn_s, kn_s, dqn_s, dkn_s, dv_s, bias_s, dq_ref, dk_ref, dv_ref, dg_ref, wsem):
        kv = pl.program_id(0)
        writes = lambda step: _col_writes(dz_ref, wsem, step, [(dq_ref, dm.o_aq, GW), (dk_ref, dm.o_ak, HD),
                                                               (dv_ref, dm.o_av, HD), (dg_ref, dm.o_ag, GW)])
        _att_prep(q_ref, k_ref, qg_ref, kg_ref, cs_ref, su_ref, sd_ref, qn_s, kn_s, nC)

        @pl.when(kv > 0)
        def _():
            for cp in writes(kv - 1):
                cp.wait()

        _att_bias_tables(bias_s)
        dkn_s[...] = jnp.zeros_like(dkn_s)
        dv_s[...] = jnp.zeros_like(dv_s)
        sink_col = _sink_col(sk_ref)

        def blk(n, dsink):
            row0 = pl.multiple_of(n * C, C)
            r = pl.ds(row0, C)
            w = pl.ds(_win_start(n, S), 3 * C)
            q4 = qn_s[:, r, :].reshape(G * C, HD)
            kw, vw = kn_s[w, :], v_ref[w, :]
            p, ps = _att_probs(q4, kw, sink_col, _att_bias(bias_s, n, nC))
            pb = _bf(p)
            o = _dot(pb, vw)
            do_parts = []
            for g in range(G):
                cols = slice(g * HD, (g + 1) * HD)
                gate = _f32(g_ref[r, cols])
                dy = _f32(dy_ref[r, cols])
                dg_ref[r, cols] = _bf(dy * o[g * C:(g + 1) * C, :] * _dsilu(gate))
                do_parts.append(dy * _silu(gate))
            dob = _bf(jnp.concatenate(do_parts, axis=0))
            dp = _dot_nt(dob, vw)
            drow = jnp.sum(p * dp, axis=-1, keepdims=True)
            ds = _bf(p * (dp - drow))
            dsink = dsink - ps * drow
            dqn_s[:, r, :] = (_dot(ds, kw) * scale).reshape(G, C, HD)
            dkn_s[w, :] += _dot_tn(ds, q4) * scale
            dv_s[w, :] += _dot_tn(pb, dob)
            return dsink

        dsink = _loop(nC, blk, jnp.zeros((G * C, 1), F32))

        def fin(n, acc):
            dqg, dkg = acc
            r = pl.ds(pl.multiple_of(n * C, C), C)
            cs, su, sd = cs_ref[r, :], su_ref[r, :], sd_ref[r, :]

            def norm_bwd(x, gain, dqn):
                rr = lax.rsqrt(jnp.mean(x * x, axis=-1, keepdims=True) + RMS_EPS)
                u = x * rr
                dw = _rot_part(dqn, cs, -su, -sd)
                du = dw * gain
                return rr * (du - u * jnp.mean(du * u, axis=-1, keepdims=True)), jnp.sum(dw * u, axis=0, keepdims=True)

            dk, gk = norm_bwd(_f32(k_ref[r, :]), kg_ref[...], dkn_s[r, :])
            dk_ref[r, :] = _bf(dk)
            dv_ref[r, :] = _bf(dv_s[r, :])
            dkg = dkg + gk
            for g in range(G):
                cols = slice(g * HD, (g + 1) * HD)
                dq, gq = norm_bwd(_f32(q_ref[r, cols]), qg_ref[...], dqn_s[g, r, :])
                dq_ref[r, cols] = _bf(dq)
                dqg = dqg + gq
            return dqg, dkg

        dqg, dkg = lax.fori_loop(0, nC, fin, (jnp.zeros((1, HD), F32), jnp.zeros((1, HD), F32)))
        sm_ref[...] = jnp.zeros_like(sm_ref)
        sm_ref[0:1, :] = dqg
        sm_ref[1:2, :] = dkg
        for g in range(G):
            sm_ref[2 + g:3 + g, :] = jnp.broadcast_to(_sum11(dsink[g * C:(g + 1) * C, :]), (1, HD))
        for cp in writes(kv):
            cp.start()

        @pl.when(kv == dm.AKV - 1)
        def _():
            for cp in writes(kv):
                cp.wait()

    tab = pl.BlockSpec((S, HD), lambda h: (0, 0), pipeline_mode=ONE_BUFFER)
    gain = pl.BlockSpec((1, HD), lambda h: (0, 0))
    wide = lambda off: pl.BlockSpec((S, GW), lambda h: (0, off // GW + h), pipeline_mode=ONE_BUFFER)
    thin = lambda off: pl.BlockSpec((S, HD), lambda h: (0, off // HD + h), pipeline_mode=ONE_BUFFER)
    return _call(
        body, name="att_bwd", grid=(dm.AKV,),
        in_specs=[wide(dm.o_aq), thin(dm.o_ak), thin(dm.o_av), wide(dm.o_ag), wide(0), gain, gain,
                  pl.BlockSpec((None, G, 128), lambda h: (h, 0, 0)), tab, tab, tab],
        out_specs=[pl.BlockSpec((None, 8, 128), lambda h: (h, 0, 0))],
        out_shape=[jax.ShapeDtypeStruct((dm.AKV, 8, 128), F32)],
        scratch_shapes=[pltpu.VMEM((G, S, HD), BF16), pltpu.VMEM((S, HD), BF16),
                        pltpu.VMEM((G, S, HD), F32), pltpu.VMEM((S, HD), F32), pltpu.VMEM((S, HD), F32),
                        pltpu.VMEM((3, G * C, 3 * C), F32),
                        pltpu.VMEM((S, GW), BF16), pltpu.VMEM((S, HD), BF16), pltpu.VMEM((S, HD), BF16),
                        pltpu.VMEM((S, GW), BF16), pltpu.SemaphoreType.DMA((4,))],
        args=(z, z, z, z, dyc_pre, q_gain, k_gain, _sink_rows(sink, dm), *tabs), tasks=tasks, carry=[dz])


def _grad_matmul(a, b, name, tasks=()):
    S, M = a.shape
    N = b.shape[1]
    tm, tn = min(1024, M), min(512, N)

    def body(a_ref, b_ref, o_ref):
        o_ref[...] = _bf(_dot_tn(a_ref[...], b_ref[...]))

    own, tk = _call(
        body, name=name, grid=(M // tm, N // tn),
        in_specs=[pl.BlockSpec((S, tm), lambda i, j: (0, i)), pl.BlockSpec((S, tn), lambda i, j: (0, j))],
        out_specs=[pl.BlockSpec((None, tm, tn), lambda i, j: (0, i, j))],
        out_shape=[jax.ShapeDtypeStruct((1, M, N), BF16)],
        args=(a, b), tasks=tasks)
    return own[0], tk


def _inproj_bwd(dz, w_in, dm, tasks=()):
    S, D = dm.S, dm.D
    tm = dm.tm
    tk = dm.INW // N_CHIPS if (dm.INW // N_CHIPS) % 128 == 0 else 512
    nk = dm.INW // tk

    def body(dz_ref, w_ref, dh_ref):
        part = _dot_nt(dz_ref[...], w_ref[...])

        @pl.when(pl.program_id(1) == 0)
        def _():
            dh_ref[...] = part

        @pl.when(pl.program_id(1) != 0)
        def _():
            dh_ref[...] += part

    return _call(
        body, name="inproj_bwd", grid=(S // tm, nk),
        in_specs=[pl.BlockSpec((tm, tk), lambda i, k: (i, k)), pl.BlockSpec((None, D, tk), lambda i, k: (0, 0, k))],
        out_specs=[pl.BlockSpec((tm, D), lambda i, k: (i, 0))],
        out_shape=[jax.ShapeDtypeStruct((S, D), F32)],
        args=(dz, w_in), tasks=tasks)


def _norm_bwd(dh, x, norm_g, dx_out, dm):
    S, D = dm.S, dm.D
    tm = dm.tm

    def body(dh_ref, x_ref, g_ref, dxo_ref, dx_ref, dg_ref):
        @pl.when(pl.program_id(0) == 0)
        def _():
            dg_ref[...] = jnp.zeros_like(dg_ref)

        xv = x_ref[...]
        rr = lax.rsqrt(jnp.mean(xv * xv, axis=-1, keepdims=True) + RMS_EPS)
        u = xv * rr
        dh = dh_ref[...]
        dg_ref[...] += jnp.sum(dh * u, axis=0, keepdims=True)
        du = dh * g_ref[...]
        dx_ref[...] = dxo_ref[...] + rr * (du - u * jnp.mean(du * u, axis=-1, keepdims=True))

    row = pl.BlockSpec((tm, D), lambda i: (i, 0))
    vec = pl.BlockSpec((1, D), lambda i: (0, 0))
    own, _ = _call(
        body, name="norm_bwd", grid=(S // tm,), in_specs=[row, row, vec, row], out_specs=[row, vec],
        out_shape=[jax.ShapeDtypeStruct((S, D), F32), jax.ShapeDtypeStruct((1, D), F32)],
        args=(dh, x, norm_g, dx_out))
    return own


def _row_block(rows, width, itemsize):
    target = max(16, (2 * 1024 * 1024) // (width * itemsize))
    for rb in range(min(rows, target), 0, -1):
        if rows % rb == 0 and (rb % 16 == 0 or rb == rows):
            return rb
    return rows


def _prefetch_call(body, *, name, grid, in_specs, out_specs, out_shape, args, aliases=None):
    grid_spec = pltpu.PrefetchScalarGridSpec(num_scalar_prefetch=1, grid=grid, in_specs=in_specs, out_specs=out_specs)
    return pl.pallas_call(
        body, name=name, grid_spec=grid_spec, out_shape=out_shape, input_output_aliases=aliases or {},
        compiler_params=pltpu.CompilerParams(dimension_semantics=("arbitrary",) * len(grid),
                                             vmem_limit_bytes=VMEM_LIMIT_V7X),
    )(*args)


def _place_shard(w, l, spec, chip):
    A, rows, width = spec.shard
    rb = _row_block(rows, width, 4)
    nrb = rows // rb
    if spec.kind == "col":
        out_map = lambda a, r, chip: (a, r, chip[0])
    else:
        out_map = lambda a, r, chip: (a, chip[0] * nrb + r, 0)

    def body(chip_ref, w_ref, o_ref):
        o_ref[...] = _bf(w_ref[...])

    return _prefetch_call(
        body, name="place_" + spec.name, grid=(A, nrb),
        in_specs=[pl.BlockSpec((None, rb, width), lambda a, r, chip: (l * A + a, r, 0))],
        out_specs=pl.BlockSpec((None, rb, width), out_map),
        out_shape=jax.ShapeDtypeStruct(spec.full, BF16), args=(chip, w))


def _pair_sum(grad, land, spec, core):
    A, hr, w = spec.half
    rb = _row_block(hr, w, 2)
    nrb = hr // rb
    if spec.kind == "col":
        g_spec = pl.BlockSpec((None, rb, w), lambda j, a, r, core: (a, core[0] * nrb + r, j))
    else:
        g_spec = pl.BlockSpec((None, rb, w), lambda j, a, r, core: (a, (j * 2 + core[0]) * nrb + r, 0))

    def body(core_ref, g_ref, l_ref, o_ref):
        o_ref[...] = _bf(_f32(g_ref[...]) + _f32(l_ref[...]))

    blk = pl.BlockSpec((None, None, rb, w), lambda j, a, r, core: (j, a, r, 0))
    return _prefetch_call(
        body, name="pair_sum_" + spec.name, grid=(N_CHIPS, A, nrb), in_specs=[g_spec, blk], out_specs=blk,
        out_shape=jax.ShapeDtypeStruct((N_CHIPS,) + spec.half, BF16), args=(core, grad, land))


def _chip_sum(pair_sum, land, spec, chip_core):
    A, hr, w = spec.half
    rb = _row_block(hr, w, 4)
    nrb = hr // rb

    def body(cc_ref, p_ref, l0_ref, l1_ref, l2_ref, o_ref):
        o_ref[...] = ((_f32(p_ref[...]) + _f32(l0_ref[...])) + _f32(l1_ref[...])) + _f32(l2_ref[...])

    own = pl.BlockSpec((None, None, rb, w), lambda a, r, cc: (cc[0], a, r, 0))
    slot = lambda p: pl.BlockSpec((None, None, rb, w), lambda a, r, cc: (p, a, r, 0))
    return _prefetch_call(
        body, name="chip_sum_" + spec.name, grid=(A, nrb), in_specs=[own, slot(0), slot(1), slot(2)],
        out_specs=pl.BlockSpec((None, rb, w), lambda a, r, cc: (a, cc[1] * nrb + r, 0)),
        out_shape=jax.ShapeDtypeStruct(spec.shard, F32), args=(chip_core, pair_sum, land, land, land))


def _adamw_math(w, g, m, v):
    m = ADAM_B1 * m + (1.0 - ADAM_B1) * g
    v = ADAM_B2 * v + (1.0 - ADAM_B2) * (g * g)
    m_hat = m / (1.0 - ADAM_B1 ** ADAM_STEP)
    v_hat = v / (1.0 - ADAM_B2 ** ADAM_STEP)
    delta = -ADAM_LR * (m_hat / (jnp.sqrt(v_hat) + ADAM_EPS) + ADAM_WD * w)
    return delta, m, v


def _adamw(w, g, m, v, l, depth, spec, carried):
    A, R, C = spec.shard
    rb = _row_block(R, C, 4 * 4)
    stacked = pl.BlockSpec((None, rb, C), lambda a, r: (l * A + a, r, 0))
    n_carry = 0 if carried is None else 4

    def body(w_ref, g_ref, m_ref, v_ref, *rest):
        go_ref, d_ref, mo_ref, vo_ref = rest[n_carry:]
        g = g_ref[...]
        go_ref[...] = g
        d_ref[...], mo_ref[...], vo_ref[...] = _adamw_math(w_ref[...], g, m_ref[...], v_ref[...])

    return pl.pallas_call(
        body, name="adamw_" + spec.name, grid=(A, R // rb),
        in_specs=[stacked, pl.BlockSpec((None, rb, C), lambda a, r: (a, r, 0)), stacked, stacked] + [HBM_ANY] * n_carry,
        out_specs=[stacked] * 4,
        out_shape=[jax.ShapeDtypeStruct((depth * A, R, C), F32)] * 4,
        input_output_aliases={4 + i: i for i in range(n_carry)},
        compiler_params=pltpu.CompilerParams(dimension_semantics=("arbitrary", "arbitrary"),
                                             vmem_limit_bytes=VMEM_LIMIT_V7X),
    )(w, g, m, v, *(carried or ()))


def _small_update(g_part, w, m, v):
    R = g_part.shape[0]
    n_dev = 8

    def body(g_ref, w_ref, m_ref, v_ref, go_ref, d_ref, mo_ref, vo_ref, all_s, send_sem, recv_sem):
        x, y, c = _place()
        me = 4 * x + 2 * y + c
        all_s[me] = g_ref[...]
        cps = []
        for k in range(1, n_dev):
            peer = (x ^ ((k >> 2) & 1), y ^ ((k >> 1) & 1), c ^ (k & 1))
            cp = pltpu.make_async_remote_copy(src_ref=g_ref, dst_ref=all_s.at[me], send_sem=send_sem.at[k],
                                              recv_sem=recv_sem.at[k], device_id=peer, device_id_type=MESH)
            cp.start()
            cps.append(cp)
        for cp in cps:
            cp.wait()
        g = all_s[0]
        for d in range(1, n_dev):
            g = g + all_s[d]
        go_ref[...] = g
        d_ref[...], mo_ref[...], vo_ref[...] = _adamw_math(w_ref[...], g, m_ref[...], v_ref[...])

    vm = pl.BlockSpec(memory_space=pltpu.VMEM)
    return pl.pallas_call(
        body, name="small_update", in_specs=[vm] * 4, out_specs=[vm] * 4,
        out_shape=[jax.ShapeDtypeStruct((R, 128), F32)] * 4,
        scratch_shapes=[pltpu.VMEM((n_dev, R, 128), F32), pltpu.SemaphoreType.DMA((n_dev,)),
                        pltpu.SemaphoreType.DMA((n_dev,))],
        compiler_params=pltpu.CompilerParams(vmem_limit_bytes=VMEM_LIMIT_V7X),
    )(g_part, w, m, v)


def _pack_small(parts):
    flat = jnp.concatenate([p.reshape(-1) for p in parts])
    pad = (-flat.shape[0]) % 1024
    return jnp.pad(flat, (0, pad)).reshape(-1, 128)


def _unpack_small(packed, like):
    flat = packed.reshape(-1)
    out, at = [], 0
    for p in like:
        out.append(flat[at:at + p.size].reshape(p.shape))
        at += p.size
    return out


def kernel(x, norm_g, w_in, ret_decay_fwd, ret_decay_bwd, pool_w, pool_scale, attn_q_gain, attn_k_gain, attn_sink, w_ret, w_pool, w_att, w_out, loss_target, m_norm_g, m_w_in, m_ret_decay_fwd, m_ret_decay_bwd, m_pool_w, m_pool_scale, m_attn_q_gain, m_attn_k_gain, m_attn_sink, m_w_ret, m_w_pool, m_w_att, m_w_out, v_norm_g, v_w_in, v_ret_decay_fwd, v_ret_decay_bwd, v_pool_w, v_pool_scale, v_attn_q_gain, v_attn_k_gain, v_attn_sink, v_w_ret, v_w_pool, v_w_att, v_w_out):
    S, D = x.shape[1], x.shape[2]
    L = norm_g.shape[0]
    dm = _Dims(S, D, L)
    PD, BW, G = dm.PD, dm.BW, POOL_GROUPS
    xi, yi, ci = _place()
    chip = (2 * xi + yi).astype(jnp.int32).reshape(1)
    core = ci.astype(jnp.int32).reshape(1)
    chip_core = jnp.concatenate([chip, core])

    specs = [_Sharded("w_in", "col", 1, D, dm.INW), _Sharded("w_ret", "col", 1, BW, D), _Sharded("w_pool", "col", 1, BW, D),
             _Sharded("w_att", "col", 1, BW, D), _Sharded("w_out", "row", 1, D, D), _Sharded("pool_w", "row", G, PD, PD)]
    n_big = len(specs)
    big_w = [w_in, w_ret, w_pool, w_att, w_out, pool_w]
    big_m = [m_w_in, m_w_ret, m_w_pool, m_w_att, m_w_out, m_pool_w]
    big_v = [v_w_in, v_w_ret, v_w_pool, v_w_att, v_w_out, v_pool_w]
    stack3 = lambda a, s: a.reshape((L * s.shard[0],) + s.shard[1:])
    big_w3 = [stack3(a, s) for a, s in zip(big_w, specs)]
    big_m3 = [stack3(a, s) for a, s in zip(big_m, specs)]
    big_v3 = [stack3(a, s) for a, s in zip(big_v, specs)]

    W = [[_place_shard(big_w3[t], l, specs[t], chip) for t in range(n_big)] for l in range(L)]
    (ici0, _), = _comm_only("gather_first_ici", [_gather_ici_task(W[0], specs)])
    (W[0], _), = _comm_only("gather_first_d2d", [_gather_d2d_task(ici0, specs)])

    cosr, sinr = _ret_rope_tables(S, dm.RD)
    tabs = _att_tables(S)
    xl = x[0]
    saved = []
    for l in range(L):
        nxt = l + 1 < L
        ng = norm_g[l].reshape(1, D)
        qg, kg = attn_q_gain[l].reshape(1, ATT_HEAD_DIM), attn_k_gain[l].reshape(1, ATT_HEAD_DIM)
        psc = pool_scale[l].reshape(G, 1, PD)
        f_in, f_ret, f_pool, f_att, f_out, f_pw = W[l]
        (z, h), tk = _inproj(xl, ng, f_in, dm, tasks=[_gather_ici_task(W[l + 1][:1], specs[:1])] if nxt else ())
        if nxt:
            W[l + 1][:1] = tk[0][0]
        ya_pre = _ret_fwd(z, ret_decay_fwd[l], ret_decay_bwd[l], cosr, sinr, dm)
        yb_pre = _pool_fwd(z, f_pw, psc, dm)
        (yc_pre,), tk = _att_fwd(z, qg, kg, attn_sink[l], tabs, dm,
                                 tasks=[_gather_ici_task(W[l + 1][1:], specs[1:])] if nxt else ())
        if nxt:
            W[l + 1][1:] = tk[0][0]
        mg = z
        (x_next, ya, yb, yc, merged), tk = _merge_out(xl, mg, ya_pre, yb_pre, yc_pre, f_ret, f_pool, f_att, f_out, dm,
                                                      tasks=[_gather_d2d_task(W[l + 1], specs)] if nxt else ())
        if nxt:
            W[l + 1] = tk[0][0]
        saved.append((xl, z, h, mg, ya_pre, yb_pre, yc_pre, ya, yb, yc, merged, ng, qg, kg, psc))
        xl = x_next
    sq, dx = _loss_grad(xl, loss_target[0], dm)
    loss = lax.psum(sq[0, 0] * (0.5 / D), ("x", "y", "c"))

    grads = [None] * L
    lands = [None] * L
    pairs = [None] * L
    lands2 = [None] * L
    shards = [None] * L
    carried = [None] * n_big
    small = [None] * L

    def pair_sums(k):
        pairs[k] = [_pair_sum(g, ld, s, core) for g, ld, s in zip(grads[k], lands[k], specs)]

    def chip_sums(k):
        shards[k] = [_chip_sum(p, ld, s, chip_core) for p, ld, s in zip(pairs[k], lands2[k], specs)]

    def adamw(k):
        for t in range(n_big):
            carried[t] = _adamw(big_w3[t], shards[k][t], big_m3[t], big_v3[t], k, L, specs[t], carried[t])

    for l in reversed(range(L)):
        xl, z, h, mg, ya_pre, yb_pre, yc_pre, ya, yb, yc, merged, ng, qg, kg, psc = saved[l]
        f_in, f_ret, f_pool, f_att, f_out, f_pw = W[l]
        up1, up2 = l + 1 < L, l + 2 < L
        own, tk = _merge_bwd_gates(dx, mg, ya, yb, yc, f_out, dm, tasks=[_pair_task(grads[l + 1], specs)] if up1 else ())
        dz, dya, dyb, dyc, dxb = own
        if up1:
            lands[l + 1] = tk[0][1]
            pair_sums(l + 1)
        dya_pre, dyb_pre, dyc_pre = _merge_bwd_proj(dya, dyb, dyc, f_ret, f_pool, f_att, dm)
        g_out, _ = _grad_matmul(merged, dxb, "grad_w_out")
        g_ret, _ = _grad_matmul(ya_pre, dya, "grad_w_ret")
        g_pool, _ = _grad_matmul(yb_pre, dyb, "grad_w_pool")
        g_att, _ = _grad_matmul(yc_pre, dyc, "grad_w_att")
        (ddec, dz), tk = _ret_bwd(z, dz, dya_pre, ret_decay_fwd[l], ret_decay_bwd[l], cosr, sinr, dm,
                                  tasks=[_chip_task(pairs[l + 1][1:], specs[1:])] if up1 else ())
        if up1:
            lands2[l + 1] = [None] + tk[0][1]
        dpw, dps, dz = _pool_bwd(z, dz, dyb_pre, f_pw, psc, dm)
        (dsm, dz), tk = _att_bwd(z, dz, dyc_pre, qg, kg, attn_sink[l], tabs, dm,
                                 tasks=[_half_task(shards[l + 2], specs)] if up2 else ())
        if up2:
            shards[l + 2] = tk[0][0]
            adamw(l + 2)
        w_in_task = [_chip_task(pairs[l + 1][:1], specs[:1])] if up1 else ()
        if l > 0:
            g_in, _ = _grad_matmul(h, dz, "grad_w_in")
            (dh,), tk = _inproj_bwd(dz, f_in, dm, tasks=w_in_task)
        else:
            g_in, tk = _grad_matmul(h, dz, "grad_w_in", tasks=w_in_task)
        if up1:
            lands2[l + 1][0] = tk[0][1][0]
            chip_sums(l + 1)
        grads[l] = [g_in, g_ret, g_pool, g_att, g_out, dpw]
        if l == 0:
            (_, lands[0]), = _comm_only("grad_pair_exchange", [_pair_task(grads[0], specs)])
            pair_sums(0)
            (dh,), tk = _inproj_bwd(dz, f_in, dm, tasks=[_chip_task(pairs[0], specs)])
            lands2[0] = tk[0][1]
            chip_sums(0)
        dx, dng = _norm_bwd(dh, xl, ng, dx, dm)
        small[l] = [dng.reshape(D), ddec[:, 0, 0], ddec[:, 1, 0], dps.reshape(BW), jnp.sum(dsm[:, 0, :], axis=0),
                    jnp.sum(dsm[:, 1, :], axis=0), dsm[:, 2:2 + ATT_GROUP, 0].reshape(dm.AH)]

    rest = [k for k in (1, 0) if k < L]
    done = _comm_only("grad_half_exchange", [_half_task(shards[k], specs) for k in rest])
    for k, (both, _) in zip(rest, done):
        shards[k] = both
        adamw(k)

    back = lambda a, like: a.reshape(like.shape)
    g_big, d_big, m_big, v_big = ([back(carried[t][i], big_w[t]) for t in range(n_big)] for i in range(4))

    small_g = [jnp.stack([small[l][i] for l in range(L)]) for i in range(7)]
    small_w = [norm_g, ret_decay_fwd, ret_decay_bwd, pool_scale, attn_q_gain, attn_k_gain, attn_sink]
    small_m = [m_norm_g, m_ret_decay_fwd, m_ret_decay_bwd, m_pool_scale, m_attn_q_gain, m_attn_k_gain, m_attn_sink]
    small_v = [v_norm_g, v_ret_decay_fwd, v_ret_decay_bwd, v_pool_scale, v_attn_q_gain, v_attn_k_gain, v_attn_sink]
    sg, sd, sm, sv = _small_update(_pack_small(small_g), _pack_small(small_w), _pack_small(small_m), _pack_small(small_v))
    g_sm, d_sm, m_sm, v_sm = (_unpack_small(a, small_w) for a in (sg, sd, sm, sv))

    def ordered(big, small_):
        return [small_[0], big[0], small_[1], small_[2], big[5], small_[3], small_[4], small_[5], small_[6],
                big[1], big[2], big[3], big[4]]

    return (loss, dx[None], *ordered(g_big, g_sm), *ordered(d_big, d_sm), *ordered(m_big, m_sm),
            *ordered(v_big, v_sm))
```

```python
import jax
import jax.numpy as jnp
from jax import lax
from jax.experimental import pallas as pl
from jax.experimental.pallas import tpu as pltpu

F32 = jnp.float32
BF16 = jnp.bfloat16
MESH = pl.DeviceIdType.MESH

RMS_EPS = 1e-6
NEG_BIG = -1e30
CHUNK = 128
RET_HEADS = 4
POOL_GROUPS = 4
ATT_HEAD_DIM = 128
ATT_GROUP = 4
ROPE_DIMS = 32
RET_ROPE_BASE = 10000.0
ROPE_THETA = 500000.0
N_BRANCHES = 3
N_CHIPS = 4

ADAM_LR = 0.001
ADAM_B1 = 0.9
ADAM_B2 = 0.999
ADAM_EPS = 1e-08
ADAM_WD = 0.01
ADAM_STEP = 10

VMEM_LIMIT_V7X = 56 * 1024 * 1024

TN = (((0,), (0,)), ((), ()))
NT = (((1,), (1,)), ((), ()))

RET_HEADS_PER_STEP = 2
LOOP_UNROLL = 2
HBM_ANY = pl.BlockSpec(memory_space=pl.ANY)
ONE_BUFFER = pl.Buffered(1)


def _sigmoid(x):
    return 1.0 / (1.0 + jnp.exp(-x))


def _silu(x):
    return x * _sigmoid(x)


def _dsilu(x):
    s = _sigmoid(x)
    return s * (1.0 + x * (1.0 - s))


def _dot(a, b):
    return jnp.dot(a, b, preferred_element_type=F32)


def _dot_tn(a, b):
    return lax.dot_general(a, b, TN, preferred_element_type=F32)


def _dot_nt(a, b):
    return lax.dot_general(a, b, NT, preferred_element_type=F32)


def _bf(x):
    return x.astype(BF16)


def _f32(x):
    return x.astype(F32)


def _loop(n, body, init):
    def several(i, carry):
        for u in range(LOOP_UNROLL):
            carry = body(i * LOOP_UNROLL + u, carry)
        return carry
    return lax.fori_loop(0, n // LOOP_UNROLL, several, init)


def _sum11(x):
    return jnp.sum(jnp.sum(x, axis=1, keepdims=True), axis=0, keepdims=True)


class _Dims:
    def __init__(self, seq, d_model, depth):
        self.S, self.D, self.L = seq, d_model, depth
        bw = d_model // 2
        self.BW = bw
        self.RD = bw // RET_HEADS
        self.PD = bw // POOL_GROUPS
        self.AH = bw // ATT_HEAD_DIM
        self.AKV = self.AH // ATT_GROUP
        self.AKW = self.AKV * ATT_HEAD_DIM
        self.o_rq, self.o_rk, self.o_rv, self.o_rg = 0, bw, 2 * bw, 3 * bw
        self.o_pv, self.o_pg = 4 * bw, 5 * bw
        self.o_aq = 6 * bw
        self.o_ak = 7 * bw
        self.o_av = 7 * bw + self.AKW
        self.o_ag = 7 * bw + 2 * self.AKW
        self.o_mg = 8 * bw + 2 * self.AKW
        self.INW = self.o_mg + N_BRANCHES * d_model
        self.nC = seq // CHUNK
        self.tm = min(512, seq)


def _place():
    return lax.axis_index("x"), lax.axis_index("y"), lax.axis_index("c")


def _chip_peers(x, y):
    return [(1 - x, y), (x, 1 - y), (1 - x, 1 - y)]


class _Sharded:
    def __init__(self, name, kind, a, r, cc):
        self.name, self.kind, self.A, self.R, self.Cc = name, kind, a, r, cc
        self.full = (a, r, cc)
        if kind == "col":
            self.shard = (a, r, cc // N_CHIPS)
            self.half_rows = r // 2
        else:
            self.shard = (a, r // N_CHIPS, cc)
            self.half_rows = r // N_CHIPS // 2
        self.half = (a, self.half_rows, self.shard[2])

    def in_full(self, ref, chip, core=None):
        hr = self.half_rows
        if self.kind == "col":
            rows = pl.ds(0, self.R) if core is None else pl.ds(core * hr, hr)
            return ref.at[:, rows, pl.ds(chip * self.shard[2], self.shard[2])]
        rows = pl.ds(chip * self.shard[1], self.shard[1]) if core is None else pl.ds(chip * self.shard[1] + core * hr, hr)
        return ref.at[:, rows, :]

    def in_shard(self, ref, core):
        return ref.at[:, pl.ds(core * self.half_rows, self.half_rows), :]


class _Task:
    def __init__(self, ro, rw, new, n_sem, copies):
        self.ro, self.rw, self.new, self.n_sem, self.copies = list(ro), list(rw), list(new), n_sem, copies


def _remote(src, dst, send_sem, recv_sem, k, device):
    def make():
        return pltpu.make_async_remote_copy(src_ref=src, dst_ref=dst, send_sem=send_sem.at[k], recv_sem=recv_sem.at[k],
                                            device_id=device, device_id_type=MESH)
    return make


def _gather_ici_task(bufs, specs):
    def copies(ro, rw, new, ss, rs):
        x, y, c = _place()
        me = 2 * x + y
        out = []
        for t, spec in enumerate(specs):
            for p, (px, py) in enumerate(_chip_peers(x, y)):
                mine = spec.in_full(rw[t], me, c)
                theirs = spec.in_full(rw[t], 2 * px + py, c)
                k = t * 3 + p
                out.append((_remote(mine, mine, ss, rs, k, (px, py, c)), _remote(mine, mine, ss, rs, k, (px, py, c)),
                            _remote(theirs, theirs, ss, rs, k, (px, py, c))))
        return out
    return _Task([], bufs, [], 3 * len(specs), copies)


def _gather_d2d_task(bufs, specs):
    def copies(ro, rw, new, ss, rs):
        x, y, c = _place()
        sib = (x, y, 1 - c)
        out = []
        for t, spec in enumerate(specs):
            for p, (px, py) in enumerate(_chip_peers(x, y)):
                got = spec.in_full(rw[t], 2 * px + py, c)
                gets = spec.in_full(rw[t], 2 * px + py, 1 - c)
                k = t * 3 + p
                out.append((_remote(got, got, ss, rs, k, sib), _remote(got, got, ss, rs, k, sib),
                            _remote(gets, gets, ss, rs, k, sib)))
        return out
    return _Task([], bufs, [], 3 * len(specs), copies)


def _pair_task(grads, specs):
    def copies(ro, rw, new, ss, rs):
        x, y, c = _place()
        sib = (x, y, 1 - c)
        out = []
        for t, spec in enumerate(specs):
            for j in range(N_CHIPS):
                k = t * N_CHIPS + j
                cp = _remote(spec.in_full(ro[t], j, 1 - c), new[t].at[j], ss, rs, k, sib)
                out.append((cp, cp, cp))
        return out
    return _Task(grads, [], [jax.ShapeDtypeStruct((N_CHIPS,) + s.half, BF16) for s in specs], N_CHIPS * len(specs), copies)


def _chip_task(pair_sums, specs):
    def copies(ro, rw, new, ss, rs):
        x, y, c = _place()
        out = []
        for t in range(len(specs)):
            for p, (px, py) in enumerate(_chip_peers(x, y)):
                cp = _remote(ro[t].at[2 * px + py], new[t].at[p], ss, rs, t * 3 + p, (px, py, c))
                out.append((cp, cp, cp))
        return out
    return _Task(pair_sums, [], [jax.ShapeDtypeStruct((3,) + s.half, BF16) for s in specs], 3 * len(specs), copies)


def _half_task(shards, specs):
    def copies(ro, rw, new, ss, rs):
        x, y, c = _place()
        sib = (x, y, 1 - c)
        out = []
        for t, spec in enumerate(specs):
            mine, theirs = spec.in_shard(rw[t], c), spec.in_shard(rw[t], 1 - c)
            out.append((_remote(mine, mine, ss, rs, t, sib), _remote(mine, mine, ss, rs, t, sib),
                        _remote(theirs, theirs, ss, rs, t, sib)))
        return out
    return _Task([], shards, [], len(specs), copies)


def _call(body, *, name, grid, in_specs, out_specs, out_shape, args, scratch_shapes=(), tasks=(), carry=()):
    tasks = [t for t in tasks if t is not None]
    carry = list(carry)
    n_in, n_out, n_scr, n_carry = len(in_specs), len(out_specs), len(scratch_shapes), len(carry)
    ro = [a for t in tasks for a in t.ro]
    rw = [a for t in tasks for a in t.rw]
    new = [s for t in tasks for s in t.new]
    n_ro, n_rw, n_new = len(ro), len(rw), len(new)

    def wrapped(*refs):
        ins = refs[:n_in]
        ro_refs = refs[n_in + n_carry:n_in + n_carry + n_ro]
        at = n_in + n_carry + n_ro + n_rw
        outs = refs[at:at + n_out + n_carry]
        at = at + n_out + n_carry
        rw_refs = refs[at:at + n_rw]
        new_refs = refs[at + n_rw:at + n_rw + n_new]
        at = at + n_rw + n_new
        scr = refs[at:at + n_scr]
        sems = refs[at + n_scr:]

        def task_copies():
            found, a, b, d = [], 0, 0, 0
            for i, t in enumerate(tasks):
                found += t.copies(ro_refs[a:a + len(t.ro)], rw_refs[b:b + len(t.rw)], new_refs[d:d + len(t.new)],
                                  sems[2 * i], sems[2 * i + 1])
                a, b, d = a + len(t.ro), b + len(t.rw), d + len(t.new)
            return found

        if tasks:
            first = pl.program_id(0) == 0
            last = pl.program_id(0) == grid[0] - 1
            for ax in range(1, len(grid)):
                first = first & (pl.program_id(ax) == 0)
                last = last & (pl.program_id(ax) == grid[ax] - 1)

            @pl.when(first)
            def _():
                for cp, _, _ in task_copies():
                    cp().start()

        body(*ins, *outs, *scr)

        if tasks:
            @pl.when(last)
            def _():
                found = task_copies()
                for _, _, recv in found:
                    recv().wait_recv()
                for _, send, _ in found:
                    send().wait_send()

    sem_shapes = []
    for t in tasks:
        sem_shapes += [pltpu.SemaphoreType.DMA((t.n_sem,)), pltpu.SemaphoreType.DMA((t.n_sem,))]
    aliases = {n_in + i: n_out + i for i in range(n_carry)}
    aliases.update({n_in + n_carry + n_ro + i: n_out + n_carry + i for i in range(n_rw)})
    res = pl.pallas_call(
        wrapped, name=name, grid=grid,
        in_specs=list(in_specs) + [HBM_ANY] * (n_carry + n_ro + n_rw),
        out_specs=list(out_specs) + [HBM_ANY] * (n_carry + n_rw + n_new),
        out_shape=list(out_shape) + [jax.ShapeDtypeStruct(a.shape, a.dtype) for a in carry + rw] + new,
        scratch_shapes=list(scratch_shapes) + sem_shapes,
        input_output_aliases=aliases,
        compiler_params=pltpu.CompilerParams(dimension_semantics=("arbitrary",) * len(grid),
                                             vmem_limit_bytes=VMEM_LIMIT_V7X),
    )(*args, *carry, *ro, *rw)
    own, rest = list(res[:n_out + n_carry]), list(res[n_out + n_carry:])
    per_task, b, d = [], 0, n_rw
    for t in tasks:
        per_task.append((rest[b:b + len(t.rw)], rest[d:d + len(t.new)]))
        b, d = b + len(t.rw), d + len(t.new)
    return own, per_task


def _comm_only(name, tasks):
    ro = [a for t in tasks for a in t.ro]
    rw = [a for t in tasks for a in t.rw]
    new = [s for t in tasks for s in t.new]
    n_ro, n_rw, n_new = len(ro), len(rw), len(new)

    def body(*refs):
        ro_refs = refs[:n_ro]
        rw_refs = refs[n_ro + n_rw:n_ro + 2 * n_rw]
        new_refs = refs[n_ro + 2 * n_rw:n_ro + 2 * n_rw + n_new]
        sems = refs[n_ro + 2 * n_rw + n_new:]
        found, a, b, d = [], 0, 0, 0
        for i, t in enumerate(tasks):
            found += t.copies(ro_refs[a:a + len(t.ro)], rw_refs[b:b + len(t.rw)], new_refs[d:d + len(t.new)],
                              sems[2 * i], sems[2 * i + 1])
            a, b, d = a + len(t.ro), b + len(t.rw), d + len(t.new)
        for cp, _, _ in found:
            cp().start()
        for _, _, recv in found:
            recv().wait_recv()
        for _, send, _ in found:
            send().wait_send()

    sem_shapes = []
    for t in tasks:
        sem_shapes += [pltpu.SemaphoreType.DMA((t.n_sem,)), pltpu.SemaphoreType.DMA((t.n_sem,))]
    res = pl.pallas_call(
        body, name=name,
        in_specs=[HBM_ANY] * (n_ro + n_rw), out_specs=[HBM_ANY] * (n_rw + n_new),
        out_shape=[jax.ShapeDtypeStruct(a.shape, a.dtype) for a in rw] + new,
        scratch_shapes=sem_shapes,
        input_output_aliases={n_ro + i: i for i in range(n_rw)},
    )(*ro, *rw)
    res = list(res)
    per_task, b, d = [], 0, n_rw
    for t in tasks:
        per_task.append((res[b:b + len(t.rw)], res[d:d + len(t.new)]))
        b, d = b + len(t.rw), d + len(t.new)
    return per_task


def _inproj(x, norm_g, w_in, dm, tasks=()):
    S, D, N = dm.S, dm.D, dm.INW
    tm, tn = min(1024, S), 512

    def body(x_ref, g_ref, w_ref, z_ref, h_ref):
        @pl.when(pl.program_id(1) == 0)
        def _():
            xv = x_ref[...]
            r = lax.rsqrt(jnp.mean(xv * xv, axis=-1, keepdims=True) + RMS_EPS)
            h_ref[...] = _bf(xv * r * g_ref[...])

        z_ref[...] = _bf(_dot(h_ref[...], w_ref[...]))

    return _call(
        body, name="inproj", grid=(S // tm, N // tn),
        in_specs=[pl.BlockSpec((tm, D), lambda i, j: (i, 0)),
                  pl.BlockSpec((1, D), lambda i, j: (0, 0)),
                  pl.BlockSpec((None, D, tn), lambda i, j: (0, 0, j))],
        out_specs=[pl.BlockSpec((tm, tn), lambda i, j: (i, j)),
                   pl.BlockSpec((tm, D), lambda i, j: (i, 0))],
        out_shape=[jax.ShapeDtypeStruct((S, N), BF16), jax.ShapeDtypeStruct((S, D), BF16)],
        args=(x, norm_g, w_in), tasks=tasks)


def _rot_half(x, cs, sn):
    h = cs.shape[-1]
    x1, x2 = x[:, :h], x[:, h:]
    return jnp.concatenate([x1 * cs - x2 * sn, x2 * cs + x1 * sn], axis=-1)


def _ret_tables(af_ref, ab_ref):
    C = CHUNK
    lgf = -jnp.exp(af_ref[...])[:, :1]
    lgb = -jnp.exp(ab_ref[...])[:, :1]
    ri = lax.broadcasted_iota(jnp.int32, (C, C), 0)
    ci = lax.broadcasted_iota(jnp.int32, (C, C), 1)
    lag = _f32(ri - ci)
    alag = jnp.abs(lag)
    low = lag >= 0
    dmask = jnp.where(low, jnp.exp(lgf * alag), jnp.exp(lgb * alag))
    j = _f32(lax.broadcasted_iota(jnp.int32, (C, 1), 0))
    return dict(lgf=lgf, lgb=lgb, alag=alag, low=low, dmask=dmask, j=j,
                w_f=jnp.exp(lgf * (C - 1.0 - j)), w_b=jnp.exp(lgb * j),
                q_f=jnp.exp(lgf * (j + 1.0)), q_b=jnp.exp(lgb * (C - j)),
                dec_f=jnp.exp(lgf * C), dec_b=jnp.exp(lgb * C))


def _decay_rows(a):
    return jnp.broadcast_to(a.reshape(RET_HEADS, 1, 1), (RET_HEADS, 1, 128))


def _heads(x, width):
    return jnp.stack([x[:, b * width:(b + 1) * width] for b in range(x.shape[1] // width)])


def _bdot(a, b):
    return lax.dot_general(a, b, (((2,), (1,)), ((0,), (0,))), preferred_element_type=F32)


def _bdot_nt(a, b):
    return lax.dot_general(a, b, (((2,), (2,)), ((0,), (0,))), preferred_element_type=F32)


def _bdot_tn(a, b):
    return lax.dot_general(a, b, (((1,), (1,)), ((0,), (0,))), preferred_element_type=F32)


def _ret_tables_heads(af_ref, ab_ref):
    C = CHUNK
    lgf = -jnp.exp(af_ref[...])[:, :, :1]
    lgb = -jnp.exp(ab_ref[...])[:, :, :1]
    ri = lax.broadcasted_iota(jnp.int32, (1, C, C), 1)
    ci = lax.broadcasted_iota(jnp.int32, (1, C, C), 2)
    lag = _f32(ri - ci)
    alag = jnp.abs(lag)
    low = lag >= 0
    dmask = jnp.where(low, jnp.exp(lgf * alag), jnp.exp(lgb * alag))
    j = _f32(lax.broadcasted_iota(jnp.int32, (1, C, 1), 1))
    return dict(lgf=lgf, lgb=lgb, alag=alag, low=low, dmask=dmask, j=j,
                w_f=jnp.exp(lgf * (C - 1.0 - j)), w_b=jnp.exp(lgb * j),
                q_f=jnp.exp(lgf * (j + 1.0)), q_b=jnp.exp(lgb * (C - j)),
                dec_f=jnp.exp(lgf * C), dec_b=jnp.exp(lgb * C))


def _rot_half_heads(x, cs, sn):
    h = cs.shape[-1]
    x1, x2 = x[..., :h], x[..., h:]
    return jnp.concatenate([x1 * cs - x2 * sn, x2 * cs + x1 * sn], axis=-1)


def _ret_fwd(z, a_f, a_b, cosr, sinr, dm):
    S, RD, C, nC, HB = dm.S, dm.RD, CHUNK, dm.nC, RET_HEADS_PER_STEP
    W = HB * RD
    scale = RD ** -0.5

    def body(q_ref, k_ref, v_ref, g_ref, af_ref, ab_ref, cos_ref, sin_ref, o_ref, qh_s, kh_s, sf_s, st_s):
        t = _ret_tables_heads(af_ref, ab_ref)
        st_s[...] = jnp.zeros_like(st_s)

        def fwd_pass(n, carry):
            r = pl.ds(pl.multiple_of(n * C, C), C)
            cs, sn = cos_ref[r, :], sin_ref[r, :]
            qh = _rot_half_heads(_heads(_f32(q_ref[r, :]), RD), cs, sn)
            kh = _rot_half_heads(_heads(_f32(k_ref[r, :]), RD), cs, sn) * scale
            qh_s[:, r, :] = _bf(qh)
            kh_s[:, r, :] = _bf(kh)
            st = st_s[...]
            sf_s[n] = _bf(st)
            st_s[...] = st * t["dec_f"] + _bdot_tn(_bf(kh * t["w_f"]), _heads(v_ref[r, :], RD))
            return carry

        _loop(nC, fwd_pass, 0)
        st_s[...] = jnp.zeros_like(st_s)

        def bwd_pass(i, carry):
            n = nC - 1 - i
            r = pl.ds(pl.multiple_of(n * C, C), C)
            qhb, khb, vb = qh_s[:, r, :], kh_s[:, r, :], _heads(v_ref[r, :], RD)
            p = _bf(_bdot_nt(qhb, khb) * t["dmask"])
            qhf = _f32(qhb)
            sb = st_s[...]
            out = (_bdot(p, vb) + _bdot(_bf(qhf * t["q_f"]), sf_s[n]) + _bdot(_bf(qhf * t["q_b"]), _bf(sb)))
            y = out * lax.rsqrt(jnp.mean(out * out, axis=-1, keepdims=True) + RMS_EPS)
            gate = _silu(_f32(g_ref[r, :]))
            for b in range(HB):
                o_ref[r, b * RD:(b + 1) * RD] = _bf(y[b] * gate[:, b * RD:(b + 1) * RD])
            st_s[...] = sb * t["dec_b"] + _bdot_tn(_bf(_f32(khb) * t["w_b"]), vb)
            return carry

        _loop(nC, bwd_pass, 0)

    zs = lambda off: pl.BlockSpec((S, W), lambda h: (0, off // W + h), pipeline_mode=ONE_BUFFER)
    dec = pl.BlockSpec((HB, 1, 128), lambda h: (h, 0, 0))
    tab = pl.BlockSpec((S, RD // 2), lambda h: (0, 0), pipeline_mode=ONE_BUFFER)
    own, _ = _call(
        body, name="ret_fwd", grid=(RET_HEADS // HB,),
        in_specs=[zs(dm.o_rq), zs(dm.o_rk), zs(dm.o_rv), zs(dm.o_rg), dec, dec, tab, tab],
        out_specs=[pl.BlockSpec((S, W), lambda h: (0, h))],
        out_shape=[jax.ShapeDtypeStruct((S, dm.BW), BF16)],
        scratch_shapes=[pltpu.VMEM((HB, S, RD), BF16), pltpu.VMEM((HB, S, RD), BF16),
                        pltpu.VMEM((nC, HB, RD, RD), BF16), pltpu.VMEM((HB, RD, RD), F32)],
        args=(z, z, z, z, _decay_rows(a_f), _decay_rows(a_b), cosr, sinr))
    return own[0]


def _band(first_row, first_col, lo, hi, shape):
    r = lax.broadcasted_iota(jnp.int32, shape, 0) + first_row
    c = lax.broadcasted_iota(jnp.int32, shape, 1) + first_col
    d = c - r
    return jnp.where((d >= lo) & (d <= hi), 1.0, 0.0).astype(BF16)


def _pool_counts(first_row, half, S, rows):
    pos = lax.broadcasted_iota(jnp.int32, (rows, 1), 0) + first_row
    lo = jnp.clip(pos - half, 0, S)
    hi = jnp.clip(pos + half, 0, S)
    return 1.0 / _f32(hi - lo)


def _pool_rows(S):
    return min(4 * CHUNK, S // 2)


def _pool_win_start(n, rows, S):
    return pl.multiple_of(jnp.clip(n * rows - CHUNK, 0, S - (rows + 2 * CHUNK)), CHUNK)


def _win_start(n, S):
    return pl.multiple_of(jnp.clip((n - 1) * CHUNK, 0, S - 3 * CHUNK), CHUNK)


def _pool_fwd(z, pool_w, pool_scale, dm):
    S, PD = dm.S, dm.PD
    PB = _pool_rows(S)
    WIN = PB + 2 * CHUNK

    def body(u_ref, g_ref, w_ref, sc_ref, o_ref):
        half = jnp.left_shift(1, pl.program_id(0))
        w = w_ref[...]
        sc = sc_ref[...]

        def blk(n, carry):
            row0 = pl.multiple_of(n * PB, PB)
            r = pl.ds(row0, PB)
            st = _pool_win_start(n, PB, S)
            band = _band(row0, st, -half, half - 1, (PB, WIN))
            mean = _dot(band, u_ref[pl.ds(st, WIN), :]) * _pool_counts(row0, half, S, PB)
            p = mean - _f32(u_ref[r, :])
            y = _dot(_bf(p), w) * sc
            o_ref[r, :] = _bf(y * _silu(_f32(g_ref[r, :])))
            return carry

        _loop(S // PB, blk, 0)

    own, _ = _call(
        body, name="pool_fwd", grid=(POOL_GROUPS,),
        in_specs=[pl.BlockSpec((S, PD), lambda g: (0, dm.o_pv // PD + g)),
                  pl.BlockSpec((S, PD), lambda g: (0, dm.o_pg // PD + g)),
                  pl.BlockSpec((None, PD, PD), lambda g: (g, 0, 0)),
                  pl.BlockSpec((None, 1, PD), lambda g: (g, 0, 0))],
        out_specs=[pl.BlockSpec((S, PD), lambda g: (0, g))],
        out_shape=[jax.ShapeDtypeStruct((S, dm.BW), BF16)],
        args=(z, z, pool_w, pool_scale))
    return own[0]


def _rot_part(x, cs, s_up, s_dn):
    h = ROPE_DIMS // 2
    lanes = x.ndim - 1
    return x * cs + pltpu.roll(x, ATT_HEAD_DIM - h, lanes) * s_up + pltpu.roll(x, h, lanes) * s_dn


def _att_tables(S):
    h = ROPE_DIMS // 2
    inv = ROPE_THETA ** (-jnp.arange(h, dtype=F32) / h)
    ang = jnp.arange(S, dtype=F32)[:, None] * inv[None, :]
    cos, sin = jnp.cos(ang), jnp.sin(ang)
    pad = jnp.zeros((S, ATT_HEAD_DIM - 2 * h), F32)
    zero = jnp.zeros((S, h), F32)
    cs = jnp.concatenate([cos, cos, pad + 1.0], axis=1)
    s_up = jnp.concatenate([-sin, zero, pad], axis=1)
    s_dn = jnp.concatenate([zero, sin, pad], axis=1)
    return cs, s_up, s_dn


def _ret_rope_tables(S, RD):
    h = RD // 2
    inv = 1.0 / (RET_ROPE_BASE ** jnp.linspace(0.0, 1.0, h, dtype=F32))
    ang = jnp.arange(S, dtype=F32)[:, None] * inv[None, :]
    return jnp.cos(ang), jnp.sin(ang)


def _att_norm_rot(x, gain, cs, s_up, s_dn):
    u = x * lax.rsqrt(jnp.mean(x * x, axis=-1, keepdims=True) + RMS_EPS)
    return _rot_part(u * gain, cs, s_up, s_dn)


def _att_bias_tables(bias_s):
    C = CHUNK
    shape = (ATT_GROUP * C, 3 * C)
    rel = lax.broadcasted_iota(jnp.int32, shape, 1) - lax.broadcasted_iota(jnp.int32, shape, 0) % C
    for which, shift in enumerate((-C, 0, -2 * C)):
        bias_s[which] = jnp.where(jnp.abs(rel + shift) <= CHUNK, 0.0, NEG_BIG)


def _att_bias(bias_s, n, nC):
    return bias_s[jnp.where(n == 0, 1, jnp.where(n == nC - 1, 2, 0))]


def _att_probs(q4, kw, sink_col, bias):
    s = _dot_nt(q4, kw) * (ATT_HEAD_DIM ** -0.5) + bias
    m = jnp.maximum(jnp.max(s, axis=-1, keepdims=True), sink_col)
    e = jnp.exp(s - m)
    es = jnp.exp(sink_col - m)
    inv = 1.0 / (jnp.sum(e, axis=-1, keepdims=True) + es)
    return e * inv, es * inv


def _sink_col(sink_ref):
    head = lax.broadcasted_iota(jnp.int32, (ATT_GROUP * CHUNK, 1), 0) // CHUNK
    col = jnp.zeros((ATT_GROUP * CHUNK, 1), F32)
    for g in range(ATT_GROUP):
        col = jnp.where(head == g, sink_ref[g:g + 1, :1], col)
    return col


def _sink_rows(sink, dm):
    return jnp.broadcast_to(sink.reshape(dm.AKV, ATT_GROUP, 1), (dm.AKV, ATT_GROUP, 128))


def _att_prep(q_ref, k_ref, qg_ref, kg_ref, cs_ref, su_ref, sd_ref, qn_s, kn_s, nC):
    C, HD = CHUNK, ATT_HEAD_DIM

    def prep(n, carry):
        r = pl.ds(pl.multiple_of(n * C, C), C)
        cs, su, sd = cs_ref[r, :], su_ref[r, :], sd_ref[r, :]
        kn_s[r, :] = _bf(_att_norm_rot(_f32(k_ref[r, :]), kg_ref[...], cs, su, sd))
        qn_s[:, r, :] = _bf(_att_norm_rot(_heads(_f32(q_ref[r, :]), HD), qg_ref[...], cs, su, sd))
        return carry

    lax.fori_loop(0, nC, prep, 0)


def _att_fwd(z, q_gain, k_gain, sink, tabs, dm, tasks=()):
    S, C, nC, HD, G = dm.S, CHUNK, dm.nC, ATT_HEAD_DIM, ATT_GROUP
    GW = G * HD

    KV = dm.AKV

    def body(q_ref, k_ref, v_ref, g_ref, qg_ref, kg_ref, sk_ref, cs_ref, su_ref, sd_ref, o_ref, qn_s, kn_s, bias_s):
        def prep(n, carry):
            r = pl.ds(pl.multiple_of(n * C, C), C)
            cs, su, sd = cs_ref[r, :], su_ref[r, :], sd_ref[r, :]
            kn_s[:, r, :] = _bf(_att_norm_rot(_heads(_f32(k_ref[r, :]), HD), kg_ref[...], cs, su, sd))
            qn_s[:, r, :] = _bf(_att_norm_rot(_heads(_f32(q_ref[r, :]), HD), qg_ref[...], cs, su, sd))
            return carry

        lax.fori_loop(0, nC, prep, 0)
        _att_bias_tables(bias_s)
        sink_col = jnp.stack([_sink_col(sk_ref.at[h]) for h in range(KV)])

        def blk(n, carry):
            row0 = pl.multiple_of(n * C, C)
            r = pl.ds(row0, C)
            w = pl.ds(_win_start(n, S), 3 * C)
            q4 = qn_s[:, r, :].reshape(KV, G * C, HD)
            s = _bdot_nt(q4, kn_s[:, w, :]) * (HD ** -0.5) + _att_bias(bias_s, n, nC)
            m = jnp.maximum(jnp.max(s, axis=-1, keepdims=True), sink_col)
            e = jnp.exp(s - m)
            inv = 1.0 / (jnp.sum(e, axis=-1, keepdims=True) + jnp.exp(sink_col - m))
            o = _bdot(_bf(e * inv), _heads(v_ref[w, :], HD))
            gate = _silu(_f32(g_ref[r, :]))
            for h in range(KV):
                for g in range(G):
                    cols = slice((h * G + g) * HD, (h * G + g + 1) * HD)
                    o_ref[r, cols] = _bf(o[h, g * C:(g + 1) * C, :] * gate[:, cols])
            return carry

        _loop(nC, blk, 0)

    tab = pl.BlockSpec((S, HD), lambda h: (0, 0), pipeline_mode=ONE_BUFFER)
    gain = pl.BlockSpec((1, HD), lambda h: (0, 0))
    one = lambda off, width: pl.BlockSpec((pl.Element(S), pl.Element(width)), lambda h: (0, off),
                                          pipeline_mode=ONE_BUFFER)
    return _call(
        body, name="att_fwd", grid=(1,),
        in_specs=[one(dm.o_aq, dm.BW), one(dm.o_ak, dm.AKW), one(dm.o_av, dm.AKW), one(dm.o_ag, dm.BW),
                  gain, gain, pl.BlockSpec((KV, G, 128), lambda h: (0, 0, 0)), tab, tab, tab],
        out_specs=[pl.BlockSpec((S, dm.BW), lambda h: (0, 0), pipeline_mode=ONE_BUFFER)],
        out_shape=[jax.ShapeDtypeStruct((S, dm.BW), BF16)],
        scratch_shapes=[pltpu.VMEM((KV * G, S, HD), BF16), pltpu.VMEM((KV, S, HD), BF16),
                        pltpu.VMEM((3, G * C, 3 * C), F32)],
        args=(z, z, z, z, q_gain, k_gain, _sink_rows(sink, dm), *tabs), tasks=tasks)


def _merge_gate_spec(tm, dm):
    return pl.BlockSpec((pl.Element(tm), pl.Element(N_BRANCHES * dm.D)), lambda i: (i * tm, dm.o_mg))


def _merge_out(x, mg, ya_pre, yb_pre, yc_pre, w_ret, w_pool, w_att, w_out, dm, tasks=()):
    S, D, BW = dm.S, dm.D, dm.BW
    tm = min(256, S)

    def body(x_ref, mg_ref, a_ref, b_ref, c_ref, wr_ref, wp_ref, wa_ref, wo_ref, xo_ref, ya_ref, yb_ref, yc_ref, m_ref):
        ya = _dot(a_ref[...], wr_ref[...])
        yb = _dot(b_ref[...], wp_ref[...])
        yc = _dot(c_ref[...], wa_ref[...])
        ya_ref[...], yb_ref[...], yc_ref[...] = _bf(ya), _bf(yb), _bf(yc)
        g0 = _sigmoid(_f32(mg_ref[:, 0:D]))
        g1 = _sigmoid(_f32(mg_ref[:, D:2 * D]))
        g2 = _sigmoid(_f32(mg_ref[:, 2 * D:3 * D]))
        merged = _bf(g0 * ya + g1 * yb + g2 * yc)
        m_ref[...] = merged
        xo_ref[...] = x_ref[...] + _dot(merged, wo_ref[...])

    act = lambda w: pl.BlockSpec((tm, w), lambda i: (i, 0))
    wsp = lambda r: pl.BlockSpec((None, r, D), lambda i: (0, 0, 0), pipeline_mode=ONE_BUFFER)
    return _call(
        body, name="merge_out", grid=(S // tm,),
        in_specs=[act(D), _merge_gate_spec(tm, dm), act(BW), act(BW), act(BW), wsp(BW), wsp(BW), wsp(BW), wsp(D)],
        out_specs=[act(D)] * 5,
        out_shape=[jax.ShapeDtypeStruct((S, D), F32)] + [jax.ShapeDtypeStruct((S, D), BF16)] * 4,
        args=(x, mg, ya_pre, yb_pre, yc_pre, w_ret, w_pool, w_att, w_out), tasks=tasks)


def _loss_grad(y, target, dm):
    S, D = dm.S, dm.D
    tm = dm.tm

    def body(y_ref, t_ref, s_ref, d_ref):
        @pl.when(pl.program_id(0) == 0)
        def _():
            s_ref[...] = jnp.zeros_like(s_ref)

        e = y_ref[...] - t_ref[...]
        d_ref[...] = e * (1.0 / D)
        s_ref[...] += jnp.sum(jnp.sum(e * e, axis=-1, keepdims=True), axis=0, keepdims=True)

    row = pl.BlockSpec((tm, D), lambda i: (i, 0))
    own, _ = _call(
        body, name="loss_grad", grid=(S // tm,), in_specs=[row, row],
        out_specs=[pl.BlockSpec((1, 128), lambda i: (0, 0)), row],
        out_shape=[jax.ShapeDtypeStruct((1, 128), F32), jax.ShapeDtypeStruct((S, D), F32)],
        args=(y, target))
    return own


def _merge_bwd_gates(dx, mg, ya, yb, yc, w_out, dm, tasks=()):
    S, D = dm.S, dm.D
    tm = min(256, S)

    def body(dx_ref, mg_ref, ya_ref, yb_ref, yc_ref, wo_ref, dmg_ref, dya_ref, dyb_ref, dyc_ref, dxb_ref):
        dxb = _bf(dx_ref[...])
        dxb_ref[...] = dxb
        dm_ = _dot_nt(dxb, wo_ref[...])
        for k, (y_ref, dy_ref) in enumerate(((ya_ref, dya_ref), (yb_ref, dyb_ref), (yc_ref, dyc_ref))):
            g = _sigmoid(_f32(mg_ref[:, k * D:(k + 1) * D]))
            dmg_ref[:, k * D:(k + 1) * D] = _bf(dm_ * _f32(y_ref[...]) * g * (1.0 - g))
            dy_ref[...] = _bf(dm_ * g)

    act = lambda w: pl.BlockSpec((tm, w), lambda i: (i, 0))
    return _call(
        body, name="merge_bwd_gates", grid=(S // tm,),
        in_specs=[act(D), _merge_gate_spec(tm, dm), act(D), act(D), act(D),
                  pl.BlockSpec((None, D, D), lambda i: (0, 0, 0), pipeline_mode=ONE_BUFFER)],
        out_specs=[_merge_gate_spec(tm, dm), act(D), act(D), act(D), act(D)],
        out_shape=[jax.ShapeDtypeStruct((S, dm.INW), BF16)] + [jax.ShapeDtypeStruct((S, D), BF16)] * 4,
        args=(dx, mg, ya, yb, yc, w_out), tasks=tasks)


def _merge_bwd_proj(dya, dyb, dyc, w_ret, w_pool, w_att, dm):
    S, D, BW = dm.S, dm.D, dm.BW
    tm = dm.tm

    def body(da_ref, db_ref, dc_ref, wr_ref, wp_ref, wa_ref, oa_ref, ob_ref, oc_ref):
        oa_ref[...] = _bf(_dot_nt(da_ref[...], wr_ref[...]))
        ob_ref[...] = _bf(_dot_nt(db_ref[...], wp_ref[...]))
        oc_ref[...] = _bf(_dot_nt(dc_ref[...], wa_ref[...]))

    act = lambda w: pl.BlockSpec((tm, w), lambda i: (i, 0))
    wsp = pl.BlockSpec((None, BW, D), lambda i: (0, 0, 0), pipeline_mode=ONE_BUFFER)
    own, _ = _call(
        body, name="merge_bwd_proj", grid=(S // tm,),
        in_specs=[act(D)] * 3 + [wsp] * 3, out_specs=[act(BW)] * 3,
        out_shape=[jax.ShapeDtypeStruct((S, BW), BF16)] * 3,
        args=(dya, dyb, dyc, w_ret, w_pool, w_att))
    return own


def _col_writes(dz_ref, sem, step, parts):
    return [pltpu.make_async_copy(src, dz_ref.at[:, pl.ds(pl.multiple_of(off + step * w, 128), w)], sem.at[k])
            for k, (src, off, w) in enumerate(parts)]


def _ret_bwd(z, dz, dya_pre, a_f, a_b, cosr, sinr, dm, tasks=()):
    S, RD, C, nC = dm.S, dm.RD, CHUNK, dm.nC
    scale = RD ** -0.5
    H = RET_HEADS

    def body(q_ref, k_ref, v_ref, g_ref, dy_ref, af_ref, ab_ref, cos_ref, sin_ref, dd_ref, dz_ref,
             qh_s, kh_s, sf_s, sb_s, do_s, dqh_s, dkh_s, dv_s, st_s, lam_s, dq_ref, dk_ref, dv_ref, dg_ref, wsem):
        head = pl.program_id(0)
        writes = lambda step: _col_writes(dz_ref, wsem, step, [(dq_ref, dm.o_rq, RD), (dk_ref, dm.o_rk, RD),
                                                               (dv_ref, dm.o_rv, RD), (dg_ref, dm.o_rg, RD)])
        t = _ret_tables(af_ref, ab_ref)
        zero_rr = jnp.zeros((RD, RD), F32)

        st_s[...] = zero_rr

        def pass0(n, carry):
            r = pl.ds(pl.multiple_of(n * C, C), C)
            cs, sn = cos_ref[r, :], sin_ref[r, :]
            qh = _rot_half(_f32(q_ref[r, :]), cs, sn)
            kh = _rot_half(_f32(k_ref[r, :]), cs, sn) * scale
            qh_s[r, :] = _bf(qh)
            kh_s[r, :] = _bf(kh)
            st = st_s[...]
            sf_s[n] = _bf(st)
            st_s[...] = st * t["dec_f"] + _dot_tn(_bf(kh * t["w_f"]), v_ref[r, :])
            return carry

        _loop(nC, pass0, 0)

        @pl.when(head > 0)
        def _():
            for cp in writes(head - 1):
                cp.wait()

        st_s[...] = zero_rr
        lam_s[...] = zero_rr

        def pass1(i, acc):
            glf, glb = acc
            n = nC - 1 - i
            r = pl.ds(pl.multiple_of(n * C, C), C)
            qhb, khb, vb = qh_s[r, :], kh_s[r, :], v_ref[r, :]
            qhf, khf = _f32(qhb), _f32(khb)
            sc_ = _dot_nt(qhb, khb)
            p = _bf(sc_ * t["dmask"])
            sb = st_s[...]
            sbb = _bf(sb)
            sb_s[n] = sbb
            sfb = sf_s[n]
            qf_b, qb_b = _bf(qhf * t["q_f"]), _bf(qhf * t["q_b"])
            out = _dot(p, vb) + _dot(qf_b, sfb) + _dot(qb_b, sbb)
            rr = lax.rsqrt(jnp.mean(out * out, axis=-1, keepdims=True) + RMS_EPS)
            y = out * rr
            g = _f32(g_ref[r, :])
            dya = _f32(dy_ref[r, :])
            dg_ref[r, :] = _bf(dya * y * _dsilu(g))
            dyn = dya * _silu(g)
            dout = rr * (dyn - y * jnp.mean(dyn * y, axis=-1, keepdims=True))
            dob = _bf(dout)
            do_s[r, :] = dob
            dp = _dot_nt(dob, vb)
            dv = _dot_tn(p, dob)
            ds = _bf(dp * t["dmask"])
            dqh = _dot(ds, khb)
            dkh = _dot_tn(ds, qhb)
            dd = dp * sc_ * t["dmask"] * t["alag"]
            glf = glf + _sum11(jnp.where(t["low"], dd, 0.0))
            glb = glb + _sum11(jnp.where(t["low"], 0.0, dd))
            tf = _dot_nt(dob, sfb)
            tb = _dot_nt(dob, sbb)
            dqh = dqh + tf * t["q_f"] + tb * t["q_b"]
            glf = glf + _sum11(jnp.sum(tf * qhf, axis=-1, keepdims=True) * t["q_f"] * (t["j"] + 1.0))
            glb = glb + _sum11(jnp.sum(tb * qhf, axis=-1, keepdims=True) * t["q_b"] * (C - t["j"]))
            lam = lam_s[...]
            lamb = _bf(lam)
            glf = glf + _sum11(lam * _f32(sfb)) * t["dec_f"] * C
            u = _dot_nt(vb, lamb)
            dkh = dkh + u * t["w_f"]
            glf = glf + _sum11(jnp.sum(u * khf, axis=-1, keepdims=True) * t["w_f"] * (C - 1.0 - t["j"]))
            dv = dv + _dot(_bf(khf * t["w_f"]), lamb)
            lam_s[...] = _dot_tn(qf_b, dob) + lam * t["dec_f"]
            dqh_s[r, :] = dqh
            dkh_s[r, :] = dkh
            dv_s[r, :] = dv
            st_s[...] = sb * t["dec_b"] + _dot_tn(_bf(khf * t["w_b"]), vb)
            return glf, glb

        glf, glb = _loop(nC, pass1, (jnp.zeros((1, 1), F32), jnp.zeros((1, 1), F32)))

        lam_s[...] = zero_rr

        def pass2(n, glb):
            r = pl.ds(pl.multiple_of(n * C, C), C)
            qhb, khb, vb, dob = qh_s[r, :], kh_s[r, :], v_ref[r, :], do_s[r, :]
            qhf, khf = _f32(qhb), _f32(khb)
            lam = lam_s[...]
            lamb = _bf(lam)
            glb = glb + _sum11(lam * _f32(sb_s[n])) * t["dec_b"] * C
            u = _dot_nt(vb, lamb)
            dkh = dkh_s[r, :] + u * t["w_b"]
            glb = glb + _sum11(jnp.sum(u * khf, axis=-1, keepdims=True) * t["w_b"] * t["j"])
            dv = dv_s[r, :] + _dot(_bf(khf * t["w_b"]), lamb)
            lam_s[...] = _dot_tn(_bf(qhf * t["q_b"]), dob) + lam * t["dec_b"]
            cs, sn = cos_ref[r, :], sin_ref[r, :]
            dq_ref[r, :] = _bf(_rot_half(dqh_s[r, :], cs, -sn))
            dk_ref[r, :] = _bf(_rot_half(dkh * scale, cs, -sn))
            dv_ref[r, :] = _bf(dv)
            return glb

        glb = _loop(nC, pass2, glb)
        row = lax.broadcasted_iota(jnp.int32, (8, 128), 0)
        da_f = glf * t["lgf"]
        da_b = glb * t["lgb"]
        dd_ref[...] = jnp.where(row == 0, da_f, jnp.where(row == 1, da_b, 0.0))
        for cp in writes(head):
            cp.start()

        @pl.when(head == H - 1)
        def _():
            for cp in writes(head):
                cp.wait()

    zs = lambda off: pl.BlockSpec((S, RD), lambda h: (0, off // RD + h), pipeline_mode=ONE_BUFFER)
    col = pl.BlockSpec((S, RD), lambda h: (0, h), pipeline_mode=ONE_BUFFER)
    dec = pl.BlockSpec((None, 1, 128), lambda h: (h, 0, 0))
    tab = pl.BlockSpec((S, RD // 2), lambda h: (0, 0), pipeline_mode=ONE_BUFFER)
    return _call(
        body, name="ret_bwd", grid=(H,),
        in_specs=[zs(dm.o_rq), zs(dm.o_rk), zs(dm.o_rv), zs(dm.o_rg), col, dec, dec, tab, tab],
        out_specs=[pl.BlockSpec((None, 8, 128), lambda h: (h, 0, 0))],
        out_shape=[jax.ShapeDtypeStruct((H, 8, 128), F32)],
        scratch_shapes=[pltpu.VMEM((S, RD), BF16), pltpu.VMEM((S, RD), BF16),
                        pltpu.VMEM((nC, RD, RD), BF16), pltpu.VMEM((nC, RD, RD), BF16),
                        pltpu.VMEM((S, RD), BF16),
                        pltpu.VMEM((S, RD), F32), pltpu.VMEM((S, RD), F32), pltpu.VMEM((S, RD), F32),
                        pltpu.VMEM((RD, RD), F32), pltpu.VMEM((RD, RD), F32)]
        + [pltpu.VMEM((S, RD), BF16)] * 4 + [pltpu.SemaphoreType.DMA((4,))],
        args=(z, z, z, z, dya_pre, _decay_rows(a_f), _decay_rows(a_b), cosr, sinr), tasks=tasks, carry=[dz])


def _pool_bwd(z, dz, dyb_pre, pool_w, pool_scale, dm):
    S, PD = dm.S, dm.PD
    PB = _pool_rows(S)
    WIN = PB + 2 * CHUNK
    G = POOL_GROUPS

    def body(u_ref, g_ref, dy_ref, w_ref, sc_ref, dw_ref, dsc_ref, dz_ref, dp_s, dpc_s, dw_s, du_ref, dg_ref, wsem):
        group = pl.program_id(0)
        writes = lambda step: _col_writes(dz_ref, wsem, step, [(du_ref, dm.o_pv, PD), (dg_ref, dm.o_pg, PD)])
        half = jnp.left_shift(1, group)
        w = w_ref[...]
        sc = sc_ref[...]
        dw_s[...] = jnp.zeros_like(dw_s)
        dsc_ref[...] = jnp.zeros_like(dsc_ref)

        @pl.when(group > 0)
        def _():
            for cp in writes(group - 1):
                cp.wait()

        def blk1(n, carry):
            row0 = pl.multiple_of(n * PB, PB)
            r = pl.ds(row0, PB)
            st = _pool_win_start(n, PB, S)
            band = _band(row0, st, -half, half - 1, (PB, WIN))
            inv = _pool_counts(row0, half, S, PB)
            pb = _bf(_dot(band, u_ref[pl.ds(st, WIN), :]) * inv - _f32(u_ref[r, :]))
            ylin = _dot(pb, w)
            y = ylin * sc
            g = _f32(g_ref[r, :])
            dyb = _f32(dy_ref[r, :])
            dg_ref[r, :] = _bf(dyb * y * _dsilu(g))
            dy = dyb * _silu(g)
            dsc_ref[...] += jnp.sum(dy * ylin, axis=0, keepdims=True)
            dyl = _bf(dy * sc)
            dw_s[...] += _dot_tn(pb, dyl)
            dp = _dot_nt(dyl, w)
            dp_s[r, :] = dp
            dpc_s[r, :] = _bf(dp * inv)
            return carry

        _loop(S // PB, blk1, 0)
        dw_ref[...] = _bf(dw_s[...])

        def blk2(n, carry):
            row0 = pl.multiple_of(n * PB, PB)
            r = pl.ds(row0, PB)
            st = _pool_win_start(n, PB, S)
            band_t = _band(row0, st, -half + 1, half, (PB, WIN))
            du_ref[r, :] = _bf(_dot(band_t, dpc_s[pl.ds(st, WIN), :]) - dp_s[r, :])
            return carry

        _loop(S // PB, blk2, 0)
        for cp in writes(group):
            cp.start()

        @pl.when(group == G - 1)
        def _():
            for cp in writes(group):
                cp.wait()

    own, _ = _call(
        body, name="pool_bwd", grid=(G,),
        in_specs=[pl.BlockSpec((S, PD), lambda g: (0, dm.o_pv // PD + g)),
                  pl.BlockSpec((S, PD), lambda g: (0, dm.o_pg // PD + g)),
                  pl.BlockSpec((S, PD), lambda g: (0, g)),
                  pl.BlockSpec((None, PD, PD), lambda g: (g, 0, 0)),
                  pl.BlockSpec((None, 1, PD), lambda g: (g, 0, 0))],
        out_specs=[pl.BlockSpec((None, PD, PD), lambda g: (g, 0, 0)), pl.BlockSpec((None, 1, PD), lambda g: (g, 0, 0))],
        out_shape=[jax.ShapeDtypeStruct((G, PD, PD), BF16), jax.ShapeDtypeStruct((G, 1, PD), F32)],
        scratch_shapes=[pltpu.VMEM((S, PD), F32), pltpu.VMEM((S, PD), BF16), pltpu.VMEM((PD, PD), F32),
                        pltpu.VMEM((S, PD), BF16), pltpu.VMEM((S, PD), BF16), pltpu.SemaphoreType.DMA((2,))],
        args=(z, z, dyb_pre, pool_w, pool_scale), carry=[dz])
    return own


def _att_bwd(z, dz, dyc_pre, q_gain, k_gain, sink, tabs, dm, tasks=()):
    S, C, nC, HD, G = dm.S, CHUNK, dm.nC, ATT_HEAD_DIM, ATT_GROUP
    GW = G * HD
    scale = HD ** -0.5

    def body(q_ref, k_ref, v_ref, g_ref, dy_ref, qg_ref, kg_ref, sk_ref, cs_ref, su_ref, sd_ref,
             sm_ref, dz_ref, qn_s, kn_s, dqn_s, dkn_s, dv_s, bias_s, dq_ref, dk_ref, dv_ref, dg_ref, wsem):
        kv = pl.program_id(0)
        writes = lambda step: _col_writes(dz_ref, wsem, step, [(dq_ref, dm.o_aq, GW), (dk_ref, dm.o_ak, HD),
                                                               (dv_ref, dm.o_av, HD), (dg_ref, dm.o_ag, GW)])
        _att_prep(q_ref, k_ref, qg_ref, kg_ref, cs_ref, su_ref, sd_ref, qn_s, kn_s, nC)

        @pl.when(kv > 0)
        def _():
            for cp in writes(kv - 1):
                cp.wait()

        _att_bias_tables(bias_s)
        dkn_s[...] = jnp.zeros_like(dkn_s)
        dv_s[...] = jnp.zeros_like(dv_s)
        sink_col = _sink_col(sk_ref)

        def blk(n, dsink):
            row0 = pl.multiple_of(n * C, C)
            r = pl.ds(row0, C)
            w = pl.ds(_win_start(n, S), 3 * C)
            q4 = qn_s[:, r, :].reshape(G * C, HD)
            kw, vw = kn_s[w, :], v_ref[w, :]
            p, ps = _att_probs(q4, kw, sink_col, _att_bias(bias_s, n, nC))
            pb = _bf(p)
            o = _dot(pb, vw)
            do_parts = []
            for g in range(G):
                cols = slice(g * HD, (g + 1) * HD)
                gate = _f32(g_ref[r, cols])
                dy = _f32(dy_ref[r, cols])
                dg_ref[r, cols] = _bf(dy * o[g * C:(g + 1) * C, :] * _dsilu(gate))
                do_parts.append(dy * _silu(gate))
            dob = _bf(jnp.concatenate(do_parts, axis=0))
            dp = _dot_nt(dob, vw)
            drow = jnp.sum(p * dp, axis=-1, keepdims=True)
            ds = _bf(p * (dp - drow))
            dsink = dsink - ps * drow
            dqn_s[:, r, :] = (_dot(ds, kw) * scale).reshape(G, C, HD)
            dkn_s[w, :] += _dot_tn(ds, q4) * scale
            dv_s[w, :] += _dot_tn(pb, dob)
            return dsink

        dsink = _loop(nC, blk, jnp.zeros((G * C, 1), F32))

        def fin(n, acc):
            dqg, dkg = acc
            r = pl.ds(pl.multiple_of(n * C, C), C)
            cs, su, sd = cs_ref[r, :], su_ref[r, :], sd_ref[r, :]

            def norm_bwd(x, gain, dqn):
                rr = lax.rsqrt(jnp.mean(x * x, axis=-1, keepdims=True) + RMS_EPS)
                u = x * rr
                dw = _rot_part(dqn, cs, -su, -sd)
                du = dw * gain
                gsum = dw * u
                if gsum.ndim == 3:
                    gsum = jnp.sum(gsum, axis=0)
                return rr * (du - u * jnp.mean(du * u, axis=-1, keepdims=True)), jnp.sum(gsum, axis=0, keepdims=True)

            dk, gk = norm_bwd(_f32(k_ref[r, :]), kg_ref[...], dkn_s[r, :])
            dk_ref[r, :] = _bf(dk)
            dv_ref[r, :] = _bf(dv_s[r, :])
            dq, gq = norm_bwd(_heads(_f32(q_ref[r, :]), HD), qg_ref[...], dqn_s[:, r, :])
            for g in range(G):
                dq_ref[r, g * HD:(g + 1) * HD] = _bf(dq[g])
            return dqg + gq, dkg + gk

        dqg, dkg = lax.fori_loop(0, nC, fin, (jnp.zeros((1, HD), F32), jnp.zeros((1, HD), F32)))
        sm_ref[...] = jnp.zeros_like(sm_ref)
        sm_ref[0:1, :] = dqg
        sm_ref[1:2, :] = dkg
        for g in range(G):
            sm_ref[2 + g:3 + g, :] = jnp.broadcast_to(_sum11(dsink[g * C:(g + 1) * C, :]), (1, HD))
        for cp in writes(kv):
            cp.start()

        @pl.when(kv == dm.AKV - 1)
        def _():
            for cp in writes(kv):
                cp.wait()

    tab = pl.BlockSpec((S, HD), lambda h: (0, 0), pipeline_mode=ONE_BUFFER)
    gain = pl.BlockSpec((1, HD), lambda h: (0, 0))
    wide = lambda off: pl.BlockSpec((S, GW), lambda h: (0, off // GW + h), pipeline_mode=ONE_BUFFER)
    thin = lambda off: pl.BlockSpec((S, HD), lambda h: (0, off // HD + h), pipeline_mode=ONE_BUFFER)
    return _call(
        body, name="att_bwd", grid=(dm.AKV,),
        in_specs=[wide(dm.o_aq), thin(dm.o_ak), thin(dm.o_av), wide(dm.o_ag), wide(0), gain, gain,
                  pl.BlockSpec((None, G, 128), lambda h: (h, 0, 0)), tab, tab, tab],
        out_specs=[pl.BlockSpec((None, 8, 128), lambda h: (h, 0, 0))],
        out_shape=[jax.ShapeDtypeStruct((dm.AKV, 8, 128), F32)],
        scratch_shapes=[pltpu.VMEM((G, S, HD), BF16), pltpu.VMEM((S, HD), BF16),
                        pltpu.VMEM((G, S, HD), F32), pltpu.VMEM((S, HD), F32), pltpu.VMEM((S, HD), F32),
                        pltpu.VMEM((3, G * C, 3 * C), F32),
                        pltpu.VMEM((S, GW), BF16), pltpu.VMEM((S, HD), BF16), pltpu.VMEM((S, HD), BF16),
                        pltpu.VMEM((S, GW), BF16), pltpu.SemaphoreType.DMA((4,))],
        args=(z, z, z, z, dyc_pre, q_gain, k_gain, _sink_rows(sink, dm), *tabs), tasks=tasks, carry=[dz])


def _grad_matmul(a, b, name, tasks=()):
    S, M = a.shape
    N = b.shape[1]
    tm, tn = min(1024, M), min(512, N)

    def body(a_ref, b_ref, o_ref):
        o_ref[...] = _bf(_dot_tn(a_ref[...], b_ref[...]))

    own, tk = _call(
        body, name=name, grid=(M // tm, N // tn),
        in_specs=[pl.BlockSpec((S, tm), lambda i, j: (0, i)), pl.BlockSpec((S, tn), lambda i, j: (0, j))],
        out_specs=[pl.BlockSpec((None, tm, tn), lambda i, j: (0, i, j))],
        out_shape=[jax.ShapeDtypeStruct((1, M, N), BF16)],
        args=(a, b), tasks=tasks)
    return own[0], tk


def _inproj_bwd(dz, w_in, dm, tasks=()):
    S, D = dm.S, dm.D
    tm = dm.tm
    tk = dm.INW // N_CHIPS if (dm.INW // N_CHIPS) % 128 == 0 else 512
    nk = dm.INW // tk

    def body(dz_ref, w_ref, dh_ref):
        part = _dot_nt(dz_ref[...], w_ref[...])

        @pl.when(pl.program_id(1) == 0)
        def _():
            dh_ref[...] = part

        @pl.when(pl.program_id(1) != 0)
        def _():
            dh_ref[...] += part

    return _call(
        body, name="inproj_bwd", grid=(S // tm, nk),
        in_specs=[pl.BlockSpec((tm, tk), lambda i, k: (i, k)), pl.BlockSpec((None, D, tk), lambda i, k: (0, 0, k))],
        out_specs=[pl.BlockSpec((tm, D), lambda i, k: (i, 0))],
        out_shape=[jax.ShapeDtypeStruct((S, D), F32)],
        args=(dz, w_in), tasks=tasks)


def _norm_bwd(dh, x, norm_g, dx_out, dm):
    S, D = dm.S, dm.D
    tm = dm.tm

    def body(dh_ref, x_ref, g_ref, dxo_ref, dx_ref, dg_ref):
        @pl.when(pl.program_id(0) == 0)
        def _():
            dg_ref[...] = jnp.zeros_like(dg_ref)

        xv = x_ref[...]
        rr = lax.rsqrt(jnp.mean(xv * xv, axis=-1, keepdims=True) + RMS_EPS)
        u = xv * rr
        dh = dh_ref[...]
        dg_ref[...] += jnp.sum(dh * u, axis=0, keepdims=True)
        du = dh * g_ref[...]
        dx_ref[...] = dxo_ref[...] + rr * (du - u * jnp.mean(du * u, axis=-1, keepdims=True))

    row = pl.BlockSpec((tm, D), lambda i: (i, 0))
    vec = pl.BlockSpec((1, D), lambda i: (0, 0))
    own, _ = _call(
        body, name="norm_bwd", grid=(S // tm,), in_specs=[row, row, vec, row], out_specs=[row, vec],
        out_shape=[jax.ShapeDtypeStruct((S, D), F32), jax.ShapeDtypeStruct((1, D), F32)],
        args=(dh, x, norm_g, dx_out))
    return own


def _row_block(rows, width, itemsize):
    target = max(16, (2 * 1024 * 1024) // (width * itemsize))
    for rb in range(min(rows, target), 0, -1):
        if rows % rb == 0 and (rb % 16 == 0 or rb == rows):
            return rb
    return rows


def _prefetch_call(body, *, name, grid, in_specs, out_specs, out_shape, args, aliases=None):
    grid_spec = pltpu.PrefetchScalarGridSpec(num_scalar_prefetch=1, grid=grid, in_specs=in_specs, out_specs=out_specs)
    return pl.pallas_call(
        body, name=name, grid_spec=grid_spec, out_shape=out_shape, input_output_aliases=aliases or {},
        compiler_params=pltpu.CompilerParams(dimension_semantics=("arbitrary",) * len(grid),
                                             vmem_limit_bytes=VMEM_LIMIT_V7X),
    )(*args)


def _place_shard(w, l, spec, chip):
    A, rows, width = spec.shard
    rb = _row_block(rows, width, 4)
    nrb = rows // rb
    if spec.kind == "col":
        out_map = lambda a, r, chip: (a, r, chip[0])
    else:
        out_map = lambda a, r, chip: (a, chip[0] * nrb + r, 0)

    def body(chip_ref, w_ref, o_ref):
        o_ref[...] = _bf(w_ref[...])

    return _prefetch_call(
        body, name="place_" + spec.name, grid=(A, nrb),
        in_specs=[pl.BlockSpec((None, rb, width), lambda a, r, chip: (l * A + a, r, 0))],
        out_specs=pl.BlockSpec((None, rb, width), out_map),
        out_shape=jax.ShapeDtypeStruct(spec.full, BF16), args=(chip, w))


def _pair_sum(grad, land, spec, core):
    A, hr, w = spec.half
    rb = _row_block(hr, w, 2)
    nrb = hr // rb
    if spec.kind == "col":
        g_spec = pl.BlockSpec((None, rb, w), lambda j, a, r, core: (a, core[0] * nrb + r, j))
    else:
        g_spec = pl.BlockSpec((None, rb, w), lambda j, a, r, core: (a, (j * 2 + core[0]) * nrb + r, 0))

    def body(core_ref, g_ref, l_ref, o_ref):
        o_ref[...] = _bf(_f32(g_ref[...]) + _f32(l_ref[...]))

    blk = pl.BlockSpec((None, None, rb, w), lambda j, a, r, core: (j, a, r, 0))
    return _prefetch_call(
        body, name="pair_sum_" + spec.name, grid=(N_CHIPS, A, nrb), in_specs=[g_spec, blk], out_specs=blk,
        out_shape=jax.ShapeDtypeStruct((N_CHIPS,) + spec.half, BF16), args=(core, grad, land))


def _chip_sum(pair_sum, land, spec, chip_core):
    A, hr, w = spec.half
    rb = _row_block(hr, w, 4)
    nrb = hr // rb

    def body(cc_ref, p_ref, l0_ref, l1_ref, l2_ref, o_ref):
        o_ref[...] = ((_f32(p_ref[...]) + _f32(l0_ref[...])) + _f32(l1_ref[...])) + _f32(l2_ref[...])

    own = pl.BlockSpec((None, None, rb, w), lambda a, r, cc: (cc[0], a, r, 0))
    slot = lambda p: pl.BlockSpec((None, None, rb, w), lambda a, r, cc: (p, a, r, 0))
    return _prefetch_call(
        body, name="chip_sum_" + spec.name, grid=(A, nrb), in_specs=[own, slot(0), slot(1), slot(2)],
        out_specs=pl.BlockSpec((None, rb, w), lambda a, r, cc: (a, cc[1] * nrb + r, 0)),
        out_shape=jax.ShapeDtypeStruct(spec.shard, F32), args=(chip_core, pair_sum, land, land, land))


def _adamw_math(w, g, m, v):
    m = ADAM_B1 * m + (1.0 - ADAM_B1) * g
    v = ADAM_B2 * v + (1.0 - ADAM_B2) * (g * g)
    m_hat = m / (1.0 - ADAM_B1 ** ADAM_STEP)
    v_hat = v / (1.0 - ADAM_B2 ** ADAM_STEP)
    delta = -ADAM_LR * (m_hat / (jnp.sqrt(v_hat) + ADAM_EPS) + ADAM_WD * w)
    return delta, m, v


def _adamw(w, g, m, v, l, depth, spec, carried):
    A, R, C = spec.shard
    rb = _row_block(R, C, 4 * 4)
    stacked = pl.BlockSpec((None, rb, C), lambda a, r: (l * A + a, r, 0))
    n_carry = 0 if carried is None else 4

    def body(w_ref, g_ref, m_ref, v_ref, *rest):
        go_ref, d_ref, mo_ref, vo_ref = rest[n_carry:]
        g = g_ref[...]
        go_ref[...] = g
        d_ref[...], mo_ref[...], vo_ref[...] = _adamw_math(w_ref[...], g, m_ref[...], v_ref[...])

    return pl.pallas_call(
        body, name="adamw_" + spec.name, grid=(A, R // rb),
        in_specs=[stacked, pl.BlockSpec((None, rb, C), lambda a, r: (a, r, 0)), stacked, stacked] + [HBM_ANY] * n_carry,
        out_specs=[stacked] * 4,
        out_shape=[jax.ShapeDtypeStruct((depth * A, R, C), F32)] * 4,
        input_output_aliases={4 + i: i for i in range(n_carry)},
        compiler_params=pltpu.CompilerParams(dimension_semantics=("arbitrary", "arbitrary"),
                                             vmem_limit_bytes=VMEM_LIMIT_V7X),
    )(w, g, m, v, *(carried or ()))


def _small_update(g_part, w, m, v):
    R = g_part.shape[0]
    n_dev = 8

    def body(g_ref, w_ref, m_ref, v_ref, go_ref, d_ref, mo_ref, vo_ref, all_s, send_sem, recv_sem):
        x, y, c = _place()
        me = 4 * x + 2 * y + c
        all_s[me] = g_ref[...]
        cps = []
        for k in range(1, n_dev):
            peer = (x ^ ((k >> 2) & 1), y ^ ((k >> 1) & 1), c ^ (k & 1))
            cp = pltpu.make_async_remote_copy(src_ref=g_ref, dst_ref=all_s.at[me], send_sem=send_sem.at[k],
                                              recv_sem=recv_sem.at[k], device_id=peer, device_id_type=MESH)
            cp.start()
            cps.append(cp)
        for cp in cps:
            cp.wait()
        g = all_s[0]
        for d in range(1, n_dev):
            g = g + all_s[d]
        go_ref[...] = g
        d_ref[...], mo_ref[...], vo_ref[...] = _adamw_math(w_ref[...], g, m_ref[...], v_ref[...])

    vm = pl.BlockSpec(memory_space=pltpu.VMEM)
    return pl.pallas_call(
        body, name="small_update", in_specs=[vm] * 4, out_specs=[vm] * 4,
        out_shape=[jax.ShapeDtypeStruct((R, 128), F32)] * 4,
        scratch_shapes=[pltpu.VMEM((n_dev, R, 128), F32), pltpu.SemaphoreType.DMA((n_dev,)),
                        pltpu.SemaphoreType.DMA((n_dev,))],
        compiler_params=pltpu.CompilerParams(vmem_limit_bytes=VMEM_LIMIT_V7X),
    )(g_part, w, m, v)


def _pack_small(parts):
    flat = jnp.concatenate([p.reshape(-1) for p in parts])
    pad = (-flat.shape[0]) % 1024
    return jnp.pad(flat, (0, pad)).reshape(-1, 128)


def _unpack_small(packed, like):
    flat = packed.reshape(-1)
    out, at = [], 0
    for p in like:
        out.append(flat[at:at + p.size].reshape(p.shape))
        at += p.size
    return out


def kernel(x, norm_g, w_in, ret_decay_fwd, ret_decay_bwd, pool_w, pool_scale, attn_q_gain, attn_k_gain, attn_sink, w_ret, w_pool, w_att, w_out, loss_target, m_norm_g, m_w_in, m_ret_decay_fwd, m_ret_decay_bwd, m_pool_w, m_pool_scale, m_attn_q_gain, m_attn_k_gain, m_attn_sink, m_w_ret, m_w_pool, m_w_att, m_w_out, v_norm_g, v_w_in, v_ret_decay_fwd, v_ret_decay_bwd, v_pool_w, v_pool_scale, v_attn_q_gain, v_attn_k_gain, v_attn_sink, v_w_ret, v_w_pool, v_w_att, v_w_out):
    S, D = x.shape[1], x.shape[2]
    L = norm_g.shape[0]
    dm = _Dims(S, D, L)
    PD, BW, G = dm.PD, dm.BW, POOL_GROUPS
    xi, yi, ci = _place()
    chip = (2 * xi + yi).astype(jnp.int32).reshape(1)
    core = ci.astype(jnp.int32).reshape(1)
    chip_core = jnp.concatenate([chip, core])

    specs = [_Sharded("w_in", "col", 1, D, dm.INW), _Sharded("w_ret", "col", 1, BW, D), _Sharded("w_pool", "col", 1, BW, D),
             _Sharded("w_att", "col", 1, BW, D), _Sharded("w_out", "row", 1, D, D), _Sharded("pool_w", "row", G, PD, PD)]
    n_big = len(specs)
    big_w = [w_in, w_ret, w_pool, w_att, w_out, pool_w]
    big_m = [m_w_in, m_w_ret, m_w_pool, m_w_att, m_w_out, m_pool_w]
    big_v = [v_w_in, v_w_ret, v_w_pool, v_w_att, v_w_out, v_pool_w]
    stack3 = lambda a, s: a.reshape((L * s.shard[0],) + s.shard[1:])
    big_w3 = [stack3(a, s) for a, s in zip(big_w, specs)]
    big_m3 = [stack3(a, s) for a, s in zip(big_m, specs)]
    big_v3 = [stack3(a, s) for a, s in zip(big_v, specs)]

    W = [[_place_shard(big_w3[t], l, specs[t], chip) for t in range(n_big)] for l in range(L)]
    (ici0, _), = _comm_only("gather_first_ici", [_gather_ici_task(W[0], specs)])
    (W[0], _), = _comm_only("gather_first_d2d", [_gather_d2d_task(ici0, specs)])

    cosr, sinr = _ret_rope_tables(S, dm.RD)
    tabs = _att_tables(S)
    xl = x[0]
    saved = []
    for l in range(L):
        nxt = l + 1 < L
        ng = norm_g[l].reshape(1, D)
        qg, kg = attn_q_gain[l].reshape(1, ATT_HEAD_DIM), attn_k_gain[l].reshape(1, ATT_HEAD_DIM)
        psc = pool_scale[l].reshape(G, 1, PD)
        f_in, f_ret, f_pool, f_att, f_out, f_pw = W[l]
        (z, h), tk = _inproj(xl, ng, f_in, dm, tasks=[_gather_ici_task(W[l + 1][:1], specs[:1])] if nxt else ())
        if nxt:
            W[l + 1][:1] = tk[0][0]
        ya_pre = _ret_fwd(z, ret_decay_fwd[l], ret_decay_bwd[l], cosr, sinr, dm)
        yb_pre = _pool_fwd(z, f_pw, psc, dm)
        (yc_pre,), tk = _att_fwd(z, qg, kg, attn_sink[l], tabs, dm,
                                 tasks=[_gather_ici_task(W[l + 1][1:], specs[1:])] if nxt else ())
        if nxt:
            W[l + 1][1:] = tk[0][0]
        mg = z
        (x_next, ya, yb, yc, merged), tk = _merge_out(xl, mg, ya_pre, yb_pre, yc_pre, f_ret, f_pool, f_att, f_out, dm,
                                                      tasks=[_gather_d2d_task(W[l + 1], specs)] if nxt else ())
        if nxt:
            W[l + 1] = tk[0][0]
        saved.append((xl, z, h, mg, ya_pre, yb_pre, yc_pre, ya, yb, yc, merged, ng, qg, kg, psc))
        xl = x_next
    sq, dx = _loss_grad(xl, loss_target[0], dm)
    loss = lax.psum(sq[0, 0] * (0.5 / D), ("x", "y", "c"))

    grads = [None] * L
    lands = [None] * L
    pairs = [None] * L
    lands2 = [None] * L
    shards = [None] * L
    carried = [None] * n_big
    small = [None] * L

    def pair_sums(k):
        pairs[k] = [_pair_sum(g, ld, s, core) for g, ld, s in zip(grads[k], lands[k], specs)]

    def chip_sums(k):
        shards[k] = [_chip_sum(p, ld, s, chip_core) for p, ld, s in zip(pairs[k], lands2[k], specs)]

    def adamw(k):
        for t in range(n_big):
            carried[t] = _adamw(big_w3[t], shards[k][t], big_m3[t], big_v3[t], k, L, specs[t], carried[t])

    for l in reversed(range(L)):
        xl, z, h, mg, ya_pre, yb_pre, yc_pre, ya, yb, yc, merged, ng, qg, kg, psc = saved[l]
        f_in, f_ret, f_pool, f_att, f_out, f_pw = W[l]
        up1, up2 = l + 1 < L, l + 2 < L
        own, tk = _merge_bwd_gates(dx, mg, ya, yb, yc, f_out, dm, tasks=[_pair_task(grads[l + 1], specs)] if up1 else ())
        dz, dya, dyb, dyc, dxb = own
        if up1:
            lands[l + 1] = tk[0][1]
            pair_sums(l + 1)
        dya_pre, dyb_pre, dyc_pre = _merge_bwd_proj(dya, dyb, dyc, f_ret, f_pool, f_att, dm)
        g_out, _ = _grad_matmul(merged, dxb, "grad_w_out")
        g_ret, _ = _grad_matmul(ya_pre, dya, "grad_w_ret")
        g_pool, _ = _grad_matmul(yb_pre, dyb, "grad_w_pool")
        g_att, _ = _grad_matmul(yc_pre, dyc, "grad_w_att")
        (ddec, dz), tk = _ret_bwd(z, dz, dya_pre, ret_decay_fwd[l], ret_decay_bwd[l], cosr, sinr, dm,
                                  tasks=[_chip_task(pairs[l + 1][1:], specs[1:])] if up1 else ())
        if up1:
            lands2[l + 1] = [None] + tk[0][1]
        dpw, dps, dz = _pool_bwd(z, dz, dyb_pre, f_pw, psc, dm)
        (dsm, dz), tk = _att_bwd(z, dz, dyc_pre, qg, kg, attn_sink[l], tabs, dm,
                                 tasks=[_half_task(shards[l + 2], specs)] if up2 else ())
        if up2:
            shards[l + 2] = tk[0][0]
            adamw(l + 2)
        w_in_task = [_chip_task(pairs[l + 1][:1], specs[:1])] if up1 else ()
        if l > 0:
            g_in, _ = _grad_matmul(h, dz, "grad_w_in")
            (dh,), tk = _inproj_bwd(dz, f_in, dm, tasks=w_in_task)
        else:
            g_in, tk = _grad_matmul(h, dz, "grad_w_in", tasks=w_in_task)
        if up1:
            lands2[l + 1][0] = tk[0][1][0]
            chip_sums(l + 1)
        grads[l] = [g_in, g_ret, g_pool, g_att, g_out, dpw]
        if l == 0:
            (_, lands[0]), = _comm_only("grad_pair_exchange", [_pair_task(grads[0], specs)])
            pair_sums(0)
            (dh,), tk = _inproj_bwd(dz, f_in, dm, tasks=[_chip_task(pairs[0], specs)])
            lands2[0] = tk[0][1]
            chip_sums(0)
        dx, dng = _norm_bwd(dh, xl, ng, dx, dm)
        small[l] = [dng.reshape(D), ddec[:, 0, 0], ddec[:, 1, 0], dps.reshape(BW), jnp.sum(dsm[:, 0, :], axis=0),
                    jnp.sum(dsm[:, 1, :], axis=0), dsm[:, 2:2 + ATT_GROUP, 0].reshape(dm.AH)]

    rest = [k for k in (1, 0) if k < L]
    done = _comm_only("grad_half_exchange", [_half_task(shards[k], specs) for k in rest])
    for k, (both, _) in zip(rest, done):
        shards[k] = both
        adamw(k)

    back = lambda a, like: a.reshape(like.shape)
    g_big, d_big, m_big, v_big = ([back(carried[t][i], big_w[t]) for t in range(n_big)] for i in range(4))

    small_g = [jnp.stack([small[l][i] for l in range(L)]) for i in range(7)]
    small_w = [norm_g, ret_decay_fwd, ret_decay_bwd, pool_scale, attn_q_gain, attn_k_gain, attn_sink]
    small_m = [m_norm_g, m_ret_decay_fwd, m_ret_decay_bwd, m_pool_scale, m_attn_q_gain, m_attn_k_gain, m_attn_sink]
    small_v = [v_norm_g, v_ret_decay_fwd, v_ret_decay_bwd, v_pool_scale, v_attn_q_gain, v_attn_k_gain, v_attn_sink]
    sg, sd, sm, sv = _small_update(_pack_small(small_g), _pack_small(small_w), _pack_small(small_m), _pack_small(small_v))
    g_sm, d_sm, m_sm, v_sm = (_unpack_small(a, small_w) for a in (sg, sd, sm, sv))

    def ordered(big, small_):
        return [small_[0], big[0], small_[1], small_[2], big[5], small_[3], small_[4], small_[5], small_[6],
                big[1], big[2], big[3], big[4]]

    return (loss, dx[None], *ordered(g_big, g_sm), *ordered(d_big, d_sm), *ordered(m_big, m_sm),
            *ordered(v_big, v_sm))
```

```python
import jax
import jax.numpy as jnp
from jax import lax
from jax.experimental import pallas as pl
from jax.experimental.pallas import tpu as pltpu

F32 = jnp.float32
BF16 = jnp.bfloat16
MESH = pl.DeviceIdType.MESH

RMS_EPS = 1e-6
NEG_BIG = -1e30
CHUNK = 128
RET_HEADS = 4
POOL_GROUPS = 4
ATT_HEAD_DIM = 128
ATT_GROUP = 4
ROPE_DIMS = 32
RET_ROPE_BASE = 10000.0
ROPE_THETA = 500000.0
N_BRANCHES = 3
N_CHIPS = 4

ADAM_LR = 0.001
ADAM_B1 = 0.9
ADAM_B2 = 0.999
ADAM_EPS = 1e-08
ADAM_WD = 0.01
ADAM_STEP = 10

VMEM_LIMIT_V7X = 56 * 1024 * 1024

TN = (((0,), (0,)), ((), ()))
NT = (((1,), (1,)), ((), ()))

RET_HEADS_PER_STEP = 2
LOOP_UNROLL = 2
HBM_ANY = pl.BlockSpec(memory_space=pl.ANY)
ONE_BUFFER = pl.Buffered(1)


def _sigmoid(x):
    return 1.0 / (1.0 + jnp.exp(-x))


def _silu(x):
    return x * _sigmoid(x)


def _dsilu(x):
    s = _sigmoid(x)
    return s * (1.0 + x * (1.0 - s))


def _dot(a, b):
    return jnp.dot(a, b, preferred_element_type=F32)


def _dot_tn(a, b):
    return lax.dot_general(a, b, TN, preferred_element_type=F32)


def _dot_nt(a, b):
    return lax.dot_general(a, b, NT, preferred_element_type=F32)


def _bf(x):
    return x.astype(BF16)


def _f32(x):
    return x.astype(F32)


def _loop(n, body, init):
    def several(i, carry):
        for u in range(LOOP_UNROLL):
            carry = body(i * LOOP_UNROLL + u, carry)
        return carry
    return lax.fori_loop(0, n // LOOP_UNROLL, several, init)


def _sum11(x):
    return jnp.sum(jnp.sum(x, axis=1, keepdims=True), axis=0, keepdims=True)


class _Dims:
    def __init__(self, seq, d_model, depth):
        self.S, self.D, self.L = seq, d_model, depth
        bw = d_model // 2
        self.BW = bw
        self.RD = bw // RET_HEADS
        self.PD = bw // POOL_GROUPS
        self.AH = bw // ATT_HEAD_DIM
        self.AKV = self.AH // ATT_GROUP
        self.AKW = self.AKV * ATT_HEAD_DIM
        self.o_rq, self.o_rk, self.o_rv, self.o_rg = 0, bw, 2 * bw, 3 * bw
        self.o_pv, self.o_pg = 4 * bw, 5 * bw
        self.o_aq = 6 * bw
        self.o_ak = 7 * bw
        self.o_av = 7 * bw + self.AKW
        self.o_ag = 7 * bw + 2 * self.AKW
        self.o_mg = 8 * bw + 2 * self.AKW
        self.INW = self.o_mg + N_BRANCHES * d_model
        self.nC = seq // CHUNK
        self.tm = min(512, seq)


def _place():
    return lax.axis_index("x"), lax.axis_index("y"), lax.axis_index("c")


def _chip_peers(x, y):
    return [(1 - x, y), (x, 1 - y), (1 - x, 1 - y)]


class _Sharded:
    def __init__(self, name, kind, a, r, cc):
        self.name, self.kind, self.A, self.R, self.Cc = name, kind, a, r, cc
        self.full = (a, r, cc)
        if kind == "col":
            self.shard = (a, r, cc // N_CHIPS)
            self.half_rows = r // 2
        else:
            self.shard = (a, r // N_CHIPS, cc)
            self.half_rows = r // N_CHIPS // 2
        self.half = (a, self.half_rows, self.shard[2])

    def in_full(self, ref, chip, core=None):
        hr = self.half_rows
        if self.kind == "col":
            rows = pl.ds(0, self.R) if core is None else pl.ds(core * hr, hr)
            return ref.at[:, rows, pl.ds(chip * self.shard[2], self.shard[2])]
        rows = pl.ds(chip * self.shard[1], self.shard[1]) if core is None else pl.ds(chip * self.shard[1] + core * hr, hr)
        return ref.at[:, rows, :]

    def in_shard(self, ref, core):
        return ref.at[:, pl.ds(core * self.half_rows, self.half_rows), :]


class _Task:
    def __init__(self, ro, rw, new, n_sem, copies):
        self.ro, self.rw, self.new, self.n_sem, self.copies = list(ro), list(rw), list(new), n_sem, copies


def _remote(src, dst, send_sem, recv_sem, k, device):
    def make():
        return pltpu.make_async_remote_copy(src_ref=src, dst_ref=dst, send_sem=send_sem.at[k], recv_sem=recv_sem.at[k],
                                            device_id=device, device_id_type=MESH)
    return make


def _gather_ici_task(bufs, specs):
    def copies(ro, rw, new, ss, rs):
        x, y, c = _place()
        me = 2 * x + y
        out = []
        for t, spec in enumerate(specs):
            for p, (px, py) in enumerate(_chip_peers(x, y)):
                mine = spec.in_full(rw[t], me, c)
                theirs = spec.in_full(rw[t], 2 * px + py, c)
                k = t * 3 + p
                out.append((_remote(mine, mine, ss, rs, k, (px, py, c)), _remote(mine, mine, ss, rs, k, (px, py, c)),
                            _remote(theirs, theirs, ss, rs, k, (px, py, c))))
        return out
    return _Task([], bufs, [], 3 * len(specs), copies)


def _gather_d2d_task(bufs, specs):
    def copies(ro, rw, new, ss, rs):
        x, y, c = _place()
        sib = (x, y, 1 - c)
        out = []
        for t, spec in enumerate(specs):
            for p, (px, py) in enumerate(_chip_peers(x, y)):
                got = spec.in_full(rw[t], 2 * px + py, c)
                gets = spec.in_full(rw[t], 2 * px + py, 1 - c)
                k = t * 3 + p
                out.append((_remote(got, got, ss, rs, k, sib), _remote(got, got, ss, rs, k, sib),
                            _remote(gets, gets, ss, rs, k, sib)))
        return out
    return _Task([], bufs, [], 3 * len(specs), copies)


def _pair_task(grads, specs):
    def copies(ro, rw, new, ss, rs):
        x, y, c = _place()
        sib = (x, y, 1 - c)
        out = []
        for t, spec in enumerate(specs):
            for j in range(N_CHIPS):
                k = t * N_CHIPS + j
                cp = _remote(spec.in_full(ro[t], j, 1 - c), new[t].at[j], ss, rs, k, sib)
                out.append((cp, cp, cp))
        return out
    return _Task(grads, [], [jax.ShapeDtypeStruct((N_CHIPS,) + s.half, BF16) for s in specs], N_CHIPS * len(specs), copies)


def _chip_task(pair_sums, specs):
    def copies(ro, rw, new, ss, rs):
        x, y, c = _place()
        out = []
        for t in range(len(specs)):
            for p, (px, py) in enumerate(_chip_peers(x, y)):
                cp = _remote(ro[t].at[2 * px + py], new[t].at[p], ss, rs, t * 3 + p, (px, py, c))
                out.append((cp, cp, cp))
        return out
    return _Task(pair_sums, [], [jax.ShapeDtypeStruct((3,) + s.half, BF16) for s in specs], 3 * len(specs), copies)


def _chip_rows_task(pair_sum, land, spec, lo, hi):
    def copies(ro, rw, new, ss, rs):
        x, y, c = _place()
        dst = (rw or new)[0]
        out = []
        for p, (px, py) in enumerate(_chip_peers(x, y)):
            cp = _remote(ro[0].at[2 * px + py, :, pl.ds(lo, hi - lo), :], dst.at[p, :, pl.ds(lo, hi - lo), :],
                         ss, rs, p, (px, py, c))
            out.append((cp, cp, cp))
        return out
    first = land is None
    return _Task([pair_sum], [] if first else [land], [jax.ShapeDtypeStruct((3,) + spec.half, BF16)] if first else [],
                 3, copies)


def _half_task(shards, specs):
    def copies(ro, rw, new, ss, rs):
        x, y, c = _place()
        sib = (x, y, 1 - c)
        out = []
        for t, spec in enumerate(specs):
            mine, theirs = spec.in_shard(rw[t], c), spec.in_shard(rw[t], 1 - c)
            out.append((_remote(mine, mine, ss, rs, t, sib), _remote(mine, mine, ss, rs, t, sib),
                        _remote(theirs, theirs, ss, rs, t, sib)))
        return out
    return _Task([], shards, [], len(specs), copies)


def _call(body, *, name, grid, in_specs, out_specs, out_shape, args, scratch_shapes=(), tasks=(), carry=()):
    tasks = [t for t in tasks if t is not None]
    carry = list(carry)
    n_in, n_out, n_scr, n_carry = len(in_specs), len(out_specs), len(scratch_shapes), len(carry)
    ro = [a for t in tasks for a in t.ro]
    rw = [a for t in tasks for a in t.rw]
    new = [s for t in tasks for s in t.new]
    n_ro, n_rw, n_new = len(ro), len(rw), len(new)

    def wrapped(*refs):
        ins = refs[:n_in]
        ro_refs = refs[n_in + n_carry:n_in + n_carry + n_ro]
        at = n_in + n_carry + n_ro + n_rw
        outs = refs[at:at + n_out + n_carry]
        at = at + n_out + n_carry
        rw_refs = refs[at:at + n_rw]
        new_refs = refs[at + n_rw:at + n_rw + n_new]
        at = at + n_rw + n_new
        scr = refs[at:at + n_scr]
        sems = refs[at + n_scr:]

        def task_copies():
            found, a, b, d = [], 0, 0, 0
            for i, t in enumerate(tasks):
                found += t.copies(ro_refs[a:a + len(t.ro)], rw_refs[b:b + len(t.rw)], new_refs[d:d + len(t.new)],
                                  sems[2 * i], sems[2 * i + 1])
                a, b, d = a + len(t.ro), b + len(t.rw), d + len(t.new)
            return found

        if tasks:
            first = pl.program_id(0) == 0
            last = pl.program_id(0) == grid[0] - 1
            for ax in range(1, len(grid)):
                first = first & (pl.program_id(ax) == 0)
                last = last & (pl.program_id(ax) == grid[ax] - 1)

            @pl.when(first)
            def _():
                for cp, _, _ in task_copies():
                    cp().start()

        body(*ins, *outs, *scr)

        if tasks:
            @pl.when(last)
            def _():
                found = task_copies()
                for _, _, recv in found:
                    recv().wait_recv()
                for _, send, _ in found:
                    send().wait_send()

    sem_shapes = []
    for t in tasks:
        sem_shapes += [pltpu.SemaphoreType.DMA((t.n_sem,)), pltpu.SemaphoreType.DMA((t.n_sem,))]
    aliases = {n_in + i: n_out + i for i in range(n_carry)}
    aliases.update({n_in + n_carry + n_ro + i: n_out + n_carry + i for i in range(n_rw)})
    res = pl.pallas_call(
        wrapped, name=name, grid=grid,
        in_specs=list(in_specs) + [HBM_ANY] * (n_carry + n_ro + n_rw),
        out_specs=list(out_specs) + [HBM_ANY] * (n_carry + n_rw + n_new),
        out_shape=list(out_shape) + [jax.ShapeDtypeStruct(a.shape, a.dtype) for a in carry + rw] + new,
        scratch_shapes=list(scratch_shapes) + sem_shapes,
        input_output_aliases=aliases,
        compiler_params=pltpu.CompilerParams(dimension_semantics=("arbitrary",) * len(grid),
                                             vmem_limit_bytes=VMEM_LIMIT_V7X),
    )(*args, *carry, *ro, *rw)
    own, rest = list(res[:n_out + n_carry]), list(res[n_out + n_carry:])
    per_task, b, d = [], 0, n_rw
    for t in tasks:
        per_task.append((rest[b:b + len(t.rw)], rest[d:d + len(t.new)]))
        b, d = b + len(t.rw), d + len(t.new)
    return own, per_task


def _comm_only(name, tasks):
    ro = [a for t in tasks for a in t.ro]
    rw = [a for t in tasks for a in t.rw]
    new = [s for t in tasks for s in t.new]
    n_ro, n_rw, n_new = len(ro), len(rw), len(new)

    def body(*refs):
        ro_refs = refs[:n_ro]
        rw_refs = refs[n_ro + n_rw:n_ro + 2 * n_rw]
        new_refs = refs[n_ro + 2 * n_rw:n_ro + 2 * n_rw + n_new]
        sems = refs[n_ro + 2 * n_rw + n_new:]
        found, a, b, d = [], 0, 0, 0
        for i, t in enumerate(tasks):
            found += t.copies(ro_refs[a:a + len(t.ro)], rw_refs[b:b + len(t.rw)], new_refs[d:d + len(t.new)],
                              sems[2 * i], sems[2 * i + 1])
            a, b, d = a + len(t.ro), b + len(t.rw), d + len(t.new)
        for cp, _, _ in found:
            cp().start()
        for _, _, recv in found:
            recv().wait_recv()
        for _, send, _ in found:
            send().wait_send()

    sem_shapes = []
    for t in tasks:
        sem_shapes += [pltpu.SemaphoreType.DMA((t.n_sem,)), pltpu.SemaphoreType.DMA((t.n_sem,))]
    res = pl.pallas_call(
        body, name=name,
        in_specs=[HBM_ANY] * (n_ro + n_rw), out_specs=[HBM_ANY] * (n_rw + n_new),
        out_shape=[jax.ShapeDtypeStruct(a.shape, a.dtype) for a in rw] + new,
        scratch_shapes=sem_shapes,
        input_output_aliases={n_ro + i: i for i in range(n_rw)},
    )(*ro, *rw)
    res = list(res)
    per_task, b, d = [], 0, n_rw
    for t in tasks:
        per_task.append((res[b:b + len(t.rw)], res[d:d + len(t.new)]))
        b, d = b + len(t.rw), d + len(t.new)
    return per_task


def _inproj(x, norm_g, w_in, dm, tasks=()):
    S, D, N = dm.S, dm.D, dm.INW
    tm, tn = min(1024, S), 512

    def body(x_ref, g_ref, w_ref, z_ref, h_ref):
        @pl.when(pl.program_id(1) == 0)
        def _():
            xv = x_ref[...]
            r = lax.rsqrt(jnp.mean(xv * xv, axis=-1, keepdims=True) + RMS_EPS)
            h_ref[...] = _bf(xv * r * g_ref[...])

        z_ref[...] = _bf(_dot(h_ref[...], w_ref[...]))

    return _call(
        body, name="inproj", grid=(S // tm, N // tn),
        in_specs=[pl.BlockSpec((tm, D), lambda i, j: (i, 0)),
                  pl.BlockSpec((1, D), lambda i, j: (0, 0)),
                  pl.BlockSpec((None, D, tn), lambda i, j: (0, 0, j))],
        out_specs=[pl.BlockSpec((tm, tn), lambda i, j: (i, j)),
                   pl.BlockSpec((tm, D), lambda i, j: (i, 0))],
        out_shape=[jax.ShapeDtypeStruct((S, N), BF16), jax.ShapeDtypeStruct((S, D), BF16)],
        args=(x, norm_g, w_in), tasks=tasks)


def _rot_half(x, cs, sn):
    h = cs.shape[-1]
    x1, x2 = x[:, :h], x[:, h:]
    return jnp.concatenate([x1 * cs - x2 * sn, x2 * cs + x1 * sn], axis=-1)


def _ret_tables(af_ref, ab_ref):
    C = CHUNK
    lgf = -jnp.exp(af_ref[...])[:, :1]
    lgb = -jnp.exp(ab_ref[...])[:, :1]
    ri = lax.broadcasted_iota(jnp.int32, (C, C), 0)
    ci = lax.broadcasted_iota(jnp.int32, (C, C), 1)
    lag = _f32(ri - ci)
    alag = jnp.abs(lag)
    low = lag >= 0
    dmask = jnp.where(low, jnp.exp(lgf * alag), jnp.exp(lgb * alag))
    j = _f32(lax.broadcasted_iota(jnp.int32, (C, 1), 0))
    return dict(lgf=lgf, lgb=lgb, alag=alag, low=low, dmask=dmask, j=j,
                w_f=jnp.exp(lgf * (C - 1.0 - j)), w_b=jnp.exp(lgb * j),
                q_f=jnp.exp(lgf * (j + 1.0)), q_b=jnp.exp(lgb * (C - j)),
                dec_f=jnp.exp(lgf * C), dec_b=jnp.exp(lgb * C))


def _decay_rows(a):
    return jnp.broadcast_to(a.reshape(RET_HEADS, 1, 1), (RET_HEADS, 1, 128))


def _heads(x, width):
    return jnp.stack([x[:, b * width:(b + 1) * width] for b in range(x.shape[1] // width)])


def _bdot(a, b):
    return lax.dot_general(a, b, (((2,), (1,)), ((0,), (0,))), preferred_element_type=F32)


def _bdot_nt(a, b):
    return lax.dot_general(a, b, (((2,), (2,)), ((0,), (0,))), preferred_element_type=F32)


def _bdot_tn(a, b):
    return lax.dot_general(a, b, (((1,), (1,)), ((0,), (0,))), preferred_element_type=F32)


def _ret_tables_heads(af_ref, ab_ref):
    C = CHUNK
    lgf = -jnp.exp(af_ref[...])[:, :, :1]
    lgb = -jnp.exp(ab_ref[...])[:, :, :1]
    ri = lax.broadcasted_iota(jnp.int32, (1, C, C), 1)
    ci = lax.broadcasted_iota(jnp.int32, (1, C, C), 2)
    lag = _f32(ri - ci)
    alag = jnp.abs(lag)
    low = lag >= 0
    dmask = jnp.where(low, jnp.exp(lgf * alag), jnp.exp(lgb * alag))
    j = _f32(lax.broadcasted_iota(jnp.int32, (1, C, 1), 1))
    return dict(lgf=lgf, lgb=lgb, alag=alag, low=low, dmask=dmask, j=j,
                w_f=jnp.exp(lgf * (C - 1.0 - j)), w_b=jnp.exp(lgb * j),
                q_f=jnp.exp(lgf * (j + 1.0)), q_b=jnp.exp(lgb * (C - j)),
                dec_f=jnp.exp(lgf * C), dec_b=jnp.exp(lgb * C))


def _rot_half_heads(x, cs, sn):
    h = cs.shape[-1]
    x1, x2 = x[..., :h], x[..., h:]
    return jnp.concatenate([x1 * cs - x2 * sn, x2 * cs + x1 * sn], axis=-1)


def _ret_fwd(z, a_f, a_b, cosr, sinr, dm):
    S, RD, C, nC, HB = dm.S, dm.RD, CHUNK, dm.nC, RET_HEADS_PER_STEP
    W = HB * RD
    scale = RD ** -0.5

    def body(q_ref, k_ref, v_ref, g_ref, af_ref, ab_ref, cos_ref, sin_ref, o_ref, qh_s, kh_s, sf_s, st_s):
        t = _ret_tables_heads(af_ref, ab_ref)
        st_s[...] = jnp.zeros_like(st_s)

        def fwd_pass(n, carry):
            r = pl.ds(pl.multiple_of(n * C, C), C)
            cs, sn = cos_ref[r, :], sin_ref[r, :]
            qh = _rot_half_heads(_heads(_f32(q_ref[r, :]), RD), cs, sn)
            kh = _rot_half_heads(_heads(_f32(k_ref[r, :]), RD), cs, sn) * scale
            qh_s[:, r, :] = _bf(qh)
            kh_s[:, r, :] = _bf(kh)
            st = st_s[...]
            sf_s[n] = _bf(st)
            st_s[...] = st * t["dec_f"] + _bdot_tn(_bf(kh * t["w_f"]), _heads(v_ref[r, :], RD))
            return carry

        _loop(nC, fwd_pass, 0)
        st_s[...] = jnp.zeros_like(st_s)

        def bwd_pass(i, carry):
            n = nC - 1 - i
            r = pl.ds(pl.multiple_of(n * C, C), C)
            qhb, khb, vb = qh_s[:, r, :], kh_s[:, r, :], _heads(v_ref[r, :], RD)
            p = _bf(_bdot_nt(qhb, khb) * t["dmask"])
            qhf = _f32(qhb)
            sb = st_s[...]
            out = (_bdot(p, vb) + _bdot(_bf(qhf * t["q_f"]), sf_s[n]) + _bdot(_bf(qhf * t["q_b"]), _bf(sb)))
            y = out * lax.rsqrt(jnp.mean(out * out, axis=-1, keepdims=True) + RMS_EPS)
            gate = _silu(_f32(g_ref[r, :]))
            for b in range(HB):
                o_ref[r, b * RD:(b + 1) * RD] = _bf(y[b] * gate[:, b * RD:(b + 1) * RD])
            st_s[...] = sb * t["dec_b"] + _bdot_tn(_bf(_f32(khb) * t["w_b"]), vb)
            return carry

        _loop(nC, bwd_pass, 0)

    zs = lambda off: pl.BlockSpec((S, W), lambda h: (0, off // W + h), pipeline_mode=ONE_BUFFER)
    dec = pl.BlockSpec((HB, 1, 128), lambda h: (h, 0, 0))
    tab = pl.BlockSpec((S, RD // 2), lambda h: (0, 0), pipeline_mode=ONE_BUFFER)
    own, _ = _call(
        body, name="ret_fwd", grid=(RET_HEADS // HB,),
        in_specs=[zs(dm.o_rq), zs(dm.o_rk), zs(dm.o_rv), zs(dm.o_rg), dec, dec, tab, tab],
        out_specs=[pl.BlockSpec((S, W), lambda h: (0, h))],
        out_shape=[jax.ShapeDtypeStruct((S, dm.BW), BF16)],
        scratch_shapes=[pltpu.VMEM((HB, S, RD), BF16), pltpu.VMEM((HB, S, RD), BF16),
                        pltpu.VMEM((nC, HB, RD, RD), BF16), pltpu.VMEM((HB, RD, RD), F32)],
        args=(z, z, z, z, _decay_rows(a_f), _decay_rows(a_b), cosr, sinr))
    return own[0]


def _band(first_row, first_col, lo, hi, shape):
    r = lax.broadcasted_iota(jnp.int32, shape, 0) + first_row
    c = lax.broadcasted_iota(jnp.int32, shape, 1) + first_col
    d = c - r
    return jnp.where((d >= lo) & (d <= hi), 1.0, 0.0).astype(BF16)


def _pool_counts(first_row, half, S, rows):
    pos = lax.broadcasted_iota(jnp.int32, (rows, 1), 0) + first_row
    lo = jnp.clip(pos - half, 0, S)
    hi = jnp.clip(pos + half, 0, S)
    return 1.0 / _f32(hi - lo)


def _pool_rows(S):
    return min(4 * CHUNK, S // 2)


def _pool_win_start(n, rows, S):
    return pl.multiple_of(jnp.clip(n * rows - CHUNK, 0, S - (rows + 2 * CHUNK)), CHUNK)


def _win_start(n, S):
    return pl.multiple_of(jnp.clip((n - 1) * CHUNK, 0, S - 3 * CHUNK), CHUNK)


def _pool_fwd(z, pool_w, pool_scale, dm):
    S, PD = dm.S, dm.PD
    PB = _pool_rows(S)
    WIN = PB + 2 * CHUNK

    def body(u_ref, g_ref, w_ref, sc_ref, o_ref):
        half = jnp.left_shift(1, pl.program_id(0))
        w = w_ref[...]
        sc = sc_ref[...]

        def blk(n, carry):
            row0 = pl.multiple_of(n * PB, PB)
            r = pl.ds(row0, PB)
            st = _pool_win_start(n, PB, S)
            band = _band(row0, st, -half, half - 1, (PB, WIN))
            mean = _dot(band, u_ref[pl.ds(st, WIN), :]) * _pool_counts(row0, half, S, PB)
            p = mean - _f32(u_ref[r, :])
            y = _dot(_bf(p), w) * sc
            o_ref[r, :] = _bf(y * _silu(_f32(g_ref[r, :])))
            return carry

        _loop(S // PB, blk, 0)

    own, _ = _call(
        body, name="pool_fwd", grid=(POOL_GROUPS,),
        in_specs=[pl.BlockSpec((S, PD), lambda g: (0, dm.o_pv // PD + g)),
                  pl.BlockSpec((S, PD), lambda g: (0, dm.o_pg // PD + g)),
                  pl.BlockSpec((None, PD, PD), lambda g: (g, 0, 0)),
                  pl.BlockSpec((None, 1, PD), lambda g: (g, 0, 0))],
        out_specs=[pl.BlockSpec((S, PD), lambda g: (0, g))],
        out_shape=[jax.ShapeDtypeStruct((S, dm.BW), BF16)],
        args=(z, z, pool_w, pool_scale))
    return own[0]


def _rot_part(x, cs, s_up, s_dn):
    h = ROPE_DIMS // 2
    lanes = x.ndim - 1
    return x * cs + pltpu.roll(x, ATT_HEAD_DIM - h, lanes) * s_up + pltpu.roll(x, h, lanes) * s_dn


def _att_tables(S):
    h = ROPE_DIMS // 2
    inv = ROPE_THETA ** (-jnp.arange(h, dtype=F32) / h)
    ang = jnp.arange(S, dtype=F32)[:, None] * inv[None, :]
    cos, sin = jnp.cos(ang), jnp.sin(ang)
    pad = jnp.zeros((S, ATT_HEAD_DIM - 2 * h), F32)
    zero = jnp.zeros((S, h), F32)
    cs = jnp.concatenate([cos, cos, pad + 1.0], axis=1)
    s_up = jnp.concatenate([-sin, zero, pad], axis=1)
    s_dn = jnp.concatenate([zero, sin, pad], axis=1)
    return cs, s_up, s_dn


def _ret_rope_tables(S, RD):
    h = RD // 2
    inv = 1.0 / (RET_ROPE_BASE ** jnp.linspace(0.0, 1.0, h, dtype=F32))
    ang = jnp.arange(S, dtype=F32)[:, None] * inv[None, :]
    return jnp.cos(ang), jnp.sin(ang)


def _att_norm_rot(x, gain, cs, s_up, s_dn):
    u = x * lax.rsqrt(jnp.mean(x * x, axis=-1, keepdims=True) + RMS_EPS)
    return _rot_part(u * gain, cs, s_up, s_dn)


def _att_bias_tables(bias_s):
    C = CHUNK
    shape = (ATT_GROUP * C, 3 * C)
    rel = lax.broadcasted_iota(jnp.int32, shape, 1) - lax.broadcasted_iota(jnp.int32, shape, 0) % C
    for which, shift in enumerate((-C, 0, -2 * C)):
        bias_s[which] = jnp.where(jnp.abs(rel + shift) <= CHUNK, 0.0, NEG_BIG)


def _att_bias(bias_s, n, nC):
    return bias_s[jnp.where(n == 0, 1, jnp.where(n == nC - 1, 2, 0))]


def _att_probs(q4, kw, sink_col, bias):
    s = _dot_nt(q4, kw) * (ATT_HEAD_DIM ** -0.5) + bias
    m = jnp.maximum(jnp.max(s, axis=-1, keepdims=True), sink_col)
    e = jnp.exp(s - m)
    es = jnp.exp(sink_col - m)
    inv = 1.0 / (jnp.sum(e, axis=-1, keepdims=True) + es)
    return e * inv, es * inv


def _sink_col(sink_ref):
    head = lax.broadcasted_iota(jnp.int32, (ATT_GROUP * CHUNK, 1), 0) // CHUNK
    col = jnp.zeros((ATT_GROUP * CHUNK, 1), F32)
    for g in range(ATT_GROUP):
        col = jnp.where(head == g, sink_ref[g:g + 1, :1], col)
    return col


def _sink_rows(sink, dm):
    return jnp.broadcast_to(sink.reshape(dm.AKV, ATT_GROUP, 1), (dm.AKV, ATT_GROUP, 128))


def _att_prep(q_ref, k_ref, qg_ref, kg_ref, cs_ref, su_ref, sd_ref, qn_s, kn_s, nC):
    C, HD = CHUNK, ATT_HEAD_DIM

    def prep(n, carry):
        r = pl.ds(pl.multiple_of(n * C, C), C)
        cs, su, sd = cs_ref[r, :], su_ref[r, :], sd_ref[r, :]
        kn_s[r, :] = _bf(_att_norm_rot(_f32(k_ref[r, :]), kg_ref[...], cs, su, sd))
        qn_s[:, r, :] = _bf(_att_norm_rot(_heads(_f32(q_ref[r, :]), HD), qg_ref[...], cs, su, sd))
        return carry

    lax.fori_loop(0, nC, prep, 0)


def _att_fwd(z, q_gain, k_gain, sink, tabs, dm, tasks=()):
    S, C, nC, HD, G = dm.S, CHUNK, dm.nC, ATT_HEAD_DIM, ATT_GROUP
    GW = G * HD

    KV = dm.AKV

    def body(q_ref, k_ref, v_ref, g_ref, qg_ref, kg_ref, sk_ref, cs_ref, su_ref, sd_ref, o_ref, qn_s, kn_s, bias_s):
        def prep(n, carry):
            r = pl.ds(pl.multiple_of(n * C, C), C)
            cs, su, sd = cs_ref[r, :], su_ref[r, :], sd_ref[r, :]
            kn_s[:, r, :] = _bf(_att_norm_rot(_heads(_f32(k_ref[r, :]), HD), kg_ref[...], cs, su, sd))
            qn_s[:, r, :] = _bf(_att_norm_rot(_heads(_f32(q_ref[r, :]), HD), qg_ref[...], cs, su, sd))
            return carry

        lax.fori_loop(0, nC, prep, 0)
        _att_bias_tables(bias_s)
        sink_col = jnp.stack([_sink_col(sk_ref.at[h]) for h in range(KV)])

        def blk(n, carry):
            row0 = pl.multiple_of(n * C, C)
            r = pl.ds(row0, C)
            w = pl.ds(_win_start(n, S), 3 * C)
            q4 = qn_s[:, r, :].reshape(KV, G * C, HD)
            s = _bdot_nt(q4, kn_s[:, w, :]) * (HD ** -0.5) + _att_bias(bias_s, n, nC)
            m = jnp.maximum(jnp.max(s, axis=-1, keepdims=True), sink_col)
            e = jnp.exp(s - m)
            inv = 1.0 / (jnp.sum(e, axis=-1, keepdims=True) + jnp.exp(sink_col - m))
            o = _bdot(_bf(e * inv), _heads(v_ref[w, :], HD))
            gate = _silu(_f32(g_ref[r, :]))
            for h in range(KV):
                for g in range(G):
                    cols = slice((h * G + g) * HD, (h * G + g + 1) * HD)
                    o_ref[r, cols] = _bf(o[h, g * C:(g + 1) * C, :] * gate[:, cols])
            return carry

        _loop(nC, blk, 0)

    tab = pl.BlockSpec((S, HD), lambda h: (0, 0), pipeline_mode=ONE_BUFFER)
    gain = pl.BlockSpec((1, HD), lambda h: (0, 0))
    one = lambda off, width: pl.BlockSpec((pl.Element(S), pl.Element(width)), lambda h: (0, off),
                                          pipeline_mode=ONE_BUFFER)
    return _call(
        body, name="att_fwd", grid=(1,),
        in_specs=[one(dm.o_aq, dm.BW), one(dm.o_ak, dm.AKW), one(dm.o_av, dm.AKW), one(dm.o_ag, dm.BW),
                  gain, gain, pl.BlockSpec((KV, G, 128), lambda h: (0, 0, 0)), tab, tab, tab],
        out_specs=[pl.BlockSpec((S, dm.BW), lambda h: (0, 0), pipeline_mode=ONE_BUFFER)],
        out_shape=[jax.ShapeDtypeStruct((S, dm.BW), BF16)],
        scratch_shapes=[pltpu.VMEM((KV * G, S, HD), BF16), pltpu.VMEM((KV, S, HD), BF16),
                        pltpu.VMEM((3, G * C, 3 * C), F32)],
        args=(z, z, z, z, q_gain, k_gain, _sink_rows(sink, dm), *tabs), tasks=tasks)


def _merge_gate_spec(tm, dm):
    return pl.BlockSpec((pl.Element(tm), pl.Element(N_BRANCHES * dm.D)), lambda i: (i * tm, dm.o_mg))


def _merge_out(x, mg, ya_pre, yb_pre, yc_pre, w_ret, w_pool, w_att, w_out, dm, tasks=()):
    S, D, BW = dm.S, dm.D, dm.BW
    tm = min(256, S)

    def body(x_ref, mg_ref, a_ref, b_ref, c_ref, wr_ref, wp_ref, wa_ref, wo_ref, xo_ref, ya_ref, yb_ref, yc_ref, m_ref):
        ya = _dot(a_ref[...], wr_ref[...])
        yb = _dot(b_ref[...], wp_ref[...])
        yc = _dot(c_ref[...], wa_ref[...])
        ya_ref[...], yb_ref[...], yc_ref[...] = _bf(ya), _bf(yb), _bf(yc)
        g0 = _sigmoid(_f32(mg_ref[:, 0:D]))
        g1 = _sigmoid(_f32(mg_ref[:, D:2 * D]))
        g2 = _sigmoid(_f32(mg_ref[:, 2 * D:3 * D]))
        merged = _bf(g0 * ya + g1 * yb + g2 * yc)
        m_ref[...] = merged
        xo_ref[...] = x_ref[...] + _dot(merged, wo_ref[...])

    act = lambda w: pl.BlockSpec((tm, w), lambda i: (i, 0))
    wsp = lambda r: pl.BlockSpec((None, r, D), lambda i: (0, 0, 0), pipeline_mode=ONE_BUFFER)
    return _call(
        body, name="merge_out", grid=(S // tm,),
        in_specs=[act(D), _merge_gate_spec(tm, dm), act(BW), act(BW), act(BW), wsp(BW), wsp(BW), wsp(BW), wsp(D)],
        out_specs=[act(D)] * 5,
        out_shape=[jax.ShapeDtypeStruct((S, D), F32)] + [jax.ShapeDtypeStruct((S, D), BF16)] * 4,
        args=(x, mg, ya_pre, yb_pre, yc_pre, w_ret, w_pool, w_att, w_out), tasks=tasks)


def _loss_grad(y, target, dm):
    S, D = dm.S, dm.D
    tm = dm.tm

    def body(y_ref, t_ref, s_ref, d_ref):
        @pl.when(pl.program_id(0) == 0)
        def _():
            s_ref[...] = jnp.zeros_like(s_ref)

        e = y_ref[...] - t_ref[...]
        d_ref[...] = e * (1.0 / D)
        s_ref[...] += jnp.sum(jnp.sum(e * e, axis=-1, keepdims=True), axis=0, keepdims=True)

    row = pl.BlockSpec((tm, D), lambda i: (i, 0))
    own, _ = _call(
        body, name="loss_grad", grid=(S // tm,), in_specs=[row, row],
        out_specs=[pl.BlockSpec((1, 128), lambda i: (0, 0)), row],
        out_shape=[jax.ShapeDtypeStruct((1, 128), F32), jax.ShapeDtypeStruct((S, D), F32)],
        args=(y, target))
    return own


def _merge_bwd_gates(dx, mg, ya, yb, yc, w_out, dm, tasks=()):
    S, D = dm.S, dm.D
    tm = min(256, S)

    def body(dx_ref, mg_ref, ya_ref, yb_ref, yc_ref, wo_ref, dmg_ref, dya_ref, dyb_ref, dyc_ref, dxb_ref):
        dxb = _bf(dx_ref[...])
        dxb_ref[...] = dxb
        dm_ = _dot_nt(dxb, wo_ref[...])
        for k, (y_ref, dy_ref) in enumerate(((ya_ref, dya_ref), (yb_ref, dyb_ref), (yc_ref, dyc_ref))):
            g = _sigmoid(_f32(mg_ref[:, k * D:(k + 1) * D]))
            dmg_ref[:, k * D:(k + 1) * D] = _bf(dm_ * _f32(y_ref[...]) * g * (1.0 - g))
            dy_ref[...] = _bf(dm_ * g)

    act = lambda w: pl.BlockSpec((tm, w), lambda i: (i, 0))
    return _call(
        body, name="merge_bwd_gates", grid=(S // tm,),
        in_specs=[act(D), _merge_gate_spec(tm, dm), act(D), act(D), act(D),
                  pl.BlockSpec((None, D, D), lambda i: (0, 0, 0), pipeline_mode=ONE_BUFFER)],
        out_specs=[_merge_gate_spec(tm, dm), act(D), act(D), act(D), act(D)],
        out_shape=[jax.ShapeDtypeStruct((S, dm.INW), BF16)] + [jax.ShapeDtypeStruct((S, D), BF16)] * 4,
        args=(dx, mg, ya, yb, yc, w_out), tasks=tasks)


def _merge_bwd_proj(dya, dyb, dyc, w_ret, w_pool, w_att, dm):
    S, D, BW = dm.S, dm.D, dm.BW
    tm = dm.tm

    def body(da_ref, db_ref, dc_ref, wr_ref, wp_ref, wa_ref, oa_ref, ob_ref, oc_ref):
        oa_ref[...] = _bf(_dot_nt(da_ref[...], wr_ref[...]))
        ob_ref[...] = _bf(_dot_nt(db_ref[...], wp_ref[...]))
        oc_ref[...] = _bf(_dot_nt(dc_ref[...], wa_ref[...]))

    act = lambda w: pl.BlockSpec((tm, w), lambda i: (i, 0))
    wsp = pl.BlockSpec((None, BW, D), lambda i: (0, 0, 0), pipeline_mode=ONE_BUFFER)
    own, _ = _call(
        body, name="merge_bwd_proj", grid=(S // tm,),
        in_specs=[act(D)] * 3 + [wsp] * 3, out_specs=[act(BW)] * 3,
        out_shape=[jax.ShapeDtypeStruct((S, BW), BF16)] * 3,
        args=(dya, dyb, dyc, w_ret, w_pool, w_att))
    return own


def _col_writes(dz_ref, sem, step, parts):
    return [pltpu.make_async_copy(src, dz_ref.at[:, pl.ds(pl.multiple_of(off + step * w, 128), w)], sem.at[k])
            for k, (src, off, w) in enumerate(parts)]


def _ret_bwd(z, dz, dya_pre, a_f, a_b, cosr, sinr, dm, tasks=()):
    S, RD, C, nC = dm.S, dm.RD, CHUNK, dm.nC
    scale = RD ** -0.5
    H = RET_HEADS

    def body(q_ref, k_ref, v_ref, g_ref, dy_ref, af_ref, ab_ref, cos_ref, sin_ref, dd_ref, dz_ref,
             qh_s, kh_s, sf_s, sb_s, do_s, dqh_s, dkh_s, dv_s, st_s, lam_s, dq_ref, dk_ref, dv_ref, dg_ref, wsem,
             acc_dd, acc_tf, acc_tb, acc_uf, acc_ub, acc_lf, acc_lb):
        head = pl.program_id(0)
        writes = lambda step: _col_writes(dz_ref, wsem, step, [(dq_ref, dm.o_rq, RD), (dk_ref, dm.o_rk, RD),
                                                               (dv_ref, dm.o_rv, RD), (dg_ref, dm.o_rg, RD)])
        t = _ret_tables(af_ref, ab_ref)
        zero_rr = jnp.zeros((RD, RD), F32)

        st_s[...] = zero_rr

        def pass0(n, carry):
            r = pl.ds(pl.multiple_of(n * C, C), C)
            cs, sn = cos_ref[r, :], sin_ref[r, :]
            qh = _rot_half(_f32(q_ref[r, :]), cs, sn)
            kh = _rot_half(_f32(k_ref[r, :]), cs, sn) * scale
            qh_s[r, :] = _bf(qh)
            kh_s[r, :] = _bf(kh)
            st = st_s[...]
            sf_s[n] = _bf(st)
            st_s[...] = st * t["dec_f"] + _dot_tn(_bf(kh * t["w_f"]), v_ref[r, :])
            return carry

        _loop(nC, pass0, 0)

        @pl.when(head > 0)
        def _():
            for cp in writes(head - 1):
                cp.wait()

        st_s[...] = zero_rr
        lam_s[...] = zero_rr

        def pass1(i, carry):
            n = nC - 1 - i
            r = pl.ds(pl.multiple_of(n * C, C), C)
            qhb, khb, vb = qh_s[r, :], kh_s[r, :], v_ref[r, :]
            qhf, khf = _f32(qhb), _f32(khb)
            sc_ = _dot_nt(qhb, khb)
            p = _bf(sc_ * t["dmask"])
            sb = st_s[...]
            sbb = _bf(sb)
            sb_s[n] = sbb
            st_s[...] = sb * t["dec_b"] + _dot_tn(_bf(khf * t["w_b"]), vb)
            sfb = sf_s[n]
            qf_b, qb_b = _bf(qhf * t["q_f"]), _bf(qhf * t["q_b"])
            out = _dot(p, vb) + _dot(qf_b, sfb) + _dot(qb_b, sbb)
            rr = lax.rsqrt(jnp.mean(out * out, axis=-1, keepdims=True) + RMS_EPS)
            y = out * rr
            g = _f32(g_ref[r, :])
            dya = _f32(dy_ref[r, :])
            dg_ref[r, :] = _bf(dya * y * _dsilu(g))
            dyn = dya * _silu(g)
            dout = rr * (dyn - y * jnp.mean(dyn * y, axis=-1, keepdims=True))
            dob = _bf(dout)
            do_s[r, :] = dob
            dp = _dot_nt(dob, vb)
            dv = _dot_tn(p, dob)
            ds = _bf(dp * t["dmask"])
            dqh = _dot(ds, khb)
            dkh = _dot_tn(ds, qhb)
            acc_dd[...] += dp * sc_
            tf = _dot_nt(dob, sfb)
            tb = _dot_nt(dob, sbb)
            dqh = dqh + tf * t["q_f"] + tb * t["q_b"]
            acc_tf[...] += tf * qhf
            acc_tb[...] += tb * qhf
            lam = lam_s[...]
            lamb = _bf(lam)
            acc_lf[...] += lam * _f32(sfb)
            u = _dot_nt(vb, lamb)
            dkh = dkh + u * t["w_f"]
            acc_uf[...] += u * khf
            dv = dv + _dot(_bf(khf * t["w_f"]), lamb)
            lam_s[...] = _dot_tn(qf_b, dob) + lam * t["dec_f"]
            dqh_s[r, :] = dqh
            dkh_s[r, :] = dkh
            dv_s[r, :] = dv
            return carry

        for acc in (acc_dd, acc_tf, acc_tb, acc_uf, acc_ub, acc_lf, acc_lb):
            acc[...] = jnp.zeros_like(acc)
        _loop(nC, pass1, 0)

        lam_s[...] = zero_rr

        def pass2(n, carry):
            r = pl.ds(pl.multiple_of(n * C, C), C)
            qhb, khb, vb, dob = qh_s[r, :], kh_s[r, :], v_ref[r, :], do_s[r, :]
            qhf, khf = _f32(qhb), _f32(khb)
            lam = lam_s[...]
            lamb = _bf(lam)
            acc_lb[...] += lam * _f32(sb_s[n])
            u = _dot_nt(vb, lamb)
            dkh = dkh_s[r, :] + u * t["w_b"]
            acc_ub[...] += u * khf
            dv = dv_s[r, :] + _dot(_bf(khf * t["w_b"]), lamb)
            lam_s[...] = _dot_tn(_bf(qhf * t["q_b"]), dob) + lam * t["dec_b"]
            cs, sn = cos_ref[r, :], sin_ref[r, :]
            dq_ref[r, :] = _bf(_rot_half(dqh_s[r, :], cs, -sn))
            dk_ref[r, :] = _bf(_rot_half(dkh * scale, cs, -sn))
            dv_ref[r, :] = _bf(dv)
            return carry

        _loop(nC, pass2, 0)
        rows = lambda acc: jnp.sum(acc[...], axis=-1, keepdims=True)
        dd = acc_dd[...] * t["dmask"] * t["alag"]
        glf = (_sum11(jnp.where(t["low"], dd, 0.0)) + _sum11(rows(acc_tf) * t["q_f"] * (t["j"] + 1.0))
               + _sum11(rows(acc_uf) * t["w_f"] * (C - 1.0 - t["j"])) + _sum11(acc_lf[...]) * t["dec_f"] * C)
        glb = (_sum11(jnp.where(t["low"], 0.0, dd)) + _sum11(rows(acc_tb) * t["q_b"] * (C - t["j"]))
               + _sum11(rows(acc_ub) * t["w_b"] * t["j"]) + _sum11(acc_lb[...]) * t["dec_b"] * C)
        row = lax.broadcasted_iota(jnp.int32, (8, 128), 0)
        da_f = glf * t["lgf"]
        da_b = glb * t["lgb"]
        dd_ref[...] = jnp.where(row == 0, da_f, jnp.where(row == 1, da_b, 0.0))
        for cp in writes(head):
            cp.start()

        @pl.when(head == H - 1)
        def _():
            for cp in writes(head):
                cp.wait()

    zs = lambda off: pl.BlockSpec((S, RD), lambda h: (0, off // RD + h), pipeline_mode=ONE_BUFFER)
    col = pl.BlockSpec((S, RD), lambda h: (0, h), pipeline_mode=ONE_BUFFER)
    dec = pl.BlockSpec((None, 1, 128), lambda h: (h, 0, 0))
    tab = pl.BlockSpec((S, RD // 2), lambda h: (0, 0), pipeline_mode=ONE_BUFFER)
    return _call(
        body, name="ret_bwd", grid=(H,),
        in_specs=[zs(dm.o_rq), zs(dm.o_rk), zs(dm.o_rv), zs(dm.o_rg), col, dec, dec, tab, tab],
        out_specs=[pl.BlockSpec((None, 8, 128), lambda h: (h, 0, 0))],
        out_shape=[jax.ShapeDtypeStruct((H, 8, 128), F32)],
        scratch_shapes=[pltpu.VMEM((S, RD), BF16), pltpu.VMEM((S, RD), BF16),
                        pltpu.VMEM((nC, RD, RD), BF16), pltpu.VMEM((nC, RD, RD), BF16),
                        pltpu.VMEM((S, RD), BF16),
                        pltpu.VMEM((S, RD), F32), pltpu.VMEM((S, RD), F32), pltpu.VMEM((S, RD), F32),
                        pltpu.VMEM((RD, RD), F32), pltpu.VMEM((RD, RD), F32)]
        + [pltpu.VMEM((S, RD), BF16)] * 4 + [pltpu.SemaphoreType.DMA((4,))]
        + [pltpu.VMEM((C, C), F32)] + [pltpu.VMEM((C, RD), F32)] * 4 + [pltpu.VMEM((RD, RD), F32)] * 2,
        args=(z, z, z, z, dya_pre, _decay_rows(a_f), _decay_rows(a_b), cosr, sinr), tasks=tasks, carry=[dz])


def _pool_bwd(z, dz, dyb_pre, pool_w, pool_scale, dm):
    S, PD = dm.S, dm.PD
    PB = _pool_rows(S)
    WIN = PB + 2 * CHUNK
    G = POOL_GROUPS

    def body(u_ref, g_ref, dy_ref, w_ref, sc_ref, dw_ref, dsc_ref, dz_ref, dp_s, dpc_s, dw_s, du_ref, dg_ref, wsem):
        group = pl.program_id(0)
        writes = lambda step: _col_writes(dz_ref, wsem, step, [(du_ref, dm.o_pv, PD), (dg_ref, dm.o_pg, PD)])
        half = jnp.left_shift(1, group)
        w = w_ref[...]
        sc = sc_ref[...]
        dw_s[...] = jnp.zeros_like(dw_s)
        dsc_ref[...] = jnp.zeros_like(dsc_ref)

        @pl.when(group > 0)
        def _():
            for cp in writes(group - 1):
                cp.wait()

        def blk1(n, carry):
            row0 = pl.multiple_of(n * PB, PB)
            r = pl.ds(row0, PB)
            st = _pool_win_start(n, PB, S)
            band = _band(row0, st, -half, half - 1, (PB, WIN))
            inv = _pool_counts(row0, half, S, PB)
            pb = _bf(_dot(band, u_ref[pl.ds(st, WIN), :]) * inv - _f32(u_ref[r, :]))
            ylin = _dot(pb, w)
            y = ylin * sc
            g = _f32(g_ref[r, :])
            dyb = _f32(dy_ref[r, :])
            dg_ref[r, :] = _bf(dyb * y * _dsilu(g))
            dy = dyb * _silu(g)
            dsc_ref[...] += jnp.sum(dy * ylin, axis=0, keepdims=True)
            dyl = _bf(dy * sc)
            dw_s[...] += _dot_tn(pb, dyl)
            dp = _dot_nt(dyl, w)
            dp_s[r, :] = dp
            dpc_s[r, :] = _bf(dp * inv)
            return carry

        _loop(S // PB, blk1, 0)
        dw_ref[...] = _bf(dw_s[...])

        def blk2(n, carry):
            row0 = pl.multiple_of(n * PB, PB)
            r = pl.ds(row0, PB)
            st = _pool_win_start(n, PB, S)
            band_t = _band(row0, st, -half + 1, half, (PB, WIN))
            du_ref[r, :] = _bf(_dot(band_t, dpc_s[pl.ds(st, WIN), :]) - dp_s[r, :])
            return carry

        _loop(S // PB, blk2, 0)
        for cp in writes(group):
            cp.start()

        @pl.when(group == G - 1)
        def _():
            for cp in writes(group):
                cp.wait()

    own, _ = _call(
        body, name="pool_bwd", grid=(G,),
        in_specs=[pl.BlockSpec((S, PD), lambda g: (0, dm.o_pv // PD + g)),
                  pl.BlockSpec((S, PD), lambda g: (0, dm.o_pg // PD + g)),
                  pl.BlockSpec((S, PD), lambda g: (0, g)),
                  pl.BlockSpec((None, PD, PD), lambda g: (g, 0, 0)),
                  pl.BlockSpec((None, 1, PD), lambda g: (g, 0, 0))],
        out_specs=[pl.BlockSpec((None, PD, PD), lambda g: (g, 0, 0)), pl.BlockSpec((None, 1, PD), lambda g: (g, 0, 0))],
        out_shape=[jax.ShapeDtypeStruct((G, PD, PD), BF16), jax.ShapeDtypeStruct((G, 1, PD), F32)],
        scratch_shapes=[pltpu.VMEM((S, PD), F32), pltpu.VMEM((S, PD), BF16), pltpu.VMEM((PD, PD), F32),
                        pltpu.VMEM((S, PD), BF16), pltpu.VMEM((S, PD), BF16), pltpu.SemaphoreType.DMA((2,))],
        args=(z, z, dyb_pre, pool_w, pool_scale), carry=[dz])
    return own


def _att_bwd(z, dz, dyc_pre, q_gain, k_gain, sink, tabs, dm, tasks=()):
    S, C, nC, HD, G = dm.S, CHUNK, dm.nC, ATT_HEAD_DIM, ATT_GROUP
    GW = G * HD
    scale = HD ** -0.5

    def body(q_ref, k_ref, v_ref, g_ref, dy_ref, qg_ref, kg_ref, sk_ref, cs_ref, su_ref, sd_ref,
             sm_ref, dz_ref, qn_s, kn_s, dqn_s, dkn_s, dv_s, bias_s, dq_ref, dk_ref, dv_ref, dg_ref, wsem):
        kv = pl.program_id(0)
        writes = lambda step: _col_writes(dz_ref, wsem, step, [(dq_ref, dm.o_aq, GW), (dk_ref, dm.o_ak, HD),
                                                               (dv_ref, dm.o_av, HD), (dg_ref, dm.o_ag, GW)])
        _att_prep(q_ref, k_ref, qg_ref, kg_ref, cs_ref, su_ref, sd_ref, qn_s, kn_s, nC)

        @pl.when(kv > 0)
        def _():
            for cp in writes(kv - 1):
                cp.wait()

        _att_bias_tables(bias_s)
        dkn_s[...] = jnp.zeros_like(dkn_s)
        dv_s[...] = jnp.zeros_like(dv_s)
        sink_col = _sink_col(sk_ref)

        def blk(n, dsink):
            row0 = pl.multiple_of(n * C, C)
            r = pl.ds(row0, C)
            w = pl.ds(_win_start(n, S), 3 * C)
            q4 = qn_s[:, r, :].reshape(G * C, HD)
            kw, vw = kn_s[w, :], v_ref[w, :]
            p, ps = _att_probs(q4, kw, sink_col, _att_bias(bias_s, n, nC))
            pb = _bf(p)
            o = _dot(pb, vw)
            do_parts = []
            for g in range(G):
                cols = slice(g * HD, (g + 1) * HD)
                gate = _f32(g_ref[r, cols])
                dy = _f32(dy_ref[r, cols])
                dg_ref[r, cols] = _bf(dy * o[g * C:(g + 1) * C, :] * _dsilu(gate))
                do_parts.append(dy * _silu(gate))
            dob = _bf(jnp.concatenate(do_parts, axis=0))
            dp = _dot_nt(dob, vw)
            drow = jnp.sum(p * dp, axis=-1, keepdims=True)
            ds = _bf(p * (dp - drow))
            dsink = dsink - ps * drow
            dqn_s[:, r, :] = (_dot(ds, kw) * scale).reshape(G, C, HD)
            dkn_s[w, :] += _dot_tn(ds, q4) * scale
            dv_s[w, :] += _dot_tn(pb, dob)
            return dsink

        dsink = _loop(nC, blk, jnp.zeros((G * C, 1), F32))

        def fin(n, acc):
            dqg, dkg = acc
            r = pl.ds(pl.multiple_of(n * C, C), C)
            cs, su, sd = cs_ref[r, :], su_ref[r, :], sd_ref[r, :]

            def norm_bwd(x, gain, dqn):
                rr = lax.rsqrt(jnp.mean(x * x, axis=-1, keepdims=True) + RMS_EPS)
                u = x * rr
                dw = _rot_part(dqn, cs, -su, -sd)
                du = dw * gain
                gsum = dw * u
                if gsum.ndim == 3:
                    gsum = jnp.sum(gsum, axis=0)
                return rr * (du - u * jnp.mean(du * u, axis=-1, keepdims=True)), jnp.sum(gsum, axis=0, keepdims=True)

            dk, gk = norm_bwd(_f32(k_ref[r, :]), kg_ref[...], dkn_s[r, :])
            dk_ref[r, :] = _bf(dk)
            dv_ref[r, :] = _bf(dv_s[r, :])
            dq, gq = norm_bwd(_heads(_f32(q_ref[r, :]), HD), qg_ref[...], dqn_s[:, r, :])
            for g in range(G):
                dq_ref[r, g * HD:(g + 1) * HD] = _bf(dq[g])
            return dqg + gq, dkg + gk

        dqg, dkg = lax.fori_loop(0, nC, fin, (jnp.zeros((1, HD), F32), jnp.zeros((1, HD), F32)))
        sm_ref[...] = jnp.zeros_like(sm_ref)
        sm_ref[0:1, :] = dqg
        sm_ref[1:2, :] = dkg
        for g in range(G):
            sm_ref[2 + g:3 + g, :] = jnp.broadcast_to(_sum11(dsink[g * C:(g + 1) * C, :]), (1, HD))
        for cp in writes(kv):
            cp.start()

        @pl.when(kv == dm.AKV - 1)
        def _():
            for cp in writes(kv):
                cp.wait()

    tab = pl.BlockSpec((S, HD), lambda h: (0, 0), pipeline_mode=ONE_BUFFER)
    gain = pl.BlockSpec((1, HD), lambda h: (0, 0))
    wide = lambda off: pl.BlockSpec((S, GW), lambda h: (0, off // GW + h), pipeline_mode=ONE_BUFFER)
    thin = lambda off: pl.BlockSpec((S, HD), lambda h: (0, off // HD + h), pipeline_mode=ONE_BUFFER)
    return _call(
        body, name="att_bwd", grid=(dm.AKV,),
        in_specs=[wide(dm.o_aq), thin(dm.o_ak), thin(dm.o_av), wide(dm.o_ag), wide(0), gain, gain,
                  pl.BlockSpec((None, G, 128), lambda h: (h, 0, 0)), tab, tab, tab],
        out_specs=[pl.BlockSpec((None, 8, 128), lambda h: (h, 0, 0))],
        out_shape=[jax.ShapeDtypeStruct((dm.AKV, 8, 128), F32)],
        scratch_shapes=[pltpu.VMEM((G, S, HD), BF16), pltpu.VMEM((S, HD), BF16),
                        pltpu.VMEM((G, S, HD), F32), pltpu.VMEM((S, HD), F32), pltpu.VMEM((S, HD), F32),
                        pltpu.VMEM((3, G * C, 3 * C), F32),
                        pltpu.VMEM((S, GW), BF16), pltpu.VMEM((S, HD), BF16), pltpu.VMEM((S, HD), BF16),
                        pltpu.VMEM((S, GW), BF16), pltpu.SemaphoreType.DMA((4,))],
        args=(z, z, z, z, dyc_pre, q_gain, k_gain, _sink_rows(sink, dm), *tabs), tasks=tasks, carry=[dz])


def _grad_matmul(a, b, name, tasks=()):
    S, M = a.shape
    N = b.shape[1]
    tm, tn = min(1024, M), min(512, N)

    def body(a_ref, b_ref, o_ref):
        o_ref[...] = _bf(_dot_tn(a_ref[...], b_ref[...]))

    own, tk = _call(
        body, name=name, grid=(M // tm, N // tn),
        in_specs=[pl.BlockSpec((S, tm), lambda i, j: (0, i)), pl.BlockSpec((S, tn), lambda i, j: (0, j))],
        out_specs=[pl.BlockSpec((None, tm, tn), lambda i, j: (0, i, j))],
        out_shape=[jax.ShapeDtypeStruct((1, M, N), BF16)],
        args=(a, b), tasks=tasks)
    return own[0], tk


def _inproj_bwd(dz, w_in, dm, tasks=()):
    S, D = dm.S, dm.D
    tm = dm.tm
    tk = dm.INW // N_CHIPS if (dm.INW // N_CHIPS) % 128 == 0 else 512
    nk = dm.INW // tk

    def body(dz_ref, w_ref, dh_ref):
        part = _dot_nt(dz_ref[...], w_ref[...])

        @pl.when(pl.program_id(1) == 0)
        def _():
            dh_ref[...] = part

        @pl.when(pl.program_id(1) != 0)
        def _():
            dh_ref[...] += part

    return _call(
        body, name="inproj_bwd", grid=(S // tm, nk),
        in_specs=[pl.BlockSpec((tm, tk), lambda i, k: (i, k)), pl.BlockSpec((None, D, tk), lambda i, k: (0, 0, k))],
        out_specs=[pl.BlockSpec((tm, D), lambda i, k: (i, 0))],
        out_shape=[jax.ShapeDtypeStruct((S, D), F32)],
        args=(dz, w_in), tasks=tasks)


def _norm_bwd(dh, x, norm_g, dx_out, dm):
    S, D = dm.S, dm.D
    tm = dm.tm

    def body(dh_ref, x_ref, g_ref, dxo_ref, dx_ref, dg_ref):
        @pl.when(pl.program_id(0) == 0)
        def _():
            dg_ref[...] = jnp.zeros_like(dg_ref)

        xv = x_ref[...]
        rr = lax.rsqrt(jnp.mean(xv * xv, axis=-1, keepdims=True) + RMS_EPS)
        u = xv * rr
        dh = dh_ref[...]
        dg_ref[...] += jnp.sum(dh * u, axis=0, keepdims=True)
        du = dh * g_ref[...]
        dx_ref[...] = dxo_ref[...] + rr * (du - u * jnp.mean(du * u, axis=-1, keepdims=True))

    row = pl.BlockSpec((tm, D), lambda i: (i, 0))
    vec = pl.BlockSpec((1, D), lambda i: (0, 0))
    own, _ = _call(
        body, name="norm_bwd", grid=(S // tm,), in_specs=[row, row, vec, row], out_specs=[row, vec],
        out_shape=[jax.ShapeDtypeStruct((S, D), F32), jax.ShapeDtypeStruct((1, D), F32)],
        args=(dh, x, norm_g, dx_out))
    return own


def _row_block(rows, width, itemsize):
    target = max(16, (2 * 1024 * 1024) // (width * itemsize))
    for rb in range(min(rows, target), 0, -1):
        if rows % rb == 0 and (rb % 16 == 0 or rb == rows):
            return rb
    return rows


def _prefetch_call(body, *, name, grid, in_specs, out_specs, out_shape, args, aliases=None):
    grid_spec = pltpu.PrefetchScalarGridSpec(num_scalar_prefetch=1, grid=grid, in_specs=in_specs, out_specs=out_specs)
    return pl.pallas_call(
        body, name=name, grid_spec=grid_spec, out_shape=out_shape, input_output_aliases=aliases or {},
        compiler_params=pltpu.CompilerParams(dimension_semantics=("arbitrary",) * len(grid),
                                             vmem_limit_bytes=VMEM_LIMIT_V7X),
    )(*args)


def _place_shard(w, l, spec, chip):
    A, rows, width = spec.shard
    rb = _row_block(rows, width, 4)
    nrb = rows // rb
    if spec.kind == "col":
        out_map = lambda a, r, chip: (a, r, chip[0])
    else:
        out_map = lambda a, r, chip: (a, chip[0] * nrb + r, 0)

    def body(chip_ref, w_ref, o_ref):
        o_ref[...] = _bf(w_ref[...])

    return _prefetch_call(
        body, name="place_" + spec.name, grid=(A, nrb),
        in_specs=[pl.BlockSpec((None, rb, width), lambda a, r, chip: (l * A + a, r, 0))],
        out_specs=pl.BlockSpec((None, rb, width), out_map),
        out_shape=jax.ShapeDtypeStruct(spec.full, BF16), args=(chip, w))


def _pair_sum(grad, land, spec, core):
    A, hr, w = spec.half
    rb = _row_block(hr, w, 2)
    nrb = hr // rb
    if spec.kind == "col":
        g_spec = pl.BlockSpec((None, rb, w), lambda j, a, r, core: (a, core[0] * nrb + r, j))
    else:
        g_spec = pl.BlockSpec((None, rb, w), lambda j, a, r, core: (a, (j * 2 + core[0]) * nrb + r, 0))

    def body(core_ref, g_ref, l_ref, o_ref):
        o_ref[...] = _bf(_f32(g_ref[...]) + _f32(l_ref[...]))

    blk = pl.BlockSpec((None, None, rb, w), lambda j, a, r, core: (j, a, r, 0))
    return _prefetch_call(
        body, name="pair_sum_" + spec.name, grid=(N_CHIPS, A, nrb), in_specs=[g_spec, blk], out_specs=blk,
        out_shape=jax.ShapeDtypeStruct((N_CHIPS,) + spec.half, BF16), args=(core, grad, land))


def _chip_sum(pair_sum, land, spec, chip_core):
    A, hr, w = spec.half
    rb = _row_block(hr, w, 4)
    nrb = hr // rb

    def body(cc_ref, p_ref, l0_ref, l1_ref, l2_ref, o_ref):
        o_ref[...] = ((_f32(p_ref[...]) + _f32(l0_ref[...])) + _f32(l1_ref[...])) + _f32(l2_ref[...])

    own = pl.BlockSpec((None, None, rb, w), lambda a, r, cc: (cc[0], a, r, 0))
    slot = lambda p: pl.BlockSpec((None, None, rb, w), lambda a, r, cc: (p, a, r, 0))
    return _prefetch_call(
        body, name="chip_sum_" + spec.name, grid=(A, nrb), in_specs=[own, slot(0), slot(1), slot(2)],
        out_specs=pl.BlockSpec((None, rb, w), lambda a, r, cc: (a, cc[1] * nrb + r, 0)),
        out_shape=jax.ShapeDtypeStruct(spec.shard, F32), args=(chip_core, pair_sum, land, land, land))


def _adamw_math(w, g, m, v):
    m = ADAM_B1 * m + (1.0 - ADAM_B1) * g
    v = ADAM_B2 * v + (1.0 - ADAM_B2) * (g * g)
    m_hat = m / (1.0 - ADAM_B1 ** ADAM_STEP)
    v_hat = v / (1.0 - ADAM_B2 ** ADAM_STEP)
    delta = -ADAM_LR * (m_hat / (jnp.sqrt(v_hat) + ADAM_EPS) + ADAM_WD * w)
    return delta, m, v


def _adamw(w, g, m, v, l, depth, spec, carried):
    A, R, C = spec.shard
    rb = _row_block(R, C, 4 * 4)
    stacked = pl.BlockSpec((None, rb, C), lambda a, r: (l * A + a, r, 0))
    n_carry = 0 if carried is None else 4

    def body(w_ref, g_ref, m_ref, v_ref, *rest):
        go_ref, d_ref, mo_ref, vo_ref = rest[n_carry:]
        g = g_ref[...]
        go_ref[...] = g
        d_ref[...], mo_ref[...], vo_ref[...] = _adamw_math(w_ref[...], g, m_ref[...], v_ref[...])

    return pl.pallas_call(
        body, name="adamw_" + spec.name, grid=(A, R // rb),
        in_specs=[stacked, pl.BlockSpec((None, rb, C), lambda a, r: (a, r, 0)), stacked, stacked] + [HBM_ANY] * n_carry,
        out_specs=[stacked] * 4,
        out_shape=[jax.ShapeDtypeStruct((depth * A, R, C), F32)] * 4,
        input_output_aliases={4 + i: i for i in range(n_carry)},
        compiler_params=pltpu.CompilerParams(dimension_semantics=("arbitrary", "arbitrary"),
                                             vmem_limit_bytes=VMEM_LIMIT_V7X),
    )(w, g, m, v, *(carried or ()))


def _small_update(g_part, w, m, v):
    R = g_part.shape[0]
    n_dev = 8

    def body(g_ref, w_ref, m_ref, v_ref, go_ref, d_ref, mo_ref, vo_ref, all_s, send_sem, recv_sem):
        x, y, c = _place()
        me = 4 * x + 2 * y + c
        all_s[me] = g_ref[...]
        cps = []
        for k in range(1, n_dev):
            peer = (x ^ ((k >> 2) & 1), y ^ ((k >> 1) & 1), c ^ (k & 1))
            cp = pltpu.make_async_remote_copy(src_ref=g_ref, dst_ref=all_s.at[me], send_sem=send_sem.at[k],
                                              recv_sem=recv_sem.at[k], device_id=peer, device_id_type=MESH)
            cp.start()
            cps.append(cp)
        for cp in cps:
            cp.wait()
        g = all_s[0]
        for d in range(1, n_dev):
            g = g + all_s[d]
        go_ref[...] = g
        d_ref[...], mo_ref[...], vo_ref[...] = _adamw_math(w_ref[...], g, m_ref[...], v_ref[...])

    vm = pl.BlockSpec(memory_space=pltpu.VMEM)
    return pl.pallas_call(
        body, name="small_update", in_specs=[vm] * 4, out_specs=[vm] * 4,
        out_shape=[jax.ShapeDtypeStruct((R, 128), F32)] * 4,
        scratch_shapes=[pltpu.VMEM((n_dev, R, 128), F32), pltpu.SemaphoreType.DMA((n_dev,)),
                        pltpu.SemaphoreType.DMA((n_dev,))],
        compiler_params=pltpu.CompilerParams(vmem_limit_bytes=VMEM_LIMIT_V7X),
    )(g_part, w, m, v)


def _pack_small(parts):
    flat = jnp.concatenate([p.reshape(-1) for p in parts])
    pad = (-flat.shape[0]) % 1024
    return jnp.pad(flat, (0, pad)).reshape(-1, 128)


def _unpack_small(packed, like):
    flat = packed.reshape(-1)
    out, at = [], 0
    for p in like:
        out.append(flat[at:at + p.size].reshape(p.shape))
        at += p.size
    return out


def kernel(x, norm_g, w_in, ret_decay_fwd, ret_decay_bwd, pool_w, pool_scale, attn_q_gain, attn_k_gain, attn_sink, w_ret, w_pool, w_att, w_out, loss_target, m_norm_g, m_w_in, m_ret_decay_fwd, m_ret_decay_bwd, m_pool_w, m_pool_scale, m_attn_q_gain, m_attn_k_gain, m_attn_sink, m_w_ret, m_w_pool, m_w_att, m_w_out, v_norm_g, v_w_in, v_ret_decay_fwd, v_ret_decay_bwd, v_pool_w, v_pool_scale, v_attn_q_gain, v_attn_k_gain, v_attn_sink, v_w_ret, v_w_pool, v_w_att, v_w_out):
    S, D = x.shape[1], x.shape[2]
    L = norm_g.shape[0]
    dm = _Dims(S, D, L)
    PD, BW, G = dm.PD, dm.BW, POOL_GROUPS
    xi, yi, ci = _place()
    chip = (2 * xi + yi).astype(jnp.int32).reshape(1)
    core = ci.astype(jnp.int32).reshape(1)
    chip_core = jnp.concatenate([chip, core])

    specs = [_Sharded("w_in", "col", 1, D, dm.INW), _Sharded("w_ret", "col", 1, BW, D), _Sharded("w_pool", "col", 1, BW, D),
             _Sharded("w_att", "col", 1, BW, D), _Sharded("w_out", "row", 1, D, D), _Sharded("pool_w", "row", G, PD, PD)]
    n_big = len(specs)
    big_w = [w_in, w_ret, w_pool, w_att, w_out, pool_w]
    big_m = [m_w_in, m_w_ret, m_w_pool, m_w_att, m_w_out, m_pool_w]
    big_v = [v_w_in, v_w_ret, v_w_pool, v_w_att, v_w_out, v_pool_w]
    stack3 = lambda a, s: a.reshape((L * s.shard[0],) + s.shard[1:])
    big_w3 = [stack3(a, s) for a, s in zip(big_w, specs)]
    big_m3 = [stack3(a, s) for a, s in zip(big_m, specs)]
    big_v3 = [stack3(a, s) for a, s in zip(big_v, specs)]

    W = [[_place_shard(big_w3[t], l, specs[t], chip) for t in range(n_big)] for l in range(L)]
    (ici0, _), = _comm_only("gather_first_ici", [_gather_ici_task(W[0], specs)])
    (W[0], _), = _comm_only("gather_first_d2d", [_gather_d2d_task(ici0, specs)])

    cosr, sinr = _ret_rope_tables(S, dm.RD)
    tabs = _att_tables(S)
    xl = x[0]
    saved = []
    for l in range(L):
        nxt = l + 1 < L
        ng = norm_g[l].reshape(1, D)
        qg, kg = attn_q_gain[l].reshape(1, ATT_HEAD_DIM), attn_k_gain[l].reshape(1, ATT_HEAD_DIM)
        psc = pool_scale[l].reshape(G, 1, PD)
        f_in, f_ret, f_pool, f_att, f_out, f_pw = W[l]
        (z, h), tk = _inproj(xl, ng, f_in, dm, tasks=[_gather_ici_task(W[l + 1][:1], specs[:1])] if nxt else ())
        if nxt:
            W[l + 1][:1] = tk[0][0]
        ya_pre = _ret_fwd(z, ret_decay_fwd[l], ret_decay_bwd[l], cosr, sinr, dm)
        yb_pre = _pool_fwd(z, f_pw, psc, dm)
        (yc_pre,), tk = _att_fwd(z, qg, kg, attn_sink[l], tabs, dm,
                                 tasks=[_gather_ici_task(W[l + 1][1:], specs[1:])] if nxt else ())
        if nxt:
            W[l + 1][1:] = tk[0][0]
        mg = z
        (x_next, ya, yb, yc, merged), tk = _merge_out(xl, mg, ya_pre, yb_pre, yc_pre, f_ret, f_pool, f_att, f_out, dm,
                                                      tasks=[_gather_d2d_task(W[l + 1], specs)] if nxt else ())
        if nxt:
            W[l + 1] = tk[0][0]
        saved.append((xl, z, h, mg, ya_pre, yb_pre, yc_pre, ya, yb, yc, merged, ng, qg, kg, psc))
        xl = x_next
    sq, dx = _loss_grad(xl, loss_target[0], dm)
    loss = lax.psum(sq[0, 0] * (0.5 / D), ("x", "y", "c"))

    grads = [None] * L
    lands = [None] * L
    pairs = [None] * L
    lands2 = [None] * L
    shards = [None] * L
    carried = [None] * n_big
    small = [None] * L

    def pair_sums(k):
        pairs[k] = [_pair_sum(g, ld, s, core) for g, ld, s in zip(grads[k], lands[k], specs)]

    def chip_sums(k):
        shards[k] = [_chip_sum(p, ld, s, chip_core) for p, ld, s in zip(pairs[k], lands2[k], specs)]

    def adamw(k):
        for t in range(n_big):
            carried[t] = _adamw(big_w3[t], shards[k][t], big_m3[t], big_v3[t], k, L, specs[t], carried[t])

    for l in reversed(range(L)):
        xl, z, h, mg, ya_pre, yb_pre, yc_pre, ya, yb, yc, merged, ng, qg, kg, psc = saved[l]
        f_in, f_ret, f_pool, f_att, f_out, f_pw = W[l]
        up1, up2 = l + 1 < L, l + 2 < L
        own, tk = _merge_bwd_gates(dx, mg, ya, yb, yc, f_out, dm, tasks=[_pair_task(grads[l + 1], specs)] if up1 else ())
        dz, dya, dyb, dyc, dxb = own
        if up1:
            lands[l + 1] = tk[0][1]
            pair_sums(l + 1)
        dya_pre, dyb_pre, dyc_pre = _merge_bwd_proj(dya, dyb, dyc, f_ret, f_pool, f_att, dm)
        g_out, _ = _grad_matmul(merged, dxb, "grad_w_out")
        g_ret, _ = _grad_matmul(ya_pre, dya, "grad_w_ret")
        g_pool, _ = _grad_matmul(yb_pre, dyb, "grad_w_pool")
        g_att, _ = _grad_matmul(yc_pre, dyc, "grad_w_att")
        last = l == 0
        in_rows = specs[0].half_rows
        cuts = [0, in_rows // 4, 3 * in_rows // 4, in_rows]
        tasks = [_chip_task(pairs[l + 1][1:], specs[1:])] if up1 else []
        if up1 and last:
            tasks.append(_chip_rows_task(pairs[1][0], None, specs[0], cuts[0], cuts[1]))
        (ddec, dz), tk = _ret_bwd(z, dz, dya_pre, ret_decay_fwd[l], ret_decay_bwd[l], cosr, sinr, dm, tasks=tasks)
        if up1:
            lands2[l + 1] = [None] + tk[0][1]
            land_in = tk[1][1][0] if last else None
        dpw, dps, dz = _pool_bwd(z, dz, dyb_pre, f_pw, psc, dm)
        tasks = [_half_task(shards[l + 2], specs)] if up2 else []
        if up1 and last:
            tasks.append(_chip_rows_task(pairs[1][0], land_in, specs[0], cuts[1], cuts[2]))
        if last:
            tasks.append(_pair_task([g_ret, g_pool, g_att, g_out, dpw], specs[1:]))
        (dsm, dz), tk = _att_bwd(z, dz, dyc_pre, qg, kg, attn_sink[l], tabs, dm, tasks=tasks)
        if up2:
            shards[l + 2] = tk[0][0]
            adamw(l + 2)
        if up1 and last:
            land_in = tk[1 if up2 else 0][0][0]
        if last:
            small_lands = tk[-1][1]
            small_pairs = [_pair_sum(g, ld, s, core)
                           for g, ld, s in zip([g_ret, g_pool, g_att, g_out, dpw], small_lands, specs[1:])]
        if not last:
            g_in, _ = _grad_matmul(h, dz, "grad_w_in")
            (dh,), tk = _inproj_bwd(dz, f_in, dm, tasks=[_chip_task(pairs[l + 1][:1], specs[:1])] if up1 else ())
            if up1:
                lands2[l + 1][0] = tk[0][1][0]
                chip_sums(l + 1)
            grads[l] = [g_in, g_ret, g_pool, g_att, g_out, dpw]
        else:
            tasks = [_chip_rows_task(pairs[1][0], land_in, specs[0], cuts[2], cuts[3])] if up1 else []
            tasks.append(_chip_task(small_pairs, specs[1:]))
            g_in, tk = _grad_matmul(h, dz, "grad_w_in", tasks=tasks)
            if up1:
                lands2[1][0] = tk[0][0][0]
                chip_sums(1)
            small_lands2 = tk[-1][1]
            (_, in_land), = _comm_only("grad_pair_exchange", [_pair_task([g_in], specs[:1])])
            pairs[0] = [_pair_sum(g_in, in_land[0], specs[0], core)] + small_pairs
            (dh,), tk = _inproj_bwd(dz, f_in, dm, tasks=[_chip_task(pairs[0][:1], specs[:1])])
            lands2[0] = tk[0][1] + small_lands2
            chip_sums(0)
        dx, dng = _norm_bwd(dh, xl, ng, dx, dm)
        small[l] = [dng.reshape(D), ddec[:, 0, 0], ddec[:, 1, 0], dps.reshape(BW), jnp.sum(dsm[:, 0, :], axis=0),
                    jnp.sum(dsm[:, 1, :], axis=0), dsm[:, 2:2 + ATT_GROUP, 0].reshape(dm.AH)]

    rest = [k for k in (1, 0) if k < L]
    done = _comm_only("grad_half_exchange", [_half_task(shards[k], specs) for k in rest])
    for k, (both, _) in zip(rest, done):
        shards[k] = both
        adamw(k)

    back = lambda a, like: a.reshape(like.shape)
    g_big, d_big, m_big, v_big = ([back(carried[t][i], big_w[t]) for t in range(n_big)] for i in range(4))

    small_g = [jnp.stack([small[l][i] for l in range(L)]) for i in range(7)]
    small_w = [norm_g, ret_decay_fwd, ret_decay_bwd, pool_scale, attn_q_gain, attn_k_gain, attn_sink]
    small_m = [m_norm_g, m_ret_decay_fwd, m_ret_decay_bwd, m_pool_scale, m_attn_q_gain, m_attn_k_gain, m_attn_sink]
    small_v = [v_norm_g, v_ret_decay_fwd, v_ret_decay_bwd, v_pool_scale, v_attn_q_gain, v_attn_k_gain, v_attn_sink]
    sg, sd, sm, sv = _small_update(_pack_small(small_g), _pack_small(small_w), _pack_small(small_m), _pack_small(small_v))
    g_sm, d_sm, m_sm, v_sm = (_unpack_small(a, small_w) for a in (sg, sd, sm, sv))

    def ordered(big, small_):
        return [small_[0], big[0], small_[1], small_[2], big[5], small_[3], small_[4], small_[5], small_[6],
                big[1], big[2], big[3], big[4]]

    return (loss, dx[None], *ordered(g_big, g_sm), *ordered(d_big, d_sm), *ordered(m_big, m_sm),
            *ordered(v_big, v_sm))
```

```python
import jax
import jax.numpy as jnp
from jax import lax
from jax.experimental import pallas as pl
from jax.experimental.pallas import tpu as pltpu

F32 = jnp.float32
BF16 = jnp.bfloat16
MESH = pl.DeviceIdType.MESH

RMS_EPS = 1e-6
NEG_BIG = -1e30
CHUNK = 128
RET_HEADS = 4
POOL_GROUPS = 4
ATT_HEAD_DIM = 128
ATT_GROUP = 4
ROPE_DIMS = 32
RET_ROPE_BASE = 10000.0
ROPE_THETA = 500000.0
N_BRANCHES = 3
N_CHIPS = 4

ADAM_LR = 0.001
ADAM_B1 = 0.9
ADAM_B2 = 0.999
ADAM_EPS = 1e-08
ADAM_WD = 0.01
ADAM_STEP = 10

VMEM_LIMIT_V7X = 56 * 1024 * 1024

TN = (((0,), (0,)), ((), ()))
NT = (((1,), (1,)), ((), ()))

RET_HEADS_PER_STEP = 2
LOOP_UNROLL = 2
HBM_ANY = pl.BlockSpec(memory_space=pl.ANY)
ONE_BUFFER = pl.Buffered(1)


def _sigmoid(x):
    return 1.0 / (1.0 + jnp.exp(-x))


def _silu(x):
    return x * _sigmoid(x)


def _dsilu(x):
    s = _sigmoid(x)
    return s * (1.0 + x * (1.0 - s))


def _dot(a, b):
    return jnp.dot(a, b, preferred_element_type=F32)


def _dot_tn(a, b):
    return lax.dot_general(a, b, TN, preferred_element_type=F32)


def _dot_nt(a, b):
    return lax.dot_general(a, b, NT, preferred_element_type=F32)


def _bf(x):
    return x.astype(BF16)


def _f32(x):
    return x.astype(F32)


def _loop(n, body, init):
    def several(i, carry):
        for u in range(LOOP_UNROLL):
            carry = body(i * LOOP_UNROLL + u, carry)
        return carry
    return lax.fori_loop(0, n // LOOP_UNROLL, several, init)


def _sum11(x):
    return jnp.sum(jnp.sum(x, axis=1, keepdims=True), axis=0, keepdims=True)


class _Dims:
    def __init__(self, seq, d_model, depth):
        self.S, self.D, self.L = seq, d_model, depth
        bw = d_model // 2
        self.BW = bw
        self.RD = bw // RET_HEADS
        self.PD = bw // POOL_GROUPS
        self.AH = bw // ATT_HEAD_DIM
        self.AKV = self.AH // ATT_GROUP
        self.AKW = self.AKV * ATT_HEAD_DIM
        self.o_rq, self.o_rk, self.o_rv, self.o_rg = 0, bw, 2 * bw, 3 * bw
        self.o_pv, self.o_pg = 4 * bw, 5 * bw
        self.o_aq = 6 * bw
        self.o_ak = 7 * bw
        self.o_av = 7 * bw + self.AKW
        self.o_ag = 7 * bw + 2 * self.AKW
        self.o_mg = 8 * bw + 2 * self.AKW
        self.INW = self.o_mg + N_BRANCHES * d_model
        self.nC = seq // CHUNK
        self.tm = min(512, seq)


def _place():
    return lax.axis_index("x"), lax.axis_index("y"), lax.axis_index("c")


def _chip_peers(x, y):
    return [(1 - x, y), (x, 1 - y), (1 - x, 1 - y)]


class _Sharded:
    def __init__(self, name, kind, a, r, cc):
        self.name, self.kind, self.A, self.R, self.Cc = name, kind, a, r, cc
        self.full = (a, r, cc)
        if kind == "col":
            self.shard = (a, r, cc // N_CHIPS)
            self.half_rows = r // 2
        else:
            self.shard = (a, r // N_CHIPS, cc)
            self.half_rows = r // N_CHIPS // 2
        self.half = (a, self.half_rows, self.shard[2])

    def in_full(self, ref, chip, core=None):
        hr = self.half_rows
        if self.kind == "col":
            rows = pl.ds(0, self.R) if core is None else pl.ds(core * hr, hr)
            return ref.at[:, rows, pl.ds(chip * self.shard[2], self.shard[2])]
        rows = pl.ds(chip * self.shard[1], self.shard[1]) if core is None else pl.ds(chip * self.shard[1] + core * hr, hr)
        return ref.at[:, rows, :]

    def in_full_rows(self, ref, chip, core, lo, n):
        if self.kind == "col":
            return ref.at[:, pl.ds(core * self.half_rows + lo, n), pl.ds(chip * self.shard[2], self.shard[2])]
        return ref.at[:, pl.ds(chip * self.shard[1] + core * self.half_rows + lo, n), :]

    def in_shard(self, ref, core):
        return ref.at[:, pl.ds(core * self.half_rows, self.half_rows), :]


class _Task:
    def __init__(self, ro, rw, new, n_sem, copies):
        self.ro, self.rw, self.new, self.n_sem, self.copies = list(ro), list(rw), list(new), n_sem, copies


def _remote(src, dst, send_sem, recv_sem, k, device):
    def make():
        return pltpu.make_async_remote_copy(src_ref=src, dst_ref=dst, send_sem=send_sem.at[k], recv_sem=recv_sem.at[k],
                                            device_id=device, device_id_type=MESH)
    return make


def _gather_ici_task(bufs, specs):
    def copies(ro, rw, new, ss, rs):
        x, y, c = _place()
        me = 2 * x + y
        out = []
        for t, spec in enumerate(specs):
            for p, (px, py) in enumerate(_chip_peers(x, y)):
                mine = spec.in_full(rw[t], me, c)
                theirs = spec.in_full(rw[t], 2 * px + py, c)
                k = t * 3 + p
                out.append((_remote(mine, mine, ss, rs, k, (px, py, c)), _remote(mine, mine, ss, rs, k, (px, py, c)),
                            _remote(theirs, theirs, ss, rs, k, (px, py, c))))
        return out
    return _Task([], bufs, [], 3 * len(specs), copies)


def _gather_near_task(bufs, specs):
    def copies(ro, rw, new, ss, rs):
        x, y, c = _place()
        me = 2 * x + y
        out = []
        for t, spec in enumerate(specs):
            for p, (px, py) in enumerate(_chip_peers(x, y)[:2]):
                mine = spec.in_full(rw[t], me, c)
                theirs = spec.in_full(rw[t], 2 * px + py, c)
                k = t * 2 + p
                out.append((_remote(mine, mine, ss, rs, k, (px, py, c)), _remote(mine, mine, ss, rs, k, (px, py, c)),
                            _remote(theirs, theirs, ss, rs, k, (px, py, c))))
        return out
    return _Task([], bufs, [], 2 * len(specs), copies)


def _gather_relay_task(bufs, specs):
    def copies(ro, rw, new, ss, rs):
        x, y, c = _place()
        (xp, yp, dg) = _chip_peers(x, y)
        chip = lambda p: 2 * p[0] + p[1]
        out = []
        for t, spec in enumerate(specs):
            q = spec.half_rows // 2
            for p, (frm, to, lo) in enumerate(((yp, xp, 0), (xp, yp, q))):
                sent = spec.in_full_rows(rw[t], chip(frm), c, lo, q)
                got = spec.in_full_rows(rw[t], chip(dg), c, lo, q)
                k = t * 2 + p
                dev = (to[0], to[1], c)
                out.append((_remote(sent, sent, ss, rs, k, dev), _remote(sent, sent, ss, rs, k, dev),
                            _remote(got, got, ss, rs, k, dev)))
        return out
    return _Task([], bufs, [], 2 * len(specs), copies)


def _gather_d2d_task(bufs, specs):
    def copies(ro, rw, new, ss, rs):
        x, y, c = _place()
        sib = (x, y, 1 - c)
        out = []
        for t, spec in enumerate(specs):
            for p, (px, py) in enumerate(_chip_peers(x, y)):
                got = spec.in_full(rw[t], 2 * px + py, c)
                gets = spec.in_full(rw[t], 2 * px + py, 1 - c)
                k = t * 3 + p
                out.append((_remote(got, got, ss, rs, k, sib), _remote(got, got, ss, rs, k, sib),
                            _remote(gets, gets, ss, rs, k, sib)))
        return out
    return _Task([], bufs, [], 3 * len(specs), copies)


def _pair_task(grads, specs):
    def copies(ro, rw, new, ss, rs):
        x, y, c = _place()
        sib = (x, y, 1 - c)
        out = []
        for t, spec in enumerate(specs):
            for j in range(N_CHIPS):
                k = t * N_CHIPS + j
                cp = _remote(spec.in_full(ro[t], j, 1 - c), new[t].at[j], ss, rs, k, sib)
                out.append((cp, cp, cp))
        return out
    return _Task(grads, [], [jax.ShapeDtypeStruct((N_CHIPS,) + s.half, BF16) for s in specs], N_CHIPS * len(specs), copies)


def _chip_task(pair_sums, specs):
    def copies(ro, rw, new, ss, rs):
        x, y, c = _place()
        out = []
        for t in range(len(specs)):
            for p, (px, py) in enumerate(_chip_peers(x, y)):
                cp = _remote(ro[t].at[2 * px + py], new[t].at[p], ss, rs, t * 3 + p, (px, py, c))
                out.append((cp, cp, cp))
        return out
    return _Task(pair_sums, [], [jax.ShapeDtypeStruct((3,) + s.half, BF16) for s in specs], 3 * len(specs), copies)


def _chip_rows_task(pair_sum, land, spec, lo, hi):
    def copies(ro, rw, new, ss, rs):
        x, y, c = _place()
        dst = (rw or new)[0]
        out = []
        for p, (px, py) in enumerate(_chip_peers(x, y)):
            cp = _remote(ro[0].at[2 * px + py, :, pl.ds(lo, hi - lo), :], dst.at[p, :, pl.ds(lo, hi - lo), :],
                         ss, rs, p, (px, py, c))
            out.append((cp, cp, cp))
        return out
    first = land is None
    return _Task([pair_sum], [] if first else [land], [jax.ShapeDtypeStruct((3,) + spec.half, BF16)] if first else [],
                 3, copies)


def _half_task(shards, specs):
    def copies(ro, rw, new, ss, rs):
        x, y, c = _place()
        sib = (x, y, 1 - c)
        out = []
        for t, spec in enumerate(specs):
            mine, theirs = spec.in_shard(rw[t], c), spec.in_shard(rw[t], 1 - c)
            out.append((_remote(mine, mine, ss, rs, t, sib), _remote(mine, mine, ss, rs, t, sib),
                        _remote(theirs, theirs, ss, rs, t, sib)))
        return out
    return _Task([], shards, [], len(specs), copies)


def _call(body, *, name, grid, in_specs, out_specs, out_shape, args, scratch_shapes=(), tasks=(), carry=()):
    tasks = [t for t in tasks if t is not None]
    carry = list(carry)
    n_in, n_out, n_scr, n_carry = len(in_specs), len(out_specs), len(scratch_shapes), len(carry)
    ro = [a for t in tasks for a in t.ro]
    rw = [a for t in tasks for a in t.rw]
    new = [s for t in tasks for s in t.new]
    n_ro, n_rw, n_new = len(ro), len(rw), len(new)

    def wrapped(*refs):
        ins = refs[:n_in]
        ro_refs = refs[n_in + n_carry:n_in + n_carry + n_ro]
        at = n_in + n_carry + n_ro + n_rw
        outs = refs[at:at + n_out + n_carry]
        at = at + n_out + n_carry
        rw_refs = refs[at:at + n_rw]
        new_refs = refs[at + n_rw:at + n_rw + n_new]
        at = at + n_rw + n_new
        scr = refs[at:at + n_scr]
        sems = refs[at + n_scr:]

        def task_copies():
            found, a, b, d = [], 0, 0, 0
            for i, t in enumerate(tasks):
                found += t.copies(ro_refs[a:a + len(t.ro)], rw_refs[b:b + len(t.rw)], new_refs[d:d + len(t.new)],
                                  sems[2 * i], sems[2 * i + 1])
                a, b, d = a + len(t.ro), b + len(t.rw), d + len(t.new)
            return found

        if tasks:
            first = pl.program_id(0) == 0
            last = pl.program_id(0) == grid[0] - 1
            for ax in range(1, len(grid)):
                first = first & (pl.program_id(ax) == 0)
                last = last & (pl.program_id(ax) == grid[ax] - 1)

            @pl.when(first)
            def _():
                for cp, _, _ in task_copies():
                    cp().start()

        body(*ins, *outs, *scr)

        if tasks:
            @pl.when(last)
            def _():
                found = task_copies()
                for _, _, recv in found:
                    recv().wait_recv()
                for _, send, _ in found:
                    send().wait_send()

    sem_shapes = []
    for t in tasks:
        sem_shapes += [pltpu.SemaphoreType.DMA((t.n_sem,)), pltpu.SemaphoreType.DMA((t.n_sem,))]
    aliases = {n_in + i: n_out + i for i in range(n_carry)}
    aliases.update({n_in + n_carry + n_ro + i: n_out + n_carry + i for i in range(n_rw)})
    res = pl.pallas_call(
        wrapped, name=name, grid=grid,
        in_specs=list(in_specs) + [HBM_ANY] * (n_carry + n_ro + n_rw),
        out_specs=list(out_specs) + [HBM_ANY] * (n_carry + n_rw + n_new),
        out_shape=list(out_shape) + [jax.ShapeDtypeStruct(a.shape, a.dtype) for a in carry + rw] + new,
        scratch_shapes=list(scratch_shapes) + sem_shapes,
        input_output_aliases=aliases,
        compiler_params=pltpu.CompilerParams(dimension_semantics=("arbitrary",) * len(grid),
                                             vmem_limit_bytes=VMEM_LIMIT_V7X),
    )(*args, *carry, *ro, *rw)
    own, rest = list(res[:n_out + n_carry]), list(res[n_out + n_carry:])
    per_task, b, d = [], 0, n_rw
    for t in tasks:
        per_task.append((rest[b:b + len(t.rw)], rest[d:d + len(t.new)]))
        b, d = b + len(t.rw), d + len(t.new)
    return own, per_task


def _comm_only(name, tasks):
    ro = [a for t in tasks for a in t.ro]
    rw = [a for t in tasks for a in t.rw]
    new = [s for t in tasks for s in t.new]
    n_ro, n_rw, n_new = len(ro), len(rw), len(new)

    def body(*refs):
        ro_refs = refs[:n_ro]
        rw_refs = refs[n_ro + n_rw:n_ro + 2 * n_rw]
        new_refs = refs[n_ro + 2 * n_rw:n_ro + 2 * n_rw + n_new]
        sems = refs[n_ro + 2 * n_rw + n_new:]
        found, a, b, d = [], 0, 0, 0
        for i, t in enumerate(tasks):
            found += t.copies(ro_refs[a:a + len(t.ro)], rw_refs[b:b + len(t.rw)], new_refs[d:d + len(t.new)],
                              sems[2 * i], sems[2 * i + 1])
            a, b, d = a + len(t.ro), b + len(t.rw), d + len(t.new)
        for cp, _, _ in found:
            cp().start()
        for _, _, recv in found:
            recv().wait_recv()
        for _, send, _ in found:
            send().wait_send()

    sem_shapes = []
    for t in tasks:
        sem_shapes += [pltpu.SemaphoreType.DMA((t.n_sem,)), pltpu.SemaphoreType.DMA((t.n_sem,))]
    res = pl.pallas_call(
        body, name=name,
        in_specs=[HBM_ANY] * (n_ro + n_rw), out_specs=[HBM_ANY] * (n_rw + n_new),
        out_shape=[jax.ShapeDtypeStruct(a.shape, a.dtype) for a in rw] + new,
        scratch_shapes=sem_shapes,
        input_output_aliases={n_ro + i: i for i in range(n_rw)},
    )(*ro, *rw)
    res = list(res)
    per_task, b, d = [], 0, n_rw
    for t in tasks:
        per_task.append((res[b:b + len(t.rw)], res[d:d + len(t.new)]))
        b, d = b + len(t.rw), d + len(t.new)
    return per_task


def _inproj(x, norm_g, w_in, dm, tasks=()):
    S, D, N = dm.S, dm.D, dm.INW
    tm, tn = min(1024, S), 512

    def body(x_ref, g_ref, w_ref, z_ref, h_ref):
        @pl.when(pl.program_id(1) == 0)
        def _():
            xv = x_ref[...]
            r = lax.rsqrt(jnp.mean(xv * xv, axis=-1, keepdims=True) + RMS_EPS)
            h_ref[...] = _bf(xv * r * g_ref[...])

        z_ref[...] = _bf(_dot(h_ref[...], w_ref[...]))

    return _call(
        body, name="inproj", grid=(S // tm, N // tn),
        in_specs=[pl.BlockSpec((tm, D), lambda i, j: (i, 0)),
                  pl.BlockSpec((1, D), lambda i, j: (0, 0)),
                  pl.BlockSpec((None, D, tn), lambda i, j: (0, 0, j))],
        out_specs=[pl.BlockSpec((tm, tn), lambda i, j: (i, j)),
                   pl.BlockSpec((tm, D), lambda i, j: (i, 0))],
        out_shape=[jax.ShapeDtypeStruct((S, N), BF16), jax.ShapeDtypeStruct((S, D), BF16)],
        args=(x, norm_g, w_in), tasks=tasks)


def _rot_half(x, cs, sn):
    h = cs.shape[-1]
    x1, x2 = x[:, :h], x[:, h:]
    return jnp.concatenate([x1 * cs - x2 * sn, x2 * cs + x1 * sn], axis=-1)


def _ret_tables(af_ref, ab_ref):
    C = CHUNK
    lgf = -jnp.exp(af_ref[...])[:, :1]
    lgb = -jnp.exp(ab_ref[...])[:, :1]
    ri = lax.broadcasted_iota(jnp.int32, (C, C), 0)
    ci = lax.broadcasted_iota(jnp.int32, (C, C), 1)
    lag = _f32(ri - ci)
    alag = jnp.abs(lag)
    low = lag >= 0
    dmask = jnp.where(low, jnp.exp(lgf * alag), jnp.exp(lgb * alag))
    j = _f32(lax.broadcasted_iota(jnp.int32, (C, 1), 0))
    return dict(lgf=lgf, lgb=lgb, alag=alag, low=low, dmask=dmask, j=j,
                w_f=jnp.exp(lgf * (C - 1.0 - j)), w_b=jnp.exp(lgb * j),
                q_f=jnp.exp(lgf * (j + 1.0)), q_b=jnp.exp(lgb * (C - j)),
                dec_f=jnp.exp(lgf * C), dec_b=jnp.exp(lgb * C))


def _decay_rows(a):
    return jnp.broadcast_to(a.reshape(RET_HEADS, 1, 1), (RET_HEADS, 1, 128))


def _heads(x, width):
    return jnp.stack([x[:, b * width:(b + 1) * width] for b in range(x.shape[1] // width)])


def _bdot(a, b):
    return lax.dot_general(a, b, (((2,), (1,)), ((0,), (0,))), preferred_element_type=F32)


def _bdot_nt(a, b):
    return lax.dot_general(a, b, (((2,), (2,)), ((0,), (0,))), preferred_element_type=F32)


def _bdot_tn(a, b):
    return lax.dot_general(a, b, (((1,), (1,)), ((0,), (0,))), preferred_element_type=F32)


def _ret_tables_heads(af_ref, ab_ref):
    C = CHUNK
    lgf = -jnp.exp(af_ref[...])[:, :, :1]
    lgb = -jnp.exp(ab_ref[...])[:, :, :1]
    ri = lax.broadcasted_iota(jnp.int32, (1, C, C), 1)
    ci = lax.broadcasted_iota(jnp.int32, (1, C, C), 2)
    lag = _f32(ri - ci)
    alag = jnp.abs(lag)
    low = lag >= 0
    dmask = jnp.where(low, jnp.exp(lgf * alag), jnp.exp(lgb * alag))
    j = _f32(lax.broadcasted_iota(jnp.int32, (1, C, 1), 1))
    return dict(lgf=lgf, lgb=lgb, alag=alag, low=low, dmask=dmask, j=j,
                w_f=jnp.exp(lgf * (C - 1.0 - j)), w_b=jnp.exp(lgb * j),
                q_f=jnp.exp(lgf * (j + 1.0)), q_b=jnp.exp(lgb * (C - j)),
                dec_f=jnp.exp(lgf * C), dec_b=jnp.exp(lgb * C))


def _rot_half_heads(x, cs, sn):
    h = cs.shape[-1]
    x1, x2 = x[..., :h], x[..., h:]
    return jnp.concatenate([x1 * cs - x2 * sn, x2 * cs + x1 * sn], axis=-1)


def _ret_fwd(z, a_f, a_b, cosr, sinr, dm):
    S, RD, C, nC, HB = dm.S, dm.RD, CHUNK, dm.nC, RET_HEADS_PER_STEP
    W = HB * RD
    scale = RD ** -0.5

    def body(q_ref, k_ref, v_ref, g_ref, af_ref, ab_ref, cos_ref, sin_ref, o_ref, qh_s, kh_s, sf_s, st_s):
        t = _ret_tables_heads(af_ref, ab_ref)
        st_s[...] = jnp.zeros_like(st_s)

        def fwd_pass(n, carry):
            r = pl.ds(pl.multiple_of(n * C, C), C)
            cs, sn = cos_ref[r, :], sin_ref[r, :]
            qh = _rot_half_heads(_heads(_f32(q_ref[r, :]), RD), cs, sn)
            kh = _rot_half_heads(_heads(_f32(k_ref[r, :]), RD), cs, sn) * scale
            qh_s[:, r, :] = _bf(qh)
            kh_s[:, r, :] = _bf(kh)
            st = st_s[...]
            sf_s[n] = _bf(st)
            st_s[...] = st * t["dec_f"] + _bdot_tn(_bf(kh * t["w_f"]), _heads(v_ref[r, :], RD))
            return carry

        _loop(nC, fwd_pass, 0)
        st_s[...] = jnp.zeros_like(st_s)

        def bwd_pass(i, carry):
            n = nC - 1 - i
            r = pl.ds(pl.multiple_of(n * C, C), C)
            qhb, khb, vb = qh_s[:, r, :], kh_s[:, r, :], _heads(v_ref[r, :], RD)
            p = _bf(_bdot_nt(qhb, khb) * t["dmask"])
            qhf = _f32(qhb)
            sb = st_s[...]
            out = (_bdot(p, vb) + _bdot(_bf(qhf * t["q_f"]), sf_s[n]) + _bdot(_bf(qhf * t["q_b"]), _bf(sb)))
            y = out * lax.rsqrt(jnp.mean(out * out, axis=-1, keepdims=True) + RMS_EPS)
            gate = _silu(_f32(g_ref[r, :]))
            for b in range(HB):
                o_ref[r, b * RD:(b + 1) * RD] = _bf(y[b] * gate[:, b * RD:(b + 1) * RD])
            st_s[...] = sb * t["dec_b"] + _bdot_tn(_bf(_f32(khb) * t["w_b"]), vb)
            return carry

        _loop(nC, bwd_pass, 0)

    zs = lambda off: pl.BlockSpec((S, W), lambda h: (0, off // W + h), pipeline_mode=ONE_BUFFER)
    dec = pl.BlockSpec((HB, 1, 128), lambda h: (h, 0, 0))
    tab = pl.BlockSpec((S, RD // 2), lambda h: (0, 0), pipeline_mode=ONE_BUFFER)
    own, _ = _call(
        body, name="ret_fwd", grid=(RET_HEADS // HB,),
        in_specs=[zs(dm.o_rq), zs(dm.o_rk), zs(dm.o_rv), zs(dm.o_rg), dec, dec, tab, tab],
        out_specs=[pl.BlockSpec((S, W), lambda h: (0, h))],
        out_shape=[jax.ShapeDtypeStruct((S, dm.BW), BF16)],
        scratch_shapes=[pltpu.VMEM((HB, S, RD), BF16), pltpu.VMEM((HB, S, RD), BF16),
                        pltpu.VMEM((nC, HB, RD, RD), BF16), pltpu.VMEM((HB, RD, RD), F32)],
        args=(z, z, z, z, _decay_rows(a_f), _decay_rows(a_b), cosr, sinr))
    return own[0]


def _band(first_row, first_col, lo, hi, shape):
    r = lax.broadcasted_iota(jnp.int32, shape, 0) + first_row
    c = lax.broadcasted_iota(jnp.int32, shape, 1) + first_col
    d = c - r
    return jnp.where((d >= lo) & (d <= hi), 1.0, 0.0).astype(BF16)


def _pool_counts(first_row, half, S, rows):
    pos = lax.broadcasted_iota(jnp.int32, (rows, 1), 0) + first_row
    lo = jnp.clip(pos - half, 0, S)
    hi = jnp.clip(pos + half, 0, S)
    return 1.0 / _f32(hi - lo)


def _pool_rows(S):
    return min(4 * CHUNK, S // 2)


def _pool_win_start(n, rows, S):
    return pl.multiple_of(jnp.clip(n * rows - CHUNK, 0, S - (rows + 2 * CHUNK)), CHUNK)


def _win_start(n, S):
    return pl.multiple_of(jnp.clip((n - 1) * CHUNK, 0, S - 3 * CHUNK), CHUNK)


def _pool_fwd(z, pool_w, pool_scale, dm):
    S, PD = dm.S, dm.PD
    PB = _pool_rows(S)
    WIN = PB + 2 * CHUNK

    def body(u_ref, g_ref, w_ref, sc_ref, o_ref):
        half = jnp.left_shift(1, pl.program_id(0))
        w = w_ref[...]
        sc = sc_ref[...]

        def blk(n, carry):
            row0 = pl.multiple_of(n * PB, PB)
            r = pl.ds(row0, PB)
            st = _pool_win_start(n, PB, S)
            band = _band(row0, st, -half, half - 1, (PB, WIN))
            mean = _dot(band, u_ref[pl.ds(st, WIN), :]) * _pool_counts(row0, half, S, PB)
            p = mean - _f32(u_ref[r, :])
            y = _dot(_bf(p), w) * sc
            o_ref[r, :] = _bf(y * _silu(_f32(g_ref[r, :])))
            return carry

        _loop(S // PB, blk, 0)

    own, _ = _call(
        body, name="pool_fwd", grid=(POOL_GROUPS,),
        in_specs=[pl.BlockSpec((S, PD), lambda g: (0, dm.o_pv // PD + g)),
                  pl.BlockSpec((S, PD), lambda g: (0, dm.o_pg // PD + g)),
                  pl.BlockSpec((None, PD, PD), lambda g: (g, 0, 0)),
                  pl.BlockSpec((None, 1, PD), lambda g: (g, 0, 0))],
        out_specs=[pl.BlockSpec((S, PD), lambda g: (0, g))],
        out_shape=[jax.ShapeDtypeStruct((S, dm.BW), BF16)],
        args=(z, z, pool_w, pool_scale))
    return own[0]


def _rot_part(x, cs, s_up, s_dn):
    h = ROPE_DIMS // 2
    lanes = x.ndim - 1
    return x * cs + pltpu.roll(x, ATT_HEAD_DIM - h, lanes) * s_up + pltpu.roll(x, h, lanes) * s_dn


def _att_tables(S):
    h = ROPE_DIMS // 2
    inv = ROPE_THETA ** (-jnp.arange(h, dtype=F32) / h)
    ang = jnp.arange(S, dtype=F32)[:, None] * inv[None, :]
    cos, sin = jnp.cos(ang), jnp.sin(ang)
    pad = jnp.zeros((S, ATT_HEAD_DIM - 2 * h), F32)
    zero = jnp.zeros((S, h), F32)
    cs = jnp.concatenate([cos, cos, pad + 1.0], axis=1)
    s_up = jnp.concatenate([-sin, zero, pad], axis=1)
    s_dn = jnp.concatenate([zero, sin, pad], axis=1)
    return cs, s_up, s_dn


def _ret_rope_tables(S, RD):
    h = RD // 2
    inv = 1.0 / (RET_ROPE_BASE ** jnp.linspace(0.0, 1.0, h, dtype=F32))
    ang = jnp.arange(S, dtype=F32)[:, None] * inv[None, :]
    return jnp.cos(ang), jnp.sin(ang)


def _att_norm_rot(x, gain, cs, s_up, s_dn):
    u = x * lax.rsqrt(jnp.mean(x * x, axis=-1, keepdims=True) + RMS_EPS)
    return _rot_part(u * gain, cs, s_up, s_dn)


def _att_bias_tables(bias_s):
    C = CHUNK
    shape = (ATT_GROUP * C, 3 * C)
    rel = lax.broadcasted_iota(jnp.int32, shape, 1) - lax.broadcasted_iota(jnp.int32, shape, 0) % C
    for which, shift in enumerate((-C, 0, -2 * C)):
        bias_s[which] = jnp.where(jnp.abs(rel + shift) <= CHUNK, 0.0, NEG_BIG)


def _att_bias(bias_s, n, nC):
    return bias_s[jnp.where(n == 0, 1, jnp.where(n == nC - 1, 2, 0))]


def _att_probs(q4, kw, sink_col, bias):
    s = _dot_nt(q4, kw) * (ATT_HEAD_DIM ** -0.5) + bias
    m = jnp.maximum(jnp.max(s, axis=-1, keepdims=True), sink_col)
    e = jnp.exp(s - m)
    es = jnp.exp(sink_col - m)
    inv = 1.0 / (jnp.sum(e, axis=-1, keepdims=True) + es)
    return e * inv, es * inv


def _sink_col(sink_ref):
    head = lax.broadcasted_iota(jnp.int32, (ATT_GROUP * CHUNK, 1), 0) // CHUNK
    col = jnp.zeros((ATT_GROUP * CHUNK, 1), F32)
    for g in range(ATT_GROUP):
        col = jnp.where(head == g, sink_ref[g:g + 1, :1], col)
    return col


def _sink_rows(sink, dm):
    return jnp.broadcast_to(sink.reshape(dm.AKV, ATT_GROUP, 1), (dm.AKV, ATT_GROUP, 128))


def _att_prep(q_ref, k_ref, qg_ref, kg_ref, cs_ref, su_ref, sd_ref, qn_s, kn_s, nC):
    C, HD = CHUNK, ATT_HEAD_DIM

    def prep(n, carry):
        r = pl.ds(pl.multiple_of(n * C, C), C)
        cs, su, sd = cs_ref[r, :], su_ref[r, :], sd_ref[r, :]
        kn_s[r, :] = _bf(_att_norm_rot(_f32(k_ref[r, :]), kg_ref[...], cs, su, sd))
        qn_s[:, r, :] = _bf(_att_norm_rot(_heads(_f32(q_ref[r, :]), HD), qg_ref[...], cs, su, sd))
        return carry

    lax.fori_loop(0, nC, prep, 0)


def _att_fwd(z, q_gain, k_gain, sink, tabs, dm, tasks=()):
    S, C, nC, HD, G = dm.S, CHUNK, dm.nC, ATT_HEAD_DIM, ATT_GROUP
    GW = G * HD

    KV = dm.AKV

    def body(q_ref, k_ref, v_ref, g_ref, qg_ref, kg_ref, sk_ref, cs_ref, su_ref, sd_ref, o_ref, qn_s, kn_s, bias_s):
        def prep(n, carry):
            r = pl.ds(pl.multiple_of(n * C, C), C)
            cs, su, sd = cs_ref[r, :], su_ref[r, :], sd_ref[r, :]
            kn_s[:, r, :] = _bf(_att_norm_rot(_heads(_f32(k_ref[r, :]), HD), kg_ref[...], cs, su, sd))
            qn_s[:, r, :] = _bf(_att_norm_rot(_heads(_f32(q_ref[r, :]), HD), qg_ref[...], cs, su, sd))
            return carry

        lax.fori_loop(0, nC, prep, 0)
        _att_bias_tables(bias_s)
        sink_col = jnp.stack([_sink_col(sk_ref.at[h]) for h in range(KV)])

        def blk(n, carry):
            row0 = pl.multiple_of(n * C, C)
            r = pl.ds(row0, C)
            w = pl.ds(_win_start(n, S), 3 * C)
            q4 = qn_s[:, r, :].reshape(KV, G * C, HD)
            s = _bdot_nt(q4, kn_s[:, w, :]) * (HD ** -0.5) + _att_bias(bias_s, n, nC)
            m = jnp.maximum(jnp.max(s, axis=-1, keepdims=True), sink_col)
            e = jnp.exp(s - m)
            inv = 1.0 / (jnp.sum(e, axis=-1, keepdims=True) + jnp.exp(sink_col - m))
            o = _bdot(_bf(e * inv), _heads(v_ref[w, :], HD))
            gate = _silu(_f32(g_ref[r, :]))
            for h in range(KV):
                for g in range(G):
                    cols = slice((h * G + g) * HD, (h * G + g + 1) * HD)
                    o_ref[r, cols] = _bf(o[h, g * C:(g + 1) * C, :] * gate[:, cols])
            return carry

        _loop(nC, blk, 0)

    tab = pl.BlockSpec((S, HD), lambda h: (0, 0), pipeline_mode=ONE_BUFFER)
    gain = pl.BlockSpec((1, HD), lambda h: (0, 0))
    one = lambda off, width: pl.BlockSpec((pl.Element(S), pl.Element(width)), lambda h: (0, off),
                                          pipeline_mode=ONE_BUFFER)
    return _call(
        body, name="att_fwd", grid=(1,),
        in_specs=[one(dm.o_aq, dm.BW), one(dm.o_ak, dm.AKW), one(dm.o_av, dm.AKW), one(dm.o_ag, dm.BW),
                  gain, gain, pl.BlockSpec((KV, G, 128), lambda h: (0, 0, 0)), tab, tab, tab],
        out_specs=[pl.BlockSpec((S, dm.BW), lambda h: (0, 0), pipeline_mode=ONE_BUFFER)],
        out_shape=[jax.ShapeDtypeStruct((S, dm.BW), BF16)],
        scratch_shapes=[pltpu.VMEM((KV * G, S, HD), BF16), pltpu.VMEM((KV, S, HD), BF16),
                        pltpu.VMEM((3, G * C, 3 * C), F32)],
        args=(z, z, z, z, q_gain, k_gain, _sink_rows(sink, dm), *tabs), tasks=tasks)


def _merge_gate_spec(tm, dm):
    return pl.BlockSpec((pl.Element(tm), pl.Element(N_BRANCHES * dm.D)), lambda i: (i * tm, dm.o_mg))


def _merge_out(x, mg, ya_pre, yb_pre, yc_pre, w_ret, w_pool, w_att, w_out, dm, tasks=()):
    S, D, BW = dm.S, dm.D, dm.BW
    tm = min(256, S)

    def body(x_ref, mg_ref, a_ref, b_ref, c_ref, wr_ref, wp_ref, wa_ref, wo_ref, xo_ref, ya_ref, yb_ref, yc_ref, m_ref):
        ya = _dot(a_ref[...], wr_ref[...])
        yb = _dot(b_ref[...], wp_ref[...])
        yc = _dot(c_ref[...], wa_ref[...])
        ya_ref[...], yb_ref[...], yc_ref[...] = _bf(ya), _bf(yb), _bf(yc)
        g0 = _sigmoid(_f32(mg_ref[:, 0:D]))
        g1 = _sigmoid(_f32(mg_ref[:, D:2 * D]))
        g2 = _sigmoid(_f32(mg_ref[:, 2 * D:3 * D]))
        merged = _bf(g0 * ya + g1 * yb + g2 * yc)
        m_ref[...] = merged
        xo_ref[...] = x_ref[...] + _dot(merged, wo_ref[...])

    act = lambda w: pl.BlockSpec((tm, w), lambda i: (i, 0))
    wsp = lambda r: pl.BlockSpec((None, r, D), lambda i: (0, 0, 0), pipeline_mode=ONE_BUFFER)
    return _call(
        body, name="merge_out", grid=(S // tm,),
        in_specs=[act(D), _merge_gate_spec(tm, dm), act(BW), act(BW), act(BW), wsp(BW), wsp(BW), wsp(BW), wsp(D)],
        out_specs=[act(D)] * 5,
        out_shape=[jax.ShapeDtypeStruct((S, D), F32)] + [jax.ShapeDtypeStruct((S, D), BF16)] * 4,
        args=(x, mg, ya_pre, yb_pre, yc_pre, w_ret, w_pool, w_att, w_out), tasks=tasks)


def _loss_grad(y, target, dm):
    S, D = dm.S, dm.D
    tm = dm.tm

    def body(y_ref, t_ref, s_ref, d_ref):
        @pl.when(pl.program_id(0) == 0)
        def _():
            s_ref[...] = jnp.zeros_like(s_ref)

        e = y_ref[...] - t_ref[...]
        d_ref[...] = e * (1.0 / D)
        s_ref[...] += jnp.sum(jnp.sum(e * e, axis=-1, keepdims=True), axis=0, keepdims=True)

    row = pl.BlockSpec((tm, D), lambda i: (i, 0))
    own, _ = _call(
        body, name="loss_grad", grid=(S // tm,), in_specs=[row, row],
        out_specs=[pl.BlockSpec((1, 128), lambda i: (0, 0)), row],
        out_shape=[jax.ShapeDtypeStruct((1, 128), F32), jax.ShapeDtypeStruct((S, D), F32)],
        args=(y, target))
    return own


def _merge_bwd_gates(dx, mg, ya, yb, yc, w_out, dm, tasks=()):
    S, D = dm.S, dm.D
    tm = min(256, S)

    def body(dx_ref, mg_ref, ya_ref, yb_ref, yc_ref, wo_ref, dmg_ref, dya_ref, dyb_ref, dyc_ref, dxb_ref):
        dxb = _bf(dx_ref[...])
        dxb_ref[...] = dxb
        dm_ = _dot_nt(dxb, wo_ref[...])
        for k, (y_ref, dy_ref) in enumerate(((ya_ref, dya_ref), (yb_ref, dyb_ref), (yc_ref, dyc_ref))):
            g = _sigmoid(_f32(mg_ref[:, k * D:(k + 1) * D]))
            dmg_ref[:, k * D:(k + 1) * D] = _bf(dm_ * _f32(y_ref[...]) * g * (1.0 - g))
            dy_ref[...] = _bf(dm_ * g)

    act = lambda w: pl.BlockSpec((tm, w), lambda i: (i, 0))
    return _call(
        body, name="merge_bwd_gates", grid=(S // tm,),
        in_specs=[act(D), _merge_gate_spec(tm, dm), act(D), act(D), act(D),
                  pl.BlockSpec((None, D, D), lambda i: (0, 0, 0), pipeline_mode=ONE_BUFFER)],
        out_specs=[_merge_gate_spec(tm, dm), act(D), act(D), act(D), act(D)],
        out_shape=[jax.ShapeDtypeStruct((S, dm.INW), BF16)] + [jax.ShapeDtypeStruct((S, D), BF16)] * 4,
        args=(dx, mg, ya, yb, yc, w_out), tasks=tasks)


def _merge_bwd_proj(dya, dyb, dyc, w_ret, w_pool, w_att, dm):
    S, D, BW = dm.S, dm.D, dm.BW
    tm = dm.tm

    def body(da_ref, db_ref, dc_ref, wr_ref, wp_ref, wa_ref, oa_ref, ob_ref, oc_ref):
        oa_ref[...] = _bf(_dot_nt(da_ref[...], wr_ref[...]))
        ob_ref[...] = _bf(_dot_nt(db_ref[...], wp_ref[...]))
        oc_ref[...] = _bf(_dot_nt(dc_ref[...], wa_ref[...]))

    act = lambda w: pl.BlockSpec((tm, w), lambda i: (i, 0))
    wsp = pl.BlockSpec((None, BW, D), lambda i: (0, 0, 0), pipeline_mode=ONE_BUFFER)
    own, _ = _call(
        body, name="merge_bwd_proj", grid=(S // tm,),
        in_specs=[act(D)] * 3 + [wsp] * 3, out_specs=[act(BW)] * 3,
        out_shape=[jax.ShapeDtypeStruct((S, BW), BF16)] * 3,
        args=(dya, dyb, dyc, w_ret, w_pool, w_att))
    return own


def _col_writes(dz_ref, sem, step, parts):
    return [pltpu.make_async_copy(src, dz_ref.at[:, pl.ds(pl.multiple_of(off + step * w, 128), w)], sem.at[k])
            for k, (src, off, w) in enumerate(parts)]


def _ret_bwd(z, dz, dya_pre, a_f, a_b, cosr, sinr, dm, tasks=()):
    S, RD, C, nC = dm.S, dm.RD, CHUNK, dm.nC
    scale = RD ** -0.5
    H = RET_HEADS

    def body(q_ref, k_ref, v_ref, g_ref, dy_ref, af_ref, ab_ref, cos_ref, sin_ref, dd_ref, dz_ref,
             qh_s, kh_s, sf_s, sb_s, do_s, dqh_s, dkh_s, dv_s, st_s, lam_s, dq_ref, dk_ref, dv_ref, dg_ref, wsem,
             acc_dd, acc_tf, acc_tb, acc_uf, acc_ub, acc_lf, acc_lb):
        head = pl.program_id(0)
        writes = lambda step: _col_writes(dz_ref, wsem, step, [(dq_ref, dm.o_rq, RD), (dk_ref, dm.o_rk, RD),
                                                               (dv_ref, dm.o_rv, RD), (dg_ref, dm.o_rg, RD)])
        t = _ret_tables(af_ref, ab_ref)
        zero_rr = jnp.zeros((RD, RD), F32)

        st_s[...] = zero_rr

        def pass0(n, carry):
            r = pl.ds(pl.multiple_of(n * C, C), C)
            cs, sn = cos_ref[r, :], sin_ref[r, :]
            qh = _rot_half(_f32(q_ref[r, :]), cs, sn)
            kh = _rot_half(_f32(k_ref[r, :]), cs, sn) * scale
            qh_s[r, :] = _bf(qh)
            kh_s[r, :] = _bf(kh)
            st = st_s[...]
            sf_s[n] = _bf(st)
            st_s[...] = st * t["dec_f"] + _dot_tn(_bf(kh * t["w_f"]), v_ref[r, :])
            return carry

        _loop(nC, pass0, 0)

        @pl.when(head > 0)
        def _():
            for cp in writes(head - 1):
                cp.wait()

        st_s[...] = zero_rr
        lam_s[...] = zero_rr

        def pass1(i, carry):
            n = nC - 1 - i
            r = pl.ds(pl.multiple_of(n * C, C), C)
            qhb, khb, vb = qh_s[r, :], kh_s[r, :], v_ref[r, :]
            qhf, khf = _f32(qhb), _f32(khb)
            sc_ = _dot_nt(qhb, khb)
            p = _bf(sc_ * t["dmask"])
            sb = st_s[...]
            sbb = _bf(sb)
            sb_s[n] = sbb
            st_s[...] = sb * t["dec_b"] + _dot_tn(_bf(khf * t["w_b"]), vb)
            sfb = sf_s[n]
            qf_b, qb_b = _bf(qhf * t["q_f"]), _bf(qhf * t["q_b"])
            out = _dot(p, vb) + _dot(qf_b, sfb) + _dot(qb_b, sbb)
            rr = lax.rsqrt(jnp.mean(out * out, axis=-1, keepdims=True) + RMS_EPS)
            y = out * rr
            g = _f32(g_ref[r, :])
            dya = _f32(dy_ref[r, :])
            dg_ref[r, :] = _bf(dya * y * _dsilu(g))
            dyn = dya * _silu(g)
            dout = rr * (dyn - y * jnp.mean(dyn * y, axis=-1, keepdims=True))
            dob = _bf(dout)
            do_s[r, :] = dob
            dp = _dot_nt(dob, vb)
            dv = _dot_tn(p, dob)
            ds = _bf(dp * t["dmask"])
            dqh = _dot(ds, khb)
            dkh = _dot_tn(ds, qhb)
            acc_dd[...] += dp * sc_
            tf = _dot_nt(dob, sfb)
            tb = _dot_nt(dob, sbb)
            dqh = dqh + tf * t["q_f"] + tb * t["q_b"]
            acc_tf[...] += tf * qhf
            acc_tb[...] += tb * qhf
            lam = lam_s[...]
            lamb = _bf(lam)
            acc_lf[...] += lam * _f32(sfb)
            u = _dot_nt(vb, lamb)
            dkh = dkh + u * t["w_f"]
            acc_uf[...] += u * khf
            dv = dv + _dot(_bf(khf * t["w_f"]), lamb)
            lam_s[...] = _dot_tn(qf_b, dob) + lam * t["dec_f"]
            dqh_s[r, :] = dqh
            dkh_s[r, :] = dkh
            dv_s[r, :] = dv
            return carry

        for acc in (acc_dd, acc_tf, acc_tb, acc_uf, acc_ub, acc_lf, acc_lb):
            acc[...] = jnp.zeros_like(acc)
        _loop(nC, pass1, 0)

        lam_s[...] = zero_rr

        def pass2(n, carry):
            r = pl.ds(pl.multiple_of(n * C, C), C)
            qhb, khb, vb, dob = qh_s[r, :], kh_s[r, :], v_ref[r, :], do_s[r, :]
            qhf, khf = _f32(qhb), _f32(khb)
            lam = lam_s[...]
            lamb = _bf(lam)
            acc_lb[...] += lam * _f32(sb_s[n])
            u = _dot_nt(vb, lamb)
            dkh = dkh_s[r, :] + u * t["w_b"]
            acc_ub[...] += u * khf
            dv = dv_s[r, :] + _dot(_bf(khf * t["w_b"]), lamb)
            lam_s[...] = _dot_tn(_bf(qhf * t["q_b"]), dob) + lam * t["dec_b"]
            cs, sn = cos_ref[r, :], sin_ref[r, :]
            dq_ref[r, :] = _bf(_rot_half(dqh_s[r, :], cs, -sn))
            dk_ref[r, :] = _bf(_rot_half(dkh * scale, cs, -sn))
            dv_ref[r, :] = _bf(dv)
            return carry

        _loop(nC, pass2, 0)
        rows = lambda acc: jnp.sum(acc[...], axis=-1, keepdims=True)
        dd = acc_dd[...] * t["dmask"] * t["alag"]
        glf = (_sum11(jnp.where(t["low"], dd, 0.0)) + _sum11(rows(acc_tf) * t["q_f"] * (t["j"] + 1.0))
               + _sum11(rows(acc_uf) * t["w_f"] * (C - 1.0 - t["j"])) + _sum11(acc_lf[...]) * t["dec_f"] * C)
        glb = (_sum11(jnp.where(t["low"], 0.0, dd)) + _sum11(rows(acc_tb) * t["q_b"] * (C - t["j"]))
               + _sum11(rows(acc_ub) * t["w_b"] * t["j"]) + _sum11(acc_lb[...]) * t["dec_b"] * C)
        row = lax.broadcasted_iota(jnp.int32, (8, 128), 0)
        da_f = glf * t["lgf"]
        da_b = glb * t["lgb"]
        dd_ref[...] = jnp.where(row == 0, da_f, jnp.where(row == 1, da_b, 0.0))
        for cp in writes(head):
            cp.start()

        @pl.when(head == H - 1)
        def _():
            for cp in writes(head):
                cp.wait()

    zs = lambda off: pl.BlockSpec((S, RD), lambda h: (0, off // RD + h), pipeline_mode=ONE_BUFFER)
    col = pl.BlockSpec((S, RD), lambda h: (0, h), pipeline_mode=ONE_BUFFER)
    dec = pl.BlockSpec((None, 1, 128), lambda h: (h, 0, 0))
    tab = pl.BlockSpec((S, RD // 2), lambda h: (0, 0), pipeline_mode=ONE_BUFFER)
    return _call(
        body, name="ret_bwd", grid=(H,),
        in_specs=[zs(dm.o_rq), zs(dm.o_rk), zs(dm.o_rv), zs(dm.o_rg), col, dec, dec, tab, tab],
        out_specs=[pl.BlockSpec((None, 8, 128), lambda h: (h, 0, 0))],
        out_shape=[jax.ShapeDtypeStruct((H, 8, 128), F32)],
        scratch_shapes=[pltpu.VMEM((S, RD), BF16), pltpu.VMEM((S, RD), BF16),
                        pltpu.VMEM((nC, RD, RD), BF16), pltpu.VMEM((nC, RD, RD), BF16),
                        pltpu.VMEM((S, RD), BF16),
                        pltpu.VMEM((S, RD), F32), pltpu.VMEM((S, RD), F32), pltpu.VMEM((S, RD), F32),
                        pltpu.VMEM((RD, RD), F32), pltpu.VMEM((RD, RD), F32)]
        + [pltpu.VMEM((S, RD), BF16)] * 4 + [pltpu.SemaphoreType.DMA((4,))]
        + [pltpu.VMEM((C, C), F32)] + [pltpu.VMEM((C, RD), F32)] * 4 + [pltpu.VMEM((RD, RD), F32)] * 2,
        args=(z, z, z, z, dya_pre, _decay_rows(a_f), _decay_rows(a_b), cosr, sinr), tasks=tasks, carry=[dz])


def _pool_bwd(z, dz, dyb_pre, pool_w, pool_scale, dm):
    S, PD = dm.S, dm.PD
    PB = _pool_rows(S)
    WIN = PB + 2 * CHUNK
    G = POOL_GROUPS

    def body(u_ref, g_ref, dy_ref, w_ref, sc_ref, dw_ref, dsc_ref, dz_ref, dp_s, dpc_s, dw_s, du_ref, dg_ref, wsem):
        group = pl.program_id(0)
        writes = lambda step: _col_writes(dz_ref, wsem, step, [(du_ref, dm.o_pv, PD), (dg_ref, dm.o_pg, PD)])
        half = jnp.left_shift(1, group)
        w = w_ref[...]
        sc = sc_ref[...]
        dw_s[...] = jnp.zeros_like(dw_s)
        dsc_ref[...] = jnp.zeros_like(dsc_ref)

        @pl.when(group > 0)
        def _():
            for cp in writes(group - 1):
                cp.wait()

        def blk1(n, carry):
            row0 = pl.multiple_of(n * PB, PB)
            r = pl.ds(row0, PB)
            st = _pool_win_start(n, PB, S)
            band = _band(row0, st, -half, half - 1, (PB, WIN))
            inv = _pool_counts(row0, half, S, PB)
            pb = _bf(_dot(band, u_ref[pl.ds(st, WIN), :]) * inv - _f32(u_ref[r, :]))
            ylin = _dot(pb, w)
            y = ylin * sc
            g = _f32(g_ref[r, :])
            dyb = _f32(dy_ref[r, :])
            dg_ref[r, :] = _bf(dyb * y * _dsilu(g))
            dy = dyb * _silu(g)
            dsc_ref[...] += jnp.sum(dy * ylin, axis=0, keepdims=True)
            dyl = _bf(dy * sc)
            dw_s[...] += _dot_tn(pb, dyl)
            dp = _dot_nt(dyl, w)
            dp_s[r, :] = dp
            dpc_s[r, :] = _bf(dp * inv)
            return carry

        _loop(S // PB, blk1, 0)
        dw_ref[...] = _bf(dw_s[...])

        def blk2(n, carry):
            row0 = pl.multiple_of(n * PB, PB)
            r = pl.ds(row0, PB)
            st = _pool_win_start(n, PB, S)
            band_t = _band(row0, st, -half + 1, half, (PB, WIN))
            du_ref[r, :] = _bf(_dot(band_t, dpc_s[pl.ds(st, WIN), :]) - dp_s[r, :])
            return carry

        _loop(S // PB, blk2, 0)
        for cp in writes(group):
            cp.start()

        @pl.when(group == G - 1)
        def _():
            for cp in writes(group):
                cp.wait()

    own, _ = _call(
        body, name="pool_bwd", grid=(G,),
        in_specs=[pl.BlockSpec((S, PD), lambda g: (0, dm.o_pv // PD + g)),
                  pl.BlockSpec((S, PD), lambda g: (0, dm.o_pg // PD + g)),
                  pl.BlockSpec((S, PD), lambda g: (0, g)),
                  pl.BlockSpec((None, PD, PD), lambda g: (g, 0, 0)),
                  pl.BlockSpec((None, 1, PD), lambda g: (g, 0, 0))],
        out_specs=[pl.BlockSpec((None, PD, PD), lambda g: (g, 0, 0)), pl.BlockSpec((None, 1, PD), lambda g: (g, 0, 0))],
        out_shape=[jax.ShapeDtypeStruct((G, PD, PD), BF16), jax.ShapeDtypeStruct((G, 1, PD), F32)],
        scratch_shapes=[pltpu.VMEM((S, PD), F32), pltpu.VMEM((S, PD), BF16), pltpu.VMEM((PD, PD), F32),
                        pltpu.VMEM((S, PD), BF16), pltpu.VMEM((S, PD), BF16), pltpu.SemaphoreType.DMA((2,))],
        args=(z, z, dyb_pre, pool_w, pool_scale), carry=[dz])
    return own


def _att_bwd(z, dz, dyc_pre, q_gain, k_gain, sink, tabs, dm, tasks=()):
    S, C, nC, HD, G = dm.S, CHUNK, dm.nC, ATT_HEAD_DIM, ATT_GROUP
    GW = G * HD
    scale = HD ** -0.5

    def body(q_ref, k_ref, v_ref, g_ref, dy_ref, qg_ref, kg_ref, sk_ref, cs_ref, su_ref, sd_ref,
             sm_ref, dz_ref, qn_s, kn_s, dqn_s, dkn_s, dv_s, bias_s, dq_ref, dk_ref, dv_ref, dg_ref, wsem):
        kv = pl.program_id(0)
        writes = lambda step: _col_writes(dz_ref, wsem, step, [(dq_ref, dm.o_aq, GW), (dk_ref, dm.o_ak, HD),
                                                               (dv_ref, dm.o_av, HD), (dg_ref, dm.o_ag, GW)])
        _att_prep(q_ref, k_ref, qg_ref, kg_ref, cs_ref, su_ref, sd_ref, qn_s, kn_s, nC)

        @pl.when(kv > 0)
        def _():
            for cp in writes(kv - 1):
                cp.wait()

        _att_bias_tables(bias_s)
        dkn_s[...] = jnp.zeros_like(dkn_s)
        dv_s[...] = jnp.zeros_like(dv_s)
        sink_col = _sink_col(sk_ref)

        def blk(n, dsink):
            row0 = pl.multiple_of(n * C, C)
            r = pl.ds(row0, C)
            w = pl.ds(_win_start(n, S), 3 * C)
            q4 = qn_s[:, r, :].reshape(G * C, HD)
            kw, vw = kn_s[w, :], v_ref[w, :]
            p, ps = _att_probs(q4, kw, sink_col, _att_bias(bias_s, n, nC))
            pb = _bf(p)
            o = _dot(pb, vw)
            do_parts = []
            for g in range(G):
                cols = slice(g * HD, (g + 1) * HD)
                gate = _f32(g_ref[r, cols])
                dy = _f32(dy_ref[r, cols])
                dg_ref[r, cols] = _bf(dy * o[g * C:(g + 1) * C, :] * _dsilu(gate))
                do_parts.append(dy * _silu(gate))
            dob = _bf(jnp.concatenate(do_parts, axis=0))
            dp = _dot_nt(dob, vw)
            drow = jnp.sum(p * dp, axis=-1, keepdims=True)
            ds = _bf(p * (dp - drow))
            dsink = dsink - ps * drow
            dqn_s[:, r, :] = (_dot(ds, kw) * scale).reshape(G, C, HD)
            dkn_s[w, :] += _dot_tn(ds, q4) * scale
            dv_s[w, :] += _dot_tn(pb, dob)
            return dsink

        dsink = _loop(nC, blk, jnp.zeros((G * C, 1), F32))

        def fin(n, acc):
            dqg, dkg = acc
            r = pl.ds(pl.multiple_of(n * C, C), C)
            cs, su, sd = cs_ref[r, :], su_ref[r, :], sd_ref[r, :]

            def norm_bwd(x, gain, dqn):
                rr = lax.rsqrt(jnp.mean(x * x, axis=-1, keepdims=True) + RMS_EPS)
                u = x * rr
                dw = _rot_part(dqn, cs, -su, -sd)
                du = dw * gain
                gsum = dw * u
                if gsum.ndim == 3:
                    gsum = jnp.sum(gsum, axis=0)
                return rr * (du - u * jnp.mean(du * u, axis=-1, keepdims=True)), jnp.sum(gsum, axis=0, keepdims=True)

            dk, gk = norm_bwd(_f32(k_ref[r, :]), kg_ref[...], dkn_s[r, :])
            dk_ref[r, :] = _bf(dk)
            dv_ref[r, :] = _bf(dv_s[r, :])
            dq, gq = norm_bwd(_heads(_f32(q_ref[r, :]), HD), qg_ref[...], dqn_s[:, r, :])
            for g in range(G):
                dq_ref[r, g * HD:(g + 1) * HD] = _bf(dq[g])
            return dqg + gq, dkg + gk

        dqg, dkg = lax.fori_loop(0, nC, fin, (jnp.zeros((1, HD), F32), jnp.zeros((1, HD), F32)))
        sm_ref[...] = jnp.zeros_like(sm_ref)
        sm_ref[0:1, :] = dqg
        sm_ref[1:2, :] = dkg
        for g in range(G):
            sm_ref[2 + g:3 + g, :] = jnp.broadcast_to(_sum11(dsink[g * C:(g + 1) * C, :]), (1, HD))
        for cp in writes(kv):
            cp.start()

        @pl.when(kv == dm.AKV - 1)
        def _():
            for cp in writes(kv):
                cp.wait()

    tab = pl.BlockSpec((S, HD), lambda h: (0, 0), pipeline_mode=ONE_BUFFER)
    gain = pl.BlockSpec((1, HD), lambda h: (0, 0))
    wide = lambda off: pl.BlockSpec((S, GW), lambda h: (0, off // GW + h), pipeline_mode=ONE_BUFFER)
    thin = lambda off: pl.BlockSpec((S, HD), lambda h: (0, off // HD + h), pipeline_mode=ONE_BUFFER)
    return _call(
        body, name="att_bwd", grid=(dm.AKV,),
        in_specs=[wide(dm.o_aq), thin(dm.o_ak), thin(dm.o_av), wide(dm.o_ag), wide(0), gain, gain,
                  pl.BlockSpec((None, G, 128), lambda h: (h, 0, 0)), tab, tab, tab],
        out_specs=[pl.BlockSpec((None, 8, 128), lambda h: (h, 0, 0))],
        out_shape=[jax.ShapeDtypeStruct((dm.AKV, 8, 128), F32)],
        scratch_shapes=[pltpu.VMEM((G, S, HD), BF16), pltpu.VMEM((S, HD), BF16),
                        pltpu.VMEM((G, S, HD), F32), pltpu.VMEM((S, HD), F32), pltpu.VMEM((S, HD), F32),
                        pltpu.VMEM((3, G * C, 3 * C), F32),
                        pltpu.VMEM((S, GW), BF16), pltpu.VMEM((S, HD), BF16), pltpu.VMEM((S, HD), BF16),
                        pltpu.VMEM((S, GW), BF16), pltpu.SemaphoreType.DMA((4,))],
        args=(z, z, z, z, dyc_pre, q_gain, k_gain, _sink_rows(sink, dm), *tabs), tasks=tasks, carry=[dz])


def _grad_matmul(a, b, name, tasks=()):
    S, M = a.shape
    N = b.shape[1]
    tm, tn = min(1024, M), min(512, N)

    def body(a_ref, b_ref, o_ref):
        o_ref[...] = _bf(_dot_tn(a_ref[...], b_ref[...]))

    own, tk = _call(
        body, name=name, grid=(M // tm, N // tn),
        in_specs=[pl.BlockSpec((S, tm), lambda i, j: (0, i)), pl.BlockSpec((S, tn), lambda i, j: (0, j))],
        out_specs=[pl.BlockSpec((None, tm, tn), lambda i, j: (0, i, j))],
        out_shape=[jax.ShapeDtypeStruct((1, M, N), BF16)],
        args=(a, b), tasks=tasks)
    return own[0], tk


def _inproj_bwd(dz, w_in, dm, tasks=()):
    S, D = dm.S, dm.D
    tm = dm.tm
    tk = dm.INW // N_CHIPS if (dm.INW // N_CHIPS) % 128 == 0 else 512
    nk = dm.INW // tk

    def body(dz_ref, w_ref, dh_ref):
        part = _dot_nt(dz_ref[...], w_ref[...])

        @pl.when(pl.program_id(1) == 0)
        def _():
            dh_ref[...] = part

        @pl.when(pl.program_id(1) != 0)
        def _():
            dh_ref[...] += part

    return _call(
        body, name="inproj_bwd", grid=(S // tm, nk),
        in_specs=[pl.BlockSpec((tm, tk), lambda i, k: (i, k)), pl.BlockSpec((None, D, tk), lambda i, k: (0, 0, k))],
        out_specs=[pl.BlockSpec((tm, D), lambda i, k: (i, 0))],
        out_shape=[jax.ShapeDtypeStruct((S, D), F32)],
        args=(dz, w_in), tasks=tasks)


def _norm_bwd(dh, x, norm_g, dx_out, dm):
    S, D = dm.S, dm.D
    tm = dm.tm

    def body(dh_ref, x_ref, g_ref, dxo_ref, dx_ref, dg_ref):
        @pl.when(pl.program_id(0) == 0)
        def _():
            dg_ref[...] = jnp.zeros_like(dg_ref)

        xv = x_ref[...]
        rr = lax.rsqrt(jnp.mean(xv * xv, axis=-1, keepdims=True) + RMS_EPS)
        u = xv * rr
        dh = dh_ref[...]
        dg_ref[...] += jnp.sum(dh * u, axis=0, keepdims=True)
        du = dh * g_ref[...]
        dx_ref[...] = dxo_ref[...] + rr * (du - u * jnp.mean(du * u, axis=-1, keepdims=True))

    row = pl.BlockSpec((tm, D), lambda i: (i, 0))
    vec = pl.BlockSpec((1, D), lambda i: (0, 0))
    own, _ = _call(
        body, name="norm_bwd", grid=(S // tm,), in_specs=[row, row, vec, row], out_specs=[row, vec],
        out_shape=[jax.ShapeDtypeStruct((S, D), F32), jax.ShapeDtypeStruct((1, D), F32)],
        args=(dh, x, norm_g, dx_out))
    return own


def _row_block(rows, width, itemsize):
    target = max(16, (2 * 1024 * 1024) // (width * itemsize))
    for rb in range(min(rows, target), 0, -1):
        if rows % rb == 0 and (rb % 16 == 0 or rb == rows):
            return rb
    return rows


def _prefetch_call(body, *, name, grid, in_specs, out_specs, out_shape, args, aliases=None):
    grid_spec = pltpu.PrefetchScalarGridSpec(num_scalar_prefetch=1, grid=grid, in_specs=in_specs, out_specs=out_specs)
    return pl.pallas_call(
        body, name=name, grid_spec=grid_spec, out_shape=out_shape, input_output_aliases=aliases or {},
        compiler_params=pltpu.CompilerParams(dimension_semantics=("arbitrary",) * len(grid),
                                             vmem_limit_bytes=VMEM_LIMIT_V7X),
    )(*args)


def _place_shard(w, l, spec, chip):
    A, rows, width = spec.shard
    rb = _row_block(rows, width, 4)
    nrb = rows // rb
    if spec.kind == "col":
        out_map = lambda a, r, chip: (a, r, chip[0])
    else:
        out_map = lambda a, r, chip: (a, chip[0] * nrb + r, 0)

    def body(chip_ref, w_ref, o_ref):
        o_ref[...] = _bf(w_ref[...])

    return _prefetch_call(
        body, name="place_" + spec.name, grid=(A, nrb),
        in_specs=[pl.BlockSpec((None, rb, width), lambda a, r, chip: (l * A + a, r, 0))],
        out_specs=pl.BlockSpec((None, rb, width), out_map),
        out_shape=jax.ShapeDtypeStruct(spec.full, BF16), args=(chip, w))


def _pair_sum(grad, land, spec, core):
    A, hr, w = spec.half
    rb = _row_block(hr, w, 2)
    nrb = hr // rb
    if spec.kind == "col":
        g_spec = pl.BlockSpec((None, rb, w), lambda j, a, r, core: (a, core[0] * nrb + r, j))
    else:
        g_spec = pl.BlockSpec((None, rb, w), lambda j, a, r, core: (a, (j * 2 + core[0]) * nrb + r, 0))

    def body(core_ref, g_ref, l_ref, o_ref):
        o_ref[...] = _bf(_f32(g_ref[...]) + _f32(l_ref[...]))

    blk = pl.BlockSpec((None, None, rb, w), lambda j, a, r, core: (j, a, r, 0))
    return _prefetch_call(
        body, name="pair_sum_" + spec.name, grid=(N_CHIPS, A, nrb), in_specs=[g_spec, blk], out_specs=blk,
        out_shape=jax.ShapeDtypeStruct((N_CHIPS,) + spec.half, BF16), args=(core, grad, land))


def _chip_sum(pair_sum, land, spec, chip_core):
    A, hr, w = spec.half
    rb = _row_block(hr, w, 4)
    nrb = hr // rb

    def body(cc_ref, p_ref, l0_ref, l1_ref, l2_ref, o_ref):
        o_ref[...] = ((_f32(p_ref[...]) + _f32(l0_ref[...])) + _f32(l1_ref[...])) + _f32(l2_ref[...])

    own = pl.BlockSpec((None, None, rb, w), lambda a, r, cc: (cc[0], a, r, 0))
    slot = lambda p: pl.BlockSpec((None, None, rb, w), lambda a, r, cc: (p, a, r, 0))
    return _prefetch_call(
        body, name="chip_sum_" + spec.name, grid=(A, nrb), in_specs=[own, slot(0), slot(1), slot(2)],
        out_specs=pl.BlockSpec((None, rb, w), lambda a, r, cc: (a, cc[1] * nrb + r, 0)),
        out_shape=jax.ShapeDtypeStruct(spec.shard, F32), args=(chip_core, pair_sum, land, land, land))


def _adamw_math(w, g, m, v):
    m = ADAM_B1 * m + (1.0 - ADAM_B1) * g
    v = ADAM_B2 * v + (1.0 - ADAM_B2) * (g * g)
    m_hat = m / (1.0 - ADAM_B1 ** ADAM_STEP)
    v_hat = v / (1.0 - ADAM_B2 ** ADAM_STEP)
    delta = -ADAM_LR * (m_hat / (jnp.sqrt(v_hat) + ADAM_EPS) + ADAM_WD * w)
    return delta, m, v


def _adamw(w, g, m, v, l, depth, spec, carried):
    A, R, C = spec.shard
    rb = _row_block(R, C, 4 * 4)
    stacked = pl.BlockSpec((None, rb, C), lambda a, r: (l * A + a, r, 0))
    n_carry = 0 if carried is None else 4

    def body(w_ref, g_ref, m_ref, v_ref, *rest):
        go_ref, d_ref, mo_ref, vo_ref = rest[n_carry:]
        g = g_ref[...]
        go_ref[...] = g
        d_ref[...], mo_ref[...], vo_ref[...] = _adamw_math(w_ref[...], g, m_ref[...], v_ref[...])

    return pl.pallas_call(
        body, name="adamw_" + spec.name, grid=(A, R // rb),
        in_specs=[stacked, pl.BlockSpec((None, rb, C), lambda a, r: (a, r, 0)), stacked, stacked] + [HBM_ANY] * n_carry,
        out_specs=[stacked] * 4,
        out_shape=[jax.ShapeDtypeStruct((depth * A, R, C), F32)] * 4,
        input_output_aliases={4 + i: i for i in range(n_carry)},
        compiler_params=pltpu.CompilerParams(dimension_semantics=("arbitrary", "arbitrary"),
                                             vmem_limit_bytes=VMEM_LIMIT_V7X),
    )(w, g, m, v, *(carried or ()))


def _small_update(g_part, w, m, v):
    R = g_part.shape[0]
    n_dev = 8

    def body(g_ref, w_ref, m_ref, v_ref, go_ref, d_ref, mo_ref, vo_ref, all_s, send_sem, recv_sem):
        x, y, c = _place()
        me = 4 * x + 2 * y + c
        all_s[me] = g_ref[...]
        cps = []
        for k in range(1, n_dev):
            peer = (x ^ ((k >> 2) & 1), y ^ ((k >> 1) & 1), c ^ (k & 1))
            cp = pltpu.make_async_remote_copy(src_ref=g_ref, dst_ref=all_s.at[me], send_sem=send_sem.at[k],
                                              recv_sem=recv_sem.at[k], device_id=peer, device_id_type=MESH)
            cp.start()
            cps.append(cp)
        for cp in cps:
            cp.wait()
        g = all_s[0]
        for d in range(1, n_dev):
            g = g + all_s[d]
        go_ref[...] = g
        d_ref[...], mo_ref[...], vo_ref[...] = _adamw_math(w_ref[...], g, m_ref[...], v_ref[...])

    vm = pl.BlockSpec(memory_space=pltpu.VMEM)
    return pl.pallas_call(
        body, name="small_update", in_specs=[vm] * 4, out_specs=[vm] * 4,
        out_shape=[jax.ShapeDtypeStruct((R, 128), F32)] * 4,
        scratch_shapes=[pltpu.VMEM((n_dev, R, 128), F32), pltpu.SemaphoreType.DMA((n_dev,)),
                        pltpu.SemaphoreType.DMA((n_dev,))],
        compiler_params=pltpu.CompilerParams(vmem_limit_bytes=VMEM_LIMIT_V7X),
    )(g_part, w, m, v)


def _pack_small(parts):
    flat = jnp.concatenate([p.reshape(-1) for p in parts])
    pad = (-flat.shape[0]) % 1024
    return jnp.pad(flat, (0, pad)).reshape(-1, 128)


def _unpack_small(packed, like):
    flat = packed.reshape(-1)
    out, at = [], 0
    for p in like:
        out.append(flat[at:at + p.size].reshape(p.shape))
        at += p.size
    return out


def kernel(x, norm_g, w_in, ret_decay_fwd, ret_decay_bwd, pool_w, pool_scale, attn_q_gain, attn_k_gain, attn_sink, w_ret, w_pool, w_att, w_out, loss_target, m_norm_g, m_w_in, m_ret_decay_fwd, m_ret_decay_bwd, m_pool_w, m_pool_scale, m_attn_q_gain, m_attn_k_gain, m_attn_sink, m_w_ret, m_w_pool, m_w_att, m_w_out, v_norm_g, v_w_in, v_ret_decay_fwd, v_ret_decay_bwd, v_pool_w, v_pool_scale, v_attn_q_gain, v_attn_k_gain, v_attn_sink, v_w_ret, v_w_pool, v_w_att, v_w_out):
    S, D = x.shape[1], x.shape[2]
    L = norm_g.shape[0]
    dm = _Dims(S, D, L)
    PD, BW, G = dm.PD, dm.BW, POOL_GROUPS
    xi, yi, ci = _place()
    chip = (2 * xi + yi).astype(jnp.int32).reshape(1)
    core = ci.astype(jnp.int32).reshape(1)
    chip_core = jnp.concatenate([chip, core])

    specs = [_Sharded("w_in", "col", 1, D, dm.INW), _Sharded("w_ret", "col", 1, BW, D), _Sharded("w_pool", "col", 1, BW, D),
             _Sharded("w_att", "col", 1, BW, D), _Sharded("w_out", "row", 1, D, D), _Sharded("pool_w", "row", G, PD, PD)]
    n_big = len(specs)
    big_w = [w_in, w_ret, w_pool, w_att, w_out, pool_w]
    big_m = [m_w_in, m_w_ret, m_w_pool, m_w_att, m_w_out, m_pool_w]
    big_v = [v_w_in, v_w_ret, v_w_pool, v_w_att, v_w_out, v_pool_w]
    stack3 = lambda a, s: a.reshape((L * s.shard[0],) + s.shard[1:])
    big_w3 = [stack3(a, s) for a, s in zip(big_w, specs)]
    big_m3 = [stack3(a, s) for a, s in zip(big_m, specs)]
    big_v3 = [stack3(a, s) for a, s in zip(big_v, specs)]

    W = [[_place_shard(big_w3[t], l, specs[t], chip) for t in range(n_big)] for l in range(L)]
    (near0, _), = _comm_only("gather_first_near", [_gather_near_task(W[0], specs)])
    (ici0, _), = _comm_only("gather_first_relay", [_gather_relay_task(near0, specs)])
    (W[0], _), = _comm_only("gather_first_d2d", [_gather_d2d_task(ici0, specs)])

    cosr, sinr = _ret_rope_tables(S, dm.RD)
    tabs = _att_tables(S)
    xl = x[0]
    saved = []
    for l in range(L):
        nxt = l + 1 < L
        ng = norm_g[l].reshape(1, D)
        qg, kg = attn_q_gain[l].reshape(1, ATT_HEAD_DIM), attn_k_gain[l].reshape(1, ATT_HEAD_DIM)
        psc = pool_scale[l].reshape(G, 1, PD)
        f_in, f_ret, f_pool, f_att, f_out, f_pw = W[l]
        (z, h), tk = _inproj(xl, ng, f_in, dm, tasks=[_gather_ici_task(W[l + 1][:1], specs[:1])] if nxt else ())
        if nxt:
            W[l + 1][:1] = tk[0][0]
        ya_pre = _ret_fwd(z, ret_decay_fwd[l], ret_decay_bwd[l], cosr, sinr, dm)
        yb_pre = _pool_fwd(z, f_pw, psc, dm)
        (yc_pre,), tk = _att_fwd(z, qg, kg, attn_sink[l], tabs, dm,
                                 tasks=[_gather_ici_task(W[l + 1][1:], specs[1:])] if nxt else ())
        if nxt:
            W[l + 1][1:] = tk[0][0]
        mg = z
        (x_next, ya, yb, yc, merged), tk = _merge_out(xl, mg, ya_pre, yb_pre, yc_pre, f_ret, f_pool, f_att, f_out, dm,
                                                      tasks=[_gather_d2d_task(W[l + 1], specs)] if nxt else ())
        if nxt:
            W[l + 1] = tk[0][0]
        saved.append((xl, z, h, mg, ya_pre, yb_pre, yc_pre, ya, yb, yc, merged, ng, qg, kg, psc))
        xl = x_next
    sq, dx = _loss_grad(xl, loss_target[0], dm)
    loss = lax.psum(sq[0, 0] * (0.5 / D), ("x", "y", "c"))

    grads = [None] * L
    lands = [None] * L
    pairs = [None] * L
    lands2 = [None] * L
    shards = [None] * L
    carried = [None] * n_big
    small = [None] * L

    def pair_sums(k):
        pairs[k] = [_pair_sum(g, ld, s, core) for g, ld, s in zip(grads[k], lands[k], specs)]

    def chip_sums(k):
        shards[k] = [_chip_sum(p, ld, s, chip_core) for p, ld, s in zip(pairs[k], lands2[k], specs)]

    def adamw(k):
        for t in range(n_big):
            carried[t] = _adamw(big_w3[t], shards[k][t], big_m3[t], big_v3[t], k, L, specs[t], carried[t])

    for l in reversed(range(L)):
        xl, z, h, mg, ya_pre, yb_pre, yc_pre, ya, yb, yc, merged, ng, qg, kg, psc = saved[l]
        f_in, f_ret, f_pool, f_att, f_out, f_pw = W[l]
        up1, up2 = l + 1 < L, l + 2 < L
        own, tk = _merge_bwd_gates(dx, mg, ya, yb, yc, f_out, dm, tasks=[_pair_task(grads[l + 1], specs)] if up1 else ())
        dz, dya, dyb, dyc, dxb = own
        if up1:
            lands[l + 1] = tk[0][1]
            pair_sums(l + 1)
        dya_pre, dyb_pre, dyc_pre = _merge_bwd_proj(dya, dyb, dyc, f_ret, f_pool, f_att, dm)
        g_out, _ = _grad_matmul(merged, dxb, "grad_w_out")
        g_ret, _ = _grad_matmul(ya_pre, dya, "grad_w_ret")
        g_pool, _ = _grad_matmul(yb_pre, dyb, "grad_w_pool")
        g_att, _ = _grad_matmul(yc_pre, dyc, "grad_w_att")
        last = l == 0
        in_rows = specs[0].half_rows
        cuts = [0, in_rows // 4, 3 * in_rows // 4, in_rows]
        tasks = [_chip_task(pairs[l + 1][1:], specs[1:])] if up1 else []
        if up1 and last:
            tasks.append(_chip_rows_task(pairs[1][0], None, specs[0], cuts[0], cuts[1]))
        (ddec, dz), tk = _ret_bwd(z, dz, dya_pre, ret_decay_fwd[l], ret_decay_bwd[l], cosr, sinr, dm, tasks=tasks)
        if up1:
            lands2[l + 1] = [None] + tk[0][1]
            land_in = tk[1][1][0] if last else None
        dpw, dps, dz = _pool_bwd(z, dz, dyb_pre, f_pw, psc, dm)
        tasks = [_half_task(shards[l + 2], specs)] if up2 else []
        if up1 and last:
            tasks.append(_chip_rows_task(pairs[1][0], land_in, specs[0], cuts[1], cuts[2]))
        if last:
            tasks.append(_pair_task([g_ret, g_pool, g_att, g_out, dpw], specs[1:]))
        (dsm, dz), tk = _att_bwd(z, dz, dyc_pre, qg, kg, attn_sink[l], tabs, dm, tasks=tasks)
        if up2:
            shards[l + 2] = tk[0][0]
            adamw(l + 2)
        if up1 and last:
            land_in = tk[1 if up2 else 0][0][0]
        if last:
            small_lands = tk[-1][1]
            small_pairs = [_pair_sum(g, ld, s, core)
                           for g, ld, s in zip([g_ret, g_pool, g_att, g_out, dpw], small_lands, specs[1:])]
        if not last:
            g_in, _ = _grad_matmul(h, dz, "grad_w_in")
            (dh,), tk = _inproj_bwd(dz, f_in, dm, tasks=[_chip_task(pairs[l + 1][:1], specs[:1])] if up1 else ())
            if up1:
                lands2[l + 1][0] = tk[0][1][0]
                chip_sums(l + 1)
            grads[l] = [g_in, g_ret, g_pool, g_att, g_out, dpw]
        else:
            tasks = [_chip_rows_task(pairs[1][0], land_in, specs[0], cuts[2], cuts[3])] if up1 else []
            tasks.append(_chip_task(small_pairs, specs[1:]))
            g_in, tk = _grad_matmul(h, dz, "grad_w_in", tasks=tasks)
            if up1:
                lands2[1][0] = tk[0][0][0]
                chip_sums(1)
            small_lands2 = tk[-1][1]
            (_, in_land), = _comm_only("grad_pair_exchange", [_pair_task([g_in], specs[:1])])
            pairs[0] = [_pair_sum(g_in, in_land[0], specs[0], core)] + small_pairs
            (dh,), tk = _inproj_bwd(dz, f_in, dm, tasks=[_chip_task(pairs[0][:1], specs[:1])])
            lands2[0] = tk[0][1] + small_lands2
            chip_sums(0)
        dx, dng = _norm_bwd(dh, xl, ng, dx, dm)
        small[l] = [dng.reshape(D), ddec[:, 0, 0], ddec[:, 1, 0], dps.reshape(BW), jnp.sum(dsm[:, 0, :], axis=0),
                    jnp.sum(dsm[:, 1, :], axis=0), dsm[:, 2:2 + ATT_GROUP, 0].reshape(dm.AH)]

    rest = [k for k in (1, 0) if k < L]
    done = _comm_only("grad_half_exchange", [_half_task(shards[k], specs) for k in rest])
    for k, (both, _) in zip(rest, done):
        shards[k] = both
        adamw(k)

    back = lambda a, like: a.reshape(like.shape)
    g_big, d_big, m_big, v_big = ([back(carried[t][i], big_w[t]) for t in range(n_big)] for i in range(4))

    small_g = [jnp.stack([small[l][i] for l in range(L)]) for i in range(7)]
    small_w = [norm_g, ret_decay_fwd, ret_decay_bwd, pool_scale, attn_q_gain, attn_k_gain, attn_sink]
    small_m = [m_norm_g, m_ret_decay_fwd, m_ret_decay_bwd, m_pool_scale, m_attn_q_gain, m_attn_k_gain, m_attn_sink]
    small_v = [v_norm_g, v_ret_decay_fwd, v_ret_decay_bwd, v_pool_scale, v_attn_q_gain, v_attn_k_gain, v_attn_sink]
    sg, sd, sm, sv = _small_update(_pack_small(small_g), _pack_small(small_w), _pack_small(small_m), _pack_small(small_v))
    g_sm, d_sm, m_sm, v_sm = (_unpack_small(a, small_w) for a in (sg, sd, sm, sv))

    def ordered(big, small_):
        return [small_[0], big[0], small_[1], small_[2], big[5], small_[3], small_[4], small_[5], small_[6],
                big[1], big[2], big[3], big[4]]

    return (loss, dx[None], *ordered(g_big, g_sm), *ordered(d_big, d_sm), *ordered(m_big, m_sm),
            *ordered(v_big, v_sm))
```

```python
import jax
import jax.numpy as jnp
from jax import lax
from jax.experimental import pallas as pl
from jax.experimental.pallas import tpu as pltpu

F32 = jnp.float32
BF16 = jnp.bfloat16
MESH = pl.DeviceIdType.MESH

RMS_EPS = 1e-6
NEG_BIG = -1e30
CHUNK = 128
RET_HEADS = 4
POOL_GROUPS = 4
ATT_HEAD_DIM = 128
ATT_GROUP = 4
ROPE_DIMS = 32
RET_ROPE_BASE = 10000.0
ROPE_THETA = 500000.0
N_BRANCHES = 3
N_CHIPS = 4

ADAM_LR = 0.001
ADAM_B1 = 0.9
ADAM_B2 = 0.999
ADAM_EPS = 1e-08
ADAM_WD = 0.01
ADAM_STEP = 10

VMEM_LIMIT_V7X = 56 * 1024 * 1024

TN = (((0,), (0,)), ((), ()))
NT = (((1,), (1,)), ((), ()))

RET_HEADS_PER_STEP = 2
LOOP_UNROLL = 2
HBM_ANY = pl.BlockSpec(memory_space=pl.ANY)
ONE_BUFFER = pl.Buffered(1)


def _sigmoid(x):
    return 1.0 / (1.0 + jnp.exp(-x))


def _silu(x):
    return x * _sigmoid(x)


def _dsilu(x):
    s = _sigmoid(x)
    return s * (1.0 + x * (1.0 - s))


def _dot(a, b):
    return jnp.dot(a, b, preferred_element_type=F32)


def _dot_tn(a, b):
    return lax.dot_general(a, b, TN, preferred_element_type=F32)


def _dot_nt(a, b):
    return lax.dot_general(a, b, NT, preferred_element_type=F32)


def _bf(x):
    return x.astype(BF16)


def _f32(x):
    return x.astype(F32)


def _loop(n, body, init):
    def several(i, carry):
        for u in range(LOOP_UNROLL):
            carry = body(i * LOOP_UNROLL + u, carry)
        return carry
    return lax.fori_loop(0, n // LOOP_UNROLL, several, init)


def _sum11(x):
    return jnp.sum(jnp.sum(x, axis=1, keepdims=True), axis=0, keepdims=True)


class _Dims:
    def __init__(self, seq, d_model, depth):
        self.S, self.D, self.L = seq, d_model, depth
        bw = d_model // 2
        self.BW = bw
        self.RD = bw // RET_HEADS
        self.PD = bw // POOL_GROUPS
        self.AH = bw // ATT_HEAD_DIM
        self.AKV = self.AH // ATT_GROUP
        self.AKW = self.AKV * ATT_HEAD_DIM
        self.o_rq, self.o_rk, self.o_rv, self.o_rg = 0, bw, 2 * bw, 3 * bw
        self.o_pv, self.o_pg = 4 * bw, 5 * bw
        self.o_aq = 6 * bw
        self.o_ak = 7 * bw
        self.o_av = 7 * bw + self.AKW
        self.o_ag = 7 * bw + 2 * self.AKW
        self.o_mg = 8 * bw + 2 * self.AKW
        self.INW = self.o_mg + N_BRANCHES * d_model
        self.nC = seq // CHUNK
        self.tm = min(512, seq)


def _place():
    return lax.axis_index("x"), lax.axis_index("y"), lax.axis_index("c")


def _chip_peers(x, y):
    return [(1 - x, y), (x, 1 - y), (1 - x, 1 - y)]


class _Sharded:
    def __init__(self, name, kind, a, r, cc):
        self.name, self.kind, self.A, self.R, self.Cc = name, kind, a, r, cc
        self.full = (a, r, cc)
        if kind == "col":
            self.shard = (a, r, cc // N_CHIPS)
            self.half_rows = r // 2
        else:
            self.shard = (a, r // N_CHIPS, cc)
            self.half_rows = r // N_CHIPS // 2
        self.half = (a, self.half_rows, self.shard[2])

    def in_full(self, ref, chip, core=None):
        hr = self.half_rows
        if self.kind == "col":
            rows = pl.ds(0, self.R) if core is None else pl.ds(core * hr, hr)
            return ref.at[:, rows, pl.ds(chip * self.shard[2], self.shard[2])]
        rows = pl.ds(chip * self.shard[1], self.shard[1]) if core is None else pl.ds(chip * self.shard[1] + core * hr, hr)
        return ref.at[:, rows, :]

    def in_full_rows(self, ref, chip, core, lo, n):
        if self.kind == "col":
            return ref.at[:, pl.ds(core * self.half_rows + lo, n), pl.ds(chip * self.shard[2], self.shard[2])]
        return ref.at[:, pl.ds(chip * self.shard[1] + core * self.half_rows + lo, n), :]

    def in_shard(self, ref, core):
        return ref.at[:, pl.ds(core * self.half_rows, self.half_rows), :]


class _Task:
    def __init__(self, ro, rw, new, n_sem, copies):
        self.ro, self.rw, self.new, self.n_sem, self.copies = list(ro), list(rw), list(new), n_sem, copies


def _remote(src, dst, send_sem, recv_sem, k, device):
    def make():
        return pltpu.make_async_remote_copy(src_ref=src, dst_ref=dst, send_sem=send_sem.at[k], recv_sem=recv_sem.at[k],
                                            device_id=device, device_id_type=MESH)
    return make


def _gather_ici_task(bufs, specs):
    def copies(ro, rw, new, ss, rs):
        x, y, c = _place()
        me = 2 * x + y
        out = []
        for t, spec in enumerate(specs):
            for p, (px, py) in enumerate(_chip_peers(x, y)):
                mine = spec.in_full(rw[t], me, c)
                theirs = spec.in_full(rw[t], 2 * px + py, c)
                k = t * 3 + p
                out.append((_remote(mine, mine, ss, rs, k, (px, py, c)), _remote(mine, mine, ss, rs, k, (px, py, c)),
                            _remote(theirs, theirs, ss, rs, k, (px, py, c))))
        return out
    return _Task([], bufs, [], 3 * len(specs), copies)


def _gather_near_task(bufs, specs):
    def copies(ro, rw, new, ss, rs):
        x, y, c = _place()
        me = 2 * x + y
        out = []
        for t, spec in enumerate(specs):
            for p, (px, py) in enumerate(_chip_peers(x, y)[:2]):
                mine = spec.in_full(rw[t], me, c)
                theirs = spec.in_full(rw[t], 2 * px + py, c)
                k = t * 2 + p
                out.append((_remote(mine, mine, ss, rs, k, (px, py, c)), _remote(mine, mine, ss, rs, k, (px, py, c)),
                            _remote(theirs, theirs, ss, rs, k, (px, py, c))))
        return out
    return _Task([], bufs, [], 2 * len(specs), copies)


def _gather_relay_task(bufs, specs):
    def copies(ro, rw, new, ss, rs):
        x, y, c = _place()
        (xp, yp, dg) = _chip_peers(x, y)
        chip = lambda p: 2 * p[0] + p[1]
        out = []
        for t, spec in enumerate(specs):
            q = spec.half_rows // 2
            for p, (frm, to, lo) in enumerate(((yp, xp, 0), (xp, yp, q))):
                sent = spec.in_full_rows(rw[t], chip(frm), c, lo, q)
                got = spec.in_full_rows(rw[t], chip(dg), c, lo, q)
                k = t * 2 + p
                dev = (to[0], to[1], c)
                out.append((_remote(sent, sent, ss, rs, k, dev), _remote(sent, sent, ss, rs, k, dev),
                            _remote(got, got, ss, rs, k, dev)))
        return out
    return _Task([], bufs, [], 2 * len(specs), copies)


def _gather_d2d_task(bufs, specs):
    def copies(ro, rw, new, ss, rs):
        x, y, c = _place()
        sib = (x, y, 1 - c)
        out = []
        for t, spec in enumerate(specs):
            for p, (px, py) in enumerate(_chip_peers(x, y)):
                got = spec.in_full(rw[t], 2 * px + py, c)
                gets = spec.in_full(rw[t], 2 * px + py, 1 - c)
                k = t * 3 + p
                out.append((_remote(got, got, ss, rs, k, sib), _remote(got, got, ss, rs, k, sib),
                            _remote(gets, gets, ss, rs, k, sib)))
        return out
    return _Task([], bufs, [], 3 * len(specs), copies)


def _pair_task(grads, specs):
    def copies(ro, rw, new, ss, rs):
        x, y, c = _place()
        sib = (x, y, 1 - c)
        out = []
        for t, spec in enumerate(specs):
            for j in range(N_CHIPS):
                k = t * N_CHIPS + j
                cp = _remote(spec.in_full(ro[t], j, 1 - c), new[t].at[j], ss, rs, k, sib)
                out.append((cp, cp, cp))
        return out
    return _Task(grads, [], [jax.ShapeDtypeStruct((N_CHIPS,) + s.half, BF16) for s in specs], N_CHIPS * len(specs), copies)


def _chip_task(pair_sums, specs):
    def copies(ro, rw, new, ss, rs):
        x, y, c = _place()
        out = []
        for t in range(len(specs)):
            for p, (px, py) in enumerate(_chip_peers(x, y)):
                cp = _remote(ro[t].at[2 * px + py], new[t].at[p], ss, rs, t * 3 + p, (px, py, c))
                out.append((cp, cp, cp))
        return out
    return _Task(pair_sums, [], [jax.ShapeDtypeStruct((3,) + s.half, BF16) for s in specs], 3 * len(specs), copies)


def _chip_rows_task(pair_sum, land, spec, lo, hi):
    def copies(ro, rw, new, ss, rs):
        x, y, c = _place()
        dst = (rw or new)[0]
        out = []
        for p, (px, py) in enumerate(_chip_peers(x, y)):
            cp = _remote(ro[0].at[2 * px + py, :, pl.ds(lo, hi - lo), :], dst.at[p, :, pl.ds(lo, hi - lo), :],
                         ss, rs, p, (px, py, c))
            out.append((cp, cp, cp))
        return out
    first = land is None
    return _Task([pair_sum], [] if first else [land], [jax.ShapeDtypeStruct((3,) + spec.half, BF16)] if first else [],
                 3, copies)


def _half_task(shards, specs):
    def copies(ro, rw, new, ss, rs):
        x, y, c = _place()
        sib = (x, y, 1 - c)
        out = []
        for t, spec in enumerate(specs):
            mine, theirs = spec.in_shard(rw[t], c), spec.in_shard(rw[t], 1 - c)
            out.append((_remote(mine, mine, ss, rs, t, sib), _remote(mine, mine, ss, rs, t, sib),
                        _remote(theirs, theirs, ss, rs, t, sib)))
        return out
    return _Task([], shards, [], len(specs), copies)


def _call(body, *, name, grid, in_specs, out_specs, out_shape, args, scratch_shapes=(), tasks=(), carry=()):
    tasks = [t for t in tasks if t is not None]
    carry = list(carry)
    n_in, n_out, n_scr, n_carry = len(in_specs), len(out_specs), len(scratch_shapes), len(carry)
    ro = [a for t in tasks for a in t.ro]
    rw = [a for t in tasks for a in t.rw]
    new = [s for t in tasks for s in t.new]
    n_ro, n_rw, n_new = len(ro), len(rw), len(new)

    def wrapped(*refs):
        ins = refs[:n_in]
        ro_refs = refs[n_in + n_carry:n_in + n_carry + n_ro]
        at = n_in + n_carry + n_ro + n_rw
        outs = refs[at:at + n_out + n_carry]
        at = at + n_out + n_carry
        rw_refs = refs[at:at + n_rw]
        new_refs = refs[at + n_rw:at + n_rw + n_new]
        at = at + n_rw + n_new
        scr = refs[at:at + n_scr]
        sems = refs[at + n_scr:]

        def task_copies():
            found, a, b, d = [], 0, 0, 0
            for i, t in enumerate(tasks):
                found += t.copies(ro_refs[a:a + len(t.ro)], rw_refs[b:b + len(t.rw)], new_refs[d:d + len(t.new)],
                                  sems[2 * i], sems[2 * i + 1])
                a, b, d = a + len(t.ro), b + len(t.rw), d + len(t.new)
            return found

        if tasks:
            first = pl.program_id(0) == 0
            last = pl.program_id(0) == grid[0] - 1
            for ax in range(1, len(grid)):
                first = first & (pl.program_id(ax) == 0)
                last = last & (pl.program_id(ax) == grid[ax] - 1)

            @pl.when(first)
            def _():
                for cp, _, _ in task_copies():
                    cp().start()

        body(*ins, *outs, *scr)

        if tasks:
            @pl.when(last)
            def _():
                found = task_copies()
                for _, _, recv in found:
                    recv().wait_recv()
                for _, send, _ in found:
                    send().wait_send()

    sem_shapes = []
    for t in tasks:
        sem_shapes += [pltpu.SemaphoreType.DMA((t.n_sem,)), pltpu.SemaphoreType.DMA((t.n_sem,))]
    aliases = {n_in + i: n_out + i for i in range(n_carry)}
    aliases.update({n_in + n_carry + n_ro + i: n_out + n_carry + i for i in range(n_rw)})
    res = pl.pallas_call(
        wrapped, name=name, grid=grid,
        in_specs=list(in_specs) + [HBM_ANY] * (n_carry + n_ro + n_rw),
        out_specs=list(out_specs) + [HBM_ANY] * (n_carry + n_rw + n_new),
        out_shape=list(out_shape) + [jax.ShapeDtypeStruct(a.shape, a.dtype) for a in carry + rw] + new,
        scratch_shapes=list(scratch_shapes) + sem_shapes,
        input_output_aliases=aliases,
        compiler_params=pltpu.CompilerParams(dimension_semantics=("arbitrary",) * len(grid),
                                             vmem_limit_bytes=VMEM_LIMIT_V7X),
    )(*args, *carry, *ro, *rw)
    own, rest = list(res[:n_out + n_carry]), list(res[n_out + n_carry:])
    per_task, b, d = [], 0, n_rw
    for t in tasks:
        per_task.append((rest[b:b + len(t.rw)], rest[d:d + len(t.new)]))
        b, d = b + len(t.rw), d + len(t.new)
    return own, per_task


def _comm_only(name, tasks):
    ro = [a for t in tasks for a in t.ro]
    rw = [a for t in tasks for a in t.rw]
    new = [s for t in tasks for s in t.new]
    n_ro, n_rw, n_new = len(ro), len(rw), len(new)

    def body(*refs):
        ro_refs = refs[:n_ro]
        rw_refs = refs[n_ro + n_rw:n_ro + 2 * n_rw]
        new_refs = refs[n_ro + 2 * n_rw:n_ro + 2 * n_rw + n_new]
        sems = refs[n_ro + 2 * n_rw + n_new:]
        found, a, b, d = [], 0, 0, 0
        for i, t in enumerate(tasks):
            found += t.copies(ro_refs[a:a + len(t.ro)], rw_refs[b:b + len(t.rw)], new_refs[d:d + len(t.new)],
                              sems[2 * i], sems[2 * i + 1])
            a, b, d = a + len(t.ro), b + len(t.rw), d + len(t.new)
        for cp, _, _ in found:
            cp().start()
        for _, _, recv in found:
            recv().wait_recv()
        for _, send, _ in found:
            send().wait_send()

    sem_shapes = []
    for t in tasks:
        sem_shapes += [pltpu.SemaphoreType.DMA((t.n_sem,)), pltpu.SemaphoreType.DMA((t.n_sem,))]
    res = pl.pallas_call(
        body, name=name,
        in_specs=[HBM_ANY] * (n_ro + n_rw), out_specs=[HBM_ANY] * (n_rw + n_new),
        out_shape=[jax.ShapeDtypeStruct(a.shape, a.dtype) for a in rw] + new,
        scratch_shapes=sem_shapes,
        input_output_aliases={n_ro + i: i for i in range(n_rw)},
    )(*ro, *rw)
    res = list(res)
    per_task, b, d = [], 0, n_rw
    for t in tasks:
        per_task.append((res[b:b + len(t.rw)], res[d:d + len(t.new)]))
        b, d = b + len(t.rw), d + len(t.new)
    return per_task


def _gather_first(bufs, specs):
    n = len(specs)
    near, relay, d2d = _gather_near_task(bufs, specs), _gather_relay_task(bufs, specs), _gather_d2d_task(bufs, specs)

    def body(*refs):
        rw, sems = refs[n:2 * n], refs[2 * n:]
        nc = near.copies([], rw, [], sems[0], sems[1])
        rc = relay.copies([], rw, [], sems[2], sems[3])
        dc = d2d.copies([], rw, [], sems[4], sems[5])
        for start, _, _ in nc:
            start().start()
        for t in range(n):
            for p in range(2):
                nc[t * 2 + p][2]().wait_recv()
            for p in range(2):
                rc[t * 2 + p][0]().start()
                dc[t * 3 + p][0]().start()
        for t in range(n):
            for p in range(2):
                rc[t * 2 + p][2]().wait_recv()
            dc[t * 3 + 2][0]().start()
        for _, _, recv in dc:
            recv().wait_recv()
        for _, send, _ in nc + rc + dc:
            send().wait_send()

    sem_shapes = [pltpu.SemaphoreType.DMA((task.n_sem,)) for task in (near, relay, d2d) for _ in range(2)]
    return list(pl.pallas_call(
        body, name="gather_first", in_specs=[HBM_ANY] * n, out_specs=[HBM_ANY] * n,
        out_shape=[jax.ShapeDtypeStruct(a.shape, a.dtype) for a in bufs], scratch_shapes=sem_shapes,
        input_output_aliases={i: i for i in range(n)},
    )(*bufs))


def _inproj(x, norm_g, w_in, dm, tasks=()):
    S, D, N = dm.S, dm.D, dm.INW
    tm, tn = min(1024, S), 512

    def body(x_ref, g_ref, w_ref, z_ref, h_ref):
        @pl.when(pl.program_id(1) == 0)
        def _():
            xv = x_ref[...]
            r = lax.rsqrt(jnp.mean(xv * xv, axis=-1, keepdims=True) + RMS_EPS)
            h_ref[...] = _bf(xv * r * g_ref[...])

        z_ref[...] = _bf(_dot(h_ref[...], w_ref[...]))

    return _call(
        body, name="inproj", grid=(S // tm, N // tn),
        in_specs=[pl.BlockSpec((tm, D), lambda i, j: (i, 0)),
                  pl.BlockSpec((1, D), lambda i, j: (0, 0)),
                  pl.BlockSpec((None, D, tn), lambda i, j: (0, 0, j))],
        out_specs=[pl.BlockSpec((tm, tn), lambda i, j: (i, j)),
                   pl.BlockSpec((tm, D), lambda i, j: (i, 0))],
        out_shape=[jax.ShapeDtypeStruct((S, N), BF16), jax.ShapeDtypeStruct((S, D), BF16)],
        args=(x, norm_g, w_in), tasks=tasks)


def _rot_half(x, cs, sn):
    h = cs.shape[-1]
    x1, x2 = x[:, :h], x[:, h:]
    return jnp.concatenate([x1 * cs - x2 * sn, x2 * cs + x1 * sn], axis=-1)


def _ret_tables(af_ref, ab_ref):
    C = CHUNK
    lgf = -jnp.exp(af_ref[...])[:, :1]
    lgb = -jnp.exp(ab_ref[...])[:, :1]
    ri = lax.broadcasted_iota(jnp.int32, (C, C), 0)
    ci = lax.broadcasted_iota(jnp.int32, (C, C), 1)
    lag = _f32(ri - ci)
    alag = jnp.abs(lag)
    low = lag >= 0
    dmask = jnp.where(low, jnp.exp(lgf * alag), jnp.exp(lgb * alag))
    j = _f32(lax.broadcasted_iota(jnp.int32, (C, 1), 0))
    return dict(lgf=lgf, lgb=lgb, alag=alag, low=low, dmask=dmask, j=j,
                w_f=jnp.exp(lgf * (C - 1.0 - j)), w_b=jnp.exp(lgb * j),
                q_f=jnp.exp(lgf * (j + 1.0)), q_b=jnp.exp(lgb * (C - j)),
                dec_f=jnp.exp(lgf * C), dec_b=jnp.exp(lgb * C))


def _decay_rows(a):
    return jnp.broadcast_to(a.reshape(RET_HEADS, 1, 1), (RET_HEADS, 1, 128))


def _heads(x, width):
    return jnp.stack([x[:, b * width:(b + 1) * width] for b in range(x.shape[1] // width)])


def _bdot(a, b):
    return lax.dot_general(a, b, (((2,), (1,)), ((0,), (0,))), preferred_element_type=F32)


def _bdot_nt(a, b):
    return lax.dot_general(a, b, (((2,), (2,)), ((0,), (0,))), preferred_element_type=F32)


def _bdot_tn(a, b):
    return lax.dot_general(a, b, (((1,), (1,)), ((0,), (0,))), preferred_element_type=F32)


def _ret_tables_heads(af_ref, ab_ref):
    C = CHUNK
    lgf = -jnp.exp(af_ref[...])[:, :, :1]
    lgb = -jnp.exp(ab_ref[...])[:, :, :1]
    ri = lax.broadcasted_iota(jnp.int32, (1, C, C), 1)
    ci = lax.broadcasted_iota(jnp.int32, (1, C, C), 2)
    lag = _f32(ri - ci)
    alag = jnp.abs(lag)
    low = lag >= 0
    dmask = jnp.where(low, jnp.exp(lgf * alag), jnp.exp(lgb * alag))
    j = _f32(lax.broadcasted_iota(jnp.int32, (1, C, 1), 1))
    return dict(lgf=lgf, lgb=lgb, alag=alag, low=low, dmask=dmask, j=j,
                w_f=jnp.exp(lgf * (C - 1.0 - j)), w_b=jnp.exp(lgb * j),
                q_f=jnp.exp(lgf * (j + 1.0)), q_b=jnp.exp(lgb * (C - j)),
                dec_f=jnp.exp(lgf * C), dec_b=jnp.exp(lgb * C))


def _rot_half_heads(x, cs, sn):
    h = cs.shape[-1]
    x1, x2 = x[..., :h], x[..., h:]
    return jnp.concatenate([x1 * cs - x2 * sn, x2 * cs + x1 * sn], axis=-1)


def _ret_fwd(z, a_f, a_b, cosr, sinr, dm):
    S, RD, C, nC, HB = dm.S, dm.RD, CHUNK, dm.nC, RET_HEADS_PER_STEP
    W = HB * RD
    scale = RD ** -0.5

    def body(q_ref, k_ref, v_ref, g_ref, af_ref, ab_ref, cos_ref, sin_ref, o_ref, qh_s, kh_s, sf_s, st_s):
        t = _ret_tables_heads(af_ref, ab_ref)
        st_s[...] = jnp.zeros_like(st_s)

        def fwd_pass(n, carry):
            r = pl.ds(pl.multiple_of(n * C, C), C)
            cs, sn = cos_ref[r, :], sin_ref[r, :]
            qh = _rot_half_heads(_heads(_f32(q_ref[r, :]), RD), cs, sn)
            kh = _rot_half_heads(_heads(_f32(k_ref[r, :]), RD), cs, sn) * scale
            qh_s[:, r, :] = _bf(qh)
            kh_s[:, r, :] = _bf(kh)
            st = st_s[...]
            sf_s[n] = _bf(st)
            st_s[...] = st * t["dec_f"] + _bdot_tn(_bf(kh * t["w_f"]), _heads(v_ref[r, :], RD))
            return carry

        _loop(nC, fwd_pass, 0)
        st_s[...] = jnp.zeros_like(st_s)

        def bwd_pass(i, carry):
            n = nC - 1 - i
            r = pl.ds(pl.multiple_of(n * C, C), C)
            qhb, khb, vb = qh_s[:, r, :], kh_s[:, r, :], _heads(v_ref[r, :], RD)
            p = _bf(_bdot_nt(qhb, khb) * t["dmask"])
            qhf = _f32(qhb)
            sb = st_s[...]
            out = (_bdot(p, vb) + _bdot(_bf(qhf * t["q_f"]), sf_s[n]) + _bdot(_bf(qhf * t["q_b"]), _bf(sb)))
            y = out * lax.rsqrt(jnp.mean(out * out, axis=-1, keepdims=True) + RMS_EPS)
            gate = _silu(_f32(g_ref[r, :]))
            for b in range(HB):
                o_ref[r, b * RD:(b + 1) * RD] = _bf(y[b] * gate[:, b * RD:(b + 1) * RD])
            st_s[...] = sb * t["dec_b"] + _bdot_tn(_bf(_f32(khb) * t["w_b"]), vb)
            return carry

        _loop(nC, bwd_pass, 0)

    zs = lambda off: pl.BlockSpec((S, W), lambda h: (0, off // W + h), pipeline_mode=ONE_BUFFER)
    dec = pl.BlockSpec((HB, 1, 128), lambda h: (h, 0, 0))
    tab = pl.BlockSpec((S, RD // 2), lambda h: (0, 0), pipeline_mode=ONE_BUFFER)
    own, _ = _call(
        body, name="ret_fwd", grid=(RET_HEADS // HB,),
        in_specs=[zs(dm.o_rq), zs(dm.o_rk), zs(dm.o_rv), zs(dm.o_rg), dec, dec, tab, tab],
        out_specs=[pl.BlockSpec((S, W), lambda h: (0, h))],
        out_shape=[jax.ShapeDtypeStruct((S, dm.BW), BF16)],
        scratch_shapes=[pltpu.VMEM((HB, S, RD), BF16), pltpu.VMEM((HB, S, RD), BF16),
                        pltpu.VMEM((nC, HB, RD, RD), BF16), pltpu.VMEM((HB, RD, RD), F32)],
        args=(z, z, z, z, _decay_rows(a_f), _decay_rows(a_b), cosr, sinr))
    return own[0]


def _band(first_row, first_col, lo, hi, shape):
    r = lax.broadcasted_iota(jnp.int32, shape, 0) + first_row
    c = lax.broadcasted_iota(jnp.int32, shape, 1) + first_col
    d = c - r
    return jnp.where((d >= lo) & (d <= hi), 1.0, 0.0).astype(BF16)


def _pool_counts(first_row, half, S, rows):
    pos = lax.broadcasted_iota(jnp.int32, (rows, 1), 0) + first_row
    lo = jnp.clip(pos - half, 0, S)
    hi = jnp.clip(pos + half, 0, S)
    return 1.0 / _f32(hi - lo)


def _pool_rows(S):
    return min(4 * CHUNK, S // 2)


def _pool_win_start(n, rows, S):
    return pl.multiple_of(jnp.clip(n * rows - CHUNK, 0, S - (rows + 2 * CHUNK)), CHUNK)


def _win_start(n, S):
    return pl.multiple_of(jnp.clip((n - 1) * CHUNK, 0, S - 3 * CHUNK), CHUNK)


def _pool_fwd(z, pool_w, pool_scale, dm):
    S, PD = dm.S, dm.PD
    PB = _pool_rows(S)
    WIN = PB + 2 * CHUNK

    def body(u_ref, g_ref, w_ref, sc_ref, o_ref):
        half = jnp.left_shift(1, pl.program_id(0))
        w = w_ref[...]
        sc = sc_ref[...]

        def blk(n, carry):
            row0 = pl.multiple_of(n * PB, PB)
            r = pl.ds(row0, PB)
            st = _pool_win_start(n, PB, S)
            band = _band(row0, st, -half, half - 1, (PB, WIN))
            mean = _dot(band, u_ref[pl.ds(st, WIN), :]) * _pool_counts(row0, half, S, PB)
            p = mean - _f32(u_ref[r, :])
            y = _dot(_bf(p), w) * sc
            o_ref[r, :] = _bf(y * _silu(_f32(g_ref[r, :])))
            return carry

        _loop(S // PB, blk, 0)

    own, _ = _call(
        body, name="pool_fwd", grid=(POOL_GROUPS,),
        in_specs=[pl.BlockSpec((S, PD), lambda g: (0, dm.o_pv // PD + g)),
                  pl.BlockSpec((S, PD), lambda g: (0, dm.o_pg // PD + g)),
                  pl.BlockSpec((None, PD, PD), lambda g: (g, 0, 0)),
                  pl.BlockSpec((None, 1, PD), lambda g: (g, 0, 0))],
        out_specs=[pl.BlockSpec((S, PD), lambda g: (0, g))],
        out_shape=[jax.ShapeDtypeStruct((S, dm.BW), BF16)],
        args=(z, z, pool_w, pool_scale))
    return own[0]


def _rot_part(x, cs, s_up, s_dn):
    h = ROPE_DIMS // 2
    lanes = x.ndim - 1
    return x * cs + pltpu.roll(x, ATT_HEAD_DIM - h, lanes) * s_up + pltpu.roll(x, h, lanes) * s_dn


def _att_tables(S):
    h = ROPE_DIMS // 2
    inv = ROPE_THETA ** (-jnp.arange(h, dtype=F32) / h)
    ang = jnp.arange(S, dtype=F32)[:, None] * inv[None, :]
    cos, sin = jnp.cos(ang), jnp.sin(ang)
    pad = jnp.zeros((S, ATT_HEAD_DIM - 2 * h), F32)
    zero = jnp.zeros((S, h), F32)
    cs = jnp.concatenate([cos, cos, pad + 1.0], axis=1)
    s_up = jnp.concatenate([-sin, zero, pad], axis=1)
    s_dn = jnp.concatenate([zero, sin, pad], axis=1)
    return cs, s_up, s_dn


def _ret_rope_tables(S, RD):
    h = RD // 2
    inv = 1.0 / (RET_ROPE_BASE ** jnp.linspace(0.0, 1.0, h, dtype=F32))
    ang = jnp.arange(S, dtype=F32)[:, None] * inv[None, :]
    return jnp.cos(ang), jnp.sin(ang)


def _att_norm_rot(x, gain, cs, s_up, s_dn):
    u = x * lax.rsqrt(jnp.mean(x * x, axis=-1, keepdims=True) + RMS_EPS)
    return _rot_part(u * gain, cs, s_up, s_dn)


def _att_bias_tables(bias_s):
    C = CHUNK
    shape = (ATT_GROUP * C, 3 * C)
    rel = lax.broadcasted_iota(jnp.int32, shape, 1) - lax.broadcasted_iota(jnp.int32, shape, 0) % C
    for which, shift in enumerate((-C, 0, -2 * C)):
        bias_s[which] = jnp.where(jnp.abs(rel + shift) <= CHUNK, 0.0, NEG_BIG)


def _att_bias(bias_s, n, nC):
    return bias_s[jnp.where(n == 0, 1, jnp.where(n == nC - 1, 2, 0))]


def _att_probs(q4, kw, sink_col, bias):
    s = _dot_nt(q4, kw) * (ATT_HEAD_DIM ** -0.5) + bias
    m = jnp.maximum(jnp.max(s, axis=-1, keepdims=True), sink_col)
    e = jnp.exp(s - m)
    es = jnp.exp(sink_col - m)
    inv = 1.0 / (jnp.sum(e, axis=-1, keepdims=True) + es)
    return e * inv, es * inv


def _sink_col(sink_ref):
    head = lax.broadcasted_iota(jnp.int32, (ATT_GROUP * CHUNK, 1), 0) // CHUNK
    col = jnp.zeros((ATT_GROUP * CHUNK, 1), F32)
    for g in range(ATT_GROUP):
        col = jnp.where(head == g, sink_ref[g:g + 1, :1], col)
    return col


def _sink_rows(sink, dm):
    return jnp.broadcast_to(sink.reshape(dm.AKV, ATT_GROUP, 1), (dm.AKV, ATT_GROUP, 128))


def _att_prep(q_ref, k_ref, qg_ref, kg_ref, cs_ref, su_ref, sd_ref, qn_s, kn_s, nC):
    C, HD = CHUNK, ATT_HEAD_DIM

    def prep(n, carry):
        r = pl.ds(pl.multiple_of(n * C, C), C)
        cs, su, sd = cs_ref[r, :], su_ref[r, :], sd_ref[r, :]
        kn_s[r, :] = _bf(_att_norm_rot(_f32(k_ref[r, :]), kg_ref[...], cs, su, sd))
        qn_s[:, r, :] = _bf(_att_norm_rot(_heads(_f32(q_ref[r, :]), HD), qg_ref[...], cs, su, sd))
        return carry

    lax.fori_loop(0, nC, prep, 0)


def _att_fwd(z, q_gain, k_gain, sink, tabs, dm, tasks=()):
    S, C, nC, HD, G = dm.S, CHUNK, dm.nC, ATT_HEAD_DIM, ATT_GROUP
    GW = G * HD

    KV = dm.AKV

    def body(q_ref, k_ref, v_ref, g_ref, qg_ref, kg_ref, sk_ref, cs_ref, su_ref, sd_ref, o_ref, qn_s, kn_s, bias_s):
        def prep(n, carry):
            r = pl.ds(pl.multiple_of(n * C, C), C)
            cs, su, sd = cs_ref[r, :], su_ref[r, :], sd_ref[r, :]
            kn_s[:, r, :] = _bf(_att_norm_rot(_heads(_f32(k_ref[r, :]), HD), kg_ref[...], cs, su, sd))
            qn_s[:, r, :] = _bf(_att_norm_rot(_heads(_f32(q_ref[r, :]), HD), qg_ref[...], cs, su, sd))
            return carry

        lax.fori_loop(0, nC, prep, 0)
        _att_bias_tables(bias_s)
        sink_col = jnp.stack([_sink_col(sk_ref.at[h]) for h in range(KV)])

        def blk(n, carry):
            row0 = pl.multiple_of(n * C, C)
            r = pl.ds(row0, C)
            w = pl.ds(_win_start(n, S), 3 * C)
            q4 = qn_s[:, r, :].reshape(KV, G * C, HD)
            s = _bdot_nt(q4, kn_s[:, w, :]) * (HD ** -0.5) + _att_bias(bias_s, n, nC)
            m = jnp.maximum(jnp.max(s, axis=-1, keepdims=True), sink_col)
            e = jnp.exp(s - m)
            inv = 1.0 / (jnp.sum(e, axis=-1, keepdims=True) + jnp.exp(sink_col - m))
            o = _bdot(_bf(e * inv), _heads(v_ref[w, :], HD))
            gate = _silu(_f32(g_ref[r, :]))
            for h in range(KV):
                for g in range(G):
                    cols = slice((h * G + g) * HD, (h * G + g + 1) * HD)
                    o_ref[r, cols] = _bf(o[h, g * C:(g + 1) * C, :] * gate[:, cols])
            return carry

        _loop(nC, blk, 0)

    tab = pl.BlockSpec((S, HD), lambda h: (0, 0), pipeline_mode=ONE_BUFFER)
    gain = pl.BlockSpec((1, HD), lambda h: (0, 0))
    one = lambda off, width: pl.BlockSpec((pl.Element(S), pl.Element(width)), lambda h: (0, off),
                                          pipeline_mode=ONE_BUFFER)
    return _call(
        body, name="att_fwd", grid=(1,),
        in_specs=[one(dm.o_aq, dm.BW), one(dm.o_ak, dm.AKW), one(dm.o_av, dm.AKW), one(dm.o_ag, dm.BW),
                  gain, gain, pl.BlockSpec((KV, G, 128), lambda h: (0, 0, 0)), tab, tab, tab],
        out_specs=[pl.BlockSpec((S, dm.BW), lambda h: (0, 0), pipeline_mode=ONE_BUFFER)],
        out_shape=[jax.ShapeDtypeStruct((S, dm.BW), BF16)],
        scratch_shapes=[pltpu.VMEM((KV * G, S, HD), BF16), pltpu.VMEM((KV, S, HD), BF16),
                        pltpu.VMEM((3, G * C, 3 * C), F32)],
        args=(z, z, z, z, q_gain, k_gain, _sink_rows(sink, dm), *tabs), tasks=tasks)


def _merge_gate_spec(tm, dm):
    return pl.BlockSpec((pl.Element(tm), pl.Element(N_BRANCHES * dm.D)), lambda i: (i * tm, dm.o_mg))


def _merge_out(x, mg, ya_pre, yb_pre, yc_pre, w_ret, w_pool, w_att, w_out, dm, tasks=()):
    S, D, BW = dm.S, dm.D, dm.BW
    tm = min(256, S)

    def body(x_ref, mg_ref, a_ref, b_ref, c_ref, wr_ref, wp_ref, wa_ref, wo_ref, xo_ref, ya_ref, yb_ref, yc_ref, m_ref):
        ya = _dot(a_ref[...], wr_ref[...])
        yb = _dot(b_ref[...], wp_ref[...])
        yc = _dot(c_ref[...], wa_ref[...])
        ya_ref[...], yb_ref[...], yc_ref[...] = _bf(ya), _bf(yb), _bf(yc)
        g0 = _sigmoid(_f32(mg_ref[:, 0:D]))
        g1 = _sigmoid(_f32(mg_ref[:, D:2 * D]))
        g2 = _sigmoid(_f32(mg_ref[:, 2 * D:3 * D]))
        merged = _bf(g0 * ya + g1 * yb + g2 * yc)
        m_ref[...] = merged
        xo_ref[...] = x_ref[...] + _dot(merged, wo_ref[...])

    act = lambda w: pl.BlockSpec((tm, w), lambda i: (i, 0))
    wsp = lambda r: pl.BlockSpec((None, r, D), lambda i: (0, 0, 0), pipeline_mode=ONE_BUFFER)
    return _call(
        body, name="merge_out", grid=(S // tm,),
        in_specs=[act(D), _merge_gate_spec(tm, dm), act(BW), act(BW), act(BW), wsp(BW), wsp(BW), wsp(BW), wsp(D)],
        out_specs=[act(D)] * 5,
        out_shape=[jax.ShapeDtypeStruct((S, D), F32)] + [jax.ShapeDtypeStruct((S, D), BF16)] * 4,
        args=(x, mg, ya_pre, yb_pre, yc_pre, w_ret, w_pool, w_att, w_out), tasks=tasks)


def _loss_grad(y, target, dm):
    S, D = dm.S, dm.D
    tm = dm.tm

    def body(y_ref, t_ref, s_ref, d_ref):
        @pl.when(pl.program_id(0) == 0)
        def _():
            s_ref[...] = jnp.zeros_like(s_ref)

        e = y_ref[...] - t_ref[...]
        d_ref[...] = e * (1.0 / D)
        s_ref[...] += jnp.sum(jnp.sum(e * e, axis=-1, keepdims=True), axis=0, keepdims=True)

    row = pl.BlockSpec((tm, D), lambda i: (i, 0))
    own, _ = _call(
        body, name="loss_grad", grid=(S // tm,), in_specs=[row, row],
        out_specs=[pl.BlockSpec((1, 128), lambda i: (0, 0)), row],
        out_shape=[jax.ShapeDtypeStruct((1, 128), F32), jax.ShapeDtypeStruct((S, D), F32)],
        args=(y, target))
    return own


def _merge_bwd_gates(dx, mg, ya, yb, yc, w_out, dm, tasks=()):
    S, D = dm.S, dm.D
    tm = min(256, S)

    def body(dx_ref, mg_ref, ya_ref, yb_ref, yc_ref, wo_ref, dmg_ref, dya_ref, dyb_ref, dyc_ref, dxb_ref):
        dxb = _bf(dx_ref[...])
        dxb_ref[...] = dxb
        dm_ = _dot_nt(dxb, wo_ref[...])
        for k, (y_ref, dy_ref) in enumerate(((ya_ref, dya_ref), (yb_ref, dyb_ref), (yc_ref, dyc_ref))):
            g = _sigmoid(_f32(mg_ref[:, k * D:(k + 1) * D]))
            dmg_ref[:, k * D:(k + 1) * D] = _bf(dm_ * _f32(y_ref[...]) * g * (1.0 - g))
            dy_ref[...] = _bf(dm_ * g)

    act = lambda w: pl.BlockSpec((tm, w), lambda i: (i, 0))
    return _call(
        body, name="merge_bwd_gates", grid=(S // tm,),
        in_specs=[act(D), _merge_gate_spec(tm, dm), act(D), act(D), act(D),
                  pl.BlockSpec((None, D, D), lambda i: (0, 0, 0), pipeline_mode=ONE_BUFFER)],
        out_specs=[_merge_gate_spec(tm, dm), act(D), act(D), act(D), act(D)],
        out_shape=[jax.ShapeDtypeStruct((S, dm.INW), BF16)] + [jax.ShapeDtypeStruct((S, D), BF16)] * 4,
        args=(dx, mg, ya, yb, yc, w_out), tasks=tasks)


def _merge_bwd_proj(dya, dyb, dyc, w_ret, w_pool, w_att, dm):
    S, D, BW = dm.S, dm.D, dm.BW
    tm = dm.tm

    def body(da_ref, db_ref, dc_ref, wr_ref, wp_ref, wa_ref, oa_ref, ob_ref, oc_ref):
        oa_ref[...] = _bf(_dot_nt(da_ref[...], wr_ref[...]))
        ob_ref[...] = _bf(_dot_nt(db_ref[...], wp_ref[...]))
        oc_ref[...] = _bf(_dot_nt(dc_ref[...], wa_ref[...]))

    act = lambda w: pl.BlockSpec((tm, w), lambda i: (i, 0))
    wsp = pl.BlockSpec((None, BW, D), lambda i: (0, 0, 0), pipeline_mode=ONE_BUFFER)
    own, _ = _call(
        body, name="merge_bwd_proj", grid=(S // tm,),
        in_specs=[act(D)] * 3 + [wsp] * 3, out_specs=[act(BW)] * 3,
        out_shape=[jax.ShapeDtypeStruct((S, BW), BF16)] * 3,
        args=(dya, dyb, dyc, w_ret, w_pool, w_att))
    return own


def _col_writes(dz_ref, sem, step, parts):
    return [pltpu.make_async_copy(src, dz_ref.at[:, pl.ds(pl.multiple_of(off + step * w, 128), w)], sem.at[k])
            for k, (src, off, w) in enumerate(parts)]


def _ret_bwd(z, dz, dya_pre, a_f, a_b, cosr, sinr, dm, tasks=()):
    S, RD, C, nC = dm.S, dm.RD, CHUNK, dm.nC
    scale = RD ** -0.5
    H = RET_HEADS

    def body(q_ref, k_ref, v_ref, g_ref, dy_ref, af_ref, ab_ref, cos_ref, sin_ref, dd_ref, dz_ref,
             qh_s, kh_s, sf_s, sb_s, do_s, dqh_s, dkh_s, dv_s, st_s, lam_s, dq_ref, dk_ref, dv_ref, dg_ref, wsem,
             acc_dd, acc_tf, acc_tb, acc_uf, acc_ub, acc_lf, acc_lb):
        head = pl.program_id(0)
        writes = lambda step: _col_writes(dz_ref, wsem, step, [(dq_ref, dm.o_rq, RD), (dk_ref, dm.o_rk, RD),
                                                               (dv_ref, dm.o_rv, RD), (dg_ref, dm.o_rg, RD)])
        t = _ret_tables(af_ref, ab_ref)
        zero_rr = jnp.zeros((RD, RD), F32)

        st_s[...] = zero_rr

        def pass0(n, carry):
            r = pl.ds(pl.multiple_of(n * C, C), C)
            cs, sn = cos_ref[r, :], sin_ref[r, :]
            qh = _rot_half(_f32(q_ref[r, :]), cs, sn)
            kh = _rot_half(_f32(k_ref[r, :]), cs, sn) * scale
            qh_s[r, :] = _bf(qh)
            kh_s[r, :] = _bf(kh)
            st = st_s[...]
            sf_s[n] = _bf(st)
            st_s[...] = st * t["dec_f"] + _dot_tn(_bf(kh * t["w_f"]), v_ref[r, :])
            return carry

        _loop(nC, pass0, 0)

        @pl.when(head > 0)
        def _():
            for cp in writes(head - 1):
                cp.wait()

        st_s[...] = zero_rr
        lam_s[...] = zero_rr

        def pass1(i, carry):
            n = nC - 1 - i
            r = pl.ds(pl.multiple_of(n * C, C), C)
            qhb, khb, vb = qh_s[r, :], kh_s[r, :], v_ref[r, :]
            qhf, khf = _f32(qhb), _f32(khb)
            sc_ = _dot_nt(qhb, khb)
            p = _bf(sc_ * t["dmask"])
            sb = st_s[...]
            sbb = _bf(sb)
            sb_s[n] = sbb
            st_s[...] = sb * t["dec_b"] + _dot_tn(_bf(khf * t["w_b"]), vb)
            sfb = sf_s[n]
            qf_b, qb_b = _bf(qhf * t["q_f"]), _bf(qhf * t["q_b"])
            out = _dot(p, vb) + _dot(qf_b, sfb) + _dot(qb_b, sbb)
            rr = lax.rsqrt(jnp.mean(out * out, axis=-1, keepdims=True) + RMS_EPS)
            y = out * rr
            g = _f32(g_ref[r, :])
            dya = _f32(dy_ref[r, :])
            dg_ref[r, :] = _bf(dya * y * _dsilu(g))
            dyn = dya * _silu(g)
            dout = rr * (dyn - y * jnp.mean(dyn * y, axis=-1, keepdims=True))
            dob = _bf(dout)
            do_s[r, :] = dob
            dp = _dot_nt(dob, vb)
            dv = _dot_tn(p, dob)
            ds = _bf(dp * t["dmask"])
            dqh = _dot(ds, khb)
            dkh = _dot_tn(ds, qhb)
            acc_dd[...] += dp * sc_
            tf = _dot_nt(dob, sfb)
            tb = _dot_nt(dob, sbb)
            dqh = dqh + tf * t["q_f"] + tb * t["q_b"]
            acc_tf[...] += tf * qhf
            acc_tb[...] += tb * qhf
            lam = lam_s[...]
            lamb = _bf(lam)
            acc_lf[...] += lam * _f32(sfb)
            u = _dot_nt(vb, lamb)
            dkh = dkh + u * t["w_f"]
            acc_uf[...] += u * khf
            dv = dv + _dot(_bf(khf * t["w_f"]), lamb)
            lam_s[...] = _dot_tn(qf_b, dob) + lam * t["dec_f"]
            dqh_s[r, :] = dqh
            dkh_s[r, :] = dkh
            dv_s[r, :] = dv
            return carry

        for acc in (acc_dd, acc_tf, acc_tb, acc_uf, acc_ub, acc_lf, acc_lb):
            acc[...] = jnp.zeros_like(acc)
        _loop(nC, pass1, 0)

        lam_s[...] = zero_rr

        def pass2(n, carry):
            r = pl.ds(pl.multiple_of(n * C, C), C)
            qhb, khb, vb, dob = qh_s[r, :], kh_s[r, :], v_ref[r, :], do_s[r, :]
            qhf, khf = _f32(qhb), _f32(khb)
            lam = lam_s[...]
            lamb = _bf(lam)
            acc_lb[...] += lam * _f32(sb_s[n])
            u = _dot_nt(vb, lamb)
            dkh = dkh_s[r, :] + u * t["w_b"]
            acc_ub[...] += u * khf
            dv = dv_s[r, :] + _dot(_bf(khf * t["w_b"]), lamb)
            lam_s[...] = _dot_tn(_bf(qhf * t["q_b"]), dob) + lam * t["dec_b"]
            cs, sn = cos_ref[r, :], sin_ref[r, :]
            dq_ref[r, :] = _bf(_rot_half(dqh_s[r, :], cs, -sn))
            dk_ref[r, :] = _bf(_rot_half(dkh * scale, cs, -sn))
            dv_ref[r, :] = _bf(dv)
            return carry

        _loop(nC, pass2, 0)
        rows = lambda acc: jnp.sum(acc[...], axis=-1, keepdims=True)
        dd = acc_dd[...] * t["dmask"] * t["alag"]
        glf = (_sum11(jnp.where(t["low"], dd, 0.0)) + _sum11(rows(acc_tf) * t["q_f"] * (t["j"] + 1.0))
               + _sum11(rows(acc_uf) * t["w_f"] * (C - 1.0 - t["j"])) + _sum11(acc_lf[...]) * t["dec_f"] * C)
        glb = (_sum11(jnp.where(t["low"], 0.0, dd)) + _sum11(rows(acc_tb) * t["q_b"] * (C - t["j"]))
               + _sum11(rows(acc_ub) * t["w_b"] * t["j"]) + _sum11(acc_lb[...]) * t["dec_b"] * C)
        row = lax.broadcasted_iota(jnp.int32, (8, 128), 0)
        da_f = glf * t["lgf"]
        da_b = glb * t["lgb"]
        dd_ref[...] = jnp.where(row == 0, da_f, jnp.where(row == 1, da_b, 0.0))
        for cp in writes(head):
            cp.start()

        @pl.when(head == H - 1)
        def _():
            for cp in writes(head):
                cp.wait()

    zs = lambda off: pl.BlockSpec((S, RD), lambda h: (0, off // RD + h), pipeline_mode=ONE_BUFFER)
    col = pl.BlockSpec((S, RD), lambda h: (0, h), pipeline_mode=ONE_BUFFER)
    dec = pl.BlockSpec((None, 1, 128), lambda h: (h, 0, 0))
    tab = pl.BlockSpec((S, RD // 2), lambda h: (0, 0), pipeline_mode=ONE_BUFFER)
    return _call(
        body, name="ret_bwd", grid=(H,),
        in_specs=[zs(dm.o_rq), zs(dm.o_rk), zs(dm.o_rv), zs(dm.o_rg), col, dec, dec, tab, tab],
        out_specs=[pl.BlockSpec((None, 8, 128), lambda h: (h, 0, 0))],
        out_shape=[jax.ShapeDtypeStruct((H, 8, 128), F32)],
        scratch_shapes=[pltpu.VMEM((S, RD), BF16), pltpu.VMEM((S, RD), BF16),
                        pltpu.VMEM((nC, RD, RD), BF16), pltpu.VMEM((nC, RD, RD), BF16),
                        pltpu.VMEM((S, RD), BF16),
                        pltpu.VMEM((S, RD), F32), pltpu.VMEM((S, RD), F32), pltpu.VMEM((S, RD), F32),
                        pltpu.VMEM((RD, RD), F32), pltpu.VMEM((RD, RD), F32)]
        + [pltpu.VMEM((S, RD), BF16)] * 4 + [pltpu.SemaphoreType.DMA((4,))]
        + [pltpu.VMEM((C, C), F32)] + [pltpu.VMEM((C, RD), F32)] * 4 + [pltpu.VMEM((RD, RD), F32)] * 2,
        args=(z, z, z, z, dya_pre, _decay_rows(a_f), _decay_rows(a_b), cosr, sinr), tasks=tasks, carry=[dz])


def _pool_bwd(z, dz, dyb_pre, pool_w, pool_scale, dm):
    S, PD = dm.S, dm.PD
    PB = _pool_rows(S)
    WIN = PB + 2 * CHUNK
    G = POOL_GROUPS

    def body(u_ref, g_ref, dy_ref, w_ref, sc_ref, dw_ref, dsc_ref, dz_ref, dp_s, dpc_s, dw_s, du_ref, dg_ref, wsem):
        group = pl.program_id(0)
        writes = lambda step: _col_writes(dz_ref, wsem, step, [(du_ref, dm.o_pv, PD), (dg_ref, dm.o_pg, PD)])
        half = jnp.left_shift(1, group)
        w = w_ref[...]
        sc = sc_ref[...]
        dw_s[...] = jnp.zeros_like(dw_s)
        dsc_ref[...] = jnp.zeros_like(dsc_ref)

        @pl.when(group > 0)
        def _():
            for cp in writes(group - 1):
                cp.wait()

        def blk1(n, carry):
            row0 = pl.multiple_of(n * PB, PB)
            r = pl.ds(row0, PB)
            st = _pool_win_start(n, PB, S)
            band = _band(row0, st, -half, half - 1, (PB, WIN))
            inv = _pool_counts(row0, half, S, PB)
            pb = _bf(_dot(band, u_ref[pl.ds(st, WIN), :]) * inv - _f32(u_ref[r, :]))
            ylin = _dot(pb, w)
            y = ylin * sc
            g = _f32(g_ref[r, :])
            dyb = _f32(dy_ref[r, :])
            dg_ref[r, :] = _bf(dyb * y * _dsilu(g))
            dy = dyb * _silu(g)
            dsc_ref[...] += jnp.sum(dy * ylin, axis=0, keepdims=True)
            dyl = _bf(dy * sc)
            dw_s[...] += _dot_tn(pb, dyl)
            dp = _dot_nt(dyl, w)
            dp_s[r, :] = dp
            dpc_s[r, :] = _bf(dp * inv)
            return carry

        _loop(S // PB, blk1, 0)
        dw_ref[...] = _bf(dw_s[...])

        def blk2(n, carry):
            row0 = pl.multiple_of(n * PB, PB)
            r = pl.ds(row0, PB)
            st = _pool_win_start(n, PB, S)
            band_t = _band(row0, st, -half + 1, half, (PB, WIN))
            du_ref[r, :] = _bf(_dot(band_t, dpc_s[pl.ds(st, WIN), :]) - dp_s[r, :])
            return carry

        _loop(S // PB, blk2, 0)
        for cp in writes(group):
            cp.start()

        @pl.when(group == G - 1)
        def _():
            for cp in writes(group):
                cp.wait()

    own, _ = _call(
        body, name="pool_bwd", grid=(G,),
        in_specs=[pl.BlockSpec((S, PD), lambda g: (0, dm.o_pv // PD + g)),
                  pl.BlockSpec((S, PD), lambda g: (0, dm.o_pg // PD + g)),
                  pl.BlockSpec((S, PD), lambda g: (0, g)),
                  pl.BlockSpec((None, PD, PD), lambda g: (g, 0, 0)),
                  pl.BlockSpec((None, 1, PD), lambda g: (g, 0, 0))],
        out_specs=[pl.BlockSpec((None, PD, PD), lambda g: (g, 0, 0)), pl.BlockSpec((None, 1, PD), lambda g: (g, 0, 0))],
        out_shape=[jax.ShapeDtypeStruct((G, PD, PD), BF16), jax.ShapeDtypeStruct((G, 1, PD), F32)],
        scratch_shapes=[pltpu.VMEM((S, PD), F32), pltpu.VMEM((S, PD), BF16), pltpu.VMEM((PD, PD), F32),
                        pltpu.VMEM((S, PD), BF16), pltpu.VMEM((S, PD), BF16), pltpu.SemaphoreType.DMA((2,))],
        args=(z, z, dyb_pre, pool_w, pool_scale), carry=[dz])
    return own


def _att_bwd(z, dz, dyc_pre, q_gain, k_gain, sink, tabs, dm, tasks=()):
    S, C, nC, HD, G = dm.S, CHUNK, dm.nC, ATT_HEAD_DIM, ATT_GROUP
    GW = G * HD
    scale = HD ** -0.5

    def body(q_ref, k_ref, v_ref, g_ref, dy_ref, qg_ref, kg_ref, sk_ref, cs_ref, su_ref, sd_ref,
             sm_ref, dz_ref, qn_s, kn_s, dqn_s, dkn_s, dv_s, bias_s, dq_ref, dk_ref, dv_ref, dg_ref, wsem):
        kv = pl.program_id(0)
        writes = lambda step: _col_writes(dz_ref, wsem, step, [(dq_ref, dm.o_aq, GW), (dk_ref, dm.o_ak, HD),
                                                               (dv_ref, dm.o_av, HD), (dg_ref, dm.o_ag, GW)])
        _att_prep(q_ref, k_ref, qg_ref, kg_ref, cs_ref, su_ref, sd_ref, qn_s, kn_s, nC)

        @pl.when(kv > 0)
        def _():
            for cp in writes(kv - 1):
                cp.wait()

        _att_bias_tables(bias_s)
        dkn_s[...] = jnp.zeros_like(dkn_s)
        dv_s[...] = jnp.zeros_like(dv_s)
        sink_col = _sink_col(sk_ref)

        def blk(n, dsink):
            row0 = pl.multiple_of(n * C, C)
            r = pl.ds(row0, C)
            w = pl.ds(_win_start(n, S), 3 * C)
            q4 = qn_s[:, r, :].reshape(G * C, HD)
            kw, vw = kn_s[w, :], v_ref[w, :]
            p, ps = _att_probs(q4, kw, sink_col, _att_bias(bias_s, n, nC))
            pb = _bf(p)
            o = _dot(pb, vw)
            do_parts = []
            for g in range(G):
                cols = slice(g * HD, (g + 1) * HD)
                gate = _f32(g_ref[r, cols])
                dy = _f32(dy_ref[r, cols])
                dg_ref[r, cols] = _bf(dy * o[g * C:(g + 1) * C, :] * _dsilu(gate))
                do_parts.append(dy * _silu(gate))
            dob = _bf(jnp.concatenate(do_parts, axis=0))
            dp = _dot_nt(dob, vw)
            drow = jnp.sum(p * dp, axis=-1, keepdims=True)
            ds = _bf(p * (dp - drow))
            dsink = dsink - ps * drow
            dqn_s[:, r, :] = (_dot(ds, kw) * scale).reshape(G, C, HD)
            dkn_s[w, :] += _dot_tn(ds, q4) * scale
            dv_s[w, :] += _dot_tn(pb, dob)
            return dsink

        dsink = _loop(nC, blk, jnp.zeros((G * C, 1), F32))

        def fin(n, acc):
            dqg, dkg = acc
            r = pl.ds(pl.multiple_of(n * C, C), C)
            cs, su, sd = cs_ref[r, :], su_ref[r, :], sd_ref[r, :]

            def norm_bwd(x, gain, dqn):
                rr = lax.rsqrt(jnp.mean(x * x, axis=-1, keepdims=True) + RMS_EPS)
                u = x * rr
                dw = _rot_part(dqn, cs, -su, -sd)
                du = dw * gain
                gsum = dw * u
                if gsum.ndim == 3:
                    gsum = jnp.sum(gsum, axis=0)
                return rr * (du - u * jnp.mean(du * u, axis=-1, keepdims=True)), jnp.sum(gsum, axis=0, keepdims=True)

            dk, gk = norm_bwd(_f32(k_ref[r, :]), kg_ref[...], dkn_s[r, :])
            dk_ref[r, :] = _bf(dk)
            dv_ref[r, :] = _bf(dv_s[r, :])
            dq, gq = norm_bwd(_heads(_f32(q_ref[r, :]), HD), qg_ref[...], dqn_s[:, r, :])
            for g in range(G):
                dq_ref[r, g * HD:(g + 1) * HD] = _bf(dq[g])
            return dqg + gq, dkg + gk

        dqg, dkg = lax.fori_loop(0, nC, fin, (jnp.zeros((1, HD), F32), jnp.zeros((1, HD), F32)))
        sm_ref[...] = jnp.zeros_like(sm_ref)
        sm_ref[0:1, :] = dqg
        sm_ref[1:2, :] = dkg
        for g in range(G):
            sm_ref[2 + g:3 + g, :] = jnp.broadcast_to(_sum11(dsink[g * C:(g + 1) * C, :]), (1, HD))
        for cp in writes(kv):
            cp.start()

        @pl.when(kv == dm.AKV - 1)
        def _():
            for cp in writes(kv):
                cp.wait()

    tab = pl.BlockSpec((S, HD), lambda h: (0, 0), pipeline_mode=ONE_BUFFER)
    gain = pl.BlockSpec((1, HD), lambda h: (0, 0))
    wide = lambda off: pl.BlockSpec((S, GW), lambda h: (0, off // GW + h), pipeline_mode=ONE_BUFFER)
    thin = lambda off: pl.BlockSpec((S, HD), lambda h: (0, off // HD + h), pipeline_mode=ONE_BUFFER)
    return _call(
        body, name="att_bwd", grid=(dm.AKV,),
        in_specs=[wide(dm.o_aq), thin(dm.o_ak), thin(dm.o_av), wide(dm.o_ag), wide(0), gain, gain,
                  pl.BlockSpec((None, G, 128), lambda h: (h, 0, 0)), tab, tab, tab],
        out_specs=[pl.BlockSpec((None, 8, 128), lambda h: (h, 0, 0))],
        out_shape=[jax.ShapeDtypeStruct((dm.AKV, 8, 128), F32)],
        scratch_shapes=[pltpu.VMEM((G, S, HD), BF16), pltpu.VMEM((S, HD), BF16),
                        pltpu.VMEM((G, S, HD), F32), pltpu.VMEM((S, HD), F32), pltpu.VMEM((S, HD), F32),
                        pltpu.VMEM((3, G * C, 3 * C), F32),
                        pltpu.VMEM((S, GW), BF16), pltpu.VMEM((S, HD), BF16), pltpu.VMEM((S, HD), BF16),
                        pltpu.VMEM((S, GW), BF16), pltpu.SemaphoreType.DMA((4,))],
        args=(z, z, z, z, dyc_pre, q_gain, k_gain, _sink_rows(sink, dm), *tabs), tasks=tasks, carry=[dz])


def _grad_matmul(a, b, name, tasks=()):
    S, M = a.shape
    N = b.shape[1]
    tm, tn = min(1024, M), min(512, N)

    def body(a_ref, b_ref, o_ref):
        o_ref[...] = _bf(_dot_tn(a_ref[...], b_ref[...]))

    own, tk = _call(
        body, name=name, grid=(M // tm, N // tn),
        in_specs=[pl.BlockSpec((S, tm), lambda i, j: (0, i)), pl.BlockSpec((S, tn), lambda i, j: (0, j))],
        out_specs=[pl.BlockSpec((None, tm, tn), lambda i, j: (0, i, j))],
        out_shape=[jax.ShapeDtypeStruct((1, M, N), BF16)],
        args=(a, b), tasks=tasks)
    return own[0], tk


def _inproj_bwd(dz, w_in, dm, tasks=()):
    S, D = dm.S, dm.D
    tm = dm.tm
    tk = dm.INW // N_CHIPS if (dm.INW // N_CHIPS) % 128 == 0 else 512
    nk = dm.INW // tk

    def body(dz_ref, w_ref, dh_ref):
        part = _dot_nt(dz_ref[...], w_ref[...])

        @pl.when(pl.program_id(1) == 0)
        def _():
            dh_ref[...] = part

        @pl.when(pl.program_id(1) != 0)
        def _():
            dh_ref[...] += part

    return _call(
        body, name="inproj_bwd", grid=(S // tm, nk),
        in_specs=[pl.BlockSpec((tm, tk), lambda i, k: (i, k)), pl.BlockSpec((None, D, tk), lambda i, k: (0, 0, k))],
        out_specs=[pl.BlockSpec((tm, D), lambda i, k: (i, 0))],
        out_shape=[jax.ShapeDtypeStruct((S, D), F32)],
        args=(dz, w_in), tasks=tasks)


def _norm_bwd(dh, x, norm_g, dx_out, dm):
    S, D = dm.S, dm.D
    tm = dm.tm

    def body(dh_ref, x_ref, g_ref, dxo_ref, dx_ref, dg_ref):
        @pl.when(pl.program_id(0) == 0)
        def _():
            dg_ref[...] = jnp.zeros_like(dg_ref)

        xv = x_ref[...]
        rr = lax.rsqrt(jnp.mean(xv * xv, axis=-1, keepdims=True) + RMS_EPS)
        u = xv * rr
        dh = dh_ref[...]
        dg_ref[...] += jnp.sum(dh * u, axis=0, keepdims=True)
        du = dh * g_ref[...]
        dx_ref[...] = dxo_ref[...] + rr * (du - u * jnp.mean(du * u, axis=-1, keepdims=True))

    row = pl.BlockSpec((tm, D), lambda i: (i, 0))
    vec = pl.BlockSpec((1, D), lambda i: (0, 0))
    own, _ = _call(
        body, name="norm_bwd", grid=(S // tm,), in_specs=[row, row, vec, row], out_specs=[row, vec],
        out_shape=[jax.ShapeDtypeStruct((S, D), F32), jax.ShapeDtypeStruct((1, D), F32)],
        args=(dh, x, norm_g, dx_out))
    return own


def _row_block(rows, width, itemsize):
    target = max(16, (2 * 1024 * 1024) // (width * itemsize))
    for rb in range(min(rows, target), 0, -1):
        if rows % rb == 0 and (rb % 16 == 0 or rb == rows):
            return rb
    return rows


def _prefetch_call(body, *, name, grid, in_specs, out_specs, out_shape, args, aliases=None):
    grid_spec = pltpu.PrefetchScalarGridSpec(num_scalar_prefetch=1, grid=grid, in_specs=in_specs, out_specs=out_specs)
    return pl.pallas_call(
        body, name=name, grid_spec=grid_spec, out_shape=out_shape, input_output_aliases=aliases or {},
        compiler_params=pltpu.CompilerParams(dimension_semantics=("arbitrary",) * len(grid),
                                             vmem_limit_bytes=VMEM_LIMIT_V7X),
    )(*args)


def _place_shard(w, l, spec, chip):
    A, rows, width = spec.shard
    rb = _row_block(rows, width, 4)
    nrb = rows // rb
    if spec.kind == "col":
        out_map = lambda a, r, chip: (a, r, chip[0])
    else:
        out_map = lambda a, r, chip: (a, chip[0] * nrb + r, 0)

    def body(chip_ref, w_ref, o_ref):
        o_ref[...] = _bf(w_ref[...])

    return _prefetch_call(
        body, name="place_" + spec.name, grid=(A, nrb),
        in_specs=[pl.BlockSpec((None, rb, width), lambda a, r, chip: (l * A + a, r, 0))],
        out_specs=pl.BlockSpec((None, rb, width), out_map),
        out_shape=jax.ShapeDtypeStruct(spec.full, BF16), args=(chip, w))


def _pair_sum(grad, land, spec, core):
    A, hr, w = spec.half
    rb = _row_block(hr, w, 2)
    nrb = hr // rb
    if spec.kind == "col":
        g_spec = pl.BlockSpec((None, rb, w), lambda j, a, r, core: (a, core[0] * nrb + r, j))
    else:
        g_spec = pl.BlockSpec((None, rb, w), lambda j, a, r, core: (a, (j * 2 + core[0]) * nrb + r, 0))

    def body(core_ref, g_ref, l_ref, o_ref):
        o_ref[...] = _bf(_f32(g_ref[...]) + _f32(l_ref[...]))

    blk = pl.BlockSpec((None, None, rb, w), lambda j, a, r, core: (j, a, r, 0))
    return _prefetch_call(
        body, name="pair_sum_" + spec.name, grid=(N_CHIPS, A, nrb), in_specs=[g_spec, blk], out_specs=blk,
        out_shape=jax.ShapeDtypeStruct((N_CHIPS,) + spec.half, BF16), args=(core, grad, land))


def _chip_sum(pair_sum, land, spec, chip_core):
    A, hr, w = spec.half
    rb = _row_block(hr, w, 4)
    nrb = hr // rb

    def body(cc_ref, p_ref, l0_ref, l1_ref, l2_ref, o_ref):
        o_ref[...] = ((_f32(p_ref[...]) + _f32(l0_ref[...])) + _f32(l1_ref[...])) + _f32(l2_ref[...])

    own = pl.BlockSpec((None, None, rb, w), lambda a, r, cc: (cc[0], a, r, 0))
    slot = lambda p: pl.BlockSpec((None, None, rb, w), lambda a, r, cc: (p, a, r, 0))
    return _prefetch_call(
        body, name="chip_sum_" + spec.name, grid=(A, nrb), in_specs=[own, slot(0), slot(1), slot(2)],
        out_specs=pl.BlockSpec((None, rb, w), lambda a, r, cc: (a, cc[1] * nrb + r, 0)),
        out_shape=jax.ShapeDtypeStruct(spec.shard, F32), args=(chip_core, pair_sum, land, land, land))


def _adamw_math(w, g, m, v):
    m = ADAM_B1 * m + (1.0 - ADAM_B1) * g
    v = ADAM_B2 * v + (1.0 - ADAM_B2) * (g * g)
    m_hat = m / (1.0 - ADAM_B1 ** ADAM_STEP)
    v_hat = v / (1.0 - ADAM_B2 ** ADAM_STEP)
    delta = -ADAM_LR * (m_hat / (jnp.sqrt(v_hat) + ADAM_EPS) + ADAM_WD * w)
    return delta, m, v


def _adamw(w, g, m, v, l, depth, spec, carried):
    A, R, C = spec.shard
    rb = _row_block(R, C, 4 * 4)
    stacked = pl.BlockSpec((None, rb, C), lambda a, r: (l * A + a, r, 0))
    n_carry = 0 if carried is None else 4

    def body(w_ref, g_ref, m_ref, v_ref, *rest):
        go_ref, d_ref, mo_ref, vo_ref = rest[n_carry:]
        g = g_ref[...]
        go_ref[...] = g
        d_ref[...], mo_ref[...], vo_ref[...] = _adamw_math(w_ref[...], g, m_ref[...], v_ref[...])

    return pl.pallas_call(
        body, name="adamw_" + spec.name, grid=(A, R // rb),
        in_specs=[stacked, pl.BlockSpec((None, rb, C), lambda a, r: (a, r, 0)), stacked, stacked] + [HBM_ANY] * n_carry,
        out_specs=[stacked] * 4,
        out_shape=[jax.ShapeDtypeStruct((depth * A, R, C), F32)] * 4,
        input_output_aliases={4 + i: i for i in range(n_carry)},
        compiler_params=pltpu.CompilerParams(dimension_semantics=("arbitrary", "arbitrary"),
                                             vmem_limit_bytes=VMEM_LIMIT_V7X),
    )(w, g, m, v, *(carried or ()))


def _small_update(g_part, w, m, v):
    R = g_part.shape[0]
    n_dev = 8

    def body(g_ref, w_ref, m_ref, v_ref, go_ref, d_ref, mo_ref, vo_ref, all_s, send_sem, recv_sem):
        x, y, c = _place()
        me = 4 * x + 2 * y + c
        all_s[me] = g_ref[...]
        cps = []
        for k in range(1, n_dev):
            peer = (x ^ ((k >> 2) & 1), y ^ ((k >> 1) & 1), c ^ (k & 1))
            cp = pltpu.make_async_remote_copy(src_ref=g_ref, dst_ref=all_s.at[me], send_sem=send_sem.at[k],
                                              recv_sem=recv_sem.at[k], device_id=peer, device_id_type=MESH)
            cp.start()
            cps.append(cp)
        for cp in cps:
            cp.wait()
        g = all_s[0]
        for d in range(1, n_dev):
            g = g + all_s[d]
        go_ref[...] = g
        d_ref[...], mo_ref[...], vo_ref[...] = _adamw_math(w_ref[...], g, m_ref[...], v_ref[...])

    vm = pl.BlockSpec(memory_space=pltpu.VMEM)
    return pl.pallas_call(
        body, name="small_update", in_specs=[vm] * 4, out_specs=[vm] * 4,
        out_shape=[jax.ShapeDtypeStruct((R, 128), F32)] * 4,
        scratch_shapes=[pltpu.VMEM((n_dev, R, 128), F32), pltpu.SemaphoreType.DMA((n_dev,)),
                        pltpu.SemaphoreType.DMA((n_dev,))],
        compiler_params=pltpu.CompilerParams(vmem_limit_bytes=VMEM_LIMIT_V7X),
    )(g_part, w, m, v)


def _pack_small(parts):
    flat = jnp.concatenate([p.reshape(-1) for p in parts])
    pad = (-flat.shape[0]) % 1024
    return jnp.pad(flat, (0, pad)).reshape(-1, 128)


def _unpack_small(packed, like):
    flat = packed.reshape(-1)
    out, at = [], 0
    for p in like:
        out.append(flat[at:at + p.size].reshape(p.shape))
        at += p.size
    return out


def kernel(x, norm_g, w_in, ret_decay_fwd, ret_decay_bwd, pool_w, pool_scale, attn_q_gain, attn_k_gain, attn_sink, w_ret, w_pool, w_att, w_out, loss_target, m_norm_g, m_w_in, m_ret_decay_fwd, m_ret_decay_bwd, m_pool_w, m_pool_scale, m_attn_q_gain, m_attn_k_gain, m_attn_sink, m_w_ret, m_w_pool, m_w_att, m_w_out, v_norm_g, v_w_in, v_ret_decay_fwd, v_ret_decay_bwd, v_pool_w, v_pool_scale, v_attn_q_gain, v_attn_k_gain, v_attn_sink, v_w_ret, v_w_pool, v_w_att, v_w_out):
    S, D = x.shape[1], x.shape[2]
    L = norm_g.shape[0]
    dm = _Dims(S, D, L)
    PD, BW, G = dm.PD, dm.BW, POOL_GROUPS
    xi, yi, ci = _place()
    chip = (2 * xi + yi).astype(jnp.int32).reshape(1)
    core = ci.astype(jnp.int32).reshape(1)
    chip_core = jnp.concatenate([chip, core])

    specs = [_Sharded("w_in", "col", 1, D, dm.INW), _Sharded("w_ret", "col", 1, BW, D), _Sharded("w_pool", "col", 1, BW, D),
             _Sharded("w_att", "col", 1, BW, D), _Sharded("w_out", "row", 1, D, D), _Sharded("pool_w", "row", G, PD, PD)]
    n_big = len(specs)
    big_w = [w_in, w_ret, w_pool, w_att, w_out, pool_w]
    big_m = [m_w_in, m_w_ret, m_w_pool, m_w_att, m_w_out, m_pool_w]
    big_v = [v_w_in, v_w_ret, v_w_pool, v_w_att, v_w_out, v_pool_w]
    stack3 = lambda a, s: a.reshape((L * s.shard[0],) + s.shard[1:])
    big_w3 = [stack3(a, s) for a, s in zip(big_w, specs)]
    big_m3 = [stack3(a, s) for a, s in zip(big_m, specs)]
    big_v3 = [stack3(a, s) for a, s in zip(big_v, specs)]

    W = [[_place_shard(big_w3[t], l, specs[t], chip) for t in range(n_big)] for l in range(L)]
    W[0] = _gather_first(W[0], specs)

    cosr, sinr = _ret_rope_tables(S, dm.RD)
    tabs = _att_tables(S)
    xl = x[0]
    saved = []
    for l in range(L):
        nxt = l + 1 < L
        ng = norm_g[l].reshape(1, D)
        qg, kg = attn_q_gain[l].reshape(1, ATT_HEAD_DIM), attn_k_gain[l].reshape(1, ATT_HEAD_DIM)
        psc = pool_scale[l].reshape(G, 1, PD)
        f_in, f_ret, f_pool, f_att, f_out, f_pw = W[l]
        (z, h), tk = _inproj(xl, ng, f_in, dm, tasks=[_gather_ici_task(W[l + 1][:1], specs[:1])] if nxt else ())
        if nxt:
            W[l + 1][:1] = tk[0][0]
        ya_pre = _ret_fwd(z, ret_decay_fwd[l], ret_decay_bwd[l], cosr, sinr, dm)
        yb_pre = _pool_fwd(z, f_pw, psc, dm)
        (yc_pre,), tk = _att_fwd(z, qg, kg, attn_sink[l], tabs, dm,
                                 tasks=[_gather_ici_task(W[l + 1][1:], specs[1:])] if nxt else ())
        if nxt:
            W[l + 1][1:] = tk[0][0]
        mg = z
        (x_next, ya, yb, yc, merged), tk = _merge_out(xl, mg, ya_pre, yb_pre, yc_pre, f_ret, f_pool, f_att, f_out, dm,
                                                      tasks=[_gather_d2d_task(W[l + 1], specs)] if nxt else ())
        if nxt:
            W[l + 1] = tk[0][0]
        saved.append((xl, z, h, mg, ya_pre, yb_pre, yc_pre, ya, yb, yc, merged, ng, qg, kg, psc))
        xl = x_next
    sq, dx = _loss_grad(xl, loss_target[0], dm)
    loss = lax.psum(sq[0, 0] * (0.5 / D), ("x", "y", "c"))

    grads = [None] * L
    lands = [None] * L
    pairs = [None] * L
    lands2 = [None] * L
    shards = [None] * L
    carried = [None] * n_big
    small = [None] * L

    def pair_sums(k):
        pairs[k] = [_pair_sum(g, ld, s, core) for g, ld, s in zip(grads[k], lands[k], specs)]

    def chip_sums(k):
        shards[k] = [_chip_sum(p, ld, s, chip_core) for p, ld, s in zip(pairs[k], lands2[k], specs)]

    def adamw(k):
        for t in range(n_big):
            carried[t] = _adamw(big_w3[t], shards[k][t], big_m3[t], big_v3[t], k, L, specs[t], carried[t])

    for l in reversed(range(L)):
        xl, z, h, mg, ya_pre, yb_pre, yc_pre, ya, yb, yc, merged, ng, qg, kg, psc = saved[l]
        f_in, f_ret, f_pool, f_att, f_out, f_pw = W[l]
        up1, up2 = l + 1 < L, l + 2 < L
        own, tk = _merge_bwd_gates(dx, mg, ya, yb, yc, f_out, dm, tasks=[_pair_task(grads[l + 1], specs)] if up1 else ())
        dz, dya, dyb, dyc, dxb = own
        if up1:
            lands[l + 1] = tk[0][1]
            pair_sums(l + 1)
        dya_pre, dyb_pre, dyc_pre = _merge_bwd_proj(dya, dyb, dyc, f_ret, f_pool, f_att, dm)
        g_out, _ = _grad_matmul(merged, dxb, "grad_w_out")
        g_ret, _ = _grad_matmul(ya_pre, dya, "grad_w_ret")
        g_pool, _ = _grad_matmul(yb_pre, dyb, "grad_w_pool")
        g_att, _ = _grad_matmul(yc_pre, dyc, "grad_w_att")
        last = l == 0
        in_rows = specs[0].half_rows
        cuts = [0, in_rows // 4, 3 * in_rows // 4, in_rows]
        tasks = [_chip_task(pairs[l + 1][1:], specs[1:])] if up1 else []
        if up1 and last:
            tasks.append(_chip_rows_task(pairs[1][0], None, specs[0], cuts[0], cuts[1]))
        (ddec, dz), tk = _ret_bwd(z, dz, dya_pre, ret_decay_fwd[l], ret_decay_bwd[l], cosr, sinr, dm, tasks=tasks)
        if up1:
            lands2[l + 1] = [None] + tk[0][1]
            land_in = tk[1][1][0] if last else None
        dpw, dps, dz = _pool_bwd(z, dz, dyb_pre, f_pw, psc, dm)
        tasks = [_half_task(shards[l + 2], specs)] if up2 else []
        if up1 and last:
            tasks.append(_chip_rows_task(pairs[1][0], land_in, specs[0], cuts[1], cuts[2]))
        if last:
            tasks.append(_pair_task([g_ret, g_pool, g_att, g_out, dpw], specs[1:]))
        (dsm, dz), tk = _att_bwd(z, dz, dyc_pre, qg, kg, attn_sink[l], tabs, dm, tasks=tasks)
        if up2:
            shards[l + 2] = tk[0][0]
            adamw(l + 2)
        if up1 and last:
            land_in = tk[1 if up2 else 0][0][0]
        if last:
            small_lands = tk[-1][1]
            small_pairs = [_pair_sum(g, ld, s, core)
                           for g, ld, s in zip([g_ret, g_pool, g_att, g_out, dpw], small_lands, specs[1:])]
        if not last:
            g_in, _ = _grad_matmul(h, dz, "grad_w_in")
            (dh,), tk = _inproj_bwd(dz, f_in, dm, tasks=[_chip_task(pairs[l + 1][:1], specs[:1])] if up1 else ())
            if up1:
                lands2[l + 1][0] = tk[0][1][0]
                chip_sums(l + 1)
            grads[l] = [g_in, g_ret, g_pool, g_att, g_out, dpw]
        else:
            tasks = [_chip_rows_task(pairs[1][0], land_in, specs[0], cuts[2], cuts[3])] if up1 else []
            tasks.append(_chip_task(small_pairs, specs[1:]))
            g_in, tk = _grad_matmul(h, dz, "grad_w_in", tasks=tasks)
            if up1:
                lands2[1][0] = tk[0][0][0]
                chip_sums(1)
            small_lands2 = tk[-1][1]
            (_, in_land), = _comm_only("grad_pair_exchange", [_pair_task([g_in], specs[:1])])
            pairs[0] = [_pair_sum(g_in, in_land[0], specs[0], core)] + small_pairs
            (dh,), tk = _inproj_bwd(dz, f_in, dm, tasks=[_chip_task(pairs[0][:1], specs[:1])])
            lands2[0] = tk[0][1] + small_lands2
            chip_sums(0)
        dx, dng = _norm_bwd(dh, xl, ng, dx, dm)
        small[l] = [dng.reshape(D), ddec[:, 0, 0], ddec[:, 1, 0], dps.reshape(BW), jnp.sum(dsm[:, 0, :], axis=0),
                    jnp.sum(dsm[:, 1, :], axis=0), dsm[:, 2:2 + ATT_GROUP, 0].reshape(dm.AH)]

    rest = [k for k in (1, 0) if k < L]
    done = _comm_only("grad_half_exchange", [_half_task(shards[k], specs) for k in rest])
    for k, (both, _) in zip(rest, done):
        shards[k] = both
        adamw(k)

    back = lambda a, like: a.reshape(like.shape)
    g_big, d_big, m_big, v_big = ([back(carried[t][i], big_w[t]) for t in range(n_big)] for i in range(4))

    small_g = [jnp.stack([small[l][i] for l in range(L)]) for i in range(7)]
    small_w = [norm_g, ret_decay_fwd, ret_decay_bwd, pool_scale, attn_q_gain, attn_k_gain, attn_sink]
    small_m = [m_norm_g, m_ret_decay_fwd, m_ret_decay_bwd, m_pool_scale, m_attn_q_gain, m_attn_k_gain, m_attn_sink]
    small_v = [v_norm_g, v_ret_decay_fwd, v_ret_decay_bwd, v_pool_scale, v_attn_q_gain, v_attn_k_gain, v_attn_sink]
    sg, sd, sm, sv = _small_update(_pack_small(small_g), _pack_small(small_w), _pack_small(small_m), _pack_small(small_v))
    g_sm, d_sm, m_sm, v_sm = (_unpack_small(a, small_w) for a in (sg, sd, sm, sv))

    def ordered(big, small_):
        return [small_[0], big[0], small_[1], small_[2], big[5], small_[3], small_[4], small_[5], small_[6],
                big[1], big[2], big[3], big[4]]

    return (loss, dx[None], *ordered(g_big, g_sm), *ordered(d_big, d_sm), *ordered(m_big, m_sm),
            *ordered(v_big, v_sm))
```

```python
import jax
import jax.numpy as jnp
from jax import lax
from jax.experimental import pallas as pl
from jax.experimental.pallas import tpu as pltpu

F32 = jnp.float32
BF16 = jnp.bfloat16
MESH = pl.DeviceIdType.MESH

RMS_EPS = 1e-6
NEG_BIG = -1e30
CHUNK = 128
RET_HEADS = 4
POOL_GROUPS = 4
ATT_HEAD_DIM = 128
ATT_GROUP = 4
ROPE_DIMS = 32
RET_ROPE_BASE = 10000.0
ROPE_THETA = 500000.0
N_BRANCHES = 3
N_CHIPS = 4

ADAM_LR = 0.001
ADAM_B1 = 0.9
ADAM_B2 = 0.999
ADAM_EPS = 1e-08
ADAM_WD = 0.01
ADAM_STEP = 10

VMEM_LIMIT_V7X = 56 * 1024 * 1024

TN = (((0,), (0,)), ((), ()))
NT = (((1,), (1,)), ((), ()))

RET_HEADS_PER_STEP = 2
LOOP_UNROLL = 2
HBM_ANY = pl.BlockSpec(memory_space=pl.ANY)
ONE_BUFFER = pl.Buffered(1)


def _sigmoid(x):
    return 1.0 / (1.0 + jnp.exp(-x))


def _silu(x):
    return x * _sigmoid(x)


def _dsilu(x):
    s = _sigmoid(x)
    return s * (1.0 + x * (1.0 - s))


def _dot(a, b):
    return jnp.dot(a, b, preferred_element_type=F32)


def _dot_tn(a, b):
    return lax.dot_general(a, b, TN, preferred_element_type=F32)


def _dot_nt(a, b):
    return lax.dot_general(a, b, NT, preferred_element_type=F32)


def _bf(x):
    return x.astype(BF16)


def _f32(x):
    return x.astype(F32)


def _loop(n, body, init):
    def several(i, carry):
        for u in range(LOOP_UNROLL):
            carry = body(i * LOOP_UNROLL + u, carry)
        return carry
    return lax.fori_loop(0, n // LOOP_UNROLL, several, init)


def _sum11(x):
    return jnp.sum(jnp.sum(x, axis=1, keepdims=True), axis=0, keepdims=True)


class _Dims:
    def __init__(self, seq, d_model, depth):
        self.S, self.D, self.L = seq, d_model, depth
        bw = d_model // 2
        self.BW = bw
        self.RD = bw // RET_HEADS
        self.PD = bw // POOL_GROUPS
        self.AH = bw // ATT_HEAD_DIM
        self.AKV = self.AH // ATT_GROUP
        self.AKW = self.AKV * ATT_HEAD_DIM
        self.o_rq, self.o_rk, self.o_rv, self.o_rg = 0, bw, 2 * bw, 3 * bw
        self.o_pv, self.o_pg = 4 * bw, 5 * bw
        self.o_aq = 6 * bw
        self.o_ak = 7 * bw
        self.o_av = 7 * bw + self.AKW
        self.o_ag = 7 * bw + 2 * self.AKW
        self.o_mg = 8 * bw + 2 * self.AKW
        self.INW = self.o_mg + N_BRANCHES * d_model
        self.nC = seq // CHUNK
        self.tm = min(512, seq)


def _place():
    return lax.axis_index("x"), lax.axis_index("y"), lax.axis_index("c")


def _chip_peers(x, y):
    return [(1 - x, y), (x, 1 - y), (1 - x, 1 - y)]


class _Sharded:
    def __init__(self, name, kind, a, r, cc):
        self.name, self.kind, self.A, self.R, self.Cc = name, kind, a, r, cc
        self.full = (a, r, cc)
        if kind == "col":
            self.shard = (a, r, cc // N_CHIPS)
            self.half_rows = r // 2
        else:
            self.shard = (a, r // N_CHIPS, cc)
            self.half_rows = r // N_CHIPS // 2
        self.half = (a, self.half_rows, self.shard[2])

    def in_full(self, ref, chip, core=None):
        hr = self.half_rows
        if self.kind == "col":
            rows = pl.ds(0, self.R) if core is None else pl.ds(core * hr, hr)
            return ref.at[:, rows, pl.ds(chip * self.shard[2], self.shard[2])]
        rows = pl.ds(chip * self.shard[1], self.shard[1]) if core is None else pl.ds(chip * self.shard[1] + core * hr, hr)
        return ref.at[:, rows, :]

    def in_full_rows(self, ref, chip, core, lo, n):
        if self.kind == "col":
            return ref.at[:, pl.ds(core * self.half_rows + lo, n), pl.ds(chip * self.shard[2], self.shard[2])]
        return ref.at[:, pl.ds(chip * self.shard[1] + core * self.half_rows + lo, n), :]

    def in_shard(self, ref, core):
        return ref.at[:, pl.ds(core * self.half_rows, self.half_rows), :]


class _Task:
    def __init__(self, ro, rw, new, n_sem, copies):
        self.ro, self.rw, self.new, self.n_sem, self.copies = list(ro), list(rw), list(new), n_sem, copies


def _remote(src, dst, send_sem, recv_sem, k, device):
    def make():
        return pltpu.make_async_remote_copy(src_ref=src, dst_ref=dst, send_sem=send_sem.at[k], recv_sem=recv_sem.at[k],
                                            device_id=device, device_id_type=MESH)
    return make


def _gather_ici_task(bufs, specs):
    def copies(ro, rw, new, ss, rs):
        x, y, c = _place()
        me = 2 * x + y
        out = []
        for t, spec in enumerate(specs):
            for p, (px, py) in enumerate(_chip_peers(x, y)):
                mine = spec.in_full(rw[t], me, c)
                theirs = spec.in_full(rw[t], 2 * px + py, c)
                k = t * 3 + p
                out.append((_remote(mine, mine, ss, rs, k, (px, py, c)), _remote(mine, mine, ss, rs, k, (px, py, c)),
                            _remote(theirs, theirs, ss, rs, k, (px, py, c))))
        return out
    return _Task([], bufs, [], 3 * len(specs), copies)


def _gather_near_task(bufs, specs):
    def copies(ro, rw, new, ss, rs):
        x, y, c = _place()
        me = 2 * x + y
        out = []
        for t, spec in enumerate(specs):
            for p, (px, py) in enumerate(_chip_peers(x, y)[:2]):
                mine = spec.in_full(rw[t], me, c)
                theirs = spec.in_full(rw[t], 2 * px + py, c)
                k = t * 2 + p
                out.append((_remote(mine, mine, ss, rs, k, (px, py, c)), _remote(mine, mine, ss, rs, k, (px, py, c)),
                            _remote(theirs, theirs, ss, rs, k, (px, py, c))))
        return out
    return _Task([], bufs, [], 2 * len(specs), copies)


def _gather_relay_task(bufs, specs):
    def copies(ro, rw, new, ss, rs):
        x, y, c = _place()
        (xp, yp, dg) = _chip_peers(x, y)
        chip = lambda p: 2 * p[0] + p[1]
        out = []
        for t, spec in enumerate(specs):
            q = spec.half_rows // 2
            for p, (frm, to, lo) in enumerate(((yp, xp, 0), (xp, yp, q))):
                sent = spec.in_full_rows(rw[t], chip(frm), c, lo, q)
                got = spec.in_full_rows(rw[t], chip(dg), c, lo, q)
                k = t * 2 + p
                dev = (to[0], to[1], c)
                out.append((_remote(sent, sent, ss, rs, k, dev), _remote(sent, sent, ss, rs, k, dev),
                            _remote(got, got, ss, rs, k, dev)))
        return out
    return _Task([], bufs, [], 2 * len(specs), copies)


def _gather_d2d_task(bufs, specs):
    def copies(ro, rw, new, ss, rs):
        x, y, c = _place()
        sib = (x, y, 1 - c)
        out = []
        for t, spec in enumerate(specs):
            for p, (px, py) in enumerate(_chip_peers(x, y)):
                got = spec.in_full(rw[t], 2 * px + py, c)
                gets = spec.in_full(rw[t], 2 * px + py, 1 - c)
                k = t * 3 + p
                out.append((_remote(got, got, ss, rs, k, sib), _remote(got, got, ss, rs, k, sib),
                            _remote(gets, gets, ss, rs, k, sib)))
        return out
    return _Task([], bufs, [], 3 * len(specs), copies)


def _pair_task(grads, specs):
    def copies(ro, rw, new, ss, rs):
        x, y, c = _place()
        sib = (x, y, 1 - c)
        out = []
        for t, spec in enumerate(specs):
            for j in range(N_CHIPS):
                k = t * N_CHIPS + j
                cp = _remote(spec.in_full(ro[t], j, 1 - c), new[t].at[j], ss, rs, k, sib)
                out.append((cp, cp, cp))
        return out
    return _Task(grads, [], [jax.ShapeDtypeStruct((N_CHIPS,) + s.half, BF16) for s in specs], N_CHIPS * len(specs), copies)


def _chip_task(pair_sums, specs):
    def copies(ro, rw, new, ss, rs):
        x, y, c = _place()
        out = []
        for t in range(len(specs)):
            for p, (px, py) in enumerate(_chip_peers(x, y)):
                cp = _remote(ro[t].at[2 * px + py], new[t].at[p], ss, rs, t * 3 + p, (px, py, c))
                out.append((cp, cp, cp))
        return out
    return _Task(pair_sums, [], [jax.ShapeDtypeStruct((3,) + s.half, BF16) for s in specs], 3 * len(specs), copies)


def _chip_rows_task(pair_sum, land, spec, lo, hi):
    def copies(ro, rw, new, ss, rs):
        x, y, c = _place()
        dst = (rw or new)[0]
        out = []
        for p, (px, py) in enumerate(_chip_peers(x, y)):
            cp = _remote(ro[0].at[2 * px + py, :, pl.ds(lo, hi - lo), :], dst.at[p, :, pl.ds(lo, hi - lo), :],
                         ss, rs, p, (px, py, c))
            out.append((cp, cp, cp))
        return out
    first = land is None
    return _Task([pair_sum], [] if first else [land], [jax.ShapeDtypeStruct((3,) + spec.half, BF16)] if first else [],
                 3, copies)


def _half_task(shards, specs):
    def copies(ro, rw, new, ss, rs):
        x, y, c = _place()
        sib = (x, y, 1 - c)
        out = []
        for t, spec in enumerate(specs):
            mine, theirs = spec.in_shard(rw[t], c), spec.in_shard(rw[t], 1 - c)
            out.append((_remote(mine, mine, ss, rs, t, sib), _remote(mine, mine, ss, rs, t, sib),
                        _remote(theirs, theirs, ss, rs, t, sib)))
        return out
    return _Task([], shards, [], len(specs), copies)


def _call(body, *, name, grid, in_specs, out_specs, out_shape, args, scratch_shapes=(), tasks=(), carry=()):
    tasks = [t for t in tasks if t is not None]
    carry = list(carry)
    n_in, n_out, n_scr, n_carry = len(in_specs), len(out_specs), len(scratch_shapes), len(carry)
    ro = [a for t in tasks for a in t.ro]
    rw = [a for t in tasks for a in t.rw]
    new = [s for t in tasks for s in t.new]
    n_ro, n_rw, n_new = len(ro), len(rw), len(new)

    def wrapped(*refs):
        ins = refs[:n_in]
        ro_refs = refs[n_in + n_carry:n_in + n_carry + n_ro]
        at = n_in + n_carry + n_ro + n_rw
        outs = refs[at:at + n_out + n_carry]
        at = at + n_out + n_carry
        rw_refs = refs[at:at + n_rw]
        new_refs = refs[at + n_rw:at + n_rw + n_new]
        at = at + n_rw + n_new
        scr = refs[at:at + n_scr]
        sems = refs[at + n_scr:]

        def task_copies():
            found, a, b, d = [], 0, 0, 0
            for i, t in enumerate(tasks):
                found += t.copies(ro_refs[a:a + len(t.ro)], rw_refs[b:b + len(t.rw)], new_refs[d:d + len(t.new)],
                                  sems[2 * i], sems[2 * i + 1])
                a, b, d = a + len(t.ro), b + len(t.rw), d + len(t.new)
            return found

        if tasks:
            first = pl.program_id(0) == 0
            last = pl.program_id(0) == grid[0] - 1
            for ax in range(1, len(grid)):
                first = first & (pl.program_id(ax) == 0)
                last = last & (pl.program_id(ax) == grid[ax] - 1)

            @pl.when(first)
            def _():
                for cp, _, _ in task_copies():
                    cp().start()

        body(*ins, *outs, *scr)

        if tasks:
            @pl.when(last)
            def _():
                found = task_copies()
                for _, _, recv in found:
                    recv().wait_recv()
                for _, send, _ in found:
                    send().wait_send()

    sem_shapes = []
    for t in tasks:
        sem_shapes += [pltpu.SemaphoreType.DMA((t.n_sem,)), pltpu.SemaphoreType.DMA((t.n_sem,))]
    aliases = {n_in + i: n_out + i for i in range(n_carry)}
    aliases.update({n_in + n_carry + n_ro + i: n_out + n_carry + i for i in range(n_rw)})
    res = pl.pallas_call(
        wrapped, name=name, grid=grid,
        in_specs=list(in_specs) + [HBM_ANY] * (n_carry + n_ro + n_rw),
        out_specs=list(out_specs) + [HBM_ANY] * (n_carry + n_rw + n_new),
        out_shape=list(out_shape) + [jax.ShapeDtypeStruct(a.shape, a.dtype) for a in carry + rw] + new,
        scratch_shapes=list(scratch_shapes) + sem_shapes,
        input_output_aliases=aliases,
        compiler_params=pltpu.CompilerParams(dimension_semantics=("arbitrary",) * len(grid),
                                             vmem_limit_bytes=VMEM_LIMIT_V7X),
    )(*args, *carry, *ro, *rw)
    own, rest = list(res[:n_out + n_carry]), list(res[n_out + n_carry:])
    per_task, b, d = [], 0, n_rw
    for t in tasks:
        per_task.append((rest[b:b + len(t.rw)], rest[d:d + len(t.new)]))
        b, d = b + len(t.rw), d + len(t.new)
    return own, per_task


def _comm_only(name, tasks):
    ro = [a for t in tasks for a in t.ro]
    rw = [a for t in tasks for a in t.rw]
    new = [s for t in tasks for s in t.new]
    n_ro, n_rw, n_new = len(ro), len(rw), len(new)

    def body(*refs):
        ro_refs = refs[:n_ro]
        rw_refs = refs[n_ro + n_rw:n_ro + 2 * n_rw]
        new_refs = refs[n_ro + 2 * n_rw:n_ro + 2 * n_rw + n_new]
        sems = refs[n_ro + 2 * n_rw + n_new:]
        found, a, b, d = [], 0, 0, 0
        for i, t in enumerate(tasks):
            found += t.copies(ro_refs[a:a + len(t.ro)], rw_refs[b:b + len(t.rw)], new_refs[d:d + len(t.new)],
                              sems[2 * i], sems[2 * i + 1])
            a, b, d = a + len(t.ro), b + len(t.rw), d + len(t.new)
        for cp, _, _ in found:
            cp().start()
        for _, _, recv in found:
            recv().wait_recv()
        for _, send, _ in found:
            send().wait_send()

    sem_shapes = []
    for t in tasks:
        sem_shapes += [pltpu.SemaphoreType.DMA((t.n_sem,)), pltpu.SemaphoreType.DMA((t.n_sem,))]
    res = pl.pallas_call(
        body, name=name,
        in_specs=[HBM_ANY] * (n_ro + n_rw), out_specs=[HBM_ANY] * (n_rw + n_new),
        out_shape=[jax.ShapeDtypeStruct(a.shape, a.dtype) for a in rw] + new,
        scratch_shapes=sem_shapes,
        input_output_aliases={n_ro + i: i for i in range(n_rw)},
    )(*ro, *rw)
    res = list(res)
    per_task, b, d = [], 0, n_rw
    for t in tasks:
        per_task.append((res[b:b + len(t.rw)], res[d:d + len(t.new)]))
        b, d = b + len(t.rw), d + len(t.new)
    return per_task


def _gather_first(bufs, specs):
    n = len(specs)
    near, relay, d2d = _gather_near_task(bufs, specs), _gather_relay_task(bufs, specs), _gather_d2d_task(bufs, specs)

    def body(*refs):
        rw, sems = refs[n:2 * n], refs[2 * n:]
        nc = near.copies([], rw, [], sems[0], sems[1])
        rc = relay.copies([], rw, [], sems[2], sems[3])
        dc = d2d.copies([], rw, [], sems[4], sems[5])
        for start, _, _ in nc:
            start().start()
        for t in range(n):
            for p in range(2):
                nc[t * 2 + p][2]().wait_recv()
            for p in range(2):
                rc[t * 2 + p][0]().start()
                dc[t * 3 + p][0]().start()
        for t in range(n):
            for p in range(2):
                rc[t * 2 + p][2]().wait_recv()
            dc[t * 3 + 2][0]().start()
        for _, _, recv in dc:
            recv().wait_recv()
        for _, send, _ in nc + rc + dc:
            send().wait_send()

    sem_shapes = [pltpu.SemaphoreType.DMA((task.n_sem,)) for task in (near, relay, d2d) for _ in range(2)]
    return list(pl.pallas_call(
        body, name="gather_first", in_specs=[HBM_ANY] * n, out_specs=[HBM_ANY] * n,
        out_shape=[jax.ShapeDtypeStruct(a.shape, a.dtype) for a in bufs], scratch_shapes=sem_shapes,
        input_output_aliases={i: i for i in range(n)},
    )(*bufs))


def _inproj(x, norm_g, w_in, dm, tasks=()):
    S, D, N = dm.S, dm.D, dm.INW
    tm, tn = min(1024, S), 512

    def body(x_ref, g_ref, w_ref, z_ref, h_ref):
        @pl.when(pl.program_id(1) == 0)
        def _():
            xv = x_ref[...]
            r = lax.rsqrt(jnp.mean(xv * xv, axis=-1, keepdims=True) + RMS_EPS)
            h_ref[...] = _bf(xv * r * g_ref[...])

        z_ref[...] = _bf(_dot(h_ref[...], w_ref[...]))

    return _call(
        body, name="inproj", grid=(S // tm, N // tn),
        in_specs=[pl.BlockSpec((tm, D), lambda i, j: (i, 0)),
                  pl.BlockSpec((1, D), lambda i, j: (0, 0)),
                  pl.BlockSpec((None, D, tn), lambda i, j: (0, 0, j))],
        out_specs=[pl.BlockSpec((tm, tn), lambda i, j: (i, j)),
                   pl.BlockSpec((tm, D), lambda i, j: (i, 0))],
        out_shape=[jax.ShapeDtypeStruct((S, N), BF16), jax.ShapeDtypeStruct((S, D), BF16)],
        args=(x, norm_g, w_in), tasks=tasks)


def _rot_half(x, cs, sn):
    h = cs.shape[-1]
    x1, x2 = x[:, :h], x[:, h:]
    return jnp.concatenate([x1 * cs - x2 * sn, x2 * cs + x1 * sn], axis=-1)


def _ret_tables(af_ref, ab_ref):
    C = CHUNK
    lgf = -jnp.exp(af_ref[...])[:, :1]
    lgb = -jnp.exp(ab_ref[...])[:, :1]
    ri = lax.broadcasted_iota(jnp.int32, (C, C), 0)
    ci = lax.broadcasted_iota(jnp.int32, (C, C), 1)
    lag = _f32(ri - ci)
    alag = jnp.abs(lag)
    low = lag >= 0
    dmask = jnp.where(low, jnp.exp(lgf * alag), jnp.exp(lgb * alag))
    j = _f32(lax.broadcasted_iota(jnp.int32, (C, 1), 0))
    return dict(lgf=lgf, lgb=lgb, alag=alag, low=low, dmask=dmask, j=j,
                w_f=jnp.exp(lgf * (C - 1.0 - j)), w_b=jnp.exp(lgb * j),
                q_f=jnp.exp(lgf * (j + 1.0)), q_b=jnp.exp(lgb * (C - j)),
                dec_f=jnp.exp(lgf * C), dec_b=jnp.exp(lgb * C))


def _decay_rows(a):
    return jnp.broadcast_to(a.reshape(RET_HEADS, 1, 1), (RET_HEADS, 1, 128))


def _heads(x, width):
    return jnp.stack([x[:, b * width:(b + 1) * width] for b in range(x.shape[1] // width)])


def _bdot(a, b):
    return lax.dot_general(a, b, (((2,), (1,)), ((0,), (0,))), preferred_element_type=F32)


def _bdot_nt(a, b):
    return lax.dot_general(a, b, (((2,), (2,)), ((0,), (0,))), preferred_element_type=F32)


def _bdot_tn(a, b):
    return lax.dot_general(a, b, (((1,), (1,)), ((0,), (0,))), preferred_element_type=F32)


def _ret_tables_heads(af_ref, ab_ref):
    C = CHUNK
    lgf = -jnp.exp(af_ref[...])[:, :, :1]
    lgb = -jnp.exp(ab_ref[...])[:, :, :1]
    ri = lax.broadcasted_iota(jnp.int32, (1, C, C), 1)
    ci = lax.broadcasted_iota(jnp.int32, (1, C, C), 2)
    lag = _f32(ri - ci)
    alag = jnp.abs(lag)
    low = lag >= 0
    dmask = jnp.where(low, jnp.exp(lgf * alag), jnp.exp(lgb * alag))
    j = _f32(lax.broadcasted_iota(jnp.int32, (1, C, 1), 1))
    return dict(lgf=lgf, lgb=lgb, alag=alag, low=low, dmask=dmask, j=j,
                w_f=jnp.exp(lgf * (C - 1.0 - j)), w_b=jnp.exp(lgb * j),
                q_f=jnp.exp(lgf * (j + 1.0)), q_b=jnp.exp(lgb * (C - j)),
                dec_f=jnp.exp(lgf * C), dec_b=jnp.exp(lgb * C))


def _rot_half_heads(x, cs, sn):
    h = cs.shape[-1]
    x1, x2 = x[..., :h], x[..., h:]
    return jnp.concatenate([x1 * cs - x2 * sn, x2 * cs + x1 * sn], axis=-1)


def _ret_fwd(z, a_f, a_b, cosr, sinr, dm):
    S, RD, C, nC, HB = dm.S, dm.RD, CHUNK, dm.nC, RET_HEADS_PER_STEP
    W = HB * RD
    scale = RD ** -0.5

    def body(q_ref, k_ref, v_ref, g_ref, af_ref, ab_ref, cos_ref, sin_ref, o_ref, qh_s, kh_s, sf_s, st_s):
        t = _ret_tables_heads(af_ref, ab_ref)
        st_s[...] = jnp.zeros_like(st_s)

        def fwd_pass(n, carry):
            r = pl.ds(pl.multiple_of(n * C, C), C)
            cs, sn = cos_ref[r, :], sin_ref[r, :]
            qh = _rot_half_heads(_heads(_f32(q_ref[r, :]), RD), cs, sn)
            kh = _rot_half_heads(_heads(_f32(k_ref[r, :]), RD), cs, sn) * scale
            qh_s[:, r, :] = _bf(qh)
            kh_s[:, r, :] = _bf(kh)
            st = st_s[...]
            sf_s[n] = _bf(st)
            st_s[...] = st * t["dec_f"] + _bdot_tn(_bf(kh * t["w_f"]), _heads(v_ref[r, :], RD))
            return carry

        _loop(nC, fwd_pass, 0)
        st_s[...] = jnp.zeros_like(st_s)

        def bwd_pass(i, carry):
            n = nC - 1 - i
            r = pl.ds(pl.multiple_of(n * C, C), C)
            qhb, khb, vb = qh_s[:, r, :], kh_s[:, r, :], _heads(v_ref[r, :], RD)
            p = _bf(_bdot_nt(qhb, khb) * t["dmask"])
            qhf = _f32(qhb)
            sb = st_s[...]
            out = (_bdot(p, vb) + _bdot(_bf(qhf * t["q_f"]), sf_s[n]) + _bdot(_bf(qhf * t["q_b"]), _bf(sb)))
            y = out * lax.rsqrt(jnp.mean(out * out, axis=-1, keepdims=True) + RMS_EPS)
            gate = _silu(_f32(g_ref[r, :]))
            for b in range(HB):
                o_ref[r, b * RD:(b + 1) * RD] = _bf(y[b] * gate[:, b * RD:(b + 1) * RD])
            st_s[...] = sb * t["dec_b"] + _bdot_tn(_bf(_f32(khb) * t["w_b"]), vb)
            return carry

        _loop(nC, bwd_pass, 0)

    zs = lambda off: pl.BlockSpec((S, W), lambda h: (0, off // W + h), pipeline_mode=ONE_BUFFER)
    dec = pl.BlockSpec((HB, 1, 128), lambda h: (h, 0, 0))
    tab = pl.BlockSpec((S, RD // 2), lambda h: (0, 0), pipeline_mode=ONE_BUFFER)
    own, _ = _call(
        body, name="ret_fwd", grid=(RET_HEADS // HB,),
        in_specs=[zs(dm.o_rq), zs(dm.o_rk), zs(dm.o_rv), zs(dm.o_rg), dec, dec, tab, tab],
        out_specs=[pl.BlockSpec((S, W), lambda h: (0, h))],
        out_shape=[jax.ShapeDtypeStruct((S, dm.BW), BF16)],
        scratch_shapes=[pltpu.VMEM((HB, S, RD), BF16), pltpu.VMEM((HB, S, RD), BF16),
                        pltpu.VMEM((nC, HB, RD, RD), BF16), pltpu.VMEM((HB, RD, RD), F32)],
        args=(z, z, z, z, _decay_rows(a_f), _decay_rows(a_b), cosr, sinr))
    return own[0]


def _band(first_row, first_col, lo, hi, shape):
    r = lax.broadcasted_iota(jnp.int32, shape, 0) + first_row
    c = lax.broadcasted_iota(jnp.int32, shape, 1) + first_col
    d = c - r
    return jnp.where((d >= lo) & (d <= hi), 1.0, 0.0).astype(BF16)


def _pool_counts(first_row, half, S, rows):
    pos = lax.broadcasted_iota(jnp.int32, (rows, 1), 0) + first_row
    lo = jnp.clip(pos - half, 0, S)
    hi = jnp.clip(pos + half, 0, S)
    return 1.0 / _f32(hi - lo)


def _pool_rows(S):
    return min(4 * CHUNK, S // 2)


def _pool_win_start(n, rows, S):
    return pl.multiple_of(jnp.clip(n * rows - CHUNK, 0, S - (rows + 2 * CHUNK)), CHUNK)


def _win_start(n, S):
    return pl.multiple_of(jnp.clip((n - 1) * CHUNK, 0, S - 3 * CHUNK), CHUNK)


def _pool_fwd(z, pool_w, pool_scale, dm):
    S, PD = dm.S, dm.PD
    PB = _pool_rows(S)
    WIN = PB + 2 * CHUNK

    def body(u_ref, g_ref, w_ref, sc_ref, o_ref):
        half = jnp.left_shift(1, pl.program_id(0))
        w = w_ref[...]
        sc = sc_ref[...]

        def blk(n, carry):
            row0 = pl.multiple_of(n * PB, PB)
            r = pl.ds(row0, PB)
            st = _pool_win_start(n, PB, S)
            band = _band(row0, st, -half, half - 1, (PB, WIN))
            mean = _dot(band, u_ref[pl.ds(st, WIN), :]) * _pool_counts(row0, half, S, PB)
            p = mean - _f32(u_ref[r, :])
            y = _dot(_bf(p), w) * sc
            o_ref[r, :] = _bf(y * _silu(_f32(g_ref[r, :])))
            return carry

        _loop(S // PB, blk, 0)

    own, _ = _call(
        body, name="pool_fwd", grid=(POOL_GROUPS,),
        in_specs=[pl.BlockSpec((S, PD), lambda g: (0, dm.o_pv // PD + g)),
                  pl.BlockSpec((S, PD), lambda g: (0, dm.o_pg // PD + g)),
                  pl.BlockSpec((None, PD, PD), lambda g: (g, 0, 0)),
                  pl.BlockSpec((None, 1, PD), lambda g: (g, 0, 0))],
        out_specs=[pl.BlockSpec((S, PD), lambda g: (0, g))],
        out_shape=[jax.ShapeDtypeStruct((S, dm.BW), BF16)],
        args=(z, z, pool_w, pool_scale))
    return own[0]


def _rot_part(x, cs, s_up, s_dn):
    h = ROPE_DIMS // 2
    lanes = x.ndim - 1
    return x * cs + pltpu.roll(x, ATT_HEAD_DIM - h, lanes) * s_up + pltpu.roll(x, h, lanes) * s_dn


def _att_tables(S):
    h = ROPE_DIMS // 2
    inv = ROPE_THETA ** (-jnp.arange(h, dtype=F32) / h)
    ang = jnp.arange(S, dtype=F32)[:, None] * inv[None, :]
    cos, sin = jnp.cos(ang), jnp.sin(ang)
    pad = jnp.zeros((S, ATT_HEAD_DIM - 2 * h), F32)
    zero = jnp.zeros((S, h), F32)
    cs = jnp.concatenate([cos, cos, pad + 1.0], axis=1)
    s_up = jnp.concatenate([-sin, zero, pad], axis=1)
    s_dn = jnp.concatenate([zero, sin, pad], axis=1)
    return cs, s_up, s_dn


def _ret_rope_tables(S, RD):
    h = RD // 2
    inv = 1.0 / (RET_ROPE_BASE ** jnp.linspace(0.0, 1.0, h, dtype=F32))
    ang = jnp.arange(S, dtype=F32)[:, None] * inv[None, :]
    return jnp.cos(ang), jnp.sin(ang)


def _att_norm_rot(x, gain, cs, s_up, s_dn):
    u = x * lax.rsqrt(jnp.mean(x * x, axis=-1, keepdims=True) + RMS_EPS)
    return _rot_part(u * gain, cs, s_up, s_dn)


def _att_bias_tables(bias_s):
    C = CHUNK
    shape = (ATT_GROUP * C, 3 * C)
    rel = lax.broadcasted_iota(jnp.int32, shape, 1) - lax.broadcasted_iota(jnp.int32, shape, 0) % C
    for which, shift in enumerate((-C, 0, -2 * C)):
        bias_s[which] = jnp.where(jnp.abs(rel + shift) <= CHUNK, 0.0, NEG_BIG)


def _att_bias(bias_s, n, nC):
    return bias_s[jnp.where(n == 0, 1, jnp.where(n == nC - 1, 2, 0))]


def _att_probs(q4, kw, sink_col, bias):
    s = _dot_nt(q4, kw) * (ATT_HEAD_DIM ** -0.5) + bias
    m = jnp.maximum(jnp.max(s, axis=-1, keepdims=True), sink_col)
    e = jnp.exp(s - m)
    es = jnp.exp(sink_col - m)
    inv = 1.0 / (jnp.sum(e, axis=-1, keepdims=True) + es)
    return e * inv, es * inv


def _sink_col(sink_ref):
    head = lax.broadcasted_iota(jnp.int32, (ATT_GROUP * CHUNK, 1), 0) // CHUNK
    col = jnp.zeros((ATT_GROUP * CHUNK, 1), F32)
    for g in range(ATT_GROUP):
        col = jnp.where(head == g, sink_ref[g:g + 1, :1], col)
    return col


def _sink_rows(sink, dm):
    return jnp.broadcast_to(sink.reshape(dm.AKV, ATT_GROUP, 1), (dm.AKV, ATT_GROUP, 128))


def _att_prep(q_ref, k_ref, qg_ref, kg_ref, cs_ref, su_ref, sd_ref, qn_s, kn_s, nC):
    C, HD = CHUNK, ATT_HEAD_DIM

    def prep(n, carry):
        r = pl.ds(pl.multiple_of(n * C, C), C)
        cs, su, sd = cs_ref[r, :], su_ref[r, :], sd_ref[r, :]
        kn_s[r, :] = _bf(_att_norm_rot(_f32(k_ref[r, :]), kg_ref[...], cs, su, sd))
        qn_s[:, r, :] = _bf(_att_norm_rot(_heads(_f32(q_ref[r, :]), HD), qg_ref[...], cs, su, sd))
        return carry

    lax.fori_loop(0, nC, prep, 0)


def _att_fwd(z, q_gain, k_gain, sink, tabs, dm, tasks=()):
    S, C, nC, HD, G = dm.S, CHUNK, dm.nC, ATT_HEAD_DIM, ATT_GROUP
    GW = G * HD

    KV = dm.AKV

    def body(q_ref, k_ref, v_ref, g_ref, qg_ref, kg_ref, sk_ref, cs_ref, su_ref, sd_ref, o_ref, qn_s, kn_s, bias_s):
        def prep(n, carry):
            r = pl.ds(pl.multiple_of(n * C, C), C)
            cs, su, sd = cs_ref[r, :], su_ref[r, :], sd_ref[r, :]
            kn_s[:, r, :] = _bf(_att_norm_rot(_heads(_f32(k_ref[r, :]), HD), kg_ref[...], cs, su, sd))
            qn_s[:, r, :] = _bf(_att_norm_rot(_heads(_f32(q_ref[r, :]), HD), qg_ref[...], cs, su, sd))
            return carry

        lax.fori_loop(0, nC, prep, 0)
        _att_bias_tables(bias_s)
        sink_col = jnp.stack([_sink_col(sk_ref.at[h]) for h in range(KV)])

        def blk(n, carry):
            row0 = pl.multiple_of(n * C, C)
            r = pl.ds(row0, C)
            w = pl.ds(_win_start(n, S), 3 * C)
            q4 = qn_s[:, r, :].reshape(KV, G * C, HD)
            s = _bdot_nt(q4, kn_s[:, w, :]) * (HD ** -0.5) + _att_bias(bias_s, n, nC)
            m = jnp.maximum(jnp.max(s, axis=-1, keepdims=True), sink_col)
            e = jnp.exp(s - m)
            inv = 1.0 / (jnp.sum(e, axis=-1, keepdims=True) + jnp.exp(sink_col - m))
            o = _bdot(_bf(e * inv), _heads(v_ref[w, :], HD))
            gate = _silu(_f32(g_ref[r, :]))
            for h in range(KV):
                for g in range(G):
                    cols = slice((h * G + g) * HD, (h * G + g + 1) * HD)
                    o_ref[r, cols] = _bf(o[h, g * C:(g + 1) * C, :] * gate[:, cols])
            return carry

        _loop(nC, blk, 0)

    tab = pl.BlockSpec((S, HD), lambda h: (0, 0), pipeline_mode=ONE_BUFFER)
    gain = pl.BlockSpec((1, HD), lambda h: (0, 0))
    one = lambda off, width: pl.BlockSpec((pl.Element(S), pl.Element(width)), lambda h: (0, off),
                                          pipeline_mode=ONE_BUFFER)
    return _call(
        body, name="att_fwd", grid=(1,),
        in_specs=[one(dm.o_aq, dm.BW), one(dm.o_ak, dm.AKW), one(dm.o_av, dm.AKW), one(dm.o_ag, dm.BW),
                  gain, gain, pl.BlockSpec((KV, G, 128), lambda h: (0, 0, 0)), tab, tab, tab],
        out_specs=[pl.BlockSpec((S, dm.BW), lambda h: (0, 0), pipeline_mode=ONE_BUFFER)],
        out_shape=[jax.ShapeDtypeStruct((S, dm.BW), BF16)],
        scratch_shapes=[pltpu.VMEM((KV * G, S, HD), BF16), pltpu.VMEM((KV, S, HD), BF16),
                        pltpu.VMEM((3, G * C, 3 * C), F32)],
        args=(z, z, z, z, q_gain, k_gain, _sink_rows(sink, dm), *tabs), tasks=tasks)


def _merge_gate_spec(tm, dm):
    return pl.BlockSpec((pl.Element(tm), pl.Element(N_BRANCHES * dm.D)), lambda i: (i * tm, dm.o_mg))


def _merge_out(x, mg, ya_pre, yb_pre, yc_pre, w_ret, w_pool, w_att, w_out, dm, tasks=()):
    S, D, BW = dm.S, dm.D, dm.BW
    tm = min(256, S)

    def body(x_ref, mg_ref, a_ref, b_ref, c_ref, wr_ref, wp_ref, wa_ref, wo_ref, xo_ref, ya_ref, yb_ref, yc_ref, m_ref):
        ya = _dot(a_ref[...], wr_ref[...])
        yb = _dot(b_ref[...], wp_ref[...])
        yc = _dot(c_ref[...], wa_ref[...])
        ya_ref[...], yb_ref[...], yc_ref[...] = _bf(ya), _bf(yb), _bf(yc)
        g0 = _sigmoid(_f32(mg_ref[:, 0:D]))
        g1 = _sigmoid(_f32(mg_ref[:, D:2 * D]))
        g2 = _sigmoid(_f32(mg_ref[:, 2 * D:3 * D]))
        merged = _bf(g0 * ya + g1 * yb + g2 * yc)
        m_ref[...] = merged
        xo_ref[...] = x_ref[...] + _dot(merged, wo_ref[...])

    act = lambda w: pl.BlockSpec((tm, w), lambda i: (i, 0))
    wsp = lambda r: pl.BlockSpec((None, r, D), lambda i: (0, 0, 0), pipeline_mode=ONE_BUFFER)
    return _call(
        body, name="merge_out", grid=(S // tm,),
        in_specs=[act(D), _merge_gate_spec(tm, dm), act(BW), act(BW), act(BW), wsp(BW), wsp(BW), wsp(BW), wsp(D)],
        out_specs=[act(D)] * 5,
        out_shape=[jax.ShapeDtypeStruct((S, D), F32)] + [jax.ShapeDtypeStruct((S, D), BF16)] * 4,
        args=(x, mg, ya_pre, yb_pre, yc_pre, w_ret, w_pool, w_att, w_out), tasks=tasks)


def _loss_grad(y, target, dm):
    S, D = dm.S, dm.D
    tm = dm.tm

    def body(y_ref, t_ref, s_ref, d_ref):
        @pl.when(pl.program_id(0) == 0)
        def _():
            s_ref[...] = jnp.zeros_like(s_ref)

        e = y_ref[...] - t_ref[...]
        d_ref[...] = e * (1.0 / D)
        s_ref[...] += jnp.sum(jnp.sum(e * e, axis=-1, keepdims=True), axis=0, keepdims=True)

    row = pl.BlockSpec((tm, D), lambda i: (i, 0))
    own, _ = _call(
        body, name="loss_grad", grid=(S // tm,), in_specs=[row, row],
        out_specs=[pl.BlockSpec((1, 128), lambda i: (0, 0)), row],
        out_shape=[jax.ShapeDtypeStruct((1, 128), F32), jax.ShapeDtypeStruct((S, D), F32)],
        args=(y, target))
    return own


def _merge_bwd_gates(dx, mg, ya, yb, yc, w_out, dm, tasks=()):
    S, D = dm.S, dm.D
    tm = min(256, S)

    def body(dx_ref, mg_ref, ya_ref, yb_ref, yc_ref, wo_ref, dmg_ref, dya_ref, dyb_ref, dyc_ref, dxb_ref):
        dxb = _bf(dx_ref[...])
        dxb_ref[...] = dxb
        dm_ = _dot_nt(dxb, wo_ref[...])
        for k, (y_ref, dy_ref) in enumerate(((ya_ref, dya_ref), (yb_ref, dyb_ref), (yc_ref, dyc_ref))):
            g = _sigmoid(_f32(mg_ref[:, k * D:(k + 1) * D]))
            dmg_ref[:, k * D:(k + 1) * D] = _bf(dm_ * _f32(y_ref[...]) * g * (1.0 - g))
            dy_ref[...] = _bf(dm_ * g)

    act = lambda w: pl.BlockSpec((tm, w), lambda i: (i, 0))
    return _call(
        body, name="merge_bwd_gates", grid=(S // tm,),
        in_specs=[act(D), _merge_gate_spec(tm, dm), act(D), act(D), act(D),
                  pl.BlockSpec((None, D, D), lambda i: (0, 0, 0), pipeline_mode=ONE_BUFFER)],
        out_specs=[_merge_gate_spec(tm, dm), act(D), act(D), act(D), act(D)],
        out_shape=[jax.ShapeDtypeStruct((S, dm.INW), BF16)] + [jax.ShapeDtypeStruct((S, D), BF16)] * 4,
        args=(dx, mg, ya, yb, yc, w_out), tasks=tasks)


def _merge_bwd_proj(dya, dyb, dyc, w_ret, w_pool, w_att, dm):
    S, D, BW = dm.S, dm.D, dm.BW
    tm = dm.tm

    def body(da_ref, db_ref, dc_ref, wr_ref, wp_ref, wa_ref, oa_ref, ob_ref, oc_ref):
        oa_ref[...] = _bf(_dot_nt(da_ref[...], wr_ref[...]))
        ob_ref[...] = _bf(_dot_nt(db_ref[...], wp_ref[...]))
        oc_ref[...] = _bf(_dot_nt(dc_ref[...], wa_ref[...]))

    act = lambda w: pl.BlockSpec((tm, w), lambda i: (i, 0))
    wsp = pl.BlockSpec((None, BW, D), lambda i: (0, 0, 0), pipeline_mode=ONE_BUFFER)
    own, _ = _call(
        body, name="merge_bwd_proj", grid=(S // tm,),
        in_specs=[act(D)] * 3 + [wsp] * 3, out_specs=[act(BW)] * 3,
        out_shape=[jax.ShapeDtypeStruct((S, BW), BF16)] * 3,
        args=(dya, dyb, dyc, w_ret, w_pool, w_att))
    return own


def _col_writes(dz_ref, sem, step, parts):
    return [pltpu.make_async_copy(src, dz_ref.at[:, pl.ds(pl.multiple_of(off + step * w, 128), w)], sem.at[k])
            for k, (src, off, w) in enumerate(parts)]


def _ret_bwd(z, dz, dya_pre, a_f, a_b, cosr, sinr, dm, tasks=()):
    S, RD, C, nC = dm.S, dm.RD, CHUNK, dm.nC
    scale = RD ** -0.5
    H = RET_HEADS

    def body(q_ref, k_ref, v_ref, g_ref, dy_ref, af_ref, ab_ref, cos_ref, sin_ref, dd_ref, dz_ref,
             qh_s, kh_s, sf_s, sb_s, do_s, dqh_s, dkh_s, dv_s, st_s, lam_s, dq_ref, dk_ref, dv_ref, dg_ref, wsem,
             acc_dd, acc_tf, acc_tb, acc_uf, acc_ub, acc_lf, acc_lb):
        head = pl.program_id(0)
        writes = lambda step: _col_writes(dz_ref, wsem, step, [(dq_ref, dm.o_rq, RD), (dk_ref, dm.o_rk, RD),
                                                               (dv_ref, dm.o_rv, RD), (dg_ref, dm.o_rg, RD)])
        t = _ret_tables(af_ref, ab_ref)
        zero_rr = jnp.zeros((RD, RD), F32)

        st_s[...] = zero_rr

        def pass0(n, carry):
            r = pl.ds(pl.multiple_of(n * C, C), C)
            cs, sn = cos_ref[r, :], sin_ref[r, :]
            qh = _rot_half(_f32(q_ref[r, :]), cs, sn)
            kh = _rot_half(_f32(k_ref[r, :]), cs, sn) * scale
            qh_s[r, :] = _bf(qh)
            kh_s[r, :] = _bf(kh)
            st = st_s[...]
            sf_s[n] = _bf(st)
            st_s[...] = st * t["dec_f"] + _dot_tn(_bf(kh * t["w_f"]), v_ref[r, :])
            return carry

        _loop(nC, pass0, 0)

        @pl.when(head > 0)
        def _():
            for cp in writes(head - 1):
                cp.wait()

        st_s[...] = zero_rr
        lam_s[...] = zero_rr

        def pass1(i, carry):
            n = nC - 1 - i
            r = pl.ds(pl.multiple_of(n * C, C), C)
            qhb, khb, vb = qh_s[r, :], kh_s[r, :], v_ref[r, :]
            qhf, khf = _f32(qhb), _f32(khb)
            sc_ = _dot_nt(qhb, khb)
            p = _bf(sc_ * t["dmask"])
            sb = st_s[...]
            sbb = _bf(sb)
            sb_s[n] = sbb
            st_s[...] = sb * t["dec_b"] + _dot_tn(_bf(khf * t["w_b"]), vb)
            sfb = sf_s[n]
            qf_b, qb_b = _bf(qhf * t["q_f"]), _bf(qhf * t["q_b"])
            out = _dot(p, vb) + _dot(qf_b, sfb) + _dot(qb_b, sbb)
            rr = lax.rsqrt(jnp.mean(out * out, axis=-1, keepdims=True) + RMS_EPS)
            y = out * rr
            g = _f32(g_ref[r, :])
            dya = _f32(dy_ref[r, :])
            dg_ref[r, :] = _bf(dya * y * _dsilu(g))
            dyn = dya * _silu(g)
            dout = rr * (dyn - y * jnp.mean(dyn * y, axis=-1, keepdims=True))
            dob = _bf(dout)
            do_s[r, :] = dob
            dp = _dot_nt(dob, vb)
            dv = _dot_tn(p, dob)
            ds = _bf(dp * t["dmask"])
            dqh = _dot(ds, khb)
            dkh = _dot_tn(ds, qhb)
            acc_dd[...] += dp * sc_
            tf = _dot_nt(dob, sfb)
            tb = _dot_nt(dob, sbb)
            dqh = dqh + tf * t["q_f"] + tb * t["q_b"]
            acc_tf[...] += tf * qhf
            acc_tb[...] += tb * qhf
            lam = lam_s[...]
            lamb = _bf(lam)
            acc_lf[...] += lam * _f32(sfb)
            u = _dot_nt(vb, lamb)
            dkh = dkh + u * t["w_f"]
            acc_uf[...] += u * khf
            dv = dv + _dot(_bf(khf * t["w_f"]), lamb)
            lam_s[...] = _dot_tn(qf_b, dob) + lam * t["dec_f"]
            dqh_s[r, :] = dqh
            dkh_s[r, :] = dkh
            dv_s[r, :] = dv
            return carry

        for acc in (acc_dd, acc_tf, acc_tb, acc_uf, acc_ub, acc_lf, acc_lb):
            acc[...] = jnp.zeros_like(acc)
        _loop(nC, pass1, 0)

        lam_s[...] = zero_rr

        def pass2(n, carry):
            r = pl.ds(pl.multiple_of(n * C, C), C)
            qhb, khb, vb, dob = qh_s[r, :], kh_s[r, :], v_ref[r, :], do_s[r, :]
            qhf, khf = _f32(qhb), _f32(khb)
            lam = lam_s[...]
            lamb = _bf(lam)
            acc_lb[...] += lam * _f32(sb_s[n])
            u = _dot_nt(vb, lamb)
            dkh = dkh_s[r, :] + u * t["w_b"]
            acc_ub[...] += u * khf
            dv = dv_s[r, :] + _dot(_bf(khf * t["w_b"]), lamb)
            lam_s[...] = _dot_tn(_bf(qhf * t["q_b"]), dob) + lam * t["dec_b"]
            cs, sn = cos_ref[r, :], sin_ref[r, :]
            dq_ref[r, :] = _bf(_rot_half(dqh_s[r, :], cs, -sn))
            dk_ref[r, :] = _bf(_rot_half(dkh * scale, cs, -sn))
            dv_ref[r, :] = _bf(dv)
            return carry

        _loop(nC, pass2, 0)
        rows = lambda acc: jnp.sum(acc[...], axis=-1, keepdims=True)
        dd = acc_dd[...] * t["dmask"] * t["alag"]
        glf = (_sum11(jnp.where(t["low"], dd, 0.0)) + _sum11(rows(acc_tf) * t["q_f"] * (t["j"] + 1.0))
               + _sum11(rows(acc_uf) * t["w_f"] * (C - 1.0 - t["j"])) + _sum11(acc_lf[...]) * t["dec_f"] * C)
        glb = (_sum11(jnp.where(t["low"], 0.0, dd)) + _sum11(rows(acc_tb) * t["q_b"] * (C - t["j"]))
               + _sum11(rows(acc_ub) * t["w_b"] * t["j"]) + _sum11(acc_lb[...]) * t["dec_b"] * C)
        row = lax.broadcasted_iota(jnp.int32, (8, 128), 0)
        da_f = glf * t["lgf"]
        da_b = glb * t["lgb"]
        dd_ref[...] = jnp.where(row == 0, da_f, jnp.where(row == 1, da_b, 0.0))
        for cp in writes(head):
            cp.start()

        @pl.when(head == H - 1)
        def _():
            for cp in writes(head):
                cp.wait()

    zs = lambda off: pl.BlockSpec((S, RD), lambda h: (0, off // RD + h), pipeline_mode=ONE_BUFFER)
    col = pl.BlockSpec((S, RD), lambda h: (0, h), pipeline_mode=ONE_BUFFER)
    dec = pl.BlockSpec((None, 1, 128), lambda h: (h, 0, 0))
    tab = pl.BlockSpec((S, RD // 2), lambda h: (0, 0), pipeline_mode=ONE_BUFFER)
    return _call(
        body, name="ret_bwd", grid=(H,),
        in_specs=[zs(dm.o_rq), zs(dm.o_rk), zs(dm.o_rv), zs(dm.o_rg), col, dec, dec, tab, tab],
        out_specs=[pl.BlockSpec((None, 8, 128), lambda h: (h, 0, 0))],
        out_shape=[jax.ShapeDtypeStruct((H, 8, 128), F32)],
        scratch_shapes=[pltpu.VMEM((S, RD), BF16), pltpu.VMEM((S, RD), BF16),
                        pltpu.VMEM((nC, RD, RD), BF16), pltpu.VMEM((nC, RD, RD), BF16),
                        pltpu.VMEM((S, RD), BF16),
                        pltpu.VMEM((S, RD), F32), pltpu.VMEM((S, RD), F32), pltpu.VMEM((S, RD), F32),
                        pltpu.VMEM((RD, RD), F32), pltpu.VMEM((RD, RD), F32)]
        + [pltpu.VMEM((S, RD), BF16)] * 4 + [pltpu.SemaphoreType.DMA((4,))]
        + [pltpu.VMEM((C, C), F32)] + [pltpu.VMEM((C, RD), F32)] * 4 + [pltpu.VMEM((RD, RD), F32)] * 2,
        args=(z, z, z, z, dya_pre, _decay_rows(a_f), _decay_rows(a_b), cosr, sinr), tasks=tasks, carry=[dz])


def _pool_bwd(z, dz, dyb_pre, pool_w, pool_scale, dm):
    S, PD = dm.S, dm.PD
    PB = _pool_rows(S)
    WIN = PB + 2 * CHUNK
    G = POOL_GROUPS

    def body(u_ref, g_ref, dy_ref, w_ref, sc_ref, dw_ref, dsc_ref, dz_ref, dp_s, dpc_s, dw_s, du_ref, dg_ref, wsem):
        group = pl.program_id(0)
        writes = lambda step: _col_writes(dz_ref, wsem, step, [(du_ref, dm.o_pv, PD), (dg_ref, dm.o_pg, PD)])
        half = jnp.left_shift(1, group)
        w = w_ref[...]
        sc = sc_ref[...]
        dw_s[...] = jnp.zeros_like(dw_s)
        dsc_ref[...] = jnp.zeros_like(dsc_ref)

        @pl.when(group > 0)
        def _():
            for cp in writes(group - 1):
                cp.wait()

        def blk1(n, carry):
            row0 = pl.multiple_of(n * PB, PB)
            r = pl.ds(row0, PB)
            st = _pool_win_start(n, PB, S)
            band = _band(row0, st, -half, half - 1, (PB, WIN))
            inv = _pool_counts(row0, half, S, PB)
            pb = _bf(_dot(band, u_ref[pl.ds(st, WIN), :]) * inv - _f32(u_ref[r, :]))
            ylin = _dot(pb, w)
            y = ylin * sc
            g = _f32(g_ref[r, :])
            dyb = _f32(dy_ref[r, :])
            dg_ref[r, :] = _bf(dyb * y * _dsilu(g))
            dy = dyb * _silu(g)
            dsc_ref[...] += jnp.sum(dy * ylin, axis=0, keepdims=True)
            dyl = _bf(dy * sc)
            dw_s[...] += _dot_tn(pb, dyl)
            dp = _dot_nt(dyl, w)
            dp_s[r, :] = dp
            dpc_s[r, :] = _bf(dp * inv)
            return carry

        _loop(S // PB, blk1, 0)
        dw_ref[...] = _bf(dw_s[...])

        def blk2(n, carry):
            row0 = pl.multiple_of(n * PB, PB)
            r = pl.ds(row0, PB)
            st = _pool_win_start(n, PB, S)
            band_t = _band(row0, st, -half + 1, half, (PB, WIN))
            du_ref[r, :] = _bf(_dot(band_t, dpc_s[pl.ds(st, WIN), :]) - dp_s[r, :])
            return carry

        _loop(S // PB, blk2, 0)
        for cp in writes(group):
            cp.start()

        @pl.when(group == G - 1)
        def _():
            for cp in writes(group):
                cp.wait()

    own, _ = _call(
        body, name="pool_bwd", grid=(G,),
        in_specs=[pl.BlockSpec((S, PD), lambda g: (0, dm.o_pv // PD + g)),
                  pl.BlockSpec((S, PD), lambda g: (0, dm.o_pg // PD + g)),
                  pl.BlockSpec((S, PD), lambda g: (0, g)),
                  pl.BlockSpec((None, PD, PD), lambda g: (g, 0, 0)),
                  pl.BlockSpec((None, 1, PD), lambda g: (g, 0, 0))],
        out_specs=[pl.BlockSpec((None, PD, PD), lambda g: (g, 0, 0)), pl.BlockSpec((None, 1, PD), lambda g: (g, 0, 0))],
        out_shape=[jax.ShapeDtypeStruct((G, PD, PD), BF16), jax.ShapeDtypeStruct((G, 1, PD), F32)],
        scratch_shapes=[pltpu.VMEM((S, PD), F32), pltpu.VMEM((S, PD), BF16), pltpu.VMEM((PD, PD), F32),
                        pltpu.VMEM((S, PD), BF16), pltpu.VMEM((S, PD), BF16), pltpu.SemaphoreType.DMA((2,))],
        args=(z, z, dyb_pre, pool_w, pool_scale), carry=[dz])
    return own


def _att_bwd(z, dz, dyc_pre, q_gain, k_gain, sink, tabs, dm, tasks=()):
    S, C, nC, HD, G = dm.S, CHUNK, dm.nC, ATT_HEAD_DIM, ATT_GROUP
    GW = G * HD
    scale = HD ** -0.5

    def body(q_ref, k_ref, v_ref, g_ref, dy_ref, qg_ref, kg_ref, sk_ref, cs_ref, su_ref, sd_ref,
             sm_ref, dz_ref, qn_s, kn_s, dqn_s, dkn_s, dv_s, bias_s, dq_ref, dk_ref, dv_ref, dg_ref, wsem):
        kv = pl.program_id(0)
        writes = lambda step: _col_writes(dz_ref, wsem, step, [(dq_ref, dm.o_aq, GW), (dk_ref, dm.o_ak, HD),
                                                               (dv_ref, dm.o_av, HD), (dg_ref, dm.o_ag, GW)])
        _att_prep(q_ref, k_ref, qg_ref, kg_ref, cs_ref, su_ref, sd_ref, qn_s, kn_s, nC)

        @pl.when(kv > 0)
        def _():
            for cp in writes(kv - 1):
                cp.wait()

        _att_bias_tables(bias_s)
        dkn_s[...] = jnp.zeros_like(dkn_s)
        dv_s[...] = jnp.zeros_like(dv_s)
        sink_col = _sink_col(sk_ref)

        def blk(n, dsink):
            row0 = pl.multiple_of(n * C, C)
            r = pl.ds(row0, C)
            w = pl.ds(_win_start(n, S), 3 * C)
            q4 = qn_s[:, r, :].reshape(G * C, HD)
            kw, vw = kn_s[w, :], v_ref[w, :]
            p, ps = _att_probs(q4, kw, sink_col, _att_bias(bias_s, n, nC))
            pb = _bf(p)
            o = _dot(pb, vw)
            do_parts = []
            for g in range(G):
                cols = slice(g * HD, (g + 1) * HD)
                gate = _f32(g_ref[r, cols])
                dy = _f32(dy_ref[r, cols])
                dg_ref[r, cols] = _bf(dy * o[g * C:(g + 1) * C, :] * _dsilu(gate))
                do_parts.append(dy * _silu(gate))
            dob = _bf(jnp.concatenate(do_parts, axis=0))
            dp = _dot_nt(dob, vw)
            drow = jnp.sum(p * dp, axis=-1, keepdims=True)
            ds = _bf(p * (dp - drow))
            dsink = dsink - ps * drow
            dqn_s[:, r, :] = (_dot(ds, kw) * scale).reshape(G, C, HD)
            dkn_s[w, :] += _dot_tn(ds, q4) * scale
            dv_s[w, :] += _dot_tn(pb, dob)
            return dsink

        dsink = _loop(nC, blk, jnp.zeros((G * C, 1), F32))

        def fin(n, acc):
            dqg, dkg = acc
            r = pl.ds(pl.multiple_of(n * C, C), C)
            cs, su, sd = cs_ref[r, :], su_ref[r, :], sd_ref[r, :]

            def norm_bwd(x, gain, dqn):
                rr = lax.rsqrt(jnp.mean(x * x, axis=-1, keepdims=True) + RMS_EPS)
                u = x * rr
                dw = _rot_part(dqn, cs, -su, -sd)
                du = dw * gain
                gsum = dw * u
                if gsum.ndim == 3:
                    gsum = jnp.sum(gsum, axis=0)
                return rr * (du - u * jnp.mean(du * u, axis=-1, keepdims=True)), jnp.sum(gsum, axis=0, keepdims=True)

            dk, gk = norm_bwd(_f32(k_ref[r, :]), kg_ref[...], dkn_s[r, :])
            dk_ref[r, :] = _bf(dk)
            dv_ref[r, :] = _bf(dv_s[r, :])
            dq, gq = norm_bwd(_heads(_f32(q_ref[r, :]), HD), qg_ref[...], dqn_s[:, r, :])
            for g in range(G):
                dq_ref[r, g * HD:(g + 1) * HD] = _bf(dq[g])
            return dqg + gq, dkg + gk

        dqg, dkg = lax.fori_loop(0, nC, fin, (jnp.zeros((1, HD), F32), jnp.zeros((1, HD), F32)))
        sm_ref[...] = jnp.zeros_like(sm_ref)
        sm_ref[0:1, :] = dqg
        sm_ref[1:2, :] = dkg
        for g in range(G):
            sm_ref[2 + g:3 + g, :] = jnp.broadcast_to(_sum11(dsink[g * C:(g + 1) * C, :]), (1, HD))
        for cp in writes(kv):
            cp.start()

        @pl.when(kv == dm.AKV - 1)
        def _():
            for cp in writes(kv):
                cp.wait()

    tab = pl.BlockSpec((S, HD), lambda h: (0, 0), pipeline_mode=ONE_BUFFER)
    gain = pl.BlockSpec((1, HD), lambda h: (0, 0))
    wide = lambda off: pl.BlockSpec((S, GW), lambda h: (0, off // GW + h), pipeline_mode=ONE_BUFFER)
    thin = lambda off: pl.BlockSpec((S, HD), lambda h: (0, off // HD + h), pipeline_mode=ONE_BUFFER)
    return _call(
        body, name="att_bwd", grid=(dm.AKV,),
        in_specs=[wide(dm.o_aq), thin(dm.o_ak), thin(dm.o_av), wide(dm.o_ag), wide(0), gain, gain,
                  pl.BlockSpec((None, G, 128), lambda h: (h, 0, 0)), tab, tab, tab],
        out_specs=[pl.BlockSpec((None, 8, 128), lambda h: (h, 0, 0))],
        out_shape=[jax.ShapeDtypeStruct((dm.AKV, 8, 128), F32)],
        scratch_shapes=[pltpu.VMEM((G, S, HD), BF16), pltpu.VMEM((S, HD), BF16),
                        pltpu.VMEM((G, S, HD), F32), pltpu.VMEM((S, HD), F32), pltpu.VMEM((S, HD), F32),
                        pltpu.VMEM((3, G * C, 3 * C), F32),
                        pltpu.VMEM((S, GW), BF16), pltpu.VMEM((S, HD), BF16), pltpu.VMEM((S, HD), BF16),
                        pltpu.VMEM((S, GW), BF16), pltpu.SemaphoreType.DMA((4,))],
        args=(z, z, z, z, dyc_pre, q_gain, k_gain, _sink_rows(sink, dm), *tabs), tasks=tasks, carry=[dz])


def _grad_matmul(a, b, name, tasks=()):
    S, M = a.shape
    N = b.shape[1]
    tm, tn = min(1024, M), min(512, N)

    def body(a_ref, b_ref, o_ref):
        o_ref[...] = _bf(_dot_tn(a_ref[...], b_ref[...]))

    own, tk = _call(
        body, name=name, grid=(M // tm, N // tn),
        in_specs=[pl.BlockSpec((S, tm), lambda i, j: (0, i)), pl.BlockSpec((S, tn), lambda i, j: (0, j))],
        out_specs=[pl.BlockSpec((None, tm, tn), lambda i, j: (0, i, j))],
        out_shape=[jax.ShapeDtypeStruct((1, M, N), BF16)],
        args=(a, b), tasks=tasks)
    return own[0], tk


def _inproj_bwd(dz, w_in, dm, tasks=()):
    S, D = dm.S, dm.D
    tm = dm.tm
    tk = dm.INW // N_CHIPS if (dm.INW // N_CHIPS) % 128 == 0 else 512
    nk = dm.INW // tk

    def body(dz_ref, w_ref, dh_ref):
        part = _dot_nt(dz_ref[...], w_ref[...])

        @pl.when(pl.program_id(1) == 0)
        def _():
            dh_ref[...] = part

        @pl.when(pl.program_id(1) != 0)
        def _():
            dh_ref[...] += part

    return _call(
        body, name="inproj_bwd", grid=(S // tm, nk),
        in_specs=[pl.BlockSpec((tm, tk), lambda i, k: (i, k)), pl.BlockSpec((None, D, tk), lambda i, k: (0, 0, k))],
        out_specs=[pl.BlockSpec((tm, D), lambda i, k: (i, 0))],
        out_shape=[jax.ShapeDtypeStruct((S, D), F32)],
        args=(dz, w_in), tasks=tasks)


def _norm_bwd(dh, x, norm_g, dx_out, dm):
    S, D = dm.S, dm.D
    tm = dm.tm

    def body(dh_ref, x_ref, g_ref, dxo_ref, dx_ref, dg_ref):
        @pl.when(pl.program_id(0) == 0)
        def _():
            dg_ref[...] = jnp.zeros_like(dg_ref)

        xv = x_ref[...]
        rr = lax.rsqrt(jnp.mean(xv * xv, axis=-1, keepdims=True) + RMS_EPS)
        u = xv * rr
        dh = dh_ref[...]
        dg_ref[...] += jnp.sum(dh * u, axis=0, keepdims=True)
        du = dh * g_ref[...]
        dx_ref[...] = dxo_ref[...] + rr * (du - u * jnp.mean(du * u, axis=-1, keepdims=True))

    row = pl.BlockSpec((tm, D), lambda i: (i, 0))
    vec = pl.BlockSpec((1, D), lambda i: (0, 0))
    own, _ = _call(
        body, name="norm_bwd", grid=(S // tm,), in_specs=[row, row, vec, row], out_specs=[row, vec],
        out_shape=[jax.ShapeDtypeStruct((S, D), F32), jax.ShapeDtypeStruct((1, D), F32)],
        args=(dh, x, norm_g, dx_out))
    return own


def _row_block(rows, width, itemsize):
    target = max(16, (2 * 1024 * 1024) // (width * itemsize))
    for rb in range(min(rows, target), 0, -1):
        if rows % rb == 0 and (rb % 16 == 0 or rb == rows):
            return rb
    return rows


def _prefetch_call(body, *, name, grid, in_specs, out_specs, out_shape, args, aliases=None):
    grid_spec = pltpu.PrefetchScalarGridSpec(num_scalar_prefetch=1, grid=grid, in_specs=in_specs, out_specs=out_specs)
    return pl.pallas_call(
        body, name=name, grid_spec=grid_spec, out_shape=out_shape, input_output_aliases=aliases or {},
        compiler_params=pltpu.CompilerParams(dimension_semantics=("arbitrary",) * len(grid),
                                             vmem_limit_bytes=VMEM_LIMIT_V7X),
    )(*args)


def _place_shard(w, l, spec, chip):
    A, rows, width = spec.shard
    rb = _row_block(rows, width, 4)
    nrb = rows // rb
    if spec.kind == "col":
        out_map = lambda a, r, chip: (a, r, chip[0])
    else:
        out_map = lambda a, r, chip: (a, chip[0] * nrb + r, 0)

    def body(chip_ref, w_ref, o_ref):
        o_ref[...] = _bf(w_ref[...])

    return _prefetch_call(
        body, name="place_" + spec.name, grid=(A, nrb),
        in_specs=[pl.BlockSpec((None, rb, width), lambda a, r, chip: (l * A + a, r, 0))],
        out_specs=pl.BlockSpec((None, rb, width), out_map),
        out_shape=jax.ShapeDtypeStruct(spec.full, BF16), args=(chip, w))


def _pair_sum(grad, land, spec, core):
    A, hr, w = spec.half
    rb = _row_block(hr, w, 2)
    nrb = hr // rb
    if spec.kind == "col":
        g_spec = pl.BlockSpec((None, rb, w), lambda j, a, r, core: (a, core[0] * nrb + r, j))
    else:
        g_spec = pl.BlockSpec((None, rb, w), lambda j, a, r, core: (a, (j * 2 + core[0]) * nrb + r, 0))

    def body(core_ref, g_ref, l_ref, o_ref):
        o_ref[...] = _bf(_f32(g_ref[...]) + _f32(l_ref[...]))

    blk = pl.BlockSpec((None, None, rb, w), lambda j, a, r, core: (j, a, r, 0))
    return _prefetch_call(
        body, name="pair_sum_" + spec.name, grid=(N_CHIPS, A, nrb), in_specs=[g_spec, blk], out_specs=blk,
        out_shape=jax.ShapeDtypeStruct((N_CHIPS,) + spec.half, BF16), args=(core, grad, land))


def _chip_sum(pair_sum, land, spec, chip_core):
    A, hr, w = spec.half
    rb = _row_block(hr, w, 4)
    nrb = hr // rb

    def body(cc_ref, p_ref, l0_ref, l1_ref, l2_ref, o_ref):
        o_ref[...] = ((_f32(p_ref[...]) + _f32(l0_ref[...])) + _f32(l1_ref[...])) + _f32(l2_ref[...])

    own = pl.BlockSpec((None, None, rb, w), lambda a, r, cc: (cc[0], a, r, 0))
    slot = lambda p: pl.BlockSpec((None, None, rb, w), lambda a, r, cc: (p, a, r, 0))
    return _prefetch_call(
        body, name="chip_sum_" + spec.name, grid=(A, nrb), in_specs=[own, slot(0), slot(1), slot(2)],
        out_specs=pl.BlockSpec((None, rb, w), lambda a, r, cc: (a, cc[1] * nrb + r, 0)),
        out_shape=jax.ShapeDtypeStruct(spec.shard, F32), args=(chip_core, pair_sum, land, land, land))


def _adamw_math(w, g, m, v):
    m = ADAM_B1 * m + (1.0 - ADAM_B1) * g
    v = ADAM_B2 * v + (1.0 - ADAM_B2) * (g * g)
    m_hat = m / (1.0 - ADAM_B1 ** ADAM_STEP)
    v_hat = v / (1.0 - ADAM_B2 ** ADAM_STEP)
    delta = -ADAM_LR * (m_hat / (jnp.sqrt(v_hat) + ADAM_EPS) + ADAM_WD * w)
    return delta, m, v


def _adamw(w, g, m, v, l, depth, spec, carried):
    A, R, C = spec.shard
    rb = _row_block(R, C, 4 * 4)
    stacked = pl.BlockSpec((None, rb, C), lambda a, r: (l * A + a, r, 0))
    n_carry = 0 if carried is None else 4

    def body(w_ref, g_ref, m_ref, v_ref, *rest):
        go_ref, d_ref, mo_ref, vo_ref = rest[n_carry:]
        g = g_ref[...]
        go_ref[...] = g
        d_ref[...], mo_ref[...], vo_ref[...] = _adamw_math(w_ref[...], g, m_ref[...], v_ref[...])

    return pl.pallas_call(
        body, name="adamw_" + spec.name, grid=(A, R // rb),
        in_specs=[stacked, pl.BlockSpec((None, rb, C), lambda a, r: (a, r, 0)), stacked, stacked] + [HBM_ANY] * n_carry,
        out_specs=[stacked] * 4,
        out_shape=[jax.ShapeDtypeStruct((depth * A, R, C), F32)] * 4,
        input_output_aliases={4 + i: i for i in range(n_carry)},
        compiler_params=pltpu.CompilerParams(dimension_semantics=("arbitrary", "arbitrary"),
                                             vmem_limit_bytes=VMEM_LIMIT_V7X),
    )(w, g, m, v, *(carried or ()))


def _small_update(g_part, w, m, v):
    R = g_part.shape[0]
    n_dev = 8

    def body(g_ref, w_ref, m_ref, v_ref, go_ref, d_ref, mo_ref, vo_ref, all_s, send_sem, recv_sem):
        x, y, c = _place()
        me = 4 * x + 2 * y + c
        all_s[me] = g_ref[...]
        cps = []
        for k in range(1, n_dev):
            peer = (x ^ ((k >> 2) & 1), y ^ ((k >> 1) & 1), c ^ (k & 1))
            cp = pltpu.make_async_remote_copy(src_ref=g_ref, dst_ref=all_s.at[me], send_sem=send_sem.at[k],
                                              recv_sem=recv_sem.at[k], device_id=peer, device_id_type=MESH)
            cp.start()
            cps.append(cp)
        for cp in cps:
            cp.wait()
        g = all_s[0]
        for d in range(1, n_dev):
            g = g + all_s[d]
        go_ref[...] = g
        d_ref[...], mo_ref[...], vo_ref[...] = _adamw_math(w_ref[...], g, m_ref[...], v_ref[...])

    vm = pl.BlockSpec(memory_space=pltpu.VMEM)
    return pl.pallas_call(
        body, name="small_update", in_specs=[vm] * 4, out_specs=[vm] * 4,
        out_shape=[jax.ShapeDtypeStruct((R, 128), F32)] * 4,
        scratch_shapes=[pltpu.VMEM((n_dev, R, 128), F32), pltpu.SemaphoreType.DMA((n_dev,)),
                        pltpu.SemaphoreType.DMA((n_dev,))],
        compiler_params=pltpu.CompilerParams(vmem_limit_bytes=VMEM_LIMIT_V7X),
    )(g_part, w, m, v)


def _pack_small(parts):
    flat = jnp.concatenate([p.reshape(-1) for p in parts])
    pad = (-flat.shape[0]) % 1024
    return jnp.pad(flat, (0, pad)).reshape(-1, 128)


def _unpack_small(packed, like):
    flat = packed.reshape(-1)
    out, at = [], 0
    for p in like:
        out.append(flat[at:at + p.size].reshape(p.shape))
        at += p.size
    return out


def kernel(x, norm_g, w_in, ret_decay_fwd, ret_decay_bwd, pool_w, pool_scale, attn_q_gain, attn_k_gain, attn_sink, w_ret, w_pool, w_att, w_out, loss_target, m_norm_g, m_w_in, m_ret_decay_fwd, m_ret_decay_bwd, m_pool_w, m_pool_scale, m_attn_q_gain, m_attn_k_gain, m_attn_sink, m_w_ret, m_w_pool, m_w_att, m_w_out, v_norm_g, v_w_in, v_ret_decay_fwd, v_ret_decay_bwd, v_pool_w, v_pool_scale, v_attn_q_gain, v_attn_k_gain, v_attn_sink, v_w_ret, v_w_pool, v_w_att, v_w_out):
    S, D = x.shape[1], x.shape[2]
    L = norm_g.shape[0]
    dm = _Dims(S, D, L)
    PD, BW, G = dm.PD, dm.BW, POOL_GROUPS
    xi, yi, ci = _place()
    chip = (2 * xi + yi).astype(jnp.int32).reshape(1)
    core = ci.astype(jnp.int32).reshape(1)
    chip_core = jnp.concatenate([chip, core])

    specs = [_Sharded("w_in", "col", 1, D, dm.INW), _Sharded("w_ret", "col", 1, BW, D), _Sharded("w_pool", "col", 1, BW, D),
             _Sharded("w_att", "col", 1, BW, D), _Sharded("w_out", "row", 1, D, D), _Sharded("pool_w", "row", G, PD, PD)]
    n_big = len(specs)
    big_w = [w_in, w_ret, w_pool, w_att, w_out, pool_w]
    big_m = [m_w_in, m_w_ret, m_w_pool, m_w_att, m_w_out, m_pool_w]
    big_v = [v_w_in, v_w_ret, v_w_pool, v_w_att, v_w_out, v_pool_w]
    stack3 = lambda a, s: a.reshape((L * s.shard[0],) + s.shard[1:])
    big_w3 = [stack3(a, s) for a, s in zip(big_w, specs)]
    big_m3 = [stack3(a, s) for a, s in zip(big_m, specs)]
    big_v3 = [stack3(a, s) for a, s in zip(big_v, specs)]

    W = [[_place_shard(big_w3[t], l, specs[t], chip) for t in range(n_big)] for l in range(L)]
    W[0] = _gather_first(W[0], specs)

    cosr, sinr = _ret_rope_tables(S, dm.RD)
    tabs = _att_tables(S)
    xl = x[0]
    saved = []
    for l in range(L):
        nxt = l + 1 < L
        ng = norm_g[l].reshape(1, D)
        qg, kg = attn_q_gain[l].reshape(1, ATT_HEAD_DIM), attn_k_gain[l].reshape(1, ATT_HEAD_DIM)
        psc = pool_scale[l].reshape(G, 1, PD)
        f_in, f_ret, f_pool, f_att, f_out, f_pw = W[l]
        (z, h), tk = _inproj(xl, ng, f_in, dm, tasks=[_gather_ici_task(W[l + 1][:1], specs[:1])] if nxt else ())
        if nxt:
            W[l + 1][:1] = tk[0][0]
        ya_pre = _ret_fwd(z, ret_decay_fwd[l], ret_decay_bwd[l], cosr, sinr, dm)
        yb_pre = _pool_fwd(z, f_pw, psc, dm)
        (yc_pre,), tk = _att_fwd(z, qg, kg, attn_sink[l], tabs, dm,
                                 tasks=[_gather_ici_task(W[l + 1][1:], specs[1:])] if nxt else ())
        if nxt:
            W[l + 1][1:] = tk[0][0]
        mg = z
        (x_next, ya, yb, yc, merged), tk = _merge_out(xl, mg, ya_pre, yb_pre, yc_pre, f_ret, f_pool, f_att, f_out, dm,
                                                      tasks=[_gather_d2d_task(W[l + 1], specs)] if nxt else ())
        if nxt:
            W[l + 1] = tk[0][0]
        saved.append((xl, z, h, mg, ya_pre, yb_pre, yc_pre, ya, yb, yc, merged, ng, qg, kg, psc))
        xl = x_next
    sq, dx = _loss_grad(xl, loss_target[0], dm)
    loss = lax.psum(sq[0, 0] * (0.5 / D), ("x", "y", "c"))

    grads = [None] * L
    lands = [None] * L
    pairs = [None] * L
    lands2 = [None] * L
    shards = [None] * L
    carried = [None] * n_big
    small = [None] * L

    def pair_sums(k):
        pairs[k] = [_pair_sum(g, ld, s, core) for g, ld, s in zip(grads[k], lands[k], specs)]

    def chip_sums(k):
        shards[k] = [_chip_sum(p, ld, s, chip_core) for p, ld, s in zip(pairs[k], lands2[k], specs)]

    def adamw(k):
        for t in range(n_big):
            carried[t] = _adamw(big_w3[t], shards[k][t], big_m3[t], big_v3[t], k, L, specs[t], carried[t])

    for l in reversed(range(L)):
        xl, z, h, mg, ya_pre, yb_pre, yc_pre, ya, yb, yc, merged, ng, qg, kg, psc = saved[l]
        f_in, f_ret, f_pool, f_att, f_out, f_pw = W[l]
        up1, up2 = l + 1 < L, l + 2 < L
        own, tk = _merge_bwd_gates(dx, mg, ya, yb, yc, f_out, dm, tasks=[_pair_task(grads[l + 1], specs)] if up1 else ())
        dz, dya, dyb, dyc, dxb = own
        if up1:
            lands[l + 1] = tk[0][1]
            pair_sums(l + 1)
        dya_pre, dyb_pre, dyc_pre = _merge_bwd_proj(dya, dyb, dyc, f_ret, f_pool, f_att, dm)
        g_out, _ = _grad_matmul(merged, dxb, "grad_w_out")
        g_ret, _ = _grad_matmul(ya_pre, dya, "grad_w_ret")
        g_pool, _ = _grad_matmul(yb_pre, dyb, "grad_w_pool")
        g_att, _ = _grad_matmul(yc_pre, dyc, "grad_w_att")
        last = l == 0
        in_rows = specs[0].half_rows
        cuts = [0, in_rows // 4, 3 * in_rows // 4, in_rows]
        tasks = [_chip_task(pairs[l + 1][1:], specs[1:])] if up1 else []
        if up1 and last:
            tasks.append(_chip_rows_task(pairs[1][0], None, specs[0], cuts[0], cuts[1]))
        (ddec, dz), tk = _ret_bwd(z, dz, dya_pre, ret_decay_fwd[l], ret_decay_bwd[l], cosr, sinr, dm, tasks=tasks)
        if up1:
            lands2[l + 1] = [None] + tk[0][1]
            land_in = tk[1][1][0] if last else None
        dpw, dps, dz = _pool_bwd(z, dz, dyb_pre, f_pw, psc, dm)
        tasks = [_half_task(shards[l + 2], specs)] if up2 else []
        if up1 and last:
            tasks.append(_chip_rows_task(pairs[1][0], land_in, specs[0], cuts[1], cuts[2]))
        if last:
            tasks.append(_pair_task([g_ret, g_pool, g_att, g_out, dpw], specs[1:]))
        (dsm, dz), tk = _att_bwd(z, dz, dyc_pre, qg, kg, attn_sink[l], tabs, dm, tasks=tasks)
        if up2:
            shards[l + 2] = tk[0][0]
            adamw(l + 2)
        if up1 and last:
            land_in = tk[1 if up2 else 0][0][0]
        if last:
            small_lands = tk[-1][1]
            small_pairs = [_pair_sum(g, ld, s, core)
                           for g, ld, s in zip([g_ret, g_pool, g_att, g_out, dpw], small_lands, specs[1:])]
        if not last:
            g_in, _ = _grad_matmul(h, dz, "grad_w_in")
            (dh,), tk = _inproj_bwd(dz, f_in, dm, tasks=[_chip_task(pairs[l + 1][:1], specs[:1])] if up1 else ())
            if up1:
                lands2[l + 1][0] = tk[0][1][0]
                chip_sums(l + 1)
            grads[l] = [g_in, g_ret, g_pool, g_att, g_out, dpw]
        else:
            tasks = [_chip_rows_task(pairs[1][0], land_in, specs[0], cuts[2], cuts[3])] if up1 else []
            tasks.append(_chip_task(small_pairs, specs[1:]))
            g_in, tk = _grad_matmul(h, dz, "grad_w_in", tasks=tasks)
            if up1:
                lands2[1][0] = tk[0][0][0]
                chip_sums(1)
            small_lands2 = tk[-1][1]
            (_, in_land), = _comm_only("grad_pair_exchange", [_pair_task([g_in], specs[:1])])
            pairs[0] = [_pair_sum(g_in, in_land[0], specs[0], core)] + small_pairs
            tasks = [_chip_task(pairs[0][:1], specs[:1])] + ([_half_task(shards[1], specs)] if up1 else [])
            (dh,), tk = _inproj_bwd(dz, f_in, dm, tasks=tasks)
            lands2[0] = tk[0][1] + small_lands2
            if up1:
                shards[1] = tk[1][0]
            chip_sums(0)
        dx, dng = _norm_bwd(dh, xl, ng, dx, dm)
        small[l] = [dng.reshape(D), ddec[:, 0, 0], ddec[:, 1, 0], dps.reshape(BW), jnp.sum(dsm[:, 0, :], axis=0),
                    jnp.sum(dsm[:, 1, :], axis=0), dsm[:, 2:2 + ATT_GROUP, 0].reshape(dm.AH)]

    ((shards[0], _),) = _comm_only("grad_half_exchange", [_half_task(shards[0], specs)])
    for k in (1, 0):
        if k < L:
            adamw(k)

    back = lambda a, like: a.reshape(like.shape)
    g_big, d_big, m_big, v_big = ([back(carried[t][i], big_w[t]) for t in range(n_big)] for i in range(4))

    small_g = [jnp.stack([small[l][i] for l in range(L)]) for i in range(7)]
    small_w = [norm_g, ret_decay_fwd, ret_decay_bwd, pool_scale, attn_q_gain, attn_k_gain, attn_sink]
    small_m = [m_norm_g, m_ret_decay_fwd, m_ret_decay_bwd, m_pool_scale, m_attn_q_gain, m_attn_k_gain, m_attn_sink]
    small_v = [v_norm_g, v_ret_decay_fwd, v_ret_decay_bwd, v_pool_scale, v_attn_q_gain, v_attn_k_gain, v_attn_sink]
    sg, sd, sm, sv = _small_update(_pack_small(small_g), _pack_small(small_w), _pack_small(small_m), _pack_small(small_v))
    g_sm, d_sm, m_sm, v_sm = (_unpack_small(a, small_w) for a in (sg, sd, sm, sv))

    def ordered(big, small_):
        return [small_[0], big[0], small_[1], small_[2], big[5], small_[3], small_[4], small_[5], small_[6],
                big[1], big[2], big[3], big[4]]

    return (loss, dx[None], *ordered(g_big, g_sm), *ordered(d_big, d_sm), *ordered(m_big, m_sm),
            *ordered(v_big, v_sm))
```

```python
import jax
import jax.numpy as jnp
from jax import lax
from jax.experimental import pallas as pl
from jax.experimental.pallas import tpu as pltpu

F32 = jnp.float32
BF16 = jnp.bfloat16
MESH = pl.DeviceIdType.MESH

RMS_EPS = 1e-6
NEG_BIG = -1e30
CHUNK = 128
RET_HEADS = 4
POOL_GROUPS = 4
ATT_HEAD_DIM = 128
ATT_GROUP = 4
ROPE_DIMS = 32
RET_ROPE_BASE = 10000.0
ROPE_THETA = 500000.0
N_BRANCHES = 3
N_CHIPS = 4

ADAM_LR = 0.001
ADAM_B1 = 0.9
ADAM_B2 = 0.999
ADAM_EPS = 1e-08
ADAM_WD = 0.01
ADAM_STEP = 10

VMEM_LIMIT_V7X = 56 * 1024 * 1024

TN = (((0,), (0,)), ((), ()))
NT = (((1,), (1,)), ((), ()))

RET_HEADS_PER_STEP = 2
LOOP_UNROLL = 2
HBM_ANY = pl.BlockSpec(memory_space=pl.ANY)
ONE_BUFFER = pl.Buffered(1)


def _sigmoid(x):
    return 1.0 / (1.0 + jnp.exp(-x))


def _silu(x):
    return x * _sigmoid(x)


def _dsilu(x):
    s = _sigmoid(x)
    return s * (1.0 + x * (1.0 - s))


def _dot(a, b):
    return jnp.dot(a, b, preferred_element_type=F32)


def _dot_tn(a, b):
    return lax.dot_general(a, b, TN, preferred_element_type=F32)


def _dot_nt(a, b):
    return lax.dot_general(a, b, NT, preferred_element_type=F32)


def _bf(x):
    return x.astype(BF16)


def _f32(x):
    return x.astype(F32)


def _loop(n, body, init):
    def several(i, carry):
        for u in range(LOOP_UNROLL):
            carry = body(i * LOOP_UNROLL + u, carry)
        return carry
    return lax.fori_loop(0, n // LOOP_UNROLL, several, init)


def _sum11(x):
    return jnp.sum(jnp.sum(x, axis=1, keepdims=True), axis=0, keepdims=True)


class _Dims:
    def __init__(self, seq, d_model, depth):
        self.S, self.D, self.L = seq, d_model, depth
        bw = d_model // 2
        self.BW = bw
        self.RD = bw // RET_HEADS
        self.PD = bw // POOL_GROUPS
        self.AH = bw // ATT_HEAD_DIM
        self.AKV = self.AH // ATT_GROUP
        self.AKW = self.AKV * ATT_HEAD_DIM
        self.o_rq, self.o_rk, self.o_rv, self.o_rg = 0, bw, 2 * bw, 3 * bw
        self.o_pv, self.o_pg = 4 * bw, 5 * bw
        self.o_aq = 6 * bw
        self.o_ak = 7 * bw
        self.o_av = 7 * bw + self.AKW
        self.o_ag = 7 * bw + 2 * self.AKW
        self.o_mg = 8 * bw + 2 * self.AKW
        self.INW = self.o_mg + N_BRANCHES * d_model
        self.nC = seq // CHUNK
        self.tm = min(512, seq)


def _place():
    return lax.axis_index("x"), lax.axis_index("y"), lax.axis_index("c")


def _chip_peers(x, y):
    return [(1 - x, y), (x, 1 - y), (1 - x, 1 - y)]


class _Sharded:
    def __init__(self, name, kind, a, r, cc):
        self.name, self.kind, self.A, self.R, self.Cc = name, kind, a, r, cc
        self.full = (a, r, cc)
        if kind == "col":
            self.shard = (a, r, cc // N_CHIPS)
            self.half_rows = r // 2
        else:
            self.shard = (a, r // N_CHIPS, cc)
            self.half_rows = r // N_CHIPS // 2
        self.half = (a, self.half_rows, self.shard[2])

    def in_full(self, ref, chip, core=None):
        hr = self.half_rows
        if self.kind == "col":
            rows = pl.ds(0, self.R) if core is None else pl.ds(core * hr, hr)
            return ref.at[:, rows, pl.ds(chip * self.shard[2], self.shard[2])]
        rows = pl.ds(chip * self.shard[1], self.shard[1]) if core is None else pl.ds(chip * self.shard[1] + core * hr, hr)
        return ref.at[:, rows, :]

    def in_full_rows(self, ref, chip, core, lo, n):
        if self.kind == "col":
            return ref.at[:, pl.ds(core * self.half_rows + lo, n), pl.ds(chip * self.shard[2], self.shard[2])]
        return ref.at[:, pl.ds(chip * self.shard[1] + core * self.half_rows + lo, n), :]

    def in_shard(self, ref, core):
        return ref.at[:, pl.ds(core * self.half_rows, self.half_rows), :]


class _Task:
    def __init__(self, ro, rw, new, n_sem, copies):
        self.ro, self.rw, self.new, self.n_sem, self.copies = list(ro), list(rw), list(new), n_sem, copies


def _remote(src, dst, send_sem, recv_sem, k, device):
    def make():
        return pltpu.make_async_remote_copy(src_ref=src, dst_ref=dst, send_sem=send_sem.at[k], recv_sem=recv_sem.at[k],
                                            device_id=device, device_id_type=MESH)
    return make


def _gather_ici_task(bufs, specs):
    def copies(ro, rw, new, ss, rs):
        x, y, c = _place()
        me = 2 * x + y
        out = []
        for t, spec in enumerate(specs):
            for p, (px, py) in enumerate(_chip_peers(x, y)):
                mine = spec.in_full(rw[t], me, c)
                theirs = spec.in_full(rw[t], 2 * px + py, c)
                k = t * 3 + p
                out.append((_remote(mine, mine, ss, rs, k, (px, py, c)), _remote(mine, mine, ss, rs, k, (px, py, c)),
                            _remote(theirs, theirs, ss, rs, k, (px, py, c))))
        return out
    return _Task([], bufs, [], 3 * len(specs), copies)


def _gather_near_task(bufs, specs):
    def copies(ro, rw, new, ss, rs):
        x, y, c = _place()
        me = 2 * x + y
        out = []
        for t, spec in enumerate(specs):
            for p, (px, py) in enumerate(_chip_peers(x, y)[:2]):
                mine = spec.in_full(rw[t], me, c)
                theirs = spec.in_full(rw[t], 2 * px + py, c)
                k = t * 2 + p
                out.append((_remote(mine, mine, ss, rs, k, (px, py, c)), _remote(mine, mine, ss, rs, k, (px, py, c)),
                            _remote(theirs, theirs, ss, rs, k, (px, py, c))))
        return out
    return _Task([], bufs, [], 2 * len(specs), copies)


def _gather_relay_task(bufs, specs):
    def copies(ro, rw, new, ss, rs):
        x, y, c = _place()
        (xp, yp, dg) = _chip_peers(x, y)
        chip = lambda p: 2 * p[0] + p[1]
        out = []
        for t, spec in enumerate(specs):
            q = spec.half_rows // 2
            for p, (frm, to, lo) in enumerate(((yp, xp, 0), (xp, yp, q))):
                sent = spec.in_full_rows(rw[t], chip(frm), c, lo, q)
                got = spec.in_full_rows(rw[t], chip(dg), c, lo, q)
                k = t * 2 + p
                dev = (to[0], to[1], c)
                out.append((_remote(sent, sent, ss, rs, k, dev), _remote(sent, sent, ss, rs, k, dev),
                            _remote(got, got, ss, rs, k, dev)))
        return out
    return _Task([], bufs, [], 2 * len(specs), copies)


def _gather_d2d_task(bufs, specs):
    def copies(ro, rw, new, ss, rs):
        x, y, c = _place()
        sib = (x, y, 1 - c)
        out = []
        for t, spec in enumerate(specs):
            for p, (px, py) in enumerate(_chip_peers(x, y)):
                got = spec.in_full(rw[t], 2 * px + py, c)
                gets = spec.in_full(rw[t], 2 * px + py, 1 - c)
                k = t * 3 + p
                out.append((_remote(got, got, ss, rs, k, sib), _remote(got, got, ss, rs, k, sib),
                            _remote(gets, gets, ss, rs, k, sib)))
        return out
    return _Task([], bufs, [], 3 * len(specs), copies)


def _pair_task(grads, specs):
    def copies(ro, rw, new, ss, rs):
        x, y, c = _place()
        sib = (x, y, 1 - c)
        out = []
        for t, spec in enumerate(specs):
            for j in range(N_CHIPS):
                k = t * N_CHIPS + j
                cp = _remote(spec.in_full(ro[t], j, 1 - c), new[t].at[j], ss, rs, k, sib)
                out.append((cp, cp, cp))
        return out
    return _Task(grads, [], [jax.ShapeDtypeStruct((N_CHIPS,) + s.half, BF16) for s in specs], N_CHIPS * len(specs), copies)


def _chip_task(pair_sums, specs):
    def copies(ro, rw, new, ss, rs):
        x, y, c = _place()
        out = []
        for t in range(len(specs)):
            for p, (px, py) in enumerate(_chip_peers(x, y)):
                cp = _remote(ro[t].at[2 * px + py], new[t].at[p], ss, rs, t * 3 + p, (px, py, c))
                out.append((cp, cp, cp))
        return out
    return _Task(pair_sums, [], [jax.ShapeDtypeStruct((3,) + s.half, BF16) for s in specs], 3 * len(specs), copies)


def _chip_rows_task(pair_sum, land, spec, lo, hi):
    def copies(ro, rw, new, ss, rs):
        x, y, c = _place()
        dst = (rw or new)[0]
        out = []
        for p, (px, py) in enumerate(_chip_peers(x, y)):
            cp = _remote(ro[0].at[2 * px + py, :, pl.ds(lo, hi - lo), :], dst.at[p, :, pl.ds(lo, hi - lo), :],
                         ss, rs, p, (px, py, c))
            out.append((cp, cp, cp))
        return out
    first = land is None
    return _Task([pair_sum], [] if first else [land], [jax.ShapeDtypeStruct((3,) + spec.half, BF16)] if first else [],
                 3, copies)


def _half_task(shards, specs):
    def copies(ro, rw, new, ss, rs):
        x, y, c = _place()
        sib = (x, y, 1 - c)
        out = []
        for t, spec in enumerate(specs):
            mine, theirs = spec.in_shard(rw[t], c), spec.in_shard(rw[t], 1 - c)
            out.append((_remote(mine, mine, ss, rs, t, sib), _remote(mine, mine, ss, rs, t, sib),
                        _remote(theirs, theirs, ss, rs, t, sib)))
        return out
    return _Task([], shards, [], len(specs), copies)


def _call(body, *, name, grid, in_specs, out_specs, out_shape, args, scratch_shapes=(), tasks=(), carry=()):
    tasks = [t for t in tasks if t is not None]
    carry = list(carry)
    n_in, n_out, n_scr, n_carry = len(in_specs), len(out_specs), len(scratch_shapes), len(carry)
    ro = [a for t in tasks for a in t.ro]
    rw = [a for t in tasks for a in t.rw]
    new = [s for t in tasks for s in t.new]
    n_ro, n_rw, n_new = len(ro), len(rw), len(new)

    def wrapped(*refs):
        ins = refs[:n_in]
        ro_refs = refs[n_in + n_carry:n_in + n_carry + n_ro]
        at = n_in + n_carry + n_ro + n_rw
        outs = refs[at:at + n_out + n_carry]
        at = at + n_out + n_carry
        rw_refs = refs[at:at + n_rw]
        new_refs = refs[at + n_rw:at + n_rw + n_new]
        at = at + n_rw + n_new
        scr = refs[at:at + n_scr]
        sems = refs[at + n_scr:]

        def task_copies():
            found, a, b, d = [], 0, 0, 0
            for i, t in enumerate(tasks):
                found += t.copies(ro_refs[a:a + len(t.ro)], rw_refs[b:b + len(t.rw)], new_refs[d:d + len(t.new)],
                                  sems[2 * i], sems[2 * i + 1])
                a, b, d = a + len(t.ro), b + len(t.rw), d + len(t.new)
            return found

        if tasks:
            first = pl.program_id(0) == 0
            last = pl.program_id(0) == grid[0] - 1
            for ax in range(1, len(grid)):
                first = first & (pl.program_id(ax) == 0)
                last = last & (pl.program_id(ax) == grid[ax] - 1)

            @pl.when(first)
            def _():
                for cp, _, _ in task_copies():
                    cp().start()

        body(*ins, *outs, *scr)

        if tasks:
            @pl.when(last)
            def _():
                found = task_copies()
                for _, _, recv in found:
                    recv().wait_recv()
                for _, send, _ in found:
                    send().wait_send()

    sem_shapes = []
    for t in tasks:
        sem_shapes += [pltpu.SemaphoreType.DMA((t.n_sem,)), pltpu.SemaphoreType.DMA((t.n_sem,))]
    aliases = {n_in + i: n_out + i for i in range(n_carry)}
    aliases.update({n_in + n_carry + n_ro + i: n_out + n_carry + i for i in range(n_rw)})
    res = pl.pallas_call(
        wrapped, name=name, grid=grid,
        in_specs=list(in_specs) + [HBM_ANY] * (n_carry + n_ro + n_rw),
        out_specs=list(out_specs) + [HBM_ANY] * (n_carry + n_rw + n_new),
        out_shape=list(out_shape) + [jax.ShapeDtypeStruct(a.shape, a.dtype) for a in carry + rw] + new,
        scratch_shapes=list(scratch_shapes) + sem_shapes,
        input_output_aliases=aliases,
        compiler_params=pltpu.CompilerParams(dimension_semantics=("arbitrary",) * len(grid),
                                             vmem_limit_bytes=VMEM_LIMIT_V7X),
    )(*args, *carry, *ro, *rw)
    own, rest = list(res[:n_out + n_carry]), list(res[n_out + n_carry:])
    per_task, b, d = [], 0, n_rw
    for t in tasks:
        per_task.append((rest[b:b + len(t.rw)], rest[d:d + len(t.new)]))
        b, d = b + len(t.rw), d + len(t.new)
    return own, per_task


def _comm_only(name, tasks):
    ro = [a for t in tasks for a in t.ro]
    rw = [a for t in tasks for a in t.rw]
    new = [s for t in tasks for s in t.new]
    n_ro, n_rw, n_new = len(ro), len(rw), len(new)

    def body(*refs):
        ro_refs = refs[:n_ro]
        rw_refs = refs[n_ro + n_rw:n_ro + 2 * n_rw]
        new_refs = refs[n_ro + 2 * n_rw:n_ro + 2 * n_rw + n_new]
        sems = refs[n_ro + 2 * n_rw + n_new:]
        found, a, b, d = [], 0, 0, 0
        for i, t in enumerate(tasks):
            found += t.copies(ro_refs[a:a + len(t.ro)], rw_refs[b:b + len(t.rw)], new_refs[d:d + len(t.new)],
                              sems[2 * i], sems[2 * i + 1])
            a, b, d = a + len(t.ro), b + len(t.rw), d + len(t.new)
        for cp, _, _ in found:
            cp().start()
        for _, _, recv in found:
            recv().wait_recv()
        for _, send, _ in found:
            send().wait_send()

    sem_shapes = []
    for t in tasks:
        sem_shapes += [pltpu.SemaphoreType.DMA((t.n_sem,)), pltpu.SemaphoreType.DMA((t.n_sem,))]
    res = pl.pallas_call(
        body, name=name,
        in_specs=[HBM_ANY] * (n_ro + n_rw), out_specs=[HBM_ANY] * (n_rw + n_new),
        out_shape=[jax.ShapeDtypeStruct(a.shape, a.dtype) for a in rw] + new,
        scratch_shapes=sem_shapes,
        input_output_aliases={n_ro + i: i for i in range(n_rw)},
    )(*ro, *rw)
    res = list(res)
    per_task, b, d = [], 0, n_rw
    for t in tasks:
        per_task.append((res[b:b + len(t.rw)], res[d:d + len(t.new)]))
        b, d = b + len(t.rw), d + len(t.new)
    return per_task


def _gather_first(bufs, specs):
    n = len(specs)
    near, relay, d2d = _gather_near_task(bufs, specs), _gather_relay_task(bufs, specs), _gather_d2d_task(bufs, specs)

    def body(*refs):
        rw, sems = refs[n:2 * n], refs[2 * n:]
        nc = near.copies([], rw, [], sems[0], sems[1])
        rc = relay.copies([], rw, [], sems[2], sems[3])
        dc = d2d.copies([], rw, [], sems[4], sems[5])
        for start, _, _ in nc:
            start().start()
        for t in range(n):
            for p in range(2):
                nc[t * 2 + p][2]().wait_recv()
            for p in range(2):
                rc[t * 2 + p][0]().start()
                dc[t * 3 + p][0]().start()
        for t in range(n):
            for p in range(2):
                rc[t * 2 + p][2]().wait_recv()
            dc[t * 3 + 2][0]().start()
        for _, _, recv in dc:
            recv().wait_recv()
        for _, send, _ in nc + rc + dc:
            send().wait_send()

    sem_shapes = [pltpu.SemaphoreType.DMA((task.n_sem,)) for task in (near, relay, d2d) for _ in range(2)]
    return list(pl.pallas_call(
        body, name="gather_first", in_specs=[HBM_ANY] * n, out_specs=[HBM_ANY] * n,
        out_shape=[jax.ShapeDtypeStruct(a.shape, a.dtype) for a in bufs], scratch_shapes=sem_shapes,
        input_output_aliases={i: i for i in range(n)},
    )(*bufs))


def _inproj(x, norm_g, w_in, dm, tasks=()):
    S, D, N = dm.S, dm.D, dm.INW
    tm, tn = min(1024, S), 512

    def body(x_ref, g_ref, w_ref, z_ref, h_ref):
        @pl.when(pl.program_id(1) == 0)
        def _():
            xv = x_ref[...]
            r = lax.rsqrt(jnp.mean(xv * xv, axis=-1, keepdims=True) + RMS_EPS)
            h_ref[...] = _bf(xv * r * g_ref[...])

        z_ref[...] = _bf(_dot(h_ref[...], w_ref[...]))

    return _call(
        body, name="inproj", grid=(S // tm, N // tn),
        in_specs=[pl.BlockSpec((tm, D), lambda i, j: (i, 0)),
                  pl.BlockSpec((1, D), lambda i, j: (0, 0)),
                  pl.BlockSpec((None, D, tn), lambda i, j: (0, 0, j))],
        out_specs=[pl.BlockSpec((tm, tn), lambda i, j: (i, j)),
                   pl.BlockSpec((tm, D), lambda i, j: (i, 0))],
        out_shape=[jax.ShapeDtypeStruct((S, N), BF16), jax.ShapeDtypeStruct((S, D), BF16)],
        args=(x, norm_g, w_in), tasks=tasks)


def _rot_half(x, cs, sn):
    h = cs.shape[-1]
    x1, x2 = x[:, :h], x[:, h:]
    return jnp.concatenate([x1 * cs - x2 * sn, x2 * cs + x1 * sn], axis=-1)


def _ret_tables(af_ref, ab_ref):
    C = CHUNK
    lgf = -jnp.exp(af_ref[...])[:, :1]
    lgb = -jnp.exp(ab_ref[...])[:, :1]
    ri = lax.broadcasted_iota(jnp.int32, (C, C), 0)
    ci = lax.broadcasted_iota(jnp.int32, (C, C), 1)
    lag = _f32(ri - ci)
    alag = jnp.abs(lag)
    low = lag >= 0
    dmask = jnp.where(low, jnp.exp(lgf * alag), jnp.exp(lgb * alag))
    j = _f32(lax.broadcasted_iota(jnp.int32, (C, 1), 0))
    return dict(lgf=lgf, lgb=lgb, alag=alag, low=low, dmask=dmask, j=j,
                w_f=jnp.exp(lgf * (C - 1.0 - j)), w_b=jnp.exp(lgb * j),
                q_f=jnp.exp(lgf * (j + 1.0)), q_b=jnp.exp(lgb * (C - j)),
                dec_f=jnp.exp(lgf * C), dec_b=jnp.exp(lgb * C))


def _decay_rows(a):
    return jnp.broadcast_to(a.reshape(RET_HEADS, 1, 1), (RET_HEADS, 1, 128))


def _heads(x, width):
    return jnp.stack([x[:, b * width:(b + 1) * width] for b in range(x.shape[1] // width)])


def _bdot(a, b):
    return lax.dot_general(a, b, (((2,), (1,)), ((0,), (0,))), preferred_element_type=F32)


def _bdot_nt(a, b):
    return lax.dot_general(a, b, (((2,), (2,)), ((0,), (0,))), preferred_element_type=F32)


def _bdot_tn(a, b):
    return lax.dot_general(a, b, (((1,), (1,)), ((0,), (0,))), preferred_element_type=F32)


def _ret_tables_heads(af_ref, ab_ref):
    C = CHUNK
    lgf = -jnp.exp(af_ref[...])[:, :, :1]
    lgb = -jnp.exp(ab_ref[...])[:, :, :1]
    ri = lax.broadcasted_iota(jnp.int32, (1, C, C), 1)
    ci = lax.broadcasted_iota(jnp.int32, (1, C, C), 2)
    lag = _f32(ri - ci)
    alag = jnp.abs(lag)
    low = lag >= 0
    dmask = jnp.where(low, jnp.exp(lgf * alag), jnp.exp(lgb * alag))
    j = _f32(lax.broadcasted_iota(jnp.int32, (1, C, 1), 1))
    return dict(lgf=lgf, lgb=lgb, alag=alag, low=low, dmask=dmask, j=j,
                w_f=jnp.exp(lgf * (C - 1.0 - j)), w_b=jnp.exp(lgb * j),
                q_f=jnp.exp(lgf * (j + 1.0)), q_b=jnp.exp(lgb * (C - j)),
                dec_f=jnp.exp(lgf * C), dec_b=jnp.exp(lgb * C))


def _rot_half_heads(x, cs, sn):
    h = cs.shape[-1]
    x1, x2 = x[..., :h], x[..., h:]
    return jnp.concatenate([x1 * cs - x2 * sn, x2 * cs + x1 * sn], axis=-1)


def _ret_fwd(z, a_f, a_b, cosr, sinr, dm):
    S, RD, C, nC, HB = dm.S, dm.RD, CHUNK, dm.nC, RET_HEADS_PER_STEP
    W = HB * RD
    scale = RD ** -0.5

    def body(q_ref, k_ref, v_ref, g_ref, af_ref, ab_ref, cos_ref, sin_ref, o_ref, qh_s, kh_s, sf_s, st_s):
        t = _ret_tables_heads(af_ref, ab_ref)
        st_s[...] = jnp.zeros_like(st_s)

        def fwd_pass(n, carry):
            r = pl.ds(pl.multiple_of(n * C, C), C)
            cs, sn = cos_ref[r, :], sin_ref[r, :]
            qh = _rot_half_heads(_heads(_f32(q_ref[r, :]), RD), cs, sn)
            kh = _rot_half_heads(_heads(_f32(k_ref[r, :]), RD), cs, sn) * scale
            qh_s[:, r, :] = _bf(qh)
            kh_s[:, r, :] = _bf(kh)
            st = st_s[...]
            sf_s[n] = _bf(st)
            st_s[...] = st * t["dec_f"] + _bdot_tn(_bf(kh * t["w_f"]), _heads(v_ref[r, :], RD))
            return carry

        _loop(nC, fwd_pass, 0)
        st_s[...] = jnp.zeros_like(st_s)

        def bwd_pass(i, carry):
            n = nC - 1 - i
            r = pl.ds(pl.multiple_of(n * C, C), C)
            qhb, khb, vb = qh_s[:, r, :], kh_s[:, r, :], _heads(v_ref[r, :], RD)
            p = _bf(_bdot_nt(qhb, khb) * t["dmask"])
            qhf = _f32(qhb)
            sb = st_s[...]
            out = (_bdot(p, vb) + _bdot(_bf(qhf * t["q_f"]), sf_s[n]) + _bdot(_bf(qhf * t["q_b"]), _bf(sb)))
            y = out * lax.rsqrt(jnp.mean(out * out, axis=-1, keepdims=True) + RMS_EPS)
            gate = _silu(_f32(g_ref[r, :]))
            for b in range(HB):
                o_ref[r, b * RD:(b + 1) * RD] = _bf(y[b] * gate[:, b * RD:(b + 1) * RD])
            st_s[...] = sb * t["dec_b"] + _bdot_tn(_bf(_f32(khb) * t["w_b"]), vb)
            return carry

        _loop(nC, bwd_pass, 0)

    zs = lambda off: pl.BlockSpec((S, W), lambda h: (0, off // W + h), pipeline_mode=ONE_BUFFER)
    dec = pl.BlockSpec((HB, 1, 128), lambda h: (h, 0, 0))
    tab = pl.BlockSpec((S, RD // 2), lambda h: (0, 0), pipeline_mode=ONE_BUFFER)
    own, _ = _call(
        body, name="ret_fwd", grid=(RET_HEADS // HB,),
        in_specs=[zs(dm.o_rq), zs(dm.o_rk), zs(dm.o_rv), zs(dm.o_rg), dec, dec, tab, tab],
        out_specs=[pl.BlockSpec((S, W), lambda h: (0, h))],
        out_shape=[jax.ShapeDtypeStruct((S, dm.BW), BF16)],
        scratch_shapes=[pltpu.VMEM((HB, S, RD), BF16), pltpu.VMEM((HB, S, RD), BF16),
                        pltpu.VMEM((nC, HB, RD, RD), BF16), pltpu.VMEM((HB, RD, RD), F32)],
        args=(z, z, z, z, _decay_rows(a_f), _decay_rows(a_b), cosr, sinr))
    return own[0]


def _band(first_row, first_col, lo, hi, shape):
    r = lax.broadcasted_iota(jnp.int32, shape, 0) + first_row
    c = lax.broadcasted_iota(jnp.int32, shape, 1) + first_col
    d = c - r
    return jnp.where((d >= lo) & (d <= hi), 1.0, 0.0).astype(BF16)


def _pool_counts(first_row, half, S, rows):
    pos = lax.broadcasted_iota(jnp.int32, (rows, 1), 0) + first_row
    lo = jnp.clip(pos - half, 0, S)
    hi = jnp.clip(pos + half, 0, S)
    return 1.0 / _f32(hi - lo)


def _pool_rows(S):
    return min(4 * CHUNK, S // 2)


def _pool_win_start(n, rows, S):
    return pl.multiple_of(jnp.clip(n * rows - CHUNK, 0, S - (rows + 2 * CHUNK)), CHUNK)


def _win_start(n, S):
    return pl.multiple_of(jnp.clip((n - 1) * CHUNK, 0, S - 3 * CHUNK), CHUNK)


def _pool_fwd(z, pool_w, pool_scale, dm):
    S, PD = dm.S, dm.PD
    PB = _pool_rows(S)
    WIN = PB + 2 * CHUNK

    def body(u_ref, g_ref, w_ref, sc_ref, o_ref):
        half = jnp.left_shift(1, pl.program_id(0))
        w = w_ref[...]
        sc = sc_ref[...]

        def blk(n, carry):
            row0 = pl.multiple_of(n * PB, PB)
            r = pl.ds(row0, PB)
            st = _pool_win_start(n, PB, S)
            band = _band(row0, st, -half, half - 1, (PB, WIN))
            mean = _dot(band, u_ref[pl.ds(st, WIN), :]) * _pool_counts(row0, half, S, PB)
            p = mean - _f32(u_ref[r, :])
            y = _dot(_bf(p), w) * sc
            o_ref[r, :] = _bf(y * _silu(_f32(g_ref[r, :])))
            return carry

        _loop(S // PB, blk, 0)

    own, _ = _call(
        body, name="pool_fwd", grid=(POOL_GROUPS,),
        in_specs=[pl.BlockSpec((S, PD), lambda g: (0, dm.o_pv // PD + g)),
                  pl.BlockSpec((S, PD), lambda g: (0, dm.o_pg // PD + g)),
                  pl.BlockSpec((None, PD, PD), lambda g: (g, 0, 0)),
                  pl.BlockSpec((None, 1, PD), lambda g: (g, 0, 0))],
        out_specs=[pl.BlockSpec((S, PD), lambda g: (0, g))],
        out_shape=[jax.ShapeDtypeStruct((S, dm.BW), BF16)],
        args=(z, z, pool_w, pool_scale))
    return own[0]


def _rot_part(x, cs, s_up, s_dn):
    h = ROPE_DIMS // 2
    lanes = x.ndim - 1
    return x * cs + pltpu.roll(x, ATT_HEAD_DIM - h, lanes) * s_up + pltpu.roll(x, h, lanes) * s_dn


def _att_tables(S):
    h = ROPE_DIMS // 2
    inv = ROPE_THETA ** (-jnp.arange(h, dtype=F32) / h)
    ang = jnp.arange(S, dtype=F32)[:, None] * inv[None, :]
    cos, sin = jnp.cos(ang), jnp.sin(ang)
    pad = jnp.zeros((S, ATT_HEAD_DIM - 2 * h), F32)
    zero = jnp.zeros((S, h), F32)
    cs = jnp.concatenate([cos, cos, pad + 1.0], axis=1)
    s_up = jnp.concatenate([-sin, zero, pad], axis=1)
    s_dn = jnp.concatenate([zero, sin, pad], axis=1)
    return cs, s_up, s_dn


def _ret_rope_tables(S, RD):
    h = RD // 2
    inv = 1.0 / (RET_ROPE_BASE ** jnp.linspace(0.0, 1.0, h, dtype=F32))
    ang = jnp.arange(S, dtype=F32)[:, None] * inv[None, :]
    return jnp.cos(ang), jnp.sin(ang)


def _att_norm_rot(x, gain, cs, s_up, s_dn):
    u = x * lax.rsqrt(jnp.mean(x * x, axis=-1, keepdims=True) + RMS_EPS)
    return _rot_part(u * gain, cs, s_up, s_dn)


def _att_bias_tables(bias_s):
    C = CHUNK
    shape = (ATT_GROUP * C, 3 * C)
    rel = lax.broadcasted_iota(jnp.int32, shape, 1) - lax.broadcasted_iota(jnp.int32, shape, 0) % C
    for which, shift in enumerate((-C, 0, -2 * C)):
        bias_s[which] = jnp.where(jnp.abs(rel + shift) <= CHUNK, 0.0, NEG_BIG)


def _att_bias(bias_s, n, nC):
    return bias_s[jnp.where(n == 0, 1, jnp.where(n == nC - 1, 2, 0))]


def _att_probs(q4, kw, sink_col, bias):
    s = _dot_nt(q4, kw) * (ATT_HEAD_DIM ** -0.5) + bias
    m = jnp.maximum(jnp.max(s, axis=-1, keepdims=True), sink_col)
    e = jnp.exp(s - m)
    es = jnp.exp(sink_col - m)
    inv = 1.0 / (jnp.sum(e, axis=-1, keepdims=True) + es)
    return e * inv, es * inv


def _sink_col(sink_ref):
    head = lax.broadcasted_iota(jnp.int32, (ATT_GROUP * CHUNK, 1), 0) // CHUNK
    col = jnp.zeros((ATT_GROUP * CHUNK, 1), F32)
    for g in range(ATT_GROUP):
        col = jnp.where(head == g, sink_ref[g:g + 1, :1], col)
    return col


def _sink_rows(sink, dm):
    return jnp.broadcast_to(sink.reshape(dm.AKV, ATT_GROUP, 1), (dm.AKV, ATT_GROUP, 128))


def _att_prep(q_ref, k_ref, qg_ref, kg_ref, cs_ref, su_ref, sd_ref, qn_s, kn_s, nC):
    C, HD = CHUNK, ATT_HEAD_DIM

    def prep(n, carry):
        r = pl.ds(pl.multiple_of(n * C, C), C)
        cs, su, sd = cs_ref[r, :], su_ref[r, :], sd_ref[r, :]
        kn_s[r, :] = _bf(_att_norm_rot(_f32(k_ref[r, :]), kg_ref[...], cs, su, sd))
        qn_s[:, r, :] = _bf(_att_norm_rot(_heads(_f32(q_ref[r, :]), HD), qg_ref[...], cs, su, sd))
        return carry

    lax.fori_loop(0, nC, prep, 0)


def _att_fwd(z, q_gain, k_gain, sink, tabs, dm, tasks=()):
    S, C, nC, HD, G = dm.S, CHUNK, dm.nC, ATT_HEAD_DIM, ATT_GROUP
    GW = G * HD

    KV = dm.AKV

    def body(q_ref, k_ref, v_ref, g_ref, qg_ref, kg_ref, sk_ref, cs_ref, su_ref, sd_ref, o_ref, qn_s, kn_s, bias_s):
        def prep(n, carry):
            r = pl.ds(pl.multiple_of(n * C, C), C)
            cs, su, sd = cs_ref[r, :], su_ref[r, :], sd_ref[r, :]
            kn_s[:, r, :] = _bf(_att_norm_rot(_heads(_f32(k_ref[r, :]), HD), kg_ref[...], cs, su, sd))
            qn_s[:, r, :] = _bf(_att_norm_rot(_heads(_f32(q_ref[r, :]), HD), qg_ref[...], cs, su, sd))
            return carry

        lax.fori_loop(0, nC, prep, 0)
        _att_bias_tables(bias_s)
        sink_col = jnp.stack([_sink_col(sk_ref.at[h]) for h in range(KV)])

        def blk(n, carry):
            row0 = pl.multiple_of(n * C, C)
            r = pl.ds(row0, C)
            w = pl.ds(_win_start(n, S), 3 * C)
            q4 = qn_s[:, r, :].reshape(KV, G * C, HD)
            s = _bdot_nt(q4, kn_s[:, w, :]) * (HD ** -0.5) + _att_bias(bias_s, n, nC)
            m = jnp.maximum(jnp.max(s, axis=-1, keepdims=True), sink_col)
            e = jnp.exp(s - m)
            inv = 1.0 / (jnp.sum(e, axis=-1, keepdims=True) + jnp.exp(sink_col - m))
            o = _bdot(_bf(e * inv), _heads(v_ref[w, :], HD))
            gate = _silu(_f32(g_ref[r, :]))
            for h in range(KV):
                for g in range(G):
                    cols = slice((h * G + g) * HD, (h * G + g + 1) * HD)
                    o_ref[r, cols] = _bf(o[h, g * C:(g + 1) * C, :] * gate[:, cols])
            return carry

        _loop(nC, blk, 0)

    tab = pl.BlockSpec((S, HD), lambda h: (0, 0), pipeline_mode=ONE_BUFFER)
    gain = pl.BlockSpec((1, HD), lambda h: (0, 0))
    one = lambda off, width: pl.BlockSpec((pl.Element(S), pl.Element(width)), lambda h: (0, off),
                                          pipeline_mode=ONE_BUFFER)
    return _call(
        body, name="att_fwd", grid=(1,),
        in_specs=[one(dm.o_aq, dm.BW), one(dm.o_ak, dm.AKW), one(dm.o_av, dm.AKW), one(dm.o_ag, dm.BW),
                  gain, gain, pl.BlockSpec((KV, G, 128), lambda h: (0, 0, 0)), tab, tab, tab],
        out_specs=[pl.BlockSpec((S, dm.BW), lambda h: (0, 0), pipeline_mode=ONE_BUFFER)],
        out_shape=[jax.ShapeDtypeStruct((S, dm.BW), BF16)],
        scratch_shapes=[pltpu.VMEM((KV * G, S, HD), BF16), pltpu.VMEM((KV, S, HD), BF16),
                        pltpu.VMEM((3, G * C, 3 * C), F32)],
        args=(z, z, z, z, q_gain, k_gain, _sink_rows(sink, dm), *tabs), tasks=tasks)


def _merge_gate_spec(tm, dm):
    return pl.BlockSpec((pl.Element(tm), pl.Element(N_BRANCHES * dm.D)), lambda i: (i * tm, dm.o_mg))


def _merge_out(x, mg, ya_pre, yb_pre, yc_pre, w_ret, w_pool, w_att, w_out, dm, tasks=()):
    S, D, BW = dm.S, dm.D, dm.BW
    tm = min(256, S)

    def body(x_ref, mg_ref, a_ref, b_ref, c_ref, wr_ref, wp_ref, wa_ref, wo_ref, xo_ref, ya_ref, yb_ref, yc_ref, m_ref):
        ya = _dot(a_ref[...], wr_ref[...])
        yb = _dot(b_ref[...], wp_ref[...])
        yc = _dot(c_ref[...], wa_ref[...])
        ya_ref[...], yb_ref[...], yc_ref[...] = _bf(ya), _bf(yb), _bf(yc)
        g0 = _sigmoid(_f32(mg_ref[:, 0:D]))
        g1 = _sigmoid(_f32(mg_ref[:, D:2 * D]))
        g2 = _sigmoid(_f32(mg_ref[:, 2 * D:3 * D]))
        merged = _bf(g0 * ya + g1 * yb + g2 * yc)
        m_ref[...] = merged
        xo_ref[...] = x_ref[...] + _dot(merged, wo_ref[...])

    act = lambda w: pl.BlockSpec((tm, w), lambda i: (i, 0))
    wsp = lambda r: pl.BlockSpec((None, r, D), lambda i: (0, 0, 0), pipeline_mode=ONE_BUFFER)
    return _call(
        body, name="merge_out", grid=(S // tm,),
        in_specs=[act(D), _merge_gate_spec(tm, dm), act(BW), act(BW), act(BW), wsp(BW), wsp(BW), wsp(BW), wsp(D)],
        out_specs=[act(D)] * 5,
        out_shape=[jax.ShapeDtypeStruct((S, D), F32)] + [jax.ShapeDtypeStruct((S, D), BF16)] * 4,
        args=(x, mg, ya_pre, yb_pre, yc_pre, w_ret, w_pool, w_att, w_out), tasks=tasks)


def _loss_grad(y, target, dm):
    S, D = dm.S, dm.D
    tm = dm.tm

    def body(y_ref, t_ref, s_ref, d_ref):
        @pl.when(pl.program_id(0) == 0)
        def _():
            s_ref[...] = jnp.zeros_like(s_ref)

        e = y_ref[...] - t_ref[...]
        d_ref[...] = e * (1.0 / D)
        s_ref[...] += jnp.sum(jnp.sum(e * e, axis=-1, keepdims=True), axis=0, keepdims=True)

    row = pl.BlockSpec((tm, D), lambda i: (i, 0))
    own, _ = _call(
        body, name="loss_grad", grid=(S // tm,), in_specs=[row, row],
        out_specs=[pl.BlockSpec((1, 128), lambda i: (0, 0)), row],
        out_shape=[jax.ShapeDtypeStruct((1, 128), F32), jax.ShapeDtypeStruct((S, D), F32)],
        args=(y, target))
    return own


def _merge_bwd_gates(dx, mg, ya, yb, yc, w_out, dm, tasks=()):
    S, D = dm.S, dm.D
    tm = min(256, S)

    def body(dx_ref, mg_ref, ya_ref, yb_ref, yc_ref, wo_ref, dmg_ref, dya_ref, dyb_ref, dyc_ref, dxb_ref):
        dxb = _bf(dx_ref[...])
        dxb_ref[...] = dxb
        dm_ = _dot_nt(dxb, wo_ref[...])
        for k, (y_ref, dy_ref) in enumerate(((ya_ref, dya_ref), (yb_ref, dyb_ref), (yc_ref, dyc_ref))):
            g = _sigmoid(_f32(mg_ref[:, k * D:(k + 1) * D]))
            dmg_ref[:, k * D:(k + 1) * D] = _bf(dm_ * _f32(y_ref[...]) * g * (1.0 - g))
            dy_ref[...] = _bf(dm_ * g)

    act = lambda w: pl.BlockSpec((tm, w), lambda i: (i, 0))
    return _call(
        body, name="merge_bwd_gates", grid=(S // tm,),
        in_specs=[act(D), _merge_gate_spec(tm, dm), act(D), act(D), act(D),
                  pl.BlockSpec((None, D, D), lambda i: (0, 0, 0), pipeline_mode=ONE_BUFFER)],
        out_specs=[_merge_gate_spec(tm, dm), act(D), act(D), act(D), act(D)],
        out_shape=[jax.ShapeDtypeStruct((S, dm.INW), BF16)] + [jax.ShapeDtypeStruct((S, D), BF16)] * 4,
        args=(dx, mg, ya, yb, yc, w_out), tasks=tasks)


def _merge_bwd_proj(dya, dyb, dyc, w_ret, w_pool, w_att, dm):
    S, D, BW = dm.S, dm.D, dm.BW
    tm = dm.tm

    def body(da_ref, db_ref, dc_ref, wr_ref, wp_ref, wa_ref, oa_ref, ob_ref, oc_ref):
        oa_ref[...] = _bf(_dot_nt(da_ref[...], wr_ref[...]))
        ob_ref[...] = _bf(_dot_nt(db_ref[...], wp_ref[...]))
        oc_ref[...] = _bf(_dot_nt(dc_ref[...], wa_ref[...]))

    act = lambda w: pl.BlockSpec((tm, w), lambda i: (i, 0))
    wsp = pl.BlockSpec((None, BW, D), lambda i: (0, 0, 0), pipeline_mode=ONE_BUFFER)
    own, _ = _call(
        body, name="merge_bwd_proj", grid=(S // tm,),
        in_specs=[act(D)] * 3 + [wsp] * 3, out_specs=[act(BW)] * 3,
        out_shape=[jax.ShapeDtypeStruct((S, BW), BF16)] * 3,
        args=(dya, dyb, dyc, w_ret, w_pool, w_att))
    return own


def _col_writes(dz_ref, sem, step, parts):
    return [pltpu.make_async_copy(src, dz_ref.at[:, pl.ds(pl.multiple_of(off + step * w, 128), w)], sem.at[k])
            for k, (src, off, w) in enumerate(parts)]


def _ret_bwd(z, dz, dya_pre, a_f, a_b, cosr, sinr, dm, tasks=()):
    S, RD, C, nC = dm.S, dm.RD, CHUNK, dm.nC
    scale = RD ** -0.5
    H = RET_HEADS

    def body(q_ref, k_ref, v_ref, g_ref, dy_ref, af_ref, ab_ref, cos_ref, sin_ref, dd_ref, dz_ref,
             qh_s, kh_s, sf_s, sb_s, do_s, dqh_s, dkh_s, dv_s, st_s, lam_s, dq_ref, dk_ref, dv_ref, dg_ref, wsem,
             acc_dd, acc_tf, acc_tb, acc_uf, acc_ub, acc_lf, acc_lb):
        head = pl.program_id(0)
        writes = lambda step: _col_writes(dz_ref, wsem, step, [(dq_ref, dm.o_rq, RD), (dk_ref, dm.o_rk, RD),
                                                               (dv_ref, dm.o_rv, RD), (dg_ref, dm.o_rg, RD)])
        t = _ret_tables(af_ref, ab_ref)
        zero_rr = jnp.zeros((RD, RD), F32)

        st_s[...] = zero_rr

        def pass0(n, carry):
            r = pl.ds(pl.multiple_of(n * C, C), C)
            cs, sn = cos_ref[r, :], sin_ref[r, :]
            qh = _rot_half(_f32(q_ref[r, :]), cs, sn)
            kh = _rot_half(_f32(k_ref[r, :]), cs, sn) * scale
            qh_s[r, :] = _bf(qh)
            kh_s[r, :] = _bf(kh)
            st = st_s[...]
            sf_s[n] = _bf(st)
            st_s[...] = st * t["dec_f"] + _dot_tn(_bf(kh * t["w_f"]), v_ref[r, :])
            return carry

        _loop(nC, pass0, 0)

        @pl.when(head > 0)
        def _():
            for cp in writes(head - 1):
                cp.wait()

        st_s[...] = zero_rr
        lam_s[...] = zero_rr

        def pass1(i, carry):
            n = nC - 1 - i
            r = pl.ds(pl.multiple_of(n * C, C), C)
            qhb, khb, vb = qh_s[r, :], kh_s[r, :], v_ref[r, :]
            qhf, khf = _f32(qhb), _f32(khb)
            sc_ = _dot_nt(qhb, khb)
            p = _bf(sc_ * t["dmask"])
            sb = st_s[...]
            sbb = _bf(sb)
            sb_s[n] = sbb
            st_s[...] = sb * t["dec_b"] + _dot_tn(_bf(khf * t["w_b"]), vb)
            sfb = sf_s[n]
            qf_b, qb_b = _bf(qhf * t["q_f"]), _bf(qhf * t["q_b"])
            out = _dot(p, vb) + _dot(qf_b, sfb) + _dot(qb_b, sbb)
            rr = lax.rsqrt(jnp.mean(out * out, axis=-1, keepdims=True) + RMS_EPS)
            y = out * rr
            g = _f32(g_ref[r, :])
            dya = _f32(dy_ref[r, :])
            dg_ref[r, :] = _bf(dya * y * _dsilu(g))
            dyn = dya * _silu(g)
            dout = rr * (dyn - y * jnp.mean(dyn * y, axis=-1, keepdims=True))
            dob = _bf(dout)
            do_s[r, :] = dob
            dp = _dot_nt(dob, vb)
            dv = _dot_tn(p, dob)
            ds = _bf(dp * t["dmask"])
            dqh = _dot(ds, khb)
            dkh = _dot_tn(ds, qhb)
            acc_dd[...] += dp * sc_
            tf = _dot_nt(dob, sfb)
            tb = _dot_nt(dob, sbb)
            dqh = dqh + tf * t["q_f"] + tb * t["q_b"]
            acc_tf[...] += tf * qhf
            acc_tb[...] += tb * qhf
            lam = lam_s[...]
            lamb = _bf(lam)
            acc_lf[...] += lam * _f32(sfb)
            u = _dot_nt(vb, lamb)
            dkh = dkh + u * t["w_f"]
            acc_uf[...] += u * khf
            dv = dv + _dot(_bf(khf * t["w_f"]), lamb)
            lam_s[...] = _dot_tn(qf_b, dob) + lam * t["dec_f"]
            dqh_s[r, :] = dqh
            dkh_s[r, :] = dkh
            dv_s[r, :] = dv
            return carry

        for acc in (acc_dd, acc_tf, acc_tb, acc_uf, acc_ub, acc_lf, acc_lb):
            acc[...] = jnp.zeros_like(acc)
        _loop(nC, pass1, 0)

        lam_s[...] = zero_rr

        def pass2(n, carry):
            r = pl.ds(pl.multiple_of(n * C, C), C)
            qhb, khb, vb, dob = qh_s[r, :], kh_s[r, :], v_ref[r, :], do_s[r, :]
            qhf, khf = _f32(qhb), _f32(khb)
            lam = lam_s[...]
            lamb = _bf(lam)
            acc_lb[...] += lam * _f32(sb_s[n])
            u = _dot_nt(vb, lamb)
            dkh = dkh_s[r, :] + u * t["w_b"]
            acc_ub[...] += u * khf
            dv = dv_s[r, :] + _dot(_bf(khf * t["w_b"]), lamb)
            lam_s[...] = _dot_tn(_bf(qhf * t["q_b"]), dob) + lam * t["dec_b"]
            cs, sn = cos_ref[r, :], sin_ref[r, :]
            dq_ref[r, :] = _bf(_rot_half(dqh_s[r, :], cs, -sn))
            dk_ref[r, :] = _bf(_rot_half(dkh * scale, cs, -sn))
            dv_ref[r, :] = _bf(dv)
            return carry

        _loop(nC, pass2, 0)
        rows = lambda acc: jnp.sum(acc[...], axis=-1, keepdims=True)
        dd = acc_dd[...] * t["dmask"] * t["alag"]
        glf = (_sum11(jnp.where(t["low"], dd, 0.0)) + _sum11(rows(acc_tf) * t["q_f"] * (t["j"] + 1.0))
               + _sum11(rows(acc_uf) * t["w_f"] * (C - 1.0 - t["j"])) + _sum11(acc_lf[...]) * t["dec_f"] * C)
        glb = (_sum11(jnp.where(t["low"], 0.0, dd)) + _sum11(rows(acc_tb) * t["q_b"] * (C - t["j"]))
               + _sum11(rows(acc_ub) * t["w_b"] * t["j"]) + _sum11(acc_lb[...]) * t["dec_b"] * C)
        row = lax.broadcasted_iota(jnp.int32, (8, 128), 0)
        da_f = glf * t["lgf"]
        da_b = glb * t["lgb"]
        dd_ref[...] = jnp.where(row == 0, da_f, jnp.where(row == 1, da_b, 0.0))
        for cp in writes(head):
            cp.start()

        @pl.when(head == H - 1)
        def _():
            for cp in writes(head):
                cp.wait()

    zs = lambda off: pl.BlockSpec((S, RD), lambda h: (0, off // RD + h), pipeline_mode=ONE_BUFFER)
    col = pl.BlockSpec((S, RD), lambda h: (0, h), pipeline_mode=ONE_BUFFER)
    dec = pl.BlockSpec((None, 1, 128), lambda h: (h, 0, 0))
    tab = pl.BlockSpec((S, RD // 2), lambda h: (0, 0), pipeline_mode=ONE_BUFFER)
    return _call(
        body, name="ret_bwd", grid=(H,),
        in_specs=[zs(dm.o_rq), zs(dm.o_rk), zs(dm.o_rv), zs(dm.o_rg), col, dec, dec, tab, tab],
        out_specs=[pl.BlockSpec((None, 8, 128), lambda h: (h, 0, 0))],
        out_shape=[jax.ShapeDtypeStruct((H, 8, 128), F32)],
        scratch_shapes=[pltpu.VMEM((S, RD), BF16), pltpu.VMEM((S, RD), BF16),
                        pltpu.VMEM((nC, RD, RD), BF16), pltpu.VMEM((nC, RD, RD), BF16),
                        pltpu.VMEM((S, RD), BF16),
                        pltpu.VMEM((S, RD), F32), pltpu.VMEM((S, RD), F32), pltpu.VMEM((S, RD), F32),
                        pltpu.VMEM((RD, RD), F32), pltpu.VMEM((RD, RD), F32)]
        + [pltpu.VMEM((S, RD), BF16)] * 4 + [pltpu.SemaphoreType.DMA((4,))]
        + [pltpu.VMEM((C, C), F32)] + [pltpu.VMEM((C, RD), F32)] * 4 + [pltpu.VMEM((RD, RD), F32)] * 2,
        args=(z, z, z, z, dya_pre, _decay_rows(a_f), _decay_rows(a_b), cosr, sinr), tasks=tasks, carry=[dz])


def _pool_bwd(z, dz, dyb_pre, pool_w, pool_scale, dm):
    S, PD = dm.S, dm.PD
    PB = _pool_rows(S)
    WIN = PB + 2 * CHUNK
    G = POOL_GROUPS

    def body(u_ref, g_ref, dy_ref, w_ref, sc_ref, dw_ref, dsc_ref, dz_ref, dp_s, dpc_s, dw_s, du_ref, dg_ref, wsem):
        group = pl.program_id(0)
        writes = lambda step: _col_writes(dz_ref, wsem, step, [(du_ref, dm.o_pv, PD), (dg_ref, dm.o_pg, PD)])
        half = jnp.left_shift(1, group)
        w = w_ref[...]
        sc = sc_ref[...]
        dw_s[...] = jnp.zeros_like(dw_s)
        dsc_ref[...] = jnp.zeros_like(dsc_ref)

        @pl.when(group > 0)
        def _():
            for cp in writes(group - 1):
                cp.wait()

        def blk1(n, carry):
            row0 = pl.multiple_of(n * PB, PB)
            r = pl.ds(row0, PB)
            st = _pool_win_start(n, PB, S)
            band = _band(row0, st, -half, half - 1, (PB, WIN))
            inv = _pool_counts(row0, half, S, PB)
            pb = _bf(_dot(band, u_ref[pl.ds(st, WIN), :]) * inv - _f32(u_ref[r, :]))
            ylin = _dot(pb, w)
            y = ylin * sc
            g = _f32(g_ref[r, :])
            dyb = _f32(dy_ref[r, :])
            dg_ref[r, :] = _bf(dyb * y * _dsilu(g))
            dy = dyb * _silu(g)
            dsc_ref[...] += jnp.sum(dy * ylin, axis=0, keepdims=True)
            dyl = _bf(dy * sc)
            dw_s[...] += _dot_tn(pb, dyl)
            dp = _dot_nt(dyl, w)
            dp_s[r, :] = dp
            dpc_s[r, :] = _bf(dp * inv)
            return carry

        _loop(S // PB, blk1, 0)
        dw_ref[...] = _bf(dw_s[...])

        def blk2(n, carry):
            row0 = pl.multiple_of(n * PB, PB)
            r = pl.ds(row0, PB)
            st = _pool_win_start(n, PB, S)
            band_t = _band(row0, st, -half + 1, half, (PB, WIN))
            du_ref[r, :] = _bf(_dot(band_t, dpc_s[pl.ds(st, WIN), :]) - dp_s[r, :])
            return carry

        _loop(S // PB, blk2, 0)
        for cp in writes(group):
            cp.start()

        @pl.when(group == G - 1)
        def _():
            for cp in writes(group):
                cp.wait()

    own, _ = _call(
        body, name="pool_bwd", grid=(G,),
        in_specs=[pl.BlockSpec((S, PD), lambda g: (0, dm.o_pv // PD + g)),
                  pl.BlockSpec((S, PD), lambda g: (0, dm.o_pg // PD + g)),
                  pl.BlockSpec((S, PD), lambda g: (0, g)),
                  pl.BlockSpec((None, PD, PD), lambda g: (g, 0, 0)),
                  pl.BlockSpec((None, 1, PD), lambda g: (g, 0, 0))],
        out_specs=[pl.BlockSpec((None, PD, PD), lambda g: (g, 0, 0)), pl.BlockSpec((None, 1, PD), lambda g: (g, 0, 0))],
        out_shape=[jax.ShapeDtypeStruct((G, PD, PD), BF16), jax.ShapeDtypeStruct((G, 1, PD), F32)],
        scratch_shapes=[pltpu.VMEM((S, PD), F32), pltpu.VMEM((S, PD), BF16), pltpu.VMEM((PD, PD), F32),
                        pltpu.VMEM((S, PD), BF16), pltpu.VMEM((S, PD), BF16), pltpu.SemaphoreType.DMA((2,))],
        args=(z, z, dyb_pre, pool_w, pool_scale), carry=[dz])
    return own


def _att_bwd(z, dz, dyc_pre, q_gain, k_gain, sink, tabs, dm, tasks=()):
    S, C, nC, HD, G = dm.S, CHUNK, dm.nC, ATT_HEAD_DIM, ATT_GROUP
    GW = G * HD
    scale = HD ** -0.5

    def body(q_ref, k_ref, v_ref, g_ref, dy_ref, qg_ref, kg_ref, sk_ref, cs_ref, su_ref, sd_ref,
             sm_ref, dz_ref, qn_s, kn_s, dqn_s, dkn_s, dv_s, bias_s, dq_ref, dk_ref, dv_ref, dg_ref, wsem):
        kv = pl.program_id(0)
        writes = lambda step: _col_writes(dz_ref, wsem, step, [(dq_ref, dm.o_aq, GW), (dk_ref, dm.o_ak, HD),
                                                               (dv_ref, dm.o_av, HD), (dg_ref, dm.o_ag, GW)])
        _att_prep(q_ref, k_ref, qg_ref, kg_ref, cs_ref, su_ref, sd_ref, qn_s, kn_s, nC)

        @pl.when(kv > 0)
        def _():
            for cp in writes(kv - 1):
                cp.wait()

        _att_bias_tables(bias_s)
        dkn_s[...] = jnp.zeros_like(dkn_s)
        dv_s[...] = jnp.zeros_like(dv_s)
        sink_col = _sink_col(sk_ref)

        HALF = G // 2

        def blk(n, dsink):
            row0 = pl.multiple_of(n * C, C)
            r = pl.ds(row0, C)
            w = pl.ds(_win_start(n, S), 3 * C)
            kw, vw = kn_s[w, :], v_ref[w, :]
            bias = _att_bias(bias_s, n, nC)[:HALF * C, :]
            dk_sum, dv_sum, sink_parts = None, None, []
            for first in range(0, G, HALF):
                rows = slice(first * C, (first + HALF) * C)
                q2 = qn_s[first:first + HALF, r, :].reshape(HALF * C, HD)
                p, ps = _att_probs(q2, kw, sink_col[rows, :], bias)
                pb = _bf(p)
                o = _dot(pb, vw)
                do_parts = []
                for g in range(first, first + HALF):
                    cols = slice(g * HD, (g + 1) * HD)
                    gate = _f32(g_ref[r, cols])
                    dy = _f32(dy_ref[r, cols])
                    dg_ref[r, cols] = _bf(dy * o[(g - first) * C:(g - first + 1) * C, :] * _dsilu(gate))
                    do_parts.append(dy * _silu(gate))
                dob = _bf(jnp.concatenate(do_parts, axis=0))
                dp = _dot_nt(dob, vw)
                drow = jnp.sum(p * dp, axis=-1, keepdims=True)
                ds = _bf(p * (dp - drow))
                sink_parts.append(ps * drow)
                dqn_s[first:first + HALF, r, :] = (_dot(ds, kw) * scale).reshape(HALF, C, HD)
                dk, dv = _dot_tn(ds, q2), _dot_tn(pb, dob)
                dk_sum = dk if dk_sum is None else dk_sum + dk
                dv_sum = dv if dv_sum is None else dv_sum + dv
            dkn_s[w, :] += dk_sum * scale
            dv_s[w, :] += dv_sum
            return dsink - jnp.concatenate(sink_parts, axis=0)

        dsink = _loop(nC, blk, jnp.zeros((G * C, 1), F32))

        def fin(n, acc):
            dqg, dkg = acc
            r = pl.ds(pl.multiple_of(n * C, C), C)
            cs, su, sd = cs_ref[r, :], su_ref[r, :], sd_ref[r, :]

            def norm_bwd(x, gain, dqn):
                rr = lax.rsqrt(jnp.mean(x * x, axis=-1, keepdims=True) + RMS_EPS)
                u = x * rr
                dw = _rot_part(dqn, cs, -su, -sd)
                du = dw * gain
                gsum = dw * u
                if gsum.ndim == 3:
                    gsum = jnp.sum(gsum, axis=0)
                return rr * (du - u * jnp.mean(du * u, axis=-1, keepdims=True)), jnp.sum(gsum, axis=0, keepdims=True)

            dk, gk = norm_bwd(_f32(k_ref[r, :]), kg_ref[...], dkn_s[r, :])
            dk_ref[r, :] = _bf(dk)
            dv_ref[r, :] = _bf(dv_s[r, :])
            dq, gq = norm_bwd(_heads(_f32(q_ref[r, :]), HD), qg_ref[...], dqn_s[:, r, :])
            for g in range(G):
                dq_ref[r, g * HD:(g + 1) * HD] = _bf(dq[g])
            return dqg + gq, dkg + gk

        dqg, dkg = lax.fori_loop(0, nC, fin, (jnp.zeros((1, HD), F32), jnp.zeros((1, HD), F32)))
        sm_ref[...] = jnp.zeros_like(sm_ref)
        sm_ref[0:1, :] = dqg
        sm_ref[1:2, :] = dkg
        for g in range(G):
            sm_ref[2 + g:3 + g, :] = jnp.broadcast_to(_sum11(dsink[g * C:(g + 1) * C, :]), (1, HD))
        for cp in writes(kv):
            cp.start()

        @pl.when(kv == dm.AKV - 1)
        def _():
            for cp in writes(kv):
                cp.wait()

    tab = pl.BlockSpec((S, HD), lambda h: (0, 0), pipeline_mode=ONE_BUFFER)
    gain = pl.BlockSpec((1, HD), lambda h: (0, 0))
    wide = lambda off: pl.BlockSpec((S, GW), lambda h: (0, off // GW + h), pipeline_mode=ONE_BUFFER)
    thin = lambda off: pl.BlockSpec((S, HD), lambda h: (0, off // HD + h), pipeline_mode=ONE_BUFFER)
    return _call(
        body, name="att_bwd", grid=(dm.AKV,),
        in_specs=[wide(dm.o_aq), thin(dm.o_ak), thin(dm.o_av), wide(dm.o_ag), wide(0), gain, gain,
                  pl.BlockSpec((None, G, 128), lambda h: (h, 0, 0)), tab, tab, tab],
        out_specs=[pl.BlockSpec((None, 8, 128), lambda h: (h, 0, 0))],
        out_shape=[jax.ShapeDtypeStruct((dm.AKV, 8, 128), F32)],
        scratch_shapes=[pltpu.VMEM((G, S, HD), BF16), pltpu.VMEM((S, HD), BF16),
                        pltpu.VMEM((G, S, HD), F32), pltpu.VMEM((S, HD), F32), pltpu.VMEM((S, HD), F32),
                        pltpu.VMEM((3, G * C, 3 * C), F32),
                        pltpu.VMEM((S, GW), BF16), pltpu.VMEM((S, HD), BF16), pltpu.VMEM((S, HD), BF16),
                        pltpu.VMEM((S, GW), BF16), pltpu.SemaphoreType.DMA((4,))],
        args=(z, z, z, z, dyc_pre, q_gain, k_gain, _sink_rows(sink, dm), *tabs), tasks=tasks, carry=[dz])


def _grad_matmul(a, b, name, tasks=()):
    S, M = a.shape
    N = b.shape[1]
    tm, tn = min(1024, M), min(512, N)

    def body(a_ref, b_ref, o_ref):
        o_ref[...] = _bf(_dot_tn(a_ref[...], b_ref[...]))

    own, tk = _call(
        body, name=name, grid=(M // tm, N // tn),
        in_specs=[pl.BlockSpec((S, tm), lambda i, j: (0, i)), pl.BlockSpec((S, tn), lambda i, j: (0, j))],
        out_specs=[pl.BlockSpec((None, tm, tn), lambda i, j: (0, i, j))],
        out_shape=[jax.ShapeDtypeStruct((1, M, N), BF16)],
        args=(a, b), tasks=tasks)
    return own[0], tk


def _inproj_bwd(dz, w_in, dm, tasks=()):
    S, D = dm.S, dm.D
    tm = dm.tm
    tk = dm.INW // N_CHIPS if (dm.INW // N_CHIPS) % 128 == 0 else 512
    nk = dm.INW // tk

    def body(dz_ref, w_ref, dh_ref):
        part = _dot_nt(dz_ref[...], w_ref[...])

        @pl.when(pl.program_id(1) == 0)
        def _():
            dh_ref[...] = part

        @pl.when(pl.program_id(1) != 0)
        def _():
            dh_ref[...] += part

    return _call(
        body, name="inproj_bwd", grid=(S // tm, nk),
        in_specs=[pl.BlockSpec((tm, tk), lambda i, k: (i, k)), pl.BlockSpec((None, D, tk), lambda i, k: (0, 0, k))],
        out_specs=[pl.BlockSpec((tm, D), lambda i, k: (i, 0))],
        out_shape=[jax.ShapeDtypeStruct((S, D), F32)],
        args=(dz, w_in), tasks=tasks)


def _norm_bwd(dh, x, norm_g, dx_out, dm):
    S, D = dm.S, dm.D
    tm = dm.tm

    def body(dh_ref, x_ref, g_ref, dxo_ref, dx_ref, dg_ref):
        @pl.when(pl.program_id(0) == 0)
        def _():
            dg_ref[...] = jnp.zeros_like(dg_ref)

        xv = x_ref[...]
        rr = lax.rsqrt(jnp.mean(xv * xv, axis=-1, keepdims=True) + RMS_EPS)
        u = xv * rr
        dh = dh_ref[...]
        dg_ref[...] += jnp.sum(dh * u, axis=0, keepdims=True)
        du = dh * g_ref[...]
        dx_ref[...] = dxo_ref[...] + rr * (du - u * jnp.mean(du * u, axis=-1, keepdims=True))

    row = pl.BlockSpec((tm, D), lambda i: (i, 0))
    vec = pl.BlockSpec((1, D), lambda i: (0, 0))
    own, _ = _call(
        body, name="norm_bwd", grid=(S // tm,), in_specs=[row, row, vec, row], out_specs=[row, vec],
        out_shape=[jax.ShapeDtypeStruct((S, D), F32), jax.ShapeDtypeStruct((1, D), F32)],
        args=(dh, x, norm_g, dx_out))
    return own


def _row_block(rows, width, itemsize):
    target = max(16, (2 * 1024 * 1024) // (width * itemsize))
    for rb in range(min(rows, target), 0, -1):
        if rows % rb == 0 and (rb % 16 == 0 or rb == rows):
            return rb
    return rows


def _prefetch_call(body, *, name, grid, in_specs, out_specs, out_shape, args, aliases=None):
    grid_spec = pltpu.PrefetchScalarGridSpec(num_scalar_prefetch=1, grid=grid, in_specs=in_specs, out_specs=out_specs)
    return pl.pallas_call(
        body, name=name, grid_spec=grid_spec, out_shape=out_shape, input_output_aliases=aliases or {},
        compiler_params=pltpu.CompilerParams(dimension_semantics=("arbitrary",) * len(grid),
                                             vmem_limit_bytes=VMEM_LIMIT_V7X),
    )(*args)


def _place_shard(w, l, spec, chip):
    A, rows, width = spec.shard
    rb = _row_block(rows, width, 4)
    nrb = rows // rb
    if spec.kind == "col":
        out_map = lambda a, r, chip: (a, r, chip[0])
    else:
        out_map = lambda a, r, chip: (a, chip[0] * nrb + r, 0)

    def body(chip_ref, w_ref, o_ref):
        o_ref[...] = _bf(w_ref[...])

    return _prefetch_call(
        body, name="place_" + spec.name, grid=(A, nrb),
        in_specs=[pl.BlockSpec((None, rb, width), lambda a, r, chip: (l * A + a, r, 0))],
        out_specs=pl.BlockSpec((None, rb, width), out_map),
        out_shape=jax.ShapeDtypeStruct(spec.full, BF16), args=(chip, w))


def _pair_sum(grad, land, spec, core):
    A, hr, w = spec.half
    rb = _row_block(hr, w, 2)
    nrb = hr // rb
    if spec.kind == "col":
        g_spec = pl.BlockSpec((None, rb, w), lambda j, a, r, core: (a, core[0] * nrb + r, j))
    else:
        g_spec = pl.BlockSpec((None, rb, w), lambda j, a, r, core: (a, (j * 2 + core[0]) * nrb + r, 0))

    def body(core_ref, g_ref, l_ref, o_ref):
        o_ref[...] = _bf(_f32(g_ref[...]) + _f32(l_ref[...]))

    blk = pl.BlockSpec((None, None, rb, w), lambda j, a, r, core: (j, a, r, 0))
    return _prefetch_call(
        body, name="pair_sum_" + spec.name, grid=(N_CHIPS, A, nrb), in_specs=[g_spec, blk], out_specs=blk,
        out_shape=jax.ShapeDtypeStruct((N_CHIPS,) + spec.half, BF16), args=(core, grad, land))


def _chip_sum(pair_sum, land, spec, chip_core):
    A, hr, w = spec.half
    rb = _row_block(hr, w, 4)
    nrb = hr // rb

    def body(cc_ref, p_ref, l0_ref, l1_ref, l2_ref, o_ref):
        o_ref[...] = ((_f32(p_ref[...]) + _f32(l0_ref[...])) + _f32(l1_ref[...])) + _f32(l2_ref[...])

    own = pl.BlockSpec((None, None, rb, w), lambda a, r, cc: (cc[0], a, r, 0))
    slot = lambda p: pl.BlockSpec((None, None, rb, w), lambda a, r, cc: (p, a, r, 0))
    return _prefetch_call(
        body, name="chip_sum_" + spec.name, grid=(A, nrb), in_specs=[own, slot(0), slot(1), slot(2)],
        out_specs=pl.BlockSpec((None, rb, w), lambda a, r, cc: (a, cc[1] * nrb + r, 0)),
        out_shape=jax.ShapeDtypeStruct(spec.shard, F32), args=(chip_core, pair_sum, land, land, land))


def _adamw_math(w, g, m, v):
    m = ADAM_B1 * m + (1.0 - ADAM_B1) * g
    v = ADAM_B2 * v + (1.0 - ADAM_B2) * (g * g)
    m_hat = m / (1.0 - ADAM_B1 ** ADAM_STEP)
    v_hat = v / (1.0 - ADAM_B2 ** ADAM_STEP)
    delta = -ADAM_LR * (m_hat / (jnp.sqrt(v_hat) + ADAM_EPS) + ADAM_WD * w)
    return delta, m, v


def _adamw(w, g, m, v, l, depth, spec, carried):
    A, R, C = spec.shard
    rb = _row_block(R, C, 4 * 4)
    stacked = pl.BlockSpec((None, rb, C), lambda a, r: (l * A + a, r, 0))
    n_carry = 0 if carried is None else 4

    def body(w_ref, g_ref, m_ref, v_ref, *rest):
        go_ref, d_ref, mo_ref, vo_ref = rest[n_carry:]
        g = g_ref[...]
        go_ref[...] = g
        d_ref[...], mo_ref[...], vo_ref[...] = _adamw_math(w_ref[...], g, m_ref[...], v_ref[...])

    return pl.pallas_call(
        body, name="adamw_" + spec.name, grid=(A, R // rb),
        in_specs=[stacked, pl.BlockSpec((None, rb, C), lambda a, r: (a, r, 0)), stacked, stacked] + [HBM_ANY] * n_carry,
        out_specs=[stacked] * 4,
        out_shape=[jax.ShapeDtypeStruct((depth * A, R, C), F32)] * 4,
        input_output_aliases={4 + i: i for i in range(n_carry)},
        compiler_params=pltpu.CompilerParams(dimension_semantics=("arbitrary", "arbitrary"),
                                             vmem_limit_bytes=VMEM_LIMIT_V7X),
    )(w, g, m, v, *(carried or ()))


def _small_update(g_part, w, m, v):
    R = g_part.shape[0]
    n_dev = 8

    def body(g_ref, w_ref, m_ref, v_ref, go_ref, d_ref, mo_ref, vo_ref, all_s, send_sem, recv_sem):
        x, y, c = _place()
        me = 4 * x + 2 * y + c
        all_s[me] = g_ref[...]
        cps = []
        for k in range(1, n_dev):
            peer = (x ^ ((k >> 2) & 1), y ^ ((k >> 1) & 1), c ^ (k & 1))
            cp = pltpu.make_async_remote_copy(src_ref=g_ref, dst_ref=all_s.at[me], send_sem=send_sem.at[k],
                                              recv_sem=recv_sem.at[k], device_id=peer, device_id_type=MESH)
            cp.start()
            cps.append(cp)
        for cp in cps:
            cp.wait()
        g = all_s[0]
        for d in range(1, n_dev):
            g = g + all_s[d]
        go_ref[...] = g
        d_ref[...], mo_ref[...], vo_ref[...] = _adamw_math(w_ref[...], g, m_ref[...], v_ref[...])

    vm = pl.BlockSpec(memory_space=pltpu.VMEM)
    return pl.pallas_call(
        body, name="small_update", in_specs=[vm] * 4, out_specs=[vm] * 4,
        out_shape=[jax.ShapeDtypeStruct((R, 128), F32)] * 4,
        scratch_shapes=[pltpu.VMEM((n_dev, R, 128), F32), pltpu.SemaphoreType.DMA((n_dev,)),
                        pltpu.SemaphoreType.DMA((n_dev,))],
        compiler_params=pltpu.CompilerParams(vmem_limit_bytes=VMEM_LIMIT_V7X),
    )(g_part, w, m, v)


def _pack_small(parts):
    flat = jnp.concatenate([p.reshape(-1) for p in parts])
    pad = (-flat.shape[0]) % 1024
    return jnp.pad(flat, (0, pad)).reshape(-1, 128)


def _unpack_small(packed, like):
    flat = packed.reshape(-1)
    out, at = [], 0
    for p in like:
        out.append(flat[at:at + p.size].reshape(p.shape))
        at += p.size
    return out


def kernel(x, norm_g, w_in, ret_decay_fwd, ret_decay_bwd, pool_w, pool_scale, attn_q_gain, attn_k_gain, attn_sink, w_ret, w_pool, w_att, w_out, loss_target, m_norm_g, m_w_in, m_ret_decay_fwd, m_ret_decay_bwd, m_pool_w, m_pool_scale, m_attn_q_gain, m_attn_k_gain, m_attn_sink, m_w_ret, m_w_pool, m_w_att, m_w_out, v_norm_g, v_w_in, v_ret_decay_fwd, v_ret_decay_bwd, v_pool_w, v_pool_scale, v_attn_q_gain, v_attn_k_gain, v_attn_sink, v_w_ret, v_w_pool, v_w_att, v_w_out):
    S, D = x.shape[1], x.shape[2]
    L = norm_g.shape[0]
    dm = _Dims(S, D, L)
    PD, BW, G = dm.PD, dm.BW, POOL_GROUPS
    xi, yi, ci = _place()
    chip = (2 * xi + yi).astype(jnp.int32).reshape(1)
    core = ci.astype(jnp.int32).reshape(1)
    chip_core = jnp.concatenate([chip, core])

    specs = [_Sharded("w_in", "col", 1, D, dm.INW), _Sharded("w_ret", "col", 1, BW, D), _Sharded("w_pool", "col", 1, BW, D),
             _Sharded("w_att", "col", 1, BW, D), _Sharded("w_out", "row", 1, D, D), _Sharded("pool_w", "row", G, PD, PD)]
    n_big = len(specs)
    big_w = [w_in, w_ret, w_pool, w_att, w_out, pool_w]
    big_m = [m_w_in, m_w_ret, m_w_pool, m_w_att, m_w_out, m_pool_w]
    big_v = [v_w_in, v_w_ret, v_w_pool, v_w_att, v_w_out, v_pool_w]
    stack3 = lambda a, s: a.reshape((L * s.shard[0],) + s.shard[1:])
    big_w3 = [stack3(a, s) for a, s in zip(big_w, specs)]
    big_m3 = [stack3(a, s) for a, s in zip(big_m, specs)]
    big_v3 = [stack3(a, s) for a, s in zip(big_v, specs)]

    W = [[_place_shard(big_w3[t], l, specs[t], chip) for t in range(n_big)] for l in range(L)]
    W[0] = _gather_first(W[0], specs)

    cosr, sinr = _ret_rope_tables(S, dm.RD)
    tabs = _att_tables(S)
    xl = x[0]
    saved = []
    for l in range(L):
        nxt = l + 1 < L
        ng = norm_g[l].reshape(1, D)
        qg, kg = attn_q_gain[l].reshape(1, ATT_HEAD_DIM), attn_k_gain[l].reshape(1, ATT_HEAD_DIM)
        psc = pool_scale[l].reshape(G, 1, PD)
        f_in, f_ret, f_pool, f_att, f_out, f_pw = W[l]
        (z, h), tk = _inproj(xl, ng, f_in, dm, tasks=[_gather_ici_task(W[l + 1][:1], specs[:1])] if nxt else ())
        if nxt:
            W[l + 1][:1] = tk[0][0]
        ya_pre = _ret_fwd(z, ret_decay_fwd[l], ret_decay_bwd[l], cosr, sinr, dm)
        yb_pre = _pool_fwd(z, f_pw, psc, dm)
        (yc_pre,), tk = _att_fwd(z, qg, kg, attn_sink[l], tabs, dm,
                                 tasks=[_gather_ici_task(W[l + 1][1:], specs[1:])] if nxt else ())
        if nxt:
            W[l + 1][1:] = tk[0][0]
        mg = z
        (x_next, ya, yb, yc, merged), tk = _merge_out(xl, mg, ya_pre, yb_pre, yc_pre, f_ret, f_pool, f_att, f_out, dm,
                                                      tasks=[_gather_d2d_task(W[l + 1], specs)] if nxt else ())
        if nxt:
            W[l + 1] = tk[0][0]
        saved.append((xl, z, h, mg, ya_pre, yb_pre, yc_pre, ya, yb, yc, merged, ng, qg, kg, psc))
        xl = x_next
    sq, dx = _loss_grad(xl, loss_target[0], dm)
    loss = lax.psum(sq[0, 0] * (0.5 / D), ("x", "y", "c"))

    grads = [None] * L
    lands = [None] * L
    pairs = [None] * L
    lands2 = [None] * L
    shards = [None] * L
    carried = [None] * n_big
    small = [None] * L

    def pair_sums(k):
        pairs[k] = [_pair_sum(g, ld, s, core) for g, ld, s in zip(grads[k], lands[k], specs)]

    def chip_sums(k):
        shards[k] = [_chip_sum(p, ld, s, chip_core) for p, ld, s in zip(pairs[k], lands2[k], specs)]

    def adamw(k):
        for t in range(n_big):
            carried[t] = _adamw(big_w3[t], shards[k][t], big_m3[t], big_v3[t], k, L, specs[t], carried[t])

    for l in reversed(range(L)):
        xl, z, h, mg, ya_pre, yb_pre, yc_pre, ya, yb, yc, merged, ng, qg, kg, psc = saved[l]
        f_in, f_ret, f_pool, f_att, f_out, f_pw = W[l]
        up1, up2 = l + 1 < L, l + 2 < L
        own, tk = _merge_bwd_gates(dx, mg, ya, yb, yc, f_out, dm, tasks=[_pair_task(grads[l + 1], specs)] if up1 else ())
        dz, dya, dyb, dyc, dxb = own
        if up1:
            lands[l + 1] = tk[0][1]
            pair_sums(l + 1)
        dya_pre, dyb_pre, dyc_pre = _merge_bwd_proj(dya, dyb, dyc, f_ret, f_pool, f_att, dm)
        g_out, _ = _grad_matmul(merged, dxb, "grad_w_out")
        g_ret, _ = _grad_matmul(ya_pre, dya, "grad_w_ret")
        g_pool, _ = _grad_matmul(yb_pre, dyb, "grad_w_pool")
        g_att, _ = _grad_matmul(yc_pre, dyc, "grad_w_att")
        last = l == 0
        in_rows = specs[0].half_rows
        cuts = [0, in_rows // 4, 3 * in_rows // 4, in_rows]
        tasks = [_chip_task(pairs[l + 1][1:], specs[1:])] if up1 else []
        if up1 and last:
            tasks.append(_chip_rows_task(pairs[1][0], None, specs[0], cuts[0], cuts[1]))
        (ddec, dz), tk = _ret_bwd(z, dz, dya_pre, ret_decay_fwd[l], ret_decay_bwd[l], cosr, sinr, dm, tasks=tasks)
        if up1:
            lands2[l + 1] = [None] + tk[0][1]
            land_in = tk[1][1][0] if last else None
        dpw, dps, dz = _pool_bwd(z, dz, dyb_pre, f_pw, psc, dm)
        tasks = [_half_task(shards[l + 2], specs)] if up2 else []
        if up1 and last:
            tasks.append(_chip_rows_task(pairs[1][0], land_in, specs[0], cuts[1], cuts[2]))
        if last:
            tasks.append(_pair_task([g_ret, g_pool, g_att, g_out, dpw], specs[1:]))
        (dsm, dz), tk = _att_bwd(z, dz, dyc_pre, qg, kg, attn_sink[l], tabs, dm, tasks=tasks)
        if up2:
            shards[l + 2] = tk[0][0]
            adamw(l + 2)
        if up1 and last:
            land_in = tk[1 if up2 else 0][0][0]
        if last:
            small_lands = tk[-1][1]
            small_pairs = [_pair_sum(g, ld, s, core)
                           for g, ld, s in zip([g_ret, g_pool, g_att, g_out, dpw], small_lands, specs[1:])]
        if not last:
            g_in, _ = _grad_matmul(h, dz, "grad_w_in")
            (dh,), tk = _inproj_bwd(dz, f_in, dm, tasks=[_chip_task(pairs[l + 1][:1], specs[:1])] if up1 else ())
            if up1:
                lands2[l + 1][0] = tk[0][1][0]
                chip_sums(l + 1)
            grads[l] = [g_in, g_ret, g_pool, g_att, g_out, dpw]
        else:
            tasks = [_chip_rows_task(pairs[1][0], land_in, specs[0], cuts[2], cuts[3])] if up1 else []
            tasks.append(_chip_task(small_pairs, specs[1:]))
            g_in, tk = _grad_matmul(h, dz, "grad_w_in", tasks=tasks)
            if up1:
                lands2[1][0] = tk[0][0][0]
                chip_sums(1)
            small_lands2 = tk[-1][1]
            (_, in_land), = _comm_only("grad_pair_exchange", [_pair_task([g_in], specs[:1])])
            pairs[0] = [_pair_sum(g_in, in_land[0], specs[0], core)] + small_pairs
            tasks = [_chip_task(pairs[0][:1], specs[:1])] + ([_half_task(shards[1], specs)] if up1 else [])
            (dh,), tk = _inproj_bwd(dz, f_in, dm, tasks=tasks)
            lands2[0] = tk[0][1] + small_lands2
            if up1:
                shards[1] = tk[1][0]
            chip_sums(0)
        dx, dng = _norm_bwd(dh, xl, ng, dx, dm)
        small[l] = [dng.reshape(D), ddec[:, 0, 0], ddec[:, 1, 0], dps.reshape(BW), jnp.sum(dsm[:, 0, :], axis=0),
                    jnp.sum(dsm[:, 1, :], axis=0), dsm[:, 2:2 + ATT_GROUP, 0].reshape(dm.AH)]

    ((shards[0], _),) = _comm_only("grad_half_exchange", [_half_task(shards[0], specs)])
    for k in (1, 0):
        if k < L:
            adamw(k)

    back = lambda a, like: a.reshape(like.shape)
    g_big, d_big, m_big, v_big = ([back(carried[t][i], big_w[t]) for t in range(n_big)] for i in range(4))

    small_g = [jnp.stack([small[l][i] for l in range(L)]) for i in range(7)]
    small_w = [norm_g, ret_decay_fwd, ret_decay_bwd, pool_scale, attn_q_gain, attn_k_gain, attn_sink]
    small_m = [m_norm_g, m_ret_decay_fwd, m_ret_decay_bwd, m_pool_scale, m_attn_q_gain, m_attn_k_gain, m_attn_sink]
    small_v = [v_norm_g, v_ret_decay_fwd, v_ret_decay_bwd, v_pool_scale, v_attn_q_gain, v_attn_k_gain, v_attn_sink]
    sg, sd, sm, sv = _small_update(_pack_small(small_g), _pack_small(small_w), _pack_small(small_m), _pack_small(small_v))
    g_sm, d_sm, m_sm, v_sm = (_unpack_small(a, small_w) for a in (sg, sd, sm, sv))

    def ordered(big, small_):
        return [small_[0], big[0], small_[1], small_[2], big[5], small_[3], small_[4], small_[5], small_[6],
                big[1], big[2], big[3], big[4]]

    return (loss, dx[None], *ordered(g_big, g_sm), *ordered(d_big, d_sm), *ordered(m_big, m_sm),
            *ordered(v_big, v_sm))
```
